```python
import math
import jax, jax.numpy as jnp
from jax import lax
import numpy as np

D_MODEL = 2048
BATCH = 4
SEQ = 2048
DEPTH = 4
DEC_BATCH = 8
DEC_SEQ = 4
PAST_LEN = 16384
PAGE_SIZE = 128

N_MIXERS = 4
EPS = 1e-6
NH_A = 8
DV_A = D_MODEL // NH_A
DK_A = DV_A // 2
CHUNK_A = 64
FGATE_BIAS = 3.0
M_INIT = -1e30
NH_B = 16
DH_B = D_MODEL // NH_B
DIL_PAIRS = ((128, 1), (512, 4), (2048, 16))
W_MAX = 2048
NH_C = 8
DH_C = D_MODEL // (2 * NH_C)
DV_C = 2 * DH_C
Q_BLOCK = 128
D_RNN = D_MODEL
NBLK_D = 16
BS_D = D_RNN // NBLK_D
CONV_W = 4
LRU_C = 8.0
NH_P = 8
N_KEYS = 128
N_EXP = N_KEYS * N_KEYS
D_KEY = 256
TOPK_P = 16
TOK_BLOCK = 64
REL_BUCKETS = 32
REL_MAX_DIST = 2048
N_BIAS_MAPS = 16

kernel_name = "hybrid_mlstm_dilated_diff_rglru_peer_step"

F32 = jnp.float32


def rms_norm(x, g):
    xf = x.astype(F32)
    y = xf * lax.rsqrt(jnp.mean(xf * xf, axis=-1, keepdims=True) + EPS)
    return (y * g.astype(F32)).astype(x.dtype)


def rel_bucket(dist):
    exact = REL_BUCKETS // 2
    d_f = jnp.maximum(dist, 1).astype(F32)
    large = exact + (jnp.log(d_f / exact) / math.log(REL_MAX_DIST / exact) * (REL_BUCKETS - exact)).astype(jnp.int32)
    large = jnp.minimum(large, REL_BUCKETS - 1)
    return jnp.where(dist < exact, dist, large)


def ada_params(c, w, b):
    mod = jax.nn.silu(c) @ w + b
    return [m[:, None, :] for m in jnp.split(mod, 6, axis=-1)]


def modulate(x, g, shift, scale):
    return rms_norm(x, g) * (1 + scale) + shift


def mlstm_chunkwise(q, k, v, li, lf, C0, n0, m0):
    B, H, L, _ = q.shape
    c = math.gcd(L, CHUNK_A)
    nc = L // c

    def chunks(a):
        return jnp.moveaxis(a.reshape(a.shape[:2] + (nc, c) + a.shape[3:]), 2, 0)

    causal = jnp.tril(jnp.ones((c, c), dtype=bool))

    def step(carry, inp):
        C, n, m = carry
        qc, kc, vc, lic, lfc = inp
        b = jnp.cumsum(lfc, axis=-1)
        D = jnp.where(causal, b[..., :, None] - b[..., None, :] + lic[..., None, :], -jnp.inf)
        m_t = jnp.maximum(b + m[..., None], jnp.max(D, axis=-1))
        S = jnp.einsum('bhtk,bhsk->bhts', qc, kc) * jnp.exp(D - m_t[..., None])
        inter = jnp.exp(b + m[..., None] - m_t)
        num = jnp.einsum('bhts,bhsv->bhtv', S, vc) + inter[..., None] * jnp.einsum('bhtk,bhkv->bhtv', qc, C)
        den = jnp.sum(S, axis=-1) + inter * jnp.einsum('bhtk,bhk->bht', qc, n)
        h = num / jnp.maximum(jnp.abs(den), jnp.exp(-m_t))[..., None]
        m_new = m_t[..., -1]
        w = jnp.exp(b[..., -1:] - b + lic - m_new[..., None])
        decay = jnp.exp(b[..., -1] + m - m_new)
        C_new = decay[..., None, None] * C + jnp.einsum('bhs,bhsk,bhsv->bhkv', w, kc, vc)
        n_new = decay[..., None] * n + jnp.einsum('bhs,bhsk->bhk', w, kc)
        return (C_new, n_new, m_new), h

    (C, n, m), hs = lax.scan(step, (C0, n0, m0), tuple(chunks(a) for a in (q, k, v, li, lf)))
    hs = jnp.moveaxis(hs, 0, 2).reshape(B, H, L, -1)
    return hs, C, n, m


def mlstm_mixer(h, C0, n0, m0, w_in, b_gates, g_out, w_out):
    B, L, _ = h.shape
    splits = np.cumsum([NH_A * DK_A, NH_A * DK_A, NH_A * DV_A, NH_A * DV_A]).tolist()
    q, k, v, o, g = jnp.split(h @ w_in, splits, axis=-1)

    def heads(a, d):
        return a.reshape(B, L, NH_A, d).transpose(0, 2, 1, 3).astype(F32)

    q = heads(q, DK_A)
    k = heads(k, DK_A) * DK_A ** -0.5
    v = heads(v, DV_A)
    g = (g + b_gates).astype(F32).reshape(B, L, 2, NH_A).transpose(2, 0, 3, 1)
    li, lf = g[0], jax.nn.log_sigmoid(g[1])
    hs, C, n, m = mlstm_chunkwise(q, k, v, li, lf, C0.astype(F32), n0.astype(F32), m0.astype(F32))
    hs = jax.nn.sigmoid(o.astype(F32)).reshape(B, L, NH_A, DV_A) * hs.transpose(0, 2, 1, 3)
    hs = rms_norm(hs, g_out.reshape(NH_A, DV_A)).reshape(B, L, NH_A * DV_A).astype(h.dtype)
    return hs @ w_out, C, n, m


def swa_project(h, w_in, g_q, g_k):
    B, L, _ = h.shape
    q, k, v = jnp.split(h @ w_in, 3, axis=-1)
    q = rms_norm(q.reshape(B, L, NH_B, DH_B), g_q)
    k = rms_norm(k.reshape(B, L, NH_B, DH_B), g_k)
    return q, k, v.reshape(B, L, NH_B, DH_B)


def merge_branches(outs, lses):
    wts = jax.nn.softmax(jnp.stack(lses), axis=0)[..., None]
    return jnp.sum(wts * jnp.stack(outs), axis=0)


def dilated_prompt(q, k, v, rel_bias):
    B, S, H, Dh = q.shape
    outs, lses = [], []
    for w, d in DIL_PAIRS:
        J = w // d
        blk = J
        Ls = S // d
        nb = -(-Ls // blk)
        Lp = nb * blk

        def sub(a, front):
            a = a.reshape(B, Ls, d, H, Dh).transpose(0, 2, 3, 1, 4)
            return jnp.pad(a, ((0, 0), (0, 0), (0, 0), (front, Lp - Ls), (0, 0)))

        qs = sub(q, 0).reshape(B, d, H, nb, blk, Dh)
        kb = sub(k, blk).reshape(B, d, H, nb + 1, blk, Dh)
        vb = sub(v, blk).reshape(B, d, H, nb + 1, blk, Dh)
        kw = jnp.concatenate([kb[:, :, :, :-1], kb[:, :, :, 1:]], axis=-2)
        vw = jnp.concatenate([vb[:, :, :, :-1], vb[:, :, :, 1:]], axis=-2)
        iq = jnp.arange(blk)[:, None]
        ik = jnp.arange(2 * blk)[None, :]
        j = iq + blk - ik
        lk = jnp.arange(nb)[:, None, None] * blk - blk + ik[None]
        valid = (j >= 0) & (j <= J) & (lk >= 0)
        bias = rel_bias[rel_bucket(jnp.maximum(j, 0) * d)].astype(F32)
        bias = jnp.transpose(bias, (2, 0, 1))[None, None, :, None]
        s = jnp.einsum('bdhnqe,bdhnke->bdhnqk', qs, kw).astype(F32) * DH_B ** -0.5 + bias
        s = jnp.where(valid, s, -jnp.inf)
        m = jnp.max(s, axis=-1, keepdims=True)
        p = jnp.exp(s - m)
        den = jnp.sum(p, axis=-1)
        o = jnp.einsum('bdhnqk,bdhnke->bdhnqe', p, vw.astype(F32)) / den[..., None]
        lse = m[..., 0] + jnp.log(den)
        o = o.reshape(B, d, H, Lp, Dh)[:, :, :, :Ls].transpose(0, 3, 1, 2, 4).reshape(B, S, H, Dh)
        lse = lse.reshape(B, d, H, Lp)[..., :Ls].transpose(0, 3, 1, 2).reshape(B, S, H)
        outs.append(o)
        lses.append(lse)
    return merge_branches(outs, lses)


def dilated_sample(q, k_all, v_all, rel_bias):
    T = q.shape[1]
    Wb = k_all.shape[1] - T
    outs, lses = [], []
    for w, d in DIL_PAIRS:
        dist = jnp.arange(w // d + 1) * d
        idx = Wb + jnp.arange(T)[:, None] - dist[None, :]
        valid = idx >= 0
        idx = jnp.maximum(idx, 0)
        kg = k_all[:, idx]
        vg = v_all[:, idx]
        bias = rel_bias[rel_bucket(dist)].astype(F32).T[None, :, None, :]
        s = jnp.einsum('bthe,btjhe->bhtj', q, kg).astype(F32) * DH_B ** -0.5 + bias
        s = jnp.where(valid[None, None], s, -jnp.inf)
        m = jnp.max(s, axis=-1, keepdims=True)
        p = jnp.exp(s - m)
        den = jnp.sum(p, axis=-1)
        o = jnp.einsum('bhtj,btjhe->bthe', p, vg.astype(F32)) / jnp.transpose(den, (0, 2, 1))[..., None]
        lse = jnp.transpose(m[..., 0] + jnp.log(den), (0, 2, 1))
        outs.append(o)
        lses.append(lse)
    return merge_branches(outs, lses)


def diff_project(h, w_in, g_q, g_k):
    B, L, _ = h.shape
    q, k, v = jnp.split(h @ w_in, 3, axis=-1)
    q = rms_norm(q.reshape(B, L, NH_C, 2, DH_C), g_q)
    k = rms_norm(k.reshape(B, L, NH_C, 2, DH_C), g_k)
    return q, k, v.reshape(B, L, NH_C, DV_C)


def diff_attend(qb, k, v, qpos, kpos, lam, rel_bias):
    Tq, Tk = qpos.shape[0], kpos.shape[0]
    dist = qpos[:, None] - kpos[None, :]
    bias = rel_bias[rel_bucket(jnp.maximum(dist, 0))].astype(F32)
    bias = bias.reshape(Tq, Tk, 2, NH_C).transpose(2, 3, 0, 1)
    s = jnp.einsum('bqhme,bkhme->bmhqk', qb, k).astype(F32) * DH_C ** -0.5 + bias
    s = jnp.where(dist >= 0, s, -jnp.inf)
    p = jax.nn.softmax(s, axis=-1)
    a = p[:, 0] - lam * p[:, 1]
    return jnp.einsum('bhqk,bkhe->bqhe', a, v.astype(F32))


def diff_prompt(q, k, v, lam, rel_bias):
    B, S = q.shape[:2]
    kpos = jnp.arange(S)

    def block(bi):
        q0 = bi * Q_BLOCK
        qb = lax.dynamic_slice_in_dim(q, q0, Q_BLOCK, axis=1)
        return diff_attend(qb, k, v, q0 + jnp.arange(Q_BLOCK), kpos, lam, rel_bias)

    out = lax.map(block, jnp.arange(S // Q_BLOCK))
    return out.transpose(1, 0, 2, 3, 4).reshape(B, S, NH_C, DV_C)


def diff_lambda(lq1, lk1, lq2, lk2, lam_init):
    return (jnp.exp(jnp.sum(lq1.astype(F32) * lk1.astype(F32))) - jnp.exp(jnp.sum(lq2.astype(F32) * lk2.astype(F32))) + lam_init)


def diff_output(o, g_out, w_out, lam_init, dtype):
    B, L = o.shape[:2]
    o = rms_norm(o, g_out) * (1.0 - lam_init)
    return o.reshape(B, L, NH_C * DV_C).astype(dtype) @ w_out


def lin_combine(c1, c2):
    a1, b1 = c1
    a2, b2 = c2
    return a1 * a2, a2 * b1 + b2


def rglru_mixer(h, conv_state, h0, w_in, conv_w, conv_b, w_a, b_a, w_x, b_x, lam, w_out):
    B, L, _ = h.shape
    y, xb = jnp.split(h @ w_in, 2, axis=-1)
    xp = jnp.concatenate([conv_state.astype(xb.dtype), xb], axis=1)
    new_conv = xp[:, -(CONV_W - 1):]
    conv = sum(conv_w[j] * xp[:, j:j + L] for j in range(CONV_W)) + conv_b
    conv = conv.astype(F32)
    xblk = conv.reshape(B, L, NBLK_D, BS_D)
    r = jax.nn.sigmoid(jnp.einsum('blnd,nde->blne', xblk, w_a.astype(F32)).reshape(B, L, D_RNN) + b_a)
    ig = jax.nn.sigmoid(jnp.einsum('blnd,nde->blne', xblk, w_x.astype(F32)).reshape(B, L, D_RNN) + b_x)
    log_a = -LRU_C * r * jax.nn.softplus(-lam.astype(F32))
    a = jnp.exp(log_a)
    bx = jnp.sqrt(-jnp.expm1(2.0 * log_a)) * ig * conv
    A, Bc = lax.associative_scan(lin_combine, (a, bx), axis=1)
    hs = A * h0.astype(F32)[:, None] + Bc
    out = (hs * jax.nn.gelu(y.astype(F32), approximate=False)).astype(h.dtype) @ w_out
    return out, new_conv, hs[:, -1]


def peer_ffn(h, w_q, keys, u, v):
    B, L, D = h.shape
    n = B * L
    nblk = -(-n // TOK_BLOCK)
    x = jnp.pad(h.reshape(n, D), ((0, nblk * TOK_BLOCK - n), (0, 0))).reshape(nblk, TOK_BLOCK, D)

    def block(xb):
        q = (xb @ w_q).astype(F32).reshape(TOK_BLOCK, NH_P, 2, D_KEY // 2)
        s = jnp.einsum('tnpd,npkd->tnpk', q, keys.astype(F32))
        sv, si = lax.top_k(s, TOPK_P)
        cand = sv[:, :, 0, :, None] + sv[:, :, 1, None, :]
        cidx = si[:, :, 0, :, None] * N_KEYS + si[:, :, 1, None, :]
        sc, pos = lax.top_k(cand.reshape(TOK_BLOCK, NH_P, TOPK_P * TOPK_P), TOPK_P)
        idx = jnp.take_along_axis(cidx.reshape(TOK_BLOCK, NH_P, TOPK_P * TOPK_P), pos, axis=-1)
        g = jax.nn.softmax(sc, axis=-1)
        act = jax.nn.gelu(jnp.einsum('td,tnkd->tnk', xb.astype(F32), u[idx].astype(F32)), approximate=False)
        return jnp.einsum('tnk,tnkd->td', g * act, v[idx].astype(F32))

    out = lax.map(block, x).reshape(nblk * TOK_BLOCK, D)[:n]
    return out.reshape(B, L, D).astype(h.dtype)


def setup_inputs(seed: int = 0) -> dict:
    key = jax.random.key(seed)
    ks = iter(jax.random.split(key, 64))

    def nrm(shape, scale):
        return scale * jax.random.normal(next(ks), shape, F32)

    D = D_MODEL
    n_pages = PAST_LEN // PAGE_SIZE
    n_used = DEC_BATCH * n_pages
    n_pool = n_used + max(1, n_used // 4)
    win_buf = min(W_MAX, PAST_LEN)
    page_table = jax.random.permutation(next(ks), n_pool)[:n_used].reshape(DEC_BATCH, n_pages).astype(jnp.int32)
    a0 = jax.random.uniform(next(ks), (D_RNN,), F32, 0.9, 0.999)
    a_base = a0 ** (1.0 / LRU_C)
    rglru_lambda = jnp.log(a_base) - jnp.log1p(-a_base)
    return {
        'x_prompt': nrm((BATCH, SEQ, D), 1.0),
        'x_sample': nrm((DEC_BATCH, DEC_SEQ, D), 1.0),
        'state_mlstm_C': nrm((DEC_BATCH, NH_A, DK_A, DV_A), 0.1),
        'state_mlstm_n': nrm((DEC_BATCH, NH_A, DK_A), 0.1),
        'state_mlstm_m': 1.0 + nrm((DEC_BATCH, NH_A), 0.5),
        'cache_swa_k': nrm((DEC_BATCH, win_buf, NH_B, DH_B), 1.0),
        'cache_swa_v': nrm((DEC_BATCH, win_buf, NH_B, DH_B), 1.0),
        'cache_diff_k': nrm((n_pool, PAGE_SIZE, NH_C, 2 * DH_C), 1.0),
        'cache_diff_v': nrm((n_pool, PAGE_SIZE, NH_C, DV_C), 1.0),
        'state_rglru_conv': nrm((DEC_BATCH, CONV_W - 1, D_RNN), 1.0),
        'state_rglru_h': nrm((DEC_BATCH, D_RNN), 0.5),
        'page_table': page_table,
        'c_prompt': nrm((BATCH, D), 1.0),
        'c_sample': nrm((DEC_BATCH, D), 1.0),
        'w_ada': nrm((DEPTH, D, 6 * D), 0.5 * D ** -0.5),
        'b_ada': nrm((DEPTH, 6 * D), 0.01),
        'g_norm_mix': 1.0 + nrm((DEPTH, D), 0.05),
        'g_norm_ffn': 1.0 + nrm((DEPTH, D), 0.05),
        'rel_bias': nrm((REL_BUCKETS, N_BIAS_MAPS), 0.5),
        'mlstm_w_in': nrm((D, 2 * NH_A * DK_A + 2 * NH_A * DV_A + 2 * NH_A), D ** -0.5),
        'mlstm_b_gates': jnp.concatenate([nrm((NH_A,), 0.1), FGATE_BIAS + nrm((NH_A,), 0.5)]),
        'mlstm_g_out': 1.0 + nrm((NH_A * DV_A,), 0.05),
        'mlstm_w_out': nrm((NH_A * DV_A, D), (NH_A * DV_A) ** -0.5),
        'swa_w_in': nrm((D, 3 * NH_B * DH_B), D ** -0.5),
        'swa_g_q': 1.0 + nrm((DH_B,), 0.05),
        'swa_g_k': 1.0 + nrm((DH_B,), 0.05),
        'swa_w_out': nrm((NH_B * DH_B, D), (NH_B * DH_B) ** -0.5),
        'diff_w_in': nrm((D, 2 * NH_C * 2 * DH_C + NH_C * DV_C), D ** -0.5),
        'diff_g_q': 1.0 + nrm((2, DH_C), 0.05),
        'diff_g_k': 1.0 + nrm((2, DH_C), 0.05),
        'diff_lam_q1': nrm((DH_C,), 0.1),
        'diff_lam_k1': nrm((DH_C,), 0.1),
        'diff_lam_q2': nrm((DH_C,), 0.1),
        'diff_lam_k2': nrm((DH_C,), 0.1),
        'diff_g_out': 1.0 + nrm((DV_C,), 0.05),
        'diff_w_out': nrm((NH_C * DV_C, D), (NH_C * DV_C) ** -0.5),
        'rglru_w_in': nrm((D, 2 * D_RNN), D ** -0.5),
        'rglru_conv_w': nrm((CONV_W, D_RNN), 0.5),
        'rglru_conv_b': nrm((D_RNN,), 0.01),
        'rglru_w_a': nrm((NBLK_D, BS_D, BS_D), BS_D ** -0.5),
        'rglru_b_a': nrm((D_RNN,), 0.01),
        'rglru_w_x': nrm((NBLK_D, BS_D, BS_D), BS_D ** -0.5),
        'rglru_b_x': nrm((D_RNN,), 0.01),
        'rglru_lambda': rglru_lambda,
        'rglru_w_out': nrm((D_RNN, D), D_RNN ** -0.5),
        'peer_w_q': nrm((DEPTH, D, NH_P * D_KEY), D ** -0.5),
        'peer_keys': nrm((DEPTH, NH_P, 2, N_KEYS, D_KEY // 2), (D_KEY // 2) ** -0.5),
        'peer_u': nrm((DEPTH, N_EXP, D), D ** -0.5),
        'peer_v': nrm((DEPTH, N_EXP, D), NH_P ** -0.5),
    }


def reference(x_prompt, x_sample, state_mlstm_C, state_mlstm_n, state_mlstm_m, cache_swa_k, cache_swa_v,
              cache_diff_k, cache_diff_v, state_rglru_conv, state_rglru_h, page_table, c_prompt, c_sample,
              w_ada, b_ada, g_norm_mix, g_norm_ffn, rel_bias,
              mlstm_w_in, mlstm_b_gates, mlstm_g_out, mlstm_w_out,
              swa_w_in, swa_g_q, swa_g_k, swa_w_out,
              diff_w_in, diff_g_q, diff_g_k, diff_lam_q1, diff_lam_k1, diff_lam_q2, diff_lam_k2, diff_g_out, diff_w_out,
              rglru_w_in, rglru_conv_w, rglru_conv_b, rglru_w_a, rglru_b_a, rglru_w_x, rglru_b_x, rglru_lambda, rglru_w_out,
              peer_w_q, peer_keys, peer_u, peer_v):
    xp, xs = x_prompt, x_sample
    Bp, S = xp.shape[:2]
    Bs, T = xs.shape[:2]
    P = page_table.shape[1] * PAGE_SIZE
    for i in range(DEPTH):
        kind = i % N_MIXERS
        ap = ada_params(c_prompt, w_ada[i], b_ada[i])
        as_ = ada_params(c_sample, w_ada[i], b_ada[i])
        hp = modulate(xp, g_norm_mix[i], ap[0], ap[1])
        hs = modulate(xs, g_norm_mix[i], as_[0], as_[1])
        if kind == 0:
            op, mC_p, mn_p, mm_p = mlstm_mixer(hp, jnp.zeros((Bp, NH_A, DK_A, DV_A), F32), jnp.zeros((Bp, NH_A, DK_A), F32),
                                               jnp.full((Bp, NH_A), M_INIT, F32), mlstm_w_in, mlstm_b_gates, mlstm_g_out, mlstm_w_out)
            os_, mC_s, mn_s, mm_s = mlstm_mixer(hs, state_mlstm_C, state_mlstm_n, state_mlstm_m,
                                                mlstm_w_in, mlstm_b_gates, mlstm_g_out, mlstm_w_out)
        elif kind == 1:
            qp, kp, vp = swa_project(hp, swa_w_in, swa_g_q, swa_g_k)
            op = dilated_prompt(qp, kp, vp, rel_bias).reshape(Bp, S, NH_B * DH_B).astype(hp.dtype) @ swa_w_out
            swa_k_p = kp[:, max(0, S - W_MAX):]
            swa_v_p = vp[:, max(0, S - W_MAX):]
            qs, ks_new, vs_new = swa_project(hs, swa_w_in, swa_g_q, swa_g_k)
            k_all = jnp.concatenate([cache_swa_k.astype(ks_new.dtype), ks_new], axis=1)
            v_all = jnp.concatenate([cache_swa_v.astype(vs_new.dtype), vs_new], axis=1)
            os_ = dilated_sample(qs, k_all, v_all, rel_bias).reshape(Bs, T, NH_B * DH_B).astype(hs.dtype) @ swa_w_out
            swa_k_s = k_all[:, T:]
            swa_v_s = v_all[:, T:]
        elif kind == 2:
            lam_init = 0.8 - 0.6 * math.exp(-0.3 * i)
            lam = diff_lambda(diff_lam_q1, diff_lam_k1, diff_lam_q2, diff_lam_k2, lam_init)
            qp, kp, vp = diff_project(hp, diff_w_in, diff_g_q, diff_g_k)
            op = diff_output(diff_prompt(qp, kp, vp, lam, rel_bias), diff_g_out, diff_w_out, lam_init, hp.dtype)
            diff_k_p = kp.reshape(Bp, S, NH_C, 2 * DH_C)
            diff_v_p = vp
            qs, ks_new, vs_new = diff_project(hs, diff_w_in, diff_g_q, diff_g_k)
            k_past = cache_diff_k[page_table].reshape(Bs, P, NH_C, 2, DH_C).astype(ks_new.dtype)
            v_past = cache_diff_v[page_table].reshape(Bs, P, NH_C, DV_C).astype(vs_new.dtype)
            k_all = jnp.concatenate([k_past, ks_new], axis=1)
            v_all = jnp.concatenate([v_past, vs_new], axis=1)
            o_s = diff_attend(qs, k_all, v_all, P + jnp.arange(T), jnp.arange(P + T), lam, rel_bias)
            os_ = diff_output(o_s, diff_g_out, diff_w_out, lam_init, hs.dtype)
            diff_k_s = ks_new.reshape(Bs, T, NH_C, 2 * DH_C)
            diff_v_s = vs_new
        else:
            op, conv_p, h_p = rglru_mixer(hp, jnp.zeros((Bp, CONV_W - 1, D_RNN), F32), jnp.zeros((Bp, D_RNN), F32),
                                          rglru_w_in, rglru_conv_w, rglru_conv_b, rglru_w_a, rglru_b_a,
                                          rglru_w_x, rglru_b_x, rglru_lambda, rglru_w_out)
            os_, conv_s, h_s = rglru_mixer(hs, state_rglru_conv, state_rglru_h,
                                           rglru_w_in, rglru_conv_w, rglru_conv_b, rglru_w_a, rglru_b_a,
                                           rglru_w_x, rglru_b_x, rglru_lambda, rglru_w_out)
        xp = xp + ap[2] * op
        xs = xs + as_[2] * os_
        hp = modulate(xp, g_norm_ffn[i], ap[3], ap[4])
        hs = modulate(xs, g_norm_ffn[i], as_[3], as_[4])
        xp = xp + ap[5] * peer_ffn(hp, peer_w_q[i], peer_keys[i], peer_u[i], peer_v[i])
        xs = xs + as_[5] * peer_ffn(hs, peer_w_q[i], peer_keys[i], peer_u[i], peer_v[i])
    return (xp, xs, mC_p, mC_s, mn_p, mn_s, mm_p, mm_s, swa_k_p, swa_k_s, swa_v_p, swa_v_s,
            diff_k_p, diff_k_s, diff_v_p, diff_v_s, conv_p, conv_s, h_p, h_s)
```

```python
import functools
import math

import numpy as np
import jax
import jax.numpy as jnp
from jax import lax
from jax.experimental import pallas as pl
from jax.experimental.pallas import tpu as pltpu

F32 = jnp.float32
BF16 = jnp.bfloat16
HIGHEST = lax.Precision.HIGHEST

EPS = 1e-6
NEG = -1e30
VMEM_LIMIT = 56 * 1024 * 1024

NH_A, CHUNK_A, M_INIT = 8, 256, -1e30
NH_B = 16
DIL_PAIRS = ((128, 1), (512, 4), (2048, 16))
NH_C = 8
NBLK_D, CONV_W, LRU_C = 16, 4, 8.0
NH_P, N_KEYS, TOPK_P = 8, 128, 16
REL_BUCKETS, REL_MAX_DIST = 32, 2048
PAGE_SIZE = 128


def _cparams(sem):
    return pltpu.CompilerParams(dimension_semantics=sem, vmem_limit_bytes=VMEM_LIMIT)


def _bdot(a, b):
    return jnp.dot(a.astype(BF16), b.astype(BF16), preferred_element_type=F32)


def _bdot_nt(a, b):
    return lax.dot_general(a.astype(BF16), b.astype(BF16), (((1,), (1,)), ((), ())),
                           preferred_element_type=F32)


def _gelu(x):
    return 0.5 * x * (1.0 + lax.erf(x * (1.0 / math.sqrt(2.0))))


def _log_sigmoid(x):
    return jnp.minimum(x, 0.0) - jnp.log1p(jnp.exp(-jnp.abs(x)))


def _softplus(x):
    return jnp.maximum(x, 0.0) + jnp.log1p(jnp.exp(-jnp.abs(x)))


def _pick(n, pref):
    for t in pref:
        if n % t == 0:
            return t
    return n


def _ada_kernel(c_ref, w_ref, b_ref, o_ref):
    c = c_ref[...]
    a = c * jax.nn.sigmoid(c)
    o_ref[0] = jnp.dot(a, w_ref[0], preferred_element_type=F32, precision=HIGHEST) + b_ref[0]


def ada_all(c_all, w_ada, b_ada):
    R, D = c_all.shape
    nl, _, N = w_ada.shape
    tn = _pick(N, (1024, 512, 256, 128))
    return pl.pallas_call(
        _ada_kernel,
        grid=(nl, N // tn),
        in_specs=[pl.BlockSpec((R, D), lambda l, j: (0, 0)),
                  pl.BlockSpec((1, D, tn), lambda l, j: (l, 0, j)),
                  pl.BlockSpec((1, 1, tn), lambda l, j: (l, 0, j))],
        out_specs=pl.BlockSpec((1, R, tn), lambda l, j: (l, 0, j)),
        out_shape=jax.ShapeDtypeStruct((nl, R, N), F32),
        compiler_params=_cparams(("arbitrary", "arbitrary")),
        name="ada",
    )(c_all, w_ada, b_ada.reshape(nl, 1, N))


def _modulate_kernel(x_ref, g_ref, sh_ref, sc_ref, o_ref):
    x = x_ref[0]
    ms = jnp.mean(x * x, axis=-1, keepdims=True)
    y = x * lax.rsqrt(ms + EPS) * g_ref[...]
    o_ref[0] = (y * (1.0 + sc_ref[0]) + sh_ref[0]).astype(o_ref.dtype)


def modulate(x, g, shift, scale):
    B, L, D = x.shape
    tl = _pick(L, (512, 256, 128))
    return pl.pallas_call(
        _modulate_kernel,
        grid=(B, L // tl),
        in_specs=[pl.BlockSpec((1, tl, D), lambda b, i: (b, i, 0)),
                  pl.BlockSpec((1, D), lambda b, i: (0, 0)),
                  pl.BlockSpec((1, 1, D), lambda b, i: (b, 0, 0)),
                  pl.BlockSpec((1, 1, D), lambda b, i: (b, 0, 0))],
        out_specs=pl.BlockSpec((1, tl, D), lambda b, i: (b, i, 0)),
        out_shape=jax.ShapeDtypeStruct((B, L, D), BF16),
        compiler_params=_cparams(("arbitrary", "arbitrary")),
        name="modulate",
    )(x, g.reshape(1, D), shift, scale)


def _mm_kernel(x_ref, w_ref, o_ref):
    o_ref[0] = jnp.dot(x_ref[0], w_ref[...], preferred_element_type=F32).astype(o_ref.dtype)


def _mm_res_kernel(x_ref, w_ref, r_ref, gt_ref, o_ref):
    acc = jnp.dot(x_ref[0], w_ref[...], preferred_element_type=F32)
    o_ref[0] = r_ref[0] + gt_ref[0] * acc


def matmul(x, w, res=None, gate=None, out_dtype=F32):
    B, L, K = x.shape
    N = w.shape[1]
    tm = _pick(L, (512, 256, 128))
    tn = _pick(N, (1024, 512, 256, 128))
    grid = (N // tn, B, L // tm)
    x_spec = pl.BlockSpec((1, tm, K), lambda j, b, i: (b, i, 0))
    w_spec = pl.BlockSpec((K, tn), lambda j, b, i: (0, j))
    o_spec = pl.BlockSpec((1, tm, tn), lambda j, b, i: (b, i, j))
    if res is None:
        return pl.pallas_call(
            _mm_kernel, grid=grid, in_specs=[x_spec, w_spec], out_specs=o_spec,
            out_shape=jax.ShapeDtypeStruct((B, L, N), out_dtype),
            compiler_params=_cparams(("arbitrary",) * 3), name="matmul",
        )(x, w)
    if gate.shape[1] == 1:
        g_spec = pl.BlockSpec((1, 1, tn), lambda j, b, i: (b, 0, j))
    else:
        g_spec = pl.BlockSpec((1, tm, tn), lambda j, b, i: (b, i, j))
    return pl.pallas_call(
        _mm_res_kernel, grid=grid, in_specs=[x_spec, w_spec, o_spec, g_spec], out_specs=o_spec,
        out_shape=jax.ShapeDtypeStruct((B, L, N), F32),
        compiler_params=_cparams(("arbitrary",) * 3), name="matmul_res",
    )(x, w, res, gate)


def _mlstm_kernel(q_ref, k_ref, v_ref, o_ref, gc_ref, gr_ref, bc_ref, br_ref,
                  C0_ref, n0_ref, m0_ref, gout_ref,
                  hs_ref, C_ref, n_ref, m_ref, C_s, n_s, m_s, *, dk):
    ci = pl.program_id(2)

    @pl.when(ci == 0)
    def _():
        C_s[...] = C0_ref[0, 0]
        n_s[...] = n0_ref[0, 0]
        m_s[...] = m0_ref[0, 0]

    q = q_ref[0]
    k = k_ref[0] * (dk ** -0.5)
    v = v_ref[0]
    c = q.shape[0]
    gc = gc_ref[0, 0] + bc_ref[0]
    gr = gr_ref[0, 0] + br_ref[0]
    li_c, lf_c = gc[:, 0:1], _log_sigmoid(gc[:, 1:2])
    li_r, lf_r = gr[0:1, :], _log_sigmoid(gr[1:2, :])
    row = lax.broadcasted_iota(jnp.int32, (c, c), 0)
    col = lax.broadcasted_iota(jnp.int32, (c, c), 1)
    causal = col <= row
    b_c = jnp.sum(jnp.where(causal, lf_r, 0.0), axis=1, keepdims=True)
    b_r = jnp.sum(jnp.where(row <= col, lf_c, 0.0), axis=0, keepdims=True)
    m_prev = m_s[...]
    Dm = jnp.where(causal, b_c - b_r + li_r, NEG)
    m_t = jnp.maximum(b_c + m_prev, jnp.max(Dm, axis=1, keepdims=True))
    S = _bdot_nt(q, k) * jnp.exp(Dm - m_t)
    inter = jnp.exp(b_c + m_prev - m_t)
    C = C_s[...]
    n = n_s[...]
    num = _bdot(S, v) + inter * _bdot(q, C)
    den = jnp.sum(S, axis=1, keepdims=True) + inter * jnp.sum(q * n, axis=1, keepdims=True)
    h = num / jnp.maximum(jnp.abs(den), jnp.exp(-m_t))
    hh = jax.nn.sigmoid(o_ref[0]) * h
    ms = jnp.mean(hh * hh, axis=-1, keepdims=True)
    hs_ref[0] = (hh * lax.rsqrt(ms + EPS) * gout_ref[...]).astype(hs_ref.dtype)
    m_new = m_t[c - 1:c, :]
    b_last = b_c[c - 1:c, :]
    w_r = jnp.exp(b_last - b_r + li_r - m_new)
    w_c = jnp.exp(b_last - b_c + li_c - m_new)
    decay = jnp.exp(b_last + m_prev - m_new)
    C_new = decay * C + _bdot(k.T, w_c * v)
    n_new = decay * n + jnp.dot(w_r, k, preferred_element_type=F32, precision=HIGHEST)
    C_s[...] = C_new
    n_s[...] = n_new
    m_s[...] = m_new

    @pl.when(ci == pl.num_programs(2) - 1)
    def _():
        C_ref[0, 0] = C_new
        n_ref[0, 0] = n_new
        m_ref[0, 0] = m_new


def mlstm_core(z, g, b_gates, C0, n0, m0, g_out, chunk):
    B, L, _ = z.shape
    H = NH_A
    dk = C0.shape[2]
    dv = C0.shape[3]
    nc = L // chunk
    g4 = g.reshape(B, L, 2, H)
    gcol = jnp.transpose(g4, (0, 3, 1, 2))
    grow = jnp.transpose(g4, (0, 3, 2, 1))
    bg = b_gates.reshape(2, H)
    bcol = jnp.transpose(bg, (1, 0)).reshape(H, 1, 2)
    brow = jnp.transpose(bg, (1, 0)).reshape(H, 2, 1)
    kq = (H * dk) // dk
    vo = (2 * H * dk) // dv
    oo = vo + H
    outs = pl.pallas_call(
        functools.partial(_mlstm_kernel, dk=dk),
        grid=(B, H, nc),
        in_specs=[
            pl.BlockSpec((1, chunk, dk), lambda b, h, c: (b, c, h)),
            pl.BlockSpec((1, chunk, dk), lambda b, h, c: (b, c, kq + h)),
            pl.BlockSpec((1, chunk, dv), lambda b, h, c: (b, c, vo + h)),
            pl.BlockSpec((1, chunk, dv), lambda b, h, c: (b, c, oo + h)),
            pl.BlockSpec((1, 1, chunk, 2), lambda b, h, c: (b, h, c, 0)),
            pl.BlockSpec((1, 1, 2, chunk), lambda b, h, c: (b, h, 0, c)),
            pl.BlockSpec((1, 1, 2), lambda b, h, c: (h, 0, 0)),
            pl.BlockSpec((1, 2, 1), lambda b, h, c: (h, 0, 0)),
            pl.BlockSpec((1, 1, dk, dv), lambda b, h, c: (b, h, 0, 0)),
            pl.BlockSpec((1, 1, 1, dk), lambda b, h, c: (b, h, 0, 0)),
            pl.BlockSpec((1, 1, 1, 1), lambda b, h, c: (b, h, 0, 0)),
            pl.BlockSpec((1, dv), lambda b, h, c: (0, h)),
        ],
        out_specs=[
            pl.BlockSpec((1, chunk, dv), lambda b, h, c: (b, c, h)),
            pl.BlockSpec((1, 1, dk, dv), lambda b, h, c: (b, h, 0, 0)),
            pl.BlockSpec((1, 1, 1, dk), lambda b, h, c: (b, h, 0, 0)),
            pl.BlockSpec((1, 1, 1, 1), lambda b, h, c: (b, h, 0, 0)),
        ],
        out_shape=[
            jax.ShapeDtypeStruct((B, L, H * dv), BF16),
            jax.ShapeDtypeStruct((B, H, dk, dv), F32),
            jax.ShapeDtypeStruct((B, H, 1, dk), F32),
            jax.ShapeDtypeStruct((B, H, 1, 1), F32),
        ],
        scratch_shapes=[pltpu.VMEM((dk, dv), F32), pltpu.VMEM((1, dk), F32), pltpu.VMEM((1, 1), F32)],
        compiler_params=_cparams(("arbitrary",) * 3),
        name="mlstm",
    )(z, z, z, z, gcol, grow, bcol, brow, C0, n0.reshape(B, H, 1, dk), m0.reshape(B, H, 1, 1),
      g_out.reshape(1, H * dv))
    hs, C, n, m = outs
    return hs, C, n.reshape(B, H, dk), m.reshape(B, H)


def _headnorm_kernel(q_ref, k_ref, gq_ref, gk_ref, qo_ref, ko_ref, *, dh):
    nh = q_ref.shape[2] // dh
    for h in range(nh):
        sl = slice(h * dh, (h + 1) * dh)
        for src, g, dst in ((q_ref, gq_ref, qo_ref), (k_ref, gk_ref, ko_ref)):
            x = src[0, :, sl]
            ms = jnp.mean(x * x, axis=-1, keepdims=True)
            dst[0, :, sl] = (x * lax.rsqrt(ms + EPS) * g[:, sl]).astype(dst.dtype)


def headnorm(z, gq_full, gk_full, dh):
    B, L, W3 = z.shape
    W = W3 // 3
    tl = _pick(L, (256, 128))
    return pl.pallas_call(
        functools.partial(_headnorm_kernel, dh=dh),
        grid=(B, L // tl),
        in_specs=[pl.BlockSpec((1, tl, W), lambda b, i: (b, i, 0)),
                  pl.BlockSpec((1, tl, W), lambda b, i: (b, i, 1)),
                  pl.BlockSpec((1, W), lambda b, i: (0, 0)),
                  pl.BlockSpec((1, W), lambda b, i: (0, 0))],
        out_specs=[pl.BlockSpec((1, tl, W), lambda b, i: (b, i, 0)),
                   pl.BlockSpec((1, tl, W), lambda b, i: (b, i, 0))],
        out_shape=[jax.ShapeDtypeStruct((B, L, W), BF16), jax.ShapeDtypeStruct((B, L, W), F32)],
        compiler_params=_cparams(("arbitrary", "arbitrary")),
        name="headnorm",
    )(z, z, gq_full.reshape(1, W), gk_full.reshape(1, W))


def _rel_bucket(dist):
    exact = REL_BUCKETS // 2
    d_f = jnp.maximum(dist, 1).astype(F32)
    large = exact + (jnp.log(d_f / exact) / math.log(REL_MAX_DIST / exact) * (REL_BUCKETS - exact)).astype(jnp.int32)
    large = jnp.minimum(large, REL_BUCKETS - 1)
    return jnp.where(dist < exact, dist, large)


def _bias_of_dist(rel_bias, dist):
    bucket = _rel_bucket(jnp.maximum(dist, 0))
    out = jnp.zeros((rel_bias.shape[1],) + dist.shape, F32)
    for b in range(REL_BUCKETS):
        out = jnp.where(bucket[None] == b, rel_bias[b].reshape((-1,) + (1,) * dist.ndim), out)
    return out


def _dilated_multiplicity(dist):
    cnt = jnp.zeros(dist.shape, F32)
    for w, d in DIL_PAIRS:
        cnt = cnt + ((dist % d == 0) & (dist <= w)).astype(F32)
    return jnp.where(dist >= 0, cnt, 0.0)


def _flash_kernel(q_ref, k_ref, v_ref, bias_ref, cm_ref, o_ref, m_s, l_s, acc_s, *, scale, causal_blocks):
    qi = pl.program_id(2)
    ki = pl.program_id(3)

    @pl.when(ki == 0)
    def _():
        m_s[...] = jnp.full(m_s.shape, NEG, F32)
        l_s[...] = jnp.zeros(l_s.shape, F32)
        acc_s[...] = jnp.zeros(acc_s.shape, F32)

    def step():
        q = q_ref[...].astype(BF16)
        k = k_ref[...].astype(BF16)
        v = v_ref[...].astype(BF16)
        s = jnp.einsum('bqd,bkd->bqk', q, k, preferred_element_type=F32) * scale + bias_ref[0]
        cm = cm_ref[...]
        s = jnp.where(cm > 0.0, s, NEG)
        m_old = m_s[...]
        m_new = jnp.maximum(m_old, jnp.max(s, axis=-1, keepdims=True))
        alpha = jnp.exp(m_old - m_new)
        p = cm * jnp.exp(s - m_new)
        l_s[...] = alpha * l_s[...] + jnp.sum(p, axis=-1, keepdims=True)
        acc_s[...] = alpha * acc_s[...] + jnp.einsum('bqk,bkd->bqd', p.astype(BF16), v,
                                                     preferred_element_type=F32)
        m_s[...] = m_new

    if causal_blocks:
        pl.when(ki <= qi)(step)
    else:
        step()

    @pl.when(ki == pl.num_programs(3) - 1)
    def _():
        o_ref[...] = (acc_s[...] / l_s[...]).astype(o_ref.dtype)


def flash_attention(q, k, v, bias_tab, cm_tab, *, n_units, dqk, dv, qcol, kcol, vcol, bmap,
                    tq, tk, batch_in_block, causal_blocks, scale, out_dtype):
    B, Lq = q.shape[:2]
    Lk = k.shape[1]
    nq, nk = Lq // tq, Lk // tk
    Bb = B if batch_in_block else 1
    nb = B // Bb
    if causal_blocks:
        off = lambda qi, ki: jnp.maximum(qi - ki, 0)
        kblk = lambda qi, ki: jnp.minimum(ki, qi)
    else:
        off = lambda qi, ki: 0
        kblk = lambda qi, ki: ki
    return pl.pallas_call(
        functools.partial(_flash_kernel, scale=scale, causal_blocks=causal_blocks),
        grid=(nb, n_units, nq, nk),
        in_specs=[
            pl.BlockSpec((Bb, tq, dqk), lambda b, a, qi, ki: (b, qi, qcol(a))),
            pl.BlockSpec((Bb, tk, dqk), lambda b, a, qi, ki: (b, kblk(qi, ki), kcol(a))),
            pl.BlockSpec((Bb, tk, dv), lambda b, a, qi, ki: (b, kblk(qi, ki), vcol(a))),
            pl.BlockSpec((1, 1, tq, tk), lambda b, a, qi, ki: (bmap(a), off(qi, ki), 0, 0)),
            pl.BlockSpec((1, tq, tk), lambda b, a, qi, ki: (off(qi, ki), 0, 0)),
        ],
        out_specs=pl.BlockSpec((Bb, tq, dv), lambda b, a, qi, ki: (b, qi, a)),
        out_shape=jax.ShapeDtypeStruct((B, Lq, n_units * dv), out_dtype),
        scratch_shapes=[pltpu.VMEM((Bb, tq, 1), F32), pltpu.VMEM((Bb, tq, 1), F32),
                        pltpu.VMEM((Bb, tq, dv), F32)],
        compiler_params=_cparams(("arbitrary",) * 4),
        name="flash_attention",
    )(q, k, v, bias_tab, cm_tab)


def _toeplitz_dist(n_off, tq, tk):
    o = jnp.arange(n_off, dtype=jnp.int32)[:, None, None]
    i = jnp.arange(tq, dtype=jnp.int32)[None, :, None]
    j = jnp.arange(tk, dtype=jnp.int32)[None, None, :]
    return o * tk + i - j


def _diff_lambda(lam_ref, lam_init):
    lq1, lk1, lq2, lk2 = lam_ref[0:1, :], lam_ref[1:2, :], lam_ref[2:3, :], lam_ref[3:4, :]
    return (jnp.exp(jnp.sum(lq1 * lk1, axis=-1, keepdims=True))
            - jnp.exp(jnp.sum(lq2 * lk2, axis=-1, keepdims=True)) + lam_init)


def _diff_combine_kernel(o_ref, lam_ref, g_ref, out_ref, *, dv, lam_init):
    lam = _diff_lambda(lam_ref, lam_init)
    nh = out_ref.shape[2] // dv
    for h in range(nh):
        o0 = o_ref[0, :, (2 * h) * dv:(2 * h + 1) * dv]
        o1 = o_ref[0, :, (2 * h + 1) * dv:(2 * h + 2) * dv]
        d = o0 - lam * o1
        ms = jnp.mean(d * d, axis=-1, keepdims=True)
        out_ref[0, :, h * dv:(h + 1) * dv] = (d * lax.rsqrt(ms + EPS) * g_ref[...] * (1.0 - lam_init)
                                              ).astype(out_ref.dtype)


def diff_combine(o, lam4, g_out, lam_init):
    B, L, W2 = o.shape
    dv = g_out.shape[0]
    W = W2 // 2
    tl = _pick(L, (256, 128))
    return pl.pallas_call(
        functools.partial(_diff_combine_kernel, dv=dv, lam_init=lam_init),
        grid=(B, L // tl),
        in_specs=[pl.BlockSpec((1, tl, W2), lambda b, i: (b, i, 0)),
                  pl.BlockSpec(lam4.shape, lambda b, i: (0, 0)),
                  pl.BlockSpec((1, dv), lambda b, i: (0, 0))],
        out_specs=pl.BlockSpec((1, tl, W), lambda b, i: (b, i, 0)),
        out_shape=jax.ShapeDtypeStruct((B, L, W), BF16),
        compiler_params=_cparams(("arbitrary", "arbitrary")),
        name="diff_combine",
    )(o, lam4, g_out.reshape(1, dv))


def _diff_paged_kernel(pt_ref, tile_ref, q_ref, k_ref, v_ref, bm_ref, kn_ref, vn_ref, bmn_ref,
                       lam_ref, g_ref, out_ref, m_s, l_s, acc_s, *, scale, lam_init, n_heads, n_new):
    p = pl.program_id(1)

    @pl.when(p == 0)
    def _():
        m_s[...] = jnp.full(m_s.shape, NEG, F32)
        l_s[...] = jnp.zeros(l_s.shape, F32)
        acc_s[...] = jnp.zeros(acc_s.shape, F32)

    q = q_ref[0]

    def absorb(k2, v2, bm):
        s = _bdot_nt(q, k2) * scale + bm
        m_old = m_s[...]
        m_new = jnp.maximum(m_old, jnp.max(s, axis=-1, keepdims=True))
        alpha = jnp.exp(m_old - m_new)
        pr = jnp.exp(s - m_new)
        l_s[...] = alpha * l_s[...] + jnp.sum(pr, axis=-1, keepdims=True)
        acc_s[...] = alpha * acc_s[...] + _bdot(pr, v2)
        m_s[...] = m_new

    kp = k_ref[0]
    vp = v_ref[0]
    rows = kp.shape[0] * kp.shape[1]
    absorb(kp.reshape(rows, kp.shape[2]), vp.reshape(rows, vp.shape[2]), bm_ref[0])

    @pl.when(p == pl.num_programs(1) - 1)
    def _():
        absorb(kn_ref[0], vn_ref[0], bmn_ref[...])
        o = acc_s[...] / l_s[...]
        half = n_heads * n_new
        lam = _diff_lambda(lam_ref, lam_init)
        d = o[0:half] - lam * o[half:2 * half]
        ms = jnp.mean(d * d, axis=-1, keepdims=True)
        y = d * lax.rsqrt(ms + EPS) * g_ref[...] * (1.0 - lam_init)
        dv = y.shape[1]
        for h in range(n_heads):
            out_ref[0, :, h * dv:(h + 1) * dv] = y[h * n_new:(h + 1) * n_new].astype(out_ref.dtype)


def _np_rel_bucket(dist):
    exact = REL_BUCKETS // 2
    d_f = np.maximum(dist, 1).astype(np.float32)
    large = exact + (np.log(d_f / np.float32(exact)) / np.float32(math.log(REL_MAX_DIST / exact))
                     * np.float32(REL_BUCKETS - exact)).astype(np.int32)
    large = np.minimum(large, REL_BUCKETS - 1)
    return np.where(dist < exact, dist, large)


def diff_paged(qn, k_new, v_new, cache_k, cache_v, page_table, rel_bias, lam4, g_out, lam_init):
    B, T, H, _, dh = qn.shape
    dv = v_new.shape[-1]
    n_pages = page_table.shape[1]
    page = cache_k.shape[1]
    P = n_pages * page
    R = 2 * H * T
    qt = jnp.transpose(qn, (0, 3, 2, 1, 4))
    zeros = jnp.zeros_like(qt[:, 0])
    qm = jnp.concatenate([jnp.concatenate([qt[:, 0], zeros], axis=-1),
                          jnp.concatenate([zeros, qt[:, 1]], axis=-1)], axis=1)
    qm = qm.reshape(B, R, 2 * dh)
    pg = np.arange(n_pages)[:, None, None]
    tt = np.arange(T)[None, :, None]
    ii = np.arange(page)[None, None, :]
    dist_np = P + tt - pg * page - ii
    bucket_np = _np_rel_bucket(dist_np)
    far = np.all(bucket_np == bucket_np[0:1], axis=(1, 2))
    near_pages = [int(x) for x in np.nonzero(~far)[0]]
    tile_pages = [0] + near_pages
    tile_of_page = np.zeros((n_pages,), np.int32)
    for ti, pgi in enumerate(near_pages):
        tile_of_page[pgi] = ti + 1
    dist_tiles = jnp.asarray(dist_np[tile_pages], jnp.int32)
    bias_t = _bias_of_dist(rel_bias, dist_tiles)
    nt = len(tile_pages)
    bias_t = bias_t.reshape(2, H, nt, T, page)
    bias_rows = jnp.transpose(bias_t, (2, 0, 1, 3, 4)).reshape(nt, R, page)
    row_head = (np.arange(R) // T) % H
    head_ok = jnp.asarray(row_head[:, None] == np.arange(H)[None, :])
    bm = jnp.where(head_ok[None, :, None, :], bias_rows[:, :, :, None], NEG).reshape(nt, R, page * H)
    dn = np.arange(T)[:, None] - np.arange(T)[None, :]
    bias_n = _bias_of_dist(rel_bias, jnp.asarray(np.maximum(dn, 0), jnp.int32))
    bias_n = jnp.broadcast_to(bias_n.reshape(2, H, T, 1, T), (2, H, T, 1, T)).reshape(R, T)
    ok_n = jnp.asarray((dn >= 0)[np.arange(R) % T][:, :, None] & (row_head[:, None, None] == np.arange(H)[None, None, :]))
    bmn = jnp.where(ok_n, bias_n[:, :, None], NEG).reshape(R, T * H)
    kn2 = k_new.reshape(B, T * H, 2 * dh)
    vn2 = v_new.reshape(B, T * H, dv)

    grid_spec = pltpu.PrefetchScalarGridSpec(
        num_scalar_prefetch=2,
        grid=(B, n_pages),
        in_specs=[
            pl.BlockSpec((1, R, 2 * dh), lambda b, p, pt, tl: (b, 0, 0)),
            pl.BlockSpec((1, page, H, 2 * dh), lambda b, p, pt, tl: (pt[b * n_pages + p], 0, 0, 0)),
            pl.BlockSpec((1, page, H, dv), lambda b, p, pt, tl: (pt[b * n_pages + p], 0, 0, 0)),
            pl.BlockSpec((1, R, page * H), lambda b, p, pt, tl: (tl[p], 0, 0)),
            pl.BlockSpec((1, T * H, 2 * dh), lambda b, p, pt, tl: (b, 0, 0)),
            pl.BlockSpec((1, T * H, dv), lambda b, p, pt, tl: (b, 0, 0)),
            pl.BlockSpec((R, T * H), lambda b, p, pt, tl: (0, 0)),
            pl.BlockSpec(lam4.shape, lambda b, p, pt, tl: (0, 0)),
            pl.BlockSpec((1, dv), lambda b, p, pt, tl: (0, 0)),
        ],
        out_specs=pl.BlockSpec((1, T, H * dv), lambda b, p, pt, tl: (b, 0, 0)),
        scratch_shapes=[pltpu.VMEM((R, 1), F32), pltpu.VMEM((R, 1), F32), pltpu.VMEM((R, dv), F32)],
    )
    return pl.pallas_call(
        functools.partial(_diff_paged_kernel, scale=dh ** -0.5, lam_init=lam_init, n_heads=H, n_new=T),
        grid_spec=grid_spec,
        out_shape=jax.ShapeDtypeStruct((B, T, H * dv), BF16),
        compiler_params=_cparams(("arbitrary", "arbitrary")),
        name="diff_paged",
    )(page_table.reshape(-1), jnp.asarray(tile_of_page), qm, cache_k, cache_v, bm, kn2, vn2, bmn,
      lam4, g_out.reshape(1, dv))


def _rglru_kernel(y_ref, x_ref, cs_ref, h0_ref, cw_ref, cb_ref, wa_ref, ba_ref, wx_ref, bx_ref, lam_ref,
                  out_ref, conv_ref, hl_ref, xpad, hcar, *, l_valid):
    i = pl.program_id(1)
    T, D = x_ref.shape[1], x_ref.shape[2]

    @pl.when(i == 0)
    def _():
        xpad[0:8, :] = cs_ref[0]
        hcar[...] = h0_ref[0]

    xpad[8:8 + T, :] = x_ref[0]
    conv = cb_ref[...] + cw_ref[CONV_W - 1:CONV_W, :] * xpad[8:8 + T, :]
    for j in range(CONV_W - 1):
        s = CONV_W - 1 - j
        conv = conv + cw_ref[j:j + 1, :] * xpad[8 - s:8 - s + T, :]
    nb = wa_ref.shape[0]
    bs = D // nb
    r_parts, i_parts = [], []
    for n in range(nb):
        xb = conv[:, n * bs:(n + 1) * bs]
        r_parts.append(_bdot(xb, wa_ref[n]))
        i_parts.append(_bdot(xb, wx_ref[n]))
    r = jax.nn.sigmoid(jnp.concatenate(r_parts, axis=-1) + ba_ref[...])
    ig = jax.nn.sigmoid(jnp.concatenate(i_parts, axis=-1) + bx_ref[...])
    log_a = -LRU_C * r * _softplus(-lam_ref[...])
    a = jnp.exp(log_a)
    bb = jnp.sqrt(-jnp.tanh(log_a) * (1.0 + a * a)) * ig * conv
    rowid = lax.broadcasted_iota(jnp.int32, (T, D), 0)
    s = 1
    while s < T:
        keep = rowid >= s
        a_sh = pltpu.roll(a, s, 0)
        b_sh = pltpu.roll(bb, s, 0)
        bb = jnp.where(keep, a * b_sh + bb, bb)
        a = jnp.where(keep, a * a_sh, a)
        s *= 2
    hs = a * hcar[...] + bb
    out_ref[0] = (hs * _gelu(y_ref[0])).astype(out_ref.dtype)
    hcar[...] = hs[T - 1:T, :]
    tail = xpad[T:T + 8, :]
    xpad[0:8, :] = tail

    @pl.when(i == pl.num_programs(1) - 1)
    def _():
        hl_ref[0] = hs[l_valid - 1:l_valid, :]
        conv_ref[0] = xpad[8 + l_valid - (CONV_W - 1):8 + l_valid, :]


def rglru_core(z, conv_state, h0, conv_w, conv_b, w_a, b_a, w_x, b_x, lam, l_valid_last, tblk):
    B, L, D2 = z.shape
    D = D2 // 2
    nb = w_a.shape[0]
    cs8 = jnp.concatenate([jnp.zeros((B, 8 - (CONV_W - 1), D), F32), conv_state.astype(F32)], axis=1)
    vec = lambda a: a.reshape(1, D)
    out, conv_new, h_last = pl.pallas_call(
        functools.partial(_rglru_kernel, l_valid=l_valid_last),
        grid=(B, L // tblk),
        in_specs=[
            pl.BlockSpec((1, tblk, D), lambda b, i: (b, i, 0)),
            pl.BlockSpec((1, tblk, D), lambda b, i: (b, i, 1)),
            pl.BlockSpec((1, 8, D), lambda b, i: (b, 0, 0)),
            pl.BlockSpec((1, 1, D), lambda b, i: (b, 0, 0)),
            pl.BlockSpec((CONV_W, D), lambda b, i: (0, 0)),
            pl.BlockSpec((1, D), lambda b, i: (0, 0)),
            pl.BlockSpec(w_a.shape, lambda b, i: (0, 0, 0)),
            pl.BlockSpec((1, D), lambda b, i: (0, 0)),
            pl.BlockSpec(w_x.shape, lambda b, i: (0, 0, 0)),
            pl.BlockSpec((1, D), lambda b, i: (0, 0)),
            pl.BlockSpec((1, D), lambda b, i: (0, 0)),
        ],
        out_specs=[
            pl.BlockSpec((1, tblk, D), lambda b, i: (b, i, 0)),
            pl.BlockSpec((1, CONV_W - 1, D), lambda b, i: (b, 0, 0)),
            pl.BlockSpec((1, 1, D), lambda b, i: (b, 0, 0)),
        ],
        out_shape=[
            jax.ShapeDtypeStruct((B, L, D), BF16),
            jax.ShapeDtypeStruct((B, CONV_W - 1, D), F32),
            jax.ShapeDtypeStruct((B, 1, D), F32),
        ],
        scratch_shapes=[pltpu.VMEM((tblk + 8, D), F32), pltpu.VMEM((1, D), F32)],
        compiler_params=_cparams(("arbitrary", "arbitrary")),
        name="rglru",
    )(z, z, cs8, h0.astype(F32).reshape(B, 1, D), conv_w, vec(conv_b), w_a.astype(BF16), vec(b_a),
      w_x.astype(BF16), vec(b_x), vec(lam))
    return out, conv_new, h_last.reshape(B, D)


def _peer_kernel(x_ref, ut_ref, v_ref, g_ref, r_ref, gt_ref, o_ref, acc_s):
    e = pl.program_id(1)

    @pl.when(e == 0)
    def _():
        acc_s[...] = jnp.zeros(acc_s.shape, F32)

    s = jnp.dot(x_ref[...], ut_ref[...], preferred_element_type=F32)
    p = (_gelu(s) * g_ref[...].astype(F32)).astype(BF16)
    acc_s[...] += jnp.dot(p, v_ref[...], preferred_element_type=F32)

    @pl.when(e == pl.num_programs(1) - 1)
    def _():
        o_ref[...] = r_ref[...] + gt_ref[...] * acc_s[...]


def peer_dense(h, ut, v, G, res, gate, rows_per_gate):
    N, D = h.shape
    E = ut.shape[1]
    tb = _pick(N, (512, 256, 128))
    eb = _pick(E, (512, 256, 128))
    if rows_per_gate == 1:
        g_spec = pl.BlockSpec((tb, D), lambda i, e: (i, 0))
        gate_arr = gate
    else:
        assert rows_per_gate % tb == 0
        per = rows_per_gate // tb
        g_spec = pl.BlockSpec((1, 1, D), lambda i, e: (i // per, 0, 0))
        gate_arr = gate.reshape(gate.shape[0], 1, D)
    kern = _peer_kernel
    if rows_per_gate != 1:
        def kern(x_ref, ut_ref, v_ref, g_ref, r_ref, gt_ref, o_ref, acc_s):
            _peer_kernel(x_ref, ut_ref, v_ref, g_ref, r_ref, gt_ref.at[0], o_ref, acc_s)
    return pl.pallas_call(
        kern,
        grid=(N // tb, E // eb),
        in_specs=[
            pl.BlockSpec((tb, D), lambda i, e: (i, 0)),
            pl.BlockSpec((D, eb), lambda i, e: (0, e)),
            pl.BlockSpec((eb, D), lambda i, e: (e, 0)),
            pl.BlockSpec((tb, eb), lambda i, e: (i, e)),
            pl.BlockSpec((tb, D), lambda i, e: (i, 0)),
            g_spec,
        ],
        out_specs=pl.BlockSpec((tb, D), lambda i, e: (i, 0)),
        out_shape=jax.ShapeDtypeStruct((N, D), F32),
        scratch_shapes=[pltpu.VMEM((tb, D), F32)],
        compiler_params=_cparams(("arbitrary", "arbitrary")),
        name="peer_dense",
    )(h, ut, v, G, res, gate_arr)


def peer_route(q, keys):
    N = q.shape[0]
    nh, _, nk, dk2 = keys.shape
    qh = q.reshape(N, nh, 2, dk2)
    s = jnp.einsum('tnpd,npkd->tnpk', qh, keys.astype(F32), precision=HIGHEST)
    sv, si = lax.top_k(s, TOPK_P)
    cand = sv[:, :, 0, :, None] + sv[:, :, 1, None, :]
    sc, pos = lax.top_k(cand.reshape(N, nh, TOPK_P * TOPK_P), TOPK_P)
    i1 = jnp.take_along_axis(si[:, :, 0, :], pos // TOPK_P, axis=-1)
    i2 = jnp.take_along_axis(si[:, :, 1, :], pos % TOPK_P, axis=-1)
    g = jax.nn.softmax(sc, axis=-1)
    ks = nh * TOPK_P
    A = jnp.where(i1.reshape(N, ks, 1) == jnp.arange(nk)[None, None, :], g.reshape(N, ks, 1), 0.0).astype(BF16)
    Bm = (i2.reshape(N, ks, 1) == jnp.arange(nk)[None, None, :]).astype(BF16)
    G3 = jnp.einsum('tki,tkj->tij', A, Bm, preferred_element_type=F32)
    return G3.astype(BF16).reshape(N, nk * nk)


def _pad_rows(a, n, value=0.0):
    pad = [(0, 0)] * a.ndim
    pad[1] = (0, n - a.shape[1])
    return jnp.pad(a, pad, constant_values=value)


def kernel(x_prompt, x_sample, state_mlstm_C, state_mlstm_n, state_mlstm_m, cache_swa_k, cache_swa_v,
           cache_diff_k, cache_diff_v, state_rglru_conv, state_rglru_h, page_table, c_prompt, c_sample,
           w_ada, b_ada, g_norm_mix, g_norm_ffn, rel_bias,
           mlstm_w_in, mlstm_b_gates, mlstm_g_out, mlstm_w_out,
           swa_w_in, swa_g_q, swa_g_k, swa_w_out,
           diff_w_in, diff_g_q, diff_g_k, diff_lam_q1, diff_lam_k1, diff_lam_q2, diff_lam_k2, diff_g_out, diff_w_out,
           rglru_w_in, rglru_conv_w, rglru_conv_b, rglru_w_a, rglru_b_a, rglru_w_x, rglru_b_x, rglru_lambda, rglru_w_out,
           peer_w_q, peer_keys, peer_u, peer_v):
    xp, xs = x_prompt, x_sample
    Bp, S, D = xp.shape
    Bs, T, _ = xs.shape
    depth = w_ada.shape[0]
    Ns = Bs * T

    c_all = jnp.concatenate([c_prompt, c_sample], axis=0)
    mod = ada_all(c_all, w_ada, b_ada)
    mod = mod.reshape(depth, Bp + Bs, 6, 1, D)

    def mods(i, which):
        m = mod[i, :, which]
        return m[:Bp], m[Bp:]

    def out_proj(hp, hs, w, gate_p, gate_s):
        nonlocal xp, xs
        wb = w.astype(BF16)
        xp = matmul(hp, wb, res=xp, gate=gate_p)
        gs_rows = jnp.broadcast_to(gate_s, (Bs, T, D)).reshape(1, Ns, D)
        xs = matmul(hs.reshape(1, Ns, -1), wb, res=xs.reshape(1, Ns, D), gate=gs_rows).reshape(Bs, T, D)

    for i in range(depth):
        kind = i % 4
        sh_p, sh_s = mods(i, 0)
        sc_p, sc_s = mods(i, 1)
        gt_p, gt_s = mods(i, 2)
        hp = modulate(xp, g_norm_mix[i], sh_p, sc_p)
        hs = modulate(xs, g_norm_mix[i], sh_s, sc_s)
        hs_flat = hs.reshape(1, Ns, D)
        if kind == 0:
            H = NH_A
            dk = state_mlstm_C.shape[2]
            dv = state_mlstm_C.shape[3]
            nmain = 2 * H * dk + 2 * H * dv
            w_main = mlstm_w_in[:, :nmain].astype(BF16)
            w_gate = mlstm_w_in[:, nmain:].astype(BF16)
            zp = matmul(hp, w_main)
            gp = matmul(hp, w_gate)
            zs = matmul(hs_flat, w_main).reshape(Bs, T, nmain)
            gs = matmul(hs_flat, w_gate).reshape(Bs, T, 2 * H)
            chunk_p = _pick(S, (CHUNK_A, 128))
            op, mC_p, mn_p, mm_p = mlstm_core(
                zp, gp, mlstm_b_gates, jnp.zeros((Bp, H, dk, dv), F32), jnp.zeros((Bp, H, dk), F32),
                jnp.full((Bp, H), M_INIT, F32), mlstm_g_out, chunk_p)
            Tp = 128
            zs_pad = _pad_rows(zs, Tp)
            gs_pad = jnp.concatenate([
                _pad_rows(gs[..., :H], Tp, NEG), _pad_rows(gs[..., H:], Tp, -NEG)], axis=-1)
            os_, mC_s, mn_s, mm_s = mlstm_core(
                zs_pad, gs_pad, mlstm_b_gates, state_mlstm_C.astype(F32), state_mlstm_n.astype(F32),
                state_mlstm_m.astype(F32), mlstm_g_out, Tp)
            os_ = os_[:, :T]
            w_out = mlstm_w_out
        elif kind == 1:
            H = NH_B
            dh = D // H
            wb = swa_w_in.astype(BF16)
            zp = matmul(hp, wb)
            zs = matmul(hs_flat, wb).reshape(Bs, T, 3 * D)
            gq = jnp.tile(swa_g_q, H)
            gk = jnp.tile(swa_g_k, H)
            qn_p, kn_p = headnorm(zp, gq, gk, dh)
            qn_s, kn_s = headnorm(zs, gq, gk, dh)
            tq = _pick(S, (256, 128))
            n_off = S // tq
            dist = _toeplitz_dist(n_off, tq, tq)
            bias_tab = _bias_of_dist(rel_bias, dist)
            cm_tab = _dilated_multiplicity(dist)
            vblk = 2 * H
            op = flash_attention(qn_p, kn_p, zp, bias_tab, cm_tab, n_units=H, dqk=dh, dv=dh,
                                 qcol=lambda a: a, kcol=lambda a: a, vcol=lambda a: vblk + a, bmap=lambda a: a,
                                 tq=tq, tk=tq, batch_in_block=True, causal_blocks=True, scale=dh ** -0.5,
                                 out_dtype=BF16)
            swa_k_p = kn_p.reshape(Bp, S, H, dh)
            swa_v_p = zp[:, :, 2 * D:].reshape(Bp, S, H, dh)
            Wb = cache_swa_k.shape[1]
            vs_new = zs[:, :, 2 * D:]
            k_all = jnp.concatenate([cache_swa_k.reshape(Bs, Wb, D).astype(F32), kn_s], axis=1)
            v_all = jnp.concatenate([cache_swa_v.reshape(Bs, Wb, D).astype(F32), vs_new], axis=1)
            dist_s = (Wb + jnp.arange(T, dtype=jnp.int32)[:, None]) - jnp.arange(Wb + T, dtype=jnp.int32)[None, :]
            bias_s = _bias_of_dist(rel_bias, dist_s[None])
            cm_s = _dilated_multiplicity(dist_s[None])
            os_ = flash_attention(qn_s, k_all, v_all, bias_s, cm_s, n_units=H, dqk=dh, dv=dh,
                                  qcol=lambda a: a, kcol=lambda a: a, vcol=lambda a: a, bmap=lambda a: a,
                                  tq=T, tk=Wb + T, batch_in_block=False, causal_blocks=False, scale=dh ** -0.5,
                                  out_dtype=BF16)
            swa_k_s = k_all[:, T:].reshape(Bs, Wb, H, dh)
            swa_v_s = v_all[:, T:].reshape(Bs, Wb, H, dh)
            w_out = swa_w_out
        elif kind == 2:
            H = NH_C
            dh = D // (2 * H)
            dv = 2 * dh
            lam_init = 0.8 - 0.6 * math.exp(-0.3 * i)
            lam4 = jnp.stack([diff_lam_q1, diff_lam_k1, diff_lam_q2, diff_lam_k2]).astype(F32)
            wb = diff_w_in.astype(BF16)
            zp = matmul(hp, wb)
            zs = matmul(hs_flat, wb).reshape(Bs, T, 3 * D)
            gq = jnp.tile(diff_g_q.reshape(-1), H)
            gk = jnp.tile(diff_g_k.reshape(-1), H)
            qn_p, kn_p = headnorm(zp, gq, gk, dh)
            qn_s, kn_s = headnorm(zs, gq, gk, dh)
            tq = _pick(S, (256, 128))
            n_off = S // tq
            dist = _toeplitz_dist(n_off, tq, tq)
            bias_tab = _bias_of_dist(rel_bias, dist)
            cm_tab = (dist >= 0).astype(F32)
            vblk = (2 * D) // dv
            o2 = flash_attention(qn_p, kn_p, zp, bias_tab, cm_tab, n_units=2 * H, dqk=dh, dv=dv,
                                 qcol=lambda a: a, kcol=lambda a: a, vcol=lambda a: vblk + a // 2,
                                 bmap=lambda a: (a % 2) * H + a // 2,
                                 tq=tq, tk=tq, batch_in_block=True, causal_blocks=True, scale=dh ** -0.5,
                                 out_dtype=F32)
            op = diff_combine(o2, lam4, diff_g_out, lam_init)
            diff_k_p = kn_p.reshape(Bp, S, H, 2 * dh)
            diff_v_p = zp[:, :, 2 * D:].reshape(Bp, S, H, dv)
            diff_k_s = kn_s.reshape(Bs, T, H, 2 * dh)
            diff_v_s = zs[:, :, 2 * D:].reshape(Bs, T, H, dv)
            os_ = diff_paged(qn_s.reshape(Bs, T, H, 2, dh), diff_k_s, diff_v_s, cache_diff_k, cache_diff_v,
                             page_table, rel_bias, lam4, diff_g_out, lam_init)
            w_out = diff_w_out
        else:
            wb = rglru_w_in.astype(BF16)
            zp = matmul(hp, wb)
            zs = matmul(hs_flat, wb).reshape(Bs, T, -1)
            tblk = _pick(S, (256, 128))
            op, conv_p, h_p = rglru_core(zp, jnp.zeros((Bp, CONV_W - 1, D), F32), jnp.zeros((Bp, D), F32),
                                         rglru_conv_w, rglru_conv_b, rglru_w_a, rglru_b_a, rglru_w_x, rglru_b_x,
                                         rglru_lambda, tblk, tblk)
            os_, conv_s, h_s = rglru_core(_pad_rows(zs, 8), state_rglru_conv, state_rglru_h,
                                          rglru_conv_w, rglru_conv_b, rglru_w_a, rglru_b_a, rglru_w_x, rglru_b_x,
                                          rglru_lambda, T, 8)
            os_ = os_[:, :T]
            w_out = rglru_w_out
        out_proj(op, os_, w_out, gt_p, gt_s)

        sh_p, sh_s = mods(i, 3)
        sc_p, sc_s = mods(i, 4)
        gt_p, gt_s = mods(i, 5)
        hp = modulate(xp, g_norm_ffn[i], sh_p, sc_p)
        hs = modulate(xs, g_norm_ffn[i], sh_s, sc_s)
        wq = peer_w_q[i].astype(BF16)
        ut = jnp.transpose(peer_u[i]).astype(BF16)
        vb = peer_v[i].astype(BF16)
        qp = matmul(hp, wq).reshape(Bp * S, -1)
        qs = matmul(hs.reshape(1, Ns, D), wq).reshape(Ns, -1)
        Gp = peer_route(qp, peer_keys[i])
        Gs = peer_route(qs, peer_keys[i])
        xp = peer_dense(hp.reshape(Bp * S, D), ut, vb, Gp, xp.reshape(Bp * S, D),
                        gt_p.reshape(Bp, D), S).reshape(Bp, S, D)
        gs_rows = jnp.broadcast_to(gt_s, (Bs, T, D)).reshape(Ns, D)
        xs = peer_dense(hs.reshape(Ns, D), ut, vb, Gs, xs.reshape(Ns, D), gs_rows, 1).reshape(Bs, T, D)

    return (xp, xs, mC_p, mC_s, mn_p, mn_s, mm_p, mm_s, swa_k_p, swa_k_s, swa_v_p, swa_v_s,
            diff_k_p, diff_k_s, diff_v_p, diff_v_s, conv_p, conv_s, h_p, h_s)
```

```python
import functools
import math

import numpy as np
import jax
import jax.numpy as jnp
from jax import lax
from jax.experimental import pallas as pl
from jax.experimental.pallas import tpu as pltpu

F32 = jnp.float32
BF16 = jnp.bfloat16
HIGHEST = lax.Precision.HIGHEST

EPS = 1e-6
NEG = -1e30
VMEM_LIMIT = 56 * 1024 * 1024
LANES = 128

NH_A, CHUNK_A, M_INIT = 8, 256, -1e30
NH_B = 16
DIL_PAIRS = ((128, 1), (512, 4), (2048, 16))
NH_C = 8
NBLK_D, CONV_W, LRU_C = 16, 4, 8.0
NH_P, N_KEYS, TOPK_P = 8, 128, 16
REL_BUCKETS, REL_MAX_DIST = 32, 2048
PAGE_SIZE = 128


def _cparams(sem):
    return pltpu.CompilerParams(dimension_semantics=sem, vmem_limit_bytes=VMEM_LIMIT)


def _bdot(a, b):
    return jnp.dot(a.astype(BF16), b.astype(BF16), preferred_element_type=F32)


def _bdot_nt(a, b):
    return lax.dot_general(a.astype(BF16), b.astype(BF16), (((1,), (1,)), ((), ())),
                           preferred_element_type=F32)


def _gelu(x):
    return 0.5 * x * (1.0 + lax.erf(x * (1.0 / math.sqrt(2.0))))


def _log_sigmoid(x):
    return jnp.minimum(x, 0.0) - jnp.log1p(jnp.exp(-jnp.abs(x)))


def _softplus(x):
    return jnp.maximum(x, 0.0) + jnp.log1p(jnp.exp(-jnp.abs(x)))


def _pick(n, pref):
    for t in pref:
        if n % t == 0:
            return t
    return n


def _ada_kernel(c_ref, w_ref, b_ref, o_ref):
    c = c_ref[...]
    a = c * jax.nn.sigmoid(c)
    o_ref[0] = jnp.dot(a, w_ref[0], preferred_element_type=F32, precision=HIGHEST) + b_ref[0]


def ada_all(c_all, w_ada, b_ada):
    R, D = c_all.shape
    nl, _, N = w_ada.shape
    tn = _pick(N, (1024, 512, 256, 128))
    return pl.pallas_call(
        _ada_kernel,
        grid=(nl, N // tn),
        in_specs=[pl.BlockSpec((R, D), lambda l, j: (0, 0)),
                  pl.BlockSpec((1, D, tn), lambda l, j: (l, 0, j)),
                  pl.BlockSpec((1, 1, tn), lambda l, j: (l, 0, j))],
        out_specs=pl.BlockSpec((1, R, tn), lambda l, j: (l, 0, j)),
        out_shape=jax.ShapeDtypeStruct((nl, R, N), F32),
        compiler_params=_cparams(("arbitrary", "arbitrary")),
        name="ada",
    )(c_all, w_ada, b_ada.reshape(nl, 1, N))


def _modulate_kernel(x_ref, g_ref, sh_ref, sc_ref, o_ref):
    x = x_ref[0]
    ms = jnp.mean(x * x, axis=-1, keepdims=True)
    y = x * lax.rsqrt(ms + EPS) * g_ref[...]
    o_ref[0] = (y * (1.0 + sc_ref[0]) + sh_ref[0]).astype(o_ref.dtype)


def modulate(x, g, shift, scale):
    B, L, D = x.shape
    tl = _pick(L, (512, 256, 128))
    return pl.pallas_call(
        _modulate_kernel,
        grid=(B, L // tl),
        in_specs=[pl.BlockSpec((1, tl, D), lambda b, i: (b, i, 0)),
                  pl.BlockSpec((1, D), lambda b, i: (0, 0)),
                  pl.BlockSpec((1, 1, D), lambda b, i: (b, 0, 0)),
                  pl.BlockSpec((1, 1, D), lambda b, i: (b, 0, 0))],
        out_specs=pl.BlockSpec((1, tl, D), lambda b, i: (b, i, 0)),
        out_shape=jax.ShapeDtypeStruct((B, L, D), BF16),
        compiler_params=_cparams(("arbitrary", "arbitrary")),
        name="modulate",
    )(x, g.reshape(1, D), shift, scale)


def _mm_kernel(x_ref, w_ref, o_ref):
    o_ref[0] = jnp.dot(x_ref[0], w_ref[...], preferred_element_type=F32).astype(o_ref.dtype)


def _mm_res_kernel(x_ref, w_ref, r_ref, gt_ref, o_ref):
    acc = jnp.dot(x_ref[0], w_ref[...], preferred_element_type=F32)
    o_ref[0] = r_ref[0] + gt_ref[0] * acc


def matmul(x, w, res=None, gate=None, out_dtype=F32):
    B, L, K = x.shape
    N = w.shape[1]
    tm = _pick(L, (512, 256, 128))
    tn = _pick(N, (1024, 512, 256, 128))
    grid = (N // tn, B, L // tm)
    x_spec = pl.BlockSpec((1, tm, K), lambda j, b, i: (b, i, 0))
    w_spec = pl.BlockSpec((K, tn), lambda j, b, i: (0, j))
    o_spec = pl.BlockSpec((1, tm, tn), lambda j, b, i: (b, i, j))
    if res is None:
        return pl.pallas_call(
            _mm_kernel, grid=grid, in_specs=[x_spec, w_spec], out_specs=o_spec,
            out_shape=jax.ShapeDtypeStruct((B, L, N), out_dtype),
            compiler_params=_cparams(("arbitrary",) * 3), name="matmul",
        )(x, w)
    if gate.shape[1] == 1:
        g_spec = pl.BlockSpec((1, 1, tn), lambda j, b, i: (b, 0, j))
    else:
        g_spec = pl.BlockSpec((1, tm, tn), lambda j, b, i: (b, i, j))
    return pl.pallas_call(
        _mm_res_kernel, grid=grid, in_specs=[x_spec, w_spec, o_spec, g_spec], out_specs=o_spec,
        out_shape=jax.ShapeDtypeStruct((B, L, N), F32),
        compiler_params=_cparams(("arbitrary",) * 3), name="matmul_res",
    )(x, w, res, gate)


def _mlstm_kernel(q_ref, k_ref, v_ref, o_ref, gc_ref, gr_ref, bc_ref, br_ref,
                  C0_ref, n0_ref, m0_ref, gout_ref,
                  hs_ref, C_ref, n_ref, m_ref, C_s, n_s, m_s, *, dk):
    ci = pl.program_id(2)

    @pl.when(ci == 0)
    def _():
        C_s[...] = C0_ref[0, 0]
        n_s[...] = n0_ref[0, 0]
        m_s[...] = m0_ref[0, 0]

    q = q_ref[0]
    k = k_ref[0] * (dk ** -0.5)
    v = v_ref[0]
    c = q.shape[0]
    gc = gc_ref[0, 0] + bc_ref[0]
    gr = gr_ref[0, 0] + br_ref[0]
    li_c, lf_c = gc[:, 0:1], _log_sigmoid(gc[:, 1:2])
    li_r, lf_r = gr[0:1, :], _log_sigmoid(gr[1:2, :])
    row = lax.broadcasted_iota(jnp.int32, (c, c), 0)
    col = lax.broadcasted_iota(jnp.int32, (c, c), 1)
    causal = col <= row
    b_c = jnp.sum(jnp.where(causal, lf_r, 0.0), axis=1, keepdims=True)
    b_r = jnp.sum(jnp.where(row <= col, lf_c, 0.0), axis=0, keepdims=True)
    m_prev = m_s[...]
    Dm = jnp.where(causal, b_c - b_r + li_r, NEG)
    m_t = jnp.maximum(b_c + m_prev, jnp.max(Dm, axis=1, keepdims=True))
    S = _bdot_nt(q, k) * jnp.exp(Dm - m_t)
    inter = jnp.exp(b_c + m_prev - m_t)
    C = C_s[...]
    n = n_s[...]
    num = _bdot(S, v) + inter * _bdot(q, C)
    den = jnp.sum(S, axis=1, keepdims=True) + inter * jnp.sum(q * n, axis=1, keepdims=True)
    h = num / jnp.maximum(jnp.abs(den), jnp.exp(-m_t))
    hh = jax.nn.sigmoid(o_ref[0]) * h
    ms = jnp.mean(hh * hh, axis=-1, keepdims=True)
    hs_ref[0] = (hh * lax.rsqrt(ms + EPS) * gout_ref[...]).astype(hs_ref.dtype)
    m_new = m_t[c - 1:c, :]
    b_last = b_c[c - 1:c, :]
    w_r = jnp.exp(b_last - b_r + li_r - m_new)
    w_c = jnp.exp(b_last - b_c + li_c - m_new)
    decay = jnp.exp(b_last + m_prev - m_new)
    C_new = decay * C + _bdot(k.T, w_c * v)
    n_new = decay * n + jnp.dot(w_r, k, preferred_element_type=F32, precision=HIGHEST)
    C_s[...] = C_new
    n_s[...] = n_new
    m_s[...] = m_new

    @pl.when(ci == pl.num_programs(2) - 1)
    def _():
        C_ref[0, 0] = C_new
        n_ref[0, 0] = n_new
        m_ref[0, 0] = m_new


def mlstm_core(z, g, b_gates, C0, n0, m0, g_out, chunk):
    B, L, _ = z.shape
    H = NH_A
    dk = C0.shape[2]
    dv = C0.shape[3]
    nc = L // chunk
    g4 = g.reshape(B, L, 2, H)
    gcol = jnp.transpose(g4, (0, 3, 1, 2))
    grow = jnp.transpose(g4, (0, 3, 2, 1))
    bg = b_gates.reshape(2, H)
    bcol = jnp.transpose(bg, (1, 0)).reshape(H, 1, 2)
    brow = jnp.transpose(bg, (1, 0)).reshape(H, 2, 1)
    kq = (H * dk) // dk
    vo = (2 * H * dk) // dv
    oo = vo + H
    outs = pl.pallas_call(
        functools.partial(_mlstm_kernel, dk=dk),
        grid=(B, H, nc),
        in_specs=[
            pl.BlockSpec((1, chunk, dk), lambda b, h, c: (b, c, h)),
            pl.BlockSpec((1, chunk, dk), lambda b, h, c: (b, c, kq + h)),
            pl.BlockSpec((1, chunk, dv), lambda b, h, c: (b, c, vo + h)),
            pl.BlockSpec((1, chunk, dv), lambda b, h, c: (b, c, oo + h)),
            pl.BlockSpec((1, 1, chunk, 2), lambda b, h, c: (b, h, c, 0)),
            pl.BlockSpec((1, 1, 2, chunk), lambda b, h, c: (b, h, 0, c)),
            pl.BlockSpec((1, 1, 2), lambda b, h, c: (h, 0, 0)),
            pl.BlockSpec((1, 2, 1), lambda b, h, c: (h, 0, 0)),
            pl.BlockSpec((1, 1, dk, dv), lambda b, h, c: (b, h, 0, 0)),
            pl.BlockSpec((1, 1, 1, dk), lambda b, h, c: (b, h, 0, 0)),
            pl.BlockSpec((1, 1, 1, 1), lambda b, h, c: (b, h, 0, 0)),
            pl.BlockSpec((1, dv), lambda b, h, c: (0, h)),
        ],
        out_specs=[
            pl.BlockSpec((1, chunk, dv), lambda b, h, c: (b, c, h)),
            pl.BlockSpec((1, 1, dk, dv), lambda b, h, c: (b, h, 0, 0)),
            pl.BlockSpec((1, 1, 1, dk), lambda b, h, c: (b, h, 0, 0)),
            pl.BlockSpec((1, 1, 1, 1), lambda b, h, c: (b, h, 0, 0)),
        ],
        out_shape=[
            jax.ShapeDtypeStruct((B, L, H * dv), BF16),
            jax.ShapeDtypeStruct((B, H, dk, dv), F32),
            jax.ShapeDtypeStruct((B, H, 1, dk), F32),
            jax.ShapeDtypeStruct((B, H, 1, 1), F32),
        ],
        scratch_shapes=[pltpu.VMEM((dk, dv), F32), pltpu.VMEM((1, dk), F32), pltpu.VMEM((1, 1), F32)],
        compiler_params=_cparams(("arbitrary",) * 3),
        name="mlstm",
    )(z, z, z, z, gcol, grow, bcol, brow, C0, n0.reshape(B, H, 1, dk), m0.reshape(B, H, 1, 1),
      g_out.reshape(1, H * dv))
    hs, C, n, m = outs
    return hs, C, n.reshape(B, H, dk), m.reshape(B, H)


def _headnorm_kernel(q_ref, k_ref, gq_ref, gk_ref, qo_ref, ko_ref, *, dh, q_scale):
    nh = q_ref.shape[2] // dh
    for h in range(nh):
        sl = slice(h * dh, (h + 1) * dh)
        for src, g, dst, sc in ((q_ref, gq_ref, qo_ref, q_scale), (k_ref, gk_ref, ko_ref, 1.0)):
            x = src[0, :, sl]
            ms = jnp.mean(x * x, axis=-1, keepdims=True)
            y = x * lax.rsqrt(ms + EPS) * g[:, sl]
            dst[0, :, sl] = (y * sc if sc != 1.0 else y).astype(dst.dtype)


def headnorm(z, gq_full, gk_full, dh, q_scale):
    B, L, W3 = z.shape
    W = W3 // 3
    tl = _pick(L, (256, 128))
    return pl.pallas_call(
        functools.partial(_headnorm_kernel, dh=dh, q_scale=q_scale),
        grid=(B, L // tl),
        in_specs=[pl.BlockSpec((1, tl, W), lambda b, i: (b, i, 0)),
                  pl.BlockSpec((1, tl, W), lambda b, i: (b, i, 1)),
                  pl.BlockSpec((1, W), lambda b, i: (0, 0)),
                  pl.BlockSpec((1, W), lambda b, i: (0, 0))],
        out_specs=[pl.BlockSpec((1, tl, W), lambda b, i: (b, i, 0)),
                   pl.BlockSpec((1, tl, W), lambda b, i: (b, i, 0))],
        out_shape=[jax.ShapeDtypeStruct((B, L, W), BF16), jax.ShapeDtypeStruct((B, L, W), F32)],
        compiler_params=_cparams(("arbitrary", "arbitrary")),
        name="headnorm",
    )(z, z, gq_full.reshape(1, W), gk_full.reshape(1, W))


def _rel_bucket(dist):
    exact = REL_BUCKETS // 2
    d_f = jnp.maximum(dist, 1).astype(F32)
    large = exact + (jnp.log(d_f / exact) / math.log(REL_MAX_DIST / exact) * (REL_BUCKETS - exact)).astype(jnp.int32)
    large = jnp.minimum(large, REL_BUCKETS - 1)
    return jnp.where(dist < exact, dist, large)


def _bias_of_dist(rel_bias, dist):
    bucket = _rel_bucket(jnp.maximum(dist, 0))
    out = jnp.zeros((rel_bias.shape[1],) + dist.shape, F32)
    for b in range(REL_BUCKETS):
        out = jnp.where(bucket[None] == b, rel_bias[b].reshape((-1,) + (1,) * dist.ndim), out)
    return out


def _np_multiplicity(dist):
    cnt = np.zeros(dist.shape, np.int64)
    for w, d in DIL_PAIRS:
        cnt = cnt + ((dist % d == 0) & (dist <= w) & (dist >= 0))
    return cnt


def _swa_bias(rel_bias, dist_np):
    cnt = _np_multiplicity(dist_np)
    logc = jnp.log(jnp.asarray(np.maximum(cnt, 1), F32))
    bias = _bias_of_dist(rel_bias, jnp.asarray(np.maximum(dist_np, 0), jnp.int32))
    return jnp.where(jnp.asarray(cnt > 0)[None], bias + logc[None], NEG), cnt > 0


def _causal_bias(rel_bias, dist_np):
    bias = _bias_of_dist(rel_bias, jnp.asarray(np.maximum(dist_np, 0), jnp.int32))
    return jnp.where(jnp.asarray(dist_np >= 0)[None], bias, NEG), dist_np >= 0


def _toeplitz_dist_np(n_off, t):
    o = np.arange(n_off)[:, None, None]
    i = np.arange(t)[None, :, None]
    j = np.arange(t)[None, None, :]
    return o * t + i - j


def _flash_kernel(qi_ref, ki_ref, q_ref, k_ref, v_ref, bm_ref, o_ref, m_s, l_s, acc_s):
    p = pl.program_id(1)
    qi = qi_ref[p]
    ki = ki_ref[p]
    nb, tq, dv = acc_s.shape
    tk = k_ref.shape[1]

    @pl.when(ki == 0)
    def _():
        m_s[...] = jnp.full(m_s.shape, NEG, F32)
        l_s[...] = jnp.zeros(l_s.shape, F32)
        acc_s[...] = jnp.zeros(acc_s.shape, F32)

    bm = bm_ref[0, 0]
    for b in range(nb):
        s = _bdot_nt(q_ref[b], k_ref[b]) + bm
        m_prev = m_s[b]
        m_next = jnp.maximum(m_prev, jnp.max(s, axis=1, keepdims=True))
        pr = jnp.exp(s - jnp.tile(m_next, (1, tk // LANES)))
        alpha = jnp.exp(m_prev - m_next)
        l_s[b] = alpha * l_s[b] + jnp.sum(pr, axis=1, keepdims=True)
        acc_s[b] = acc_s[b] * jnp.tile(alpha, (1, dv // LANES)) + _bdot(pr, v_ref[b])
        m_s[b] = m_next

    @pl.when(ki == qi)
    def _():
        for b in range(nb):
            o_ref[b] = (acc_s[b] / jnp.tile(l_s[b], (1, dv // LANES))).astype(o_ref.dtype)


def flash_attention(q, k, v, bm_tab, *, n_units, dqk, dv, qcol, kcol, vcol, bmap, t, out_dtype):
    B, L = q.shape[:2]
    nq = L // t
    pairs = [(qi, ki) for qi in range(nq) for ki in range(qi + 1)]
    qi_arr = jnp.asarray(np.array([p[0] for p in pairs], np.int32))
    ki_arr = jnp.asarray(np.array([p[1] for p in pairs], np.int32))
    grid_spec = pltpu.PrefetchScalarGridSpec(
        num_scalar_prefetch=2,
        grid=(n_units, len(pairs)),
        in_specs=[
            pl.BlockSpec((B, t, dqk), lambda a, p, qa, ka: (0, qa[p], qcol(a))),
            pl.BlockSpec((B, t, dqk), lambda a, p, qa, ka: (0, ka[p], kcol(a))),
            pl.BlockSpec((B, t, dv), lambda a, p, qa, ka: (0, ka[p], vcol(a))),
            pl.BlockSpec((1, 1, t, t), lambda a, p, qa, ka: (bmap(a), qa[p] - ka[p], 0, 0)),
        ],
        out_specs=pl.BlockSpec((B, t, dv), lambda a, p, qa, ka: (0, qa[p], a)),
        scratch_shapes=[pltpu.VMEM((B, t, LANES), F32), pltpu.VMEM((B, t, LANES), F32),
                        pltpu.VMEM((B, t, dv), F32)],
    )
    return pl.pallas_call(
        _flash_kernel,
        grid_spec=grid_spec,
        out_shape=jax.ShapeDtypeStruct((B, L, n_units * dv), out_dtype),
        compiler_params=_cparams(("arbitrary", "arbitrary")),
        name="flash_attention",
    )(qi_arr, ki_arr, q, k, v, bm_tab)


def _diff_lambda(lam_ref, lam_init):
    lq1, lk1, lq2, lk2 = lam_ref[0:1, :], lam_ref[1:2, :], lam_ref[2:3, :], lam_ref[3:4, :]
    return (jnp.exp(jnp.sum(lq1 * lk1, axis=-1, keepdims=True))
            - jnp.exp(jnp.sum(lq2 * lk2, axis=-1, keepdims=True)) + lam_init)


def _diff_combine_kernel(o_ref, lam_ref, g_ref, out_ref, *, dv, lam_init):
    lam = _diff_lambda(lam_ref, lam_init)
    nh = out_ref.shape[2] // dv
    for h in range(nh):
        o0 = o_ref[0, :, (2 * h) * dv:(2 * h + 1) * dv]
        o1 = o_ref[0, :, (2 * h + 1) * dv:(2 * h + 2) * dv]
        d = o0 - lam * o1
        ms = jnp.mean(d * d, axis=-1, keepdims=True)
        out_ref[0, :, h * dv:(h + 1) * dv] = (d * lax.rsqrt(ms + EPS) * g_ref[...] * (1.0 - lam_init)
                                              ).astype(out_ref.dtype)


def diff_combine(o, lam4, g_out, lam_init):
    B, L, W2 = o.shape
    dv = g_out.shape[0]
    W = W2 // 2
    tl = _pick(L, (256, 128))
    return pl.pallas_call(
        functools.partial(_diff_combine_kernel, dv=dv, lam_init=lam_init),
        grid=(B, L // tl),
        in_specs=[pl.BlockSpec((1, tl, W2), lambda b, i: (b, i, 0)),
                  pl.BlockSpec(lam4.shape, lambda b, i: (0, 0)),
                  pl.BlockSpec((1, dv), lambda b, i: (0, 0))],
        out_specs=pl.BlockSpec((1, tl, W), lambda b, i: (b, i, 0)),
        out_shape=jax.ShapeDtypeStruct((B, L, W), BF16),
        compiler_params=_cparams(("arbitrary", "arbitrary")),
        name="diff_combine",
    )(o, lam4, g_out.reshape(1, dv))


def _decode_kernel(*refs, n_scalar, n_chunks, diff, lam_init, n_heads, n_new):
    refs = refs[n_scalar:]
    q_ref = refs[0]
    k_refs = refs[1:1 + n_chunks]
    v_refs = refs[1 + n_chunks:1 + 2 * n_chunks]
    bm_refs = refs[1 + 2 * n_chunks:1 + 3 * n_chunks]
    rest = refs[1 + 3 * n_chunks:]
    if diff:
        kn_ref, vn_ref, bmn_ref, lam_ref, g_ref, out_ref, m_s, l_s, acc_s = rest
    else:
        kn_ref, vn_ref, bmn_ref, out_ref, m_s, l_s, acc_s = rest
    p = pl.program_id(1)

    @pl.when(p == 0)
    def _():
        m_s[...] = jnp.full(m_s.shape, NEG, F32)
        l_s[...] = jnp.zeros(l_s.shape, F32)
        acc_s[...] = jnp.zeros(acc_s.shape, F32)

    q = q_ref[0]

    def absorb(k2s, v2s, bms):
        ss = [_bdot_nt(q, k2) + bm for k2, bm in zip(k2s, bms)]
        m_old = m_s[...]
        m_new = m_old
        for s in ss:
            m_new = jnp.maximum(m_new, jnp.max(s, axis=-1, keepdims=True))
        alpha = jnp.exp(m_old - m_new)
        l_new = alpha * l_s[...]
        acc = alpha * acc_s[...]
        for s, v2 in zip(ss, v2s):
            pr = jnp.exp(s - m_new)
            l_new = l_new + jnp.sum(pr, axis=-1, keepdims=True)
            acc = acc + _bdot(pr, v2)
        l_s[...] = l_new
        acc_s[...] = acc
        m_s[...] = m_new

    def rows2d(ref):
        x = ref[0]
        return x.reshape(x.shape[0] * x.shape[1], x.shape[2])

    absorb([rows2d(r) for r in k_refs], [rows2d(r) for r in v_refs], [r[0] for r in bm_refs])

    @pl.when(p == pl.num_programs(1) - 1)
    def _():
        absorb([kn_ref[0]], [vn_ref[0]], [bmn_ref[...]])
        o = acc_s[...] / l_s[...]
        if diff:
            half = n_heads * n_new
            lam = _diff_lambda(lam_ref, lam_init)
            d = o[0:half] - lam * o[half:2 * half]
            ms = jnp.mean(d * d, axis=-1, keepdims=True)
            o = d * lax.rsqrt(ms + EPS) * g_ref[...] * (1.0 - lam_init)
        dv = o.shape[1]
        for h in range(n_heads):
            out_ref[0, :, h * dv:(h + 1) * dv] = o[h * n_new:(h + 1) * n_new].astype(out_ref.dtype)


def _head_expand(bias_rows, row_head, n_heads):
    ok = jnp.asarray(row_head[:, None] == np.arange(n_heads)[None, :])
    out = jnp.where(ok[:, None, :], bias_rows[..., None], NEG)
    return out.reshape(bias_rows.shape[:-1] + (bias_rows.shape[-1] * n_heads,))


def swa_decode(qn, k_new, v_new, cache_k, cache_v, rel_bias):
    B, Wb, H, dh = cache_k.shape
    T = qn.shape[1]
    R = H * T
    tk = _pick(Wb, (512, 256, 128))
    nblk = Wb // tk
    q2 = jnp.transpose(qn.reshape(B, T, H, dh), (0, 2, 1, 3)).reshape(B, R, dh)
    row_head = np.arange(R) // T
    dist_np = (Wb + np.arange(T))[:, None] - np.arange(Wb)[None, :]
    bias, ok = _swa_bias(rel_bias, dist_np)
    assert ok[:, :tk].any(axis=1).all()
    bm = _head_expand(bias.reshape(R, Wb), row_head, H)
    bm = jnp.transpose(bm.reshape(R, nblk, tk * H), (1, 0, 2))
    dn = np.arange(T)[:, None] - np.arange(T)[None, :]
    bias_n, _ = _swa_bias(rel_bias, dn)
    bmn = _head_expand(bias_n.reshape(R, T), row_head, H)
    kn2 = k_new.reshape(B, T * H, dh)
    vn2 = v_new.reshape(B, T * H, dh)
    return pl.pallas_call(
        functools.partial(_decode_kernel, n_scalar=0, n_chunks=1, diff=False, lam_init=0.0, n_heads=H, n_new=T),
        grid=(B, nblk),
        in_specs=[
            pl.BlockSpec((1, R, dh), lambda b, p: (b, 0, 0)),
            pl.BlockSpec((1, tk, H, dh), lambda b, p: (b, p, 0, 0)),
            pl.BlockSpec((1, tk, H, dh), lambda b, p: (b, p, 0, 0)),
            pl.BlockSpec((1, R, tk * H), lambda b, p: (p, 0, 0)),
            pl.BlockSpec((1, T * H, dh), lambda b, p: (b, 0, 0)),
            pl.BlockSpec((1, T * H, dh), lambda b, p: (b, 0, 0)),
            pl.BlockSpec((R, T * H), lambda b, p: (0, 0)),
        ],
        out_specs=pl.BlockSpec((1, T, H * dh), lambda b, p: (b, 0, 0)),
        out_shape=jax.ShapeDtypeStruct((B, T, H * dh), BF16),
        scratch_shapes=[pltpu.VMEM((R, 1), F32), pltpu.VMEM((R, 1), F32), pltpu.VMEM((R, dh), F32)],
        compiler_params=_cparams(("arbitrary", "arbitrary")),
        name="swa_decode",
    )(q2, cache_k, cache_v, bm, kn2, vn2, bmn)


def _np_rel_bucket(dist):
    exact = REL_BUCKETS // 2
    d_f = np.maximum(dist, 1).astype(np.float32)
    large = exact + (np.log(d_f / np.float32(exact)) / np.float32(math.log(REL_MAX_DIST / exact))
                     * np.float32(REL_BUCKETS - exact)).astype(np.int32)
    large = np.minimum(large, REL_BUCKETS - 1)
    return np.where(dist < exact, dist, large)


def diff_paged(qn, k_new, v_new, cache_k, cache_v, page_table, rel_bias, lam4, g_out, lam_init):
    B, T, H, _, dh = qn.shape
    dv = v_new.shape[-1]
    n_pages = page_table.shape[1]
    page = cache_k.shape[1]
    P = n_pages * page
    R = 2 * H * T
    pp = next(c for c in (4, 2, 1) if n_pages % c == 0)
    qt = jnp.transpose(qn, (0, 3, 2, 1, 4))
    zeros = jnp.zeros_like(qt[:, 0])
    qm = jnp.concatenate([jnp.concatenate([qt[:, 0], zeros], axis=-1),
                          jnp.concatenate([zeros, qt[:, 1]], axis=-1)], axis=1)
    qm = qm.reshape(B, R, 2 * dh)
    pg = np.arange(n_pages)[:, None, None]
    tt = np.arange(T)[None, :, None]
    ii = np.arange(page)[None, None, :]
    dist_np = P + tt - pg * page - ii
    bucket_np = _np_rel_bucket(dist_np)
    far = np.all(bucket_np == bucket_np[0:1], axis=(1, 2))
    near_pages = [int(x) for x in np.nonzero(~far)[0]]
    tile_pages = [0] + near_pages
    tile_of_page = np.zeros((n_pages,), np.int32)
    for ti, pgi in enumerate(near_pages):
        tile_of_page[pgi] = ti + 1
    nt = len(tile_pages)
    bias_t = _bias_of_dist(rel_bias, jnp.asarray(dist_np[tile_pages], jnp.int32))
    bias_rows = jnp.transpose(bias_t.reshape(2, H, nt, T, page), (2, 0, 1, 3, 4)).reshape(nt, R, page)
    row_head = (np.arange(R) // T) % H
    bm = _head_expand(bias_rows, row_head, H)
    dn = np.arange(T)[:, None] - np.arange(T)[None, :]
    bias_n, _ = _causal_bias(rel_bias, dn)
    bmn = _head_expand(bias_n.reshape(R, T), row_head, H)
    kn2 = k_new.reshape(B, T * H, 2 * dh)
    vn2 = v_new.reshape(B, T * H, dv)

    def kv_spec(j, d):
        return pl.BlockSpec((1, page, H, d), lambda b, p, pt, tl: (pt[b * n_pages + p * pp + j], 0, 0, 0))

    def bm_spec(j):
        return pl.BlockSpec((1, R, page * H), lambda b, p, pt, tl: (tl[p * pp + j], 0, 0))

    grid_spec = pltpu.PrefetchScalarGridSpec(
        num_scalar_prefetch=2,
        grid=(B, n_pages // pp),
        in_specs=(
            [pl.BlockSpec((1, R, 2 * dh), lambda b, p, pt, tl: (b, 0, 0))]
            + [kv_spec(j, 2 * dh) for j in range(pp)]
            + [kv_spec(j, dv) for j in range(pp)]
            + [bm_spec(j) for j in range(pp)]
            + [pl.BlockSpec((1, T * H, 2 * dh), lambda b, p, pt, tl: (b, 0, 0)),
               pl.BlockSpec((1, T * H, dv), lambda b, p, pt, tl: (b, 0, 0)),
               pl.BlockSpec((R, T * H), lambda b, p, pt, tl: (0, 0)),
               pl.BlockSpec(lam4.shape, lambda b, p, pt, tl: (0, 0)),
               pl.BlockSpec((1, dv), lambda b, p, pt, tl: (0, 0))]),
        out_specs=pl.BlockSpec((1, T, H * dv), lambda b, p, pt, tl: (b, 0, 0)),
        scratch_shapes=[pltpu.VMEM((R, 1), F32), pltpu.VMEM((R, 1), F32), pltpu.VMEM((R, dv), F32)],
    )
    return pl.pallas_call(
        functools.partial(_decode_kernel, n_scalar=2, n_chunks=pp, diff=True, lam_init=lam_init,
                          n_heads=H, n_new=T),
        grid_spec=grid_spec,
        out_shape=jax.ShapeDtypeStruct((B, T, H * dv), BF16),
        compiler_params=_cparams(("arbitrary", "arbitrary")),
        name="diff_paged",
    )(page_table.reshape(-1), jnp.asarray(tile_of_page), qm, *([cache_k] * pp), *([cache_v] * pp),
      *([bm] * pp), kn2, vn2, bmn, lam4, g_out.reshape(1, dv))


def _rglru_kernel(y_ref, x_ref, cs_ref, h0_ref, cw_ref, cb_ref, wa_ref, ba_ref, wx_ref, bx_ref, lam_ref,
                  out_ref, conv_ref, hl_ref, xpad, hcar, *, l_valid):
    i = pl.program_id(1)
    T, D = x_ref.shape[1], x_ref.shape[2]

    @pl.when(i == 0)
    def _():
        xpad[0:8, :] = cs_ref[0]
        hcar[...] = h0_ref[0]

    xpad[8:8 + T, :] = x_ref[0]
    conv = cb_ref[...] + cw_ref[CONV_W - 1:CONV_W, :] * xpad[8:8 + T, :]
    for j in range(CONV_W - 1):
        s = CONV_W - 1 - j
        conv = conv + cw_ref[j:j + 1, :] * xpad[8 - s:8 - s + T, :]
    nb = wa_ref.shape[0]
    bs = D // nb
    r_parts, i_parts = [], []
    for n in range(nb):
        xb = conv[:, n * bs:(n + 1) * bs]
        r_parts.append(_bdot(xb, wa_ref[n]))
        i_parts.append(_bdot(xb, wx_ref[n]))
    r = jax.nn.sigmoid(jnp.concatenate(r_parts, axis=-1) + ba_ref[...])
    ig = jax.nn.sigmoid(jnp.concatenate(i_parts, axis=-1) + bx_ref[...])
    log_a = -LRU_C * r * _softplus(-lam_ref[...])
    a = jnp.exp(log_a)
    bb = jnp.sqrt(-jnp.tanh(log_a) * (1.0 + a * a)) * ig * conv
    rowid = lax.broadcasted_iota(jnp.int32, (T, D), 0)
    s = 1
    while s < T:
        keep = rowid >= s
        a_sh = pltpu.roll(a, s, 0)
        b_sh = pltpu.roll(bb, s, 0)
        bb = jnp.where(keep, a * b_sh + bb, bb)
        a = jnp.where(keep, a * a_sh, a)
        s *= 2
    hs = a * hcar[...] + bb
    out_ref[0] = (hs * _gelu(y_ref[0])).astype(out_ref.dtype)
    hcar[...] = hs[T - 1:T, :]
    tail = xpad[T:T + 8, :]
    xpad[0:8, :] = tail

    @pl.when(i == pl.num_programs(1) - 1)
    def _():
        hl_ref[0] = hs[l_valid - 1:l_valid, :]
        conv_ref[0] = xpad[8 + l_valid - (CONV_W - 1):8 + l_valid, :]


def rglru_core(z, conv_state, h0, conv_w, conv_b, w_a, b_a, w_x, b_x, lam, l_valid_last, tblk):
    B, L, D2 = z.shape
    D = D2 // 2
    cs8 = jnp.concatenate([jnp.zeros((B, 8 - (CONV_W - 1), D), F32), conv_state.astype(F32)], axis=1)
    vec = lambda a: a.reshape(1, D)
    out, conv_new, h_last = pl.pallas_call(
        functools.partial(_rglru_kernel, l_valid=l_valid_last),
        grid=(B, L // tblk),
        in_specs=[
            pl.BlockSpec((1, tblk, D), lambda b, i: (b, i, 0)),
            pl.BlockSpec((1, tblk, D), lambda b, i: (b, i, 1)),
            pl.BlockSpec((1, 8, D), lambda b, i: (b, 0, 0)),
            pl.BlockSpec((1, 1, D), lambda b, i: (b, 0, 0)),
            pl.BlockSpec((CONV_W, D), lambda b, i: (0, 0)),
            pl.BlockSpec((1, D), lambda b, i: (0, 0)),
            pl.BlockSpec(w_a.shape, lambda b, i: (0, 0, 0)),
            pl.BlockSpec((1, D), lambda b, i: (0, 0)),
            pl.BlockSpec(w_x.shape, lambda b, i: (0, 0, 0)),
            pl.BlockSpec((1, D), lambda b, i: (0, 0)),
            pl.BlockSpec((1, D), lambda b, i: (0, 0)),
        ],
        out_specs=[
            pl.BlockSpec((1, tblk, D), lambda b, i: (b, i, 0)),
            pl.BlockSpec((1, CONV_W - 1, D), lambda b, i: (b, 0, 0)),
            pl.BlockSpec((1, 1, D), lambda b, i: (b, 0, 0)),
        ],
        out_shape=[
            jax.ShapeDtypeStruct((B, L, D), BF16),
            jax.ShapeDtypeStruct((B, CONV_W - 1, D), F32),
            jax.ShapeDtypeStruct((B, 1, D), F32),
        ],
        scratch_shapes=[pltpu.VMEM((tblk + 8, D), F32), pltpu.VMEM((1, D), F32)],
        compiler_params=_cparams(("arbitrary", "arbitrary")),
        name="rglru",
    )(z, z, cs8, h0.astype(F32).reshape(B, 1, D), conv_w, vec(conv_b), w_a.astype(BF16), vec(b_a),
      w_x.astype(BF16), vec(b_x), vec(lam))
    return out, conv_new, h_last.reshape(B, D)


def _topk_rows(s, k):
    n, tb = s.shape
    rowf = lax.broadcasted_iota(jnp.int32, (n, tb), 0).astype(F32)
    vals, idxs = [], []
    for _ in range(k):
        m = jnp.max(s, axis=0, keepdims=True)
        idx = jnp.min(jnp.where(s == m, rowf, float(n)), axis=0, keepdims=True)
        vals.append(m)
        idxs.append(idx)
        s = jnp.where(rowf == idx, -jnp.inf, s)
    return jnp.concatenate(vals, axis=0), jnp.concatenate(idxs, axis=0)


def _pair_candidates(sv1, si1, sv2, si2, k):
    tb = sv1.shape[1]
    cands, c1, c2 = [], [], []
    a = 0
    while a < k and k // (a + 1) > 1:
        nb = k // (a + 1)
        rows = min(k, -(-nb // 8) * 8)
        v = sv1[a:a + 1] + sv2[0:rows]
        if rows != nb:
            rid = lax.broadcasted_iota(jnp.int32, (rows, tb), 0)
            v = jnp.where(rid < nb, v, -jnp.inf)
        cands.append(v)
        c1.append(jnp.broadcast_to(si1[a:a + 1], (rows, tb)))
        c2.append(si2[0:rows])
        a += 1
    cands.append(sv1[a:k] + sv2[0:1])
    c1.append(si1[a:k])
    c2.append(jnp.broadcast_to(si2[0:1], (k - a, tb)))
    return jnp.concatenate(cands, axis=0), jnp.concatenate(c1, axis=0), jnp.concatenate(c2, axis=0)


def _route_kernel(h_ref, wq_ref, k1_ref, k2_ref, e1_ref, e2_ref, g_ref, *, topk):
    q = jnp.dot(h_ref[...], wq_ref[...], preferred_element_type=F32)
    dk2 = k1_ref.shape[2]
    nt = (((1,), (1,)), ((), ()))
    s1 = lax.dot_general(k1_ref[0], q[:, :dk2], nt, precision=HIGHEST, preferred_element_type=F32)
    s2 = lax.dot_general(k2_ref[0], q[:, dk2:], nt, precision=HIGHEST, preferred_element_type=F32)
    sv1, si1 = _topk_rows(s1, topk)
    sv2, si2 = _topk_rows(s2, topk)
    cand, c1, c2 = _pair_candidates(sv1, si1, sv2, si2, topk)
    nc = cand.shape[0]
    rowf = lax.broadcasted_iota(jnp.int32, cand.shape, 0).astype(F32)
    vals = []
    for r in range(topk):
        m = jnp.max(cand, axis=0, keepdims=True)
        pos = jnp.min(jnp.where(cand == m, rowf, float(nc)), axis=0, keepdims=True)
        hit = rowf == pos
        e1_ref[0, r:r + 1, :] = jnp.sum(jnp.where(hit, c1, 0.0), axis=0, keepdims=True)
        e2_ref[0, r:r + 1, :] = jnp.sum(jnp.where(hit, c2, 0.0), axis=0, keepdims=True)
        vals.append(m)
        cand = jnp.where(hit, -jnp.inf, cand)
    ex = [jnp.exp(v - vals[0]) for v in vals]
    den = ex[0]
    for e in ex[1:]:
        den = den + e
    for r in range(topk):
        g_ref[0, r:r + 1, :] = ex[r] / den


def peer_route(h, wq, keys):
    N, D = h.shape
    nh, _, nk, dk2 = keys.shape
    tb = _pick(N, (256, 128))
    shp = jax.ShapeDtypeStruct((nh, TOPK_P, N), F32)
    o_spec = pl.BlockSpec((1, TOPK_P, tb), lambda hh, i: (hh, 0, i))
    return pl.pallas_call(
        functools.partial(_route_kernel, topk=TOPK_P),
        grid=(nh, N // tb),
        in_specs=[pl.BlockSpec((tb, D), lambda hh, i: (i, 0)),
                  pl.BlockSpec((D, 2 * dk2), lambda hh, i: (0, hh)),
                  pl.BlockSpec((1, nk, dk2), lambda hh, i: (2 * hh, 0, 0)),
                  pl.BlockSpec((1, nk, dk2), lambda hh, i: (2 * hh + 1, 0, 0))],
        out_specs=[o_spec, o_spec, o_spec],
        out_shape=[shp, shp, shp],
        compiler_params=_cparams(("arbitrary", "arbitrary")),
        name="peer_route",
    )(h, wq, keys.reshape(nh * 2, nk, dk2), keys.reshape(nh * 2, nk, dk2))


def _route_build_kernel(e1_ref, e2_ref, g_ref, o_ref, *, nkeys):
    tb, _, ks = e1_ref.shape
    sub = lax.broadcasted_iota(jnp.int32, (tb, nkeys, ks), 1).astype(F32)
    at = jnp.where(e1_ref[...] == sub, 1.0, 0.0).astype(BF16)
    bt = jnp.where(e2_ref[...] == sub, g_ref[...], 0.0).astype(BF16)
    g3 = jnp.einsum('tik,tjk->tij', at, bt, preferred_element_type=F32)
    gt = pltpu.einshape('tij->itj', g3)
    for i in range(nkeys):
        o_ref[:, i * nkeys:(i + 1) * nkeys] = gt[i].astype(o_ref.dtype)


def peer_build(e1, e2, g, nkeys):
    nh, k, N = e1.shape
    ks = nh * k
    slot = lambda a: jnp.transpose(a, (2, 0, 1)).reshape(N, 1, ks)
    tb = _pick(N, (64, 32))
    spec = pl.BlockSpec((tb, 1, ks), lambda i: (i, 0, 0))
    return pl.pallas_call(
        functools.partial(_route_build_kernel, nkeys=nkeys),
        grid=(N // tb,),
        in_specs=[spec, spec, spec],
        out_specs=pl.BlockSpec((tb, nkeys * nkeys), lambda i: (i, 0)),
        out_shape=jax.ShapeDtypeStruct((N, nkeys * nkeys), BF16),
        compiler_params=_cparams(("arbitrary",)),
        name="peer_build",
    )(slot(e1), slot(e2), slot(g))


def _peer_kernel(x_ref, ut_ref, v_ref, g_ref, r_ref, gt_ref, o_ref, acc_s):
    e = pl.program_id(1)

    @pl.when(e == 0)
    def _():
        acc_s[...] = jnp.zeros(acc_s.shape, F32)

    s = jnp.dot(x_ref[...], ut_ref[...], preferred_element_type=F32)
    p = (_gelu(s) * g_ref[...].astype(F32)).astype(BF16)
    acc_s[...] += jnp.dot(p, v_ref[...], preferred_element_type=F32)

    @pl.when(e == pl.num_programs(1) - 1)
    def _():
        o_ref[...] = r_ref[...] + gt_ref[...] * acc_s[...]


def _peer_kernel_shared_gate(x_ref, ut_ref, v_ref, g_ref, r_ref, gt_ref, o_ref, acc_s):
    _peer_kernel(x_ref, ut_ref, v_ref, g_ref, r_ref, gt_ref.at[0], o_ref, acc_s)


def peer_dense(h, ut, v, G, res, gate, rows_per_gate):
    N, D = h.shape
    E = ut.shape[1]
    tb = _pick(N, (512, 256, 128))
    eb = _pick(E, (512, 256, 128))
    if rows_per_gate == 1:
        kern = _peer_kernel
        g_spec = pl.BlockSpec((tb, D), lambda i, e: (i, 0))
        gate_arr = gate
    else:
        assert rows_per_gate % tb == 0
        per = rows_per_gate // tb
        kern = _peer_kernel_shared_gate
        g_spec = pl.BlockSpec((1, 1, D), lambda i, e: (i // per, 0, 0))
        gate_arr = gate.reshape(gate.shape[0], 1, D)
    return pl.pallas_call(
        kern,
        grid=(N // tb, E // eb),
        in_specs=[
            pl.BlockSpec((tb, D), lambda i, e: (i, 0)),
            pl.BlockSpec((D, eb), lambda i, e: (0, e)),
            pl.BlockSpec((eb, D), lambda i, e: (e, 0)),
            pl.BlockSpec((tb, eb), lambda i, e: (i, e)),
            pl.BlockSpec((tb, D), lambda i, e: (i, 0)),
            g_spec,
        ],
        out_specs=pl.BlockSpec((tb, D), lambda i, e: (i, 0)),
        out_shape=jax.ShapeDtypeStruct((N, D), F32),
        scratch_shapes=[pltpu.VMEM((tb, D), F32)],
        compiler_params=_cparams(("arbitrary", "arbitrary")),
        name="peer_dense",
    )(h, ut, v, G, res, gate_arr)


def _pad_rows(a, n, value=0.0, axis=1):
    pad = [(0, 0)] * a.ndim
    pad[axis] = (0, n - a.shape[axis])
    return jnp.pad(a, pad, constant_values=value)


def kernel(x_prompt, x_sample, state_mlstm_C, state_mlstm_n, state_mlstm_m, cache_swa_k, cache_swa_v,
           cache_diff_k, cache_diff_v, state_rglru_conv, state_rglru_h, page_table, c_prompt, c_sample,
           w_ada, b_ada, g_norm_mix, g_norm_ffn, rel_bias,
           mlstm_w_in, mlstm_b_gates, mlstm_g_out, mlstm_w_out,
           swa_w_in, swa_g_q, swa_g_k, swa_w_out,
           diff_w_in, diff_g_q, diff_g_k, diff_lam_q1, diff_lam_k1, diff_lam_q2, diff_lam_k2, diff_g_out, diff_w_out,
           rglru_w_in, rglru_conv_w, rglru_conv_b, rglru_w_a, rglru_b_a, rglru_w_x, rglru_b_x, rglru_lambda, rglru_w_out,
           peer_w_q, peer_keys, peer_u, peer_v):
    xp, xs = x_prompt, x_sample
    Bp, S, D = xp.shape
    Bs, T, _ = xs.shape
    depth = w_ada.shape[0]
    Ns = Bs * T

    c_all = jnp.concatenate([c_prompt, c_sample], axis=0)
    mod = ada_all(c_all, w_ada, b_ada)
    mod = mod.reshape(depth, Bp + Bs, 6, 1, D)

    def mods(i, which):
        m = mod[i, :, which]
        return m[:Bp], m[Bp:]

    def out_proj(hp, hs, w, gate_p, gate_s):
        nonlocal xp, xs
        wb = w.astype(BF16)
        xp = matmul(hp, wb, res=xp, gate=gate_p)
        gs_rows = jnp.broadcast_to(gate_s, (Bs, T, D)).reshape(1, Ns, D)
        xs = matmul(hs.reshape(1, Ns, -1), wb, res=xs.reshape(1, Ns, D), gate=gs_rows).reshape(Bs, T, D)

    for i in range(depth):
        kind = i % 4
        sh_p, sh_s = mods(i, 0)
        sc_p, sc_s = mods(i, 1)
        gt_p, gt_s = mods(i, 2)
        hp = modulate(xp, g_norm_mix[i], sh_p, sc_p)
        hs = modulate(xs, g_norm_mix[i], sh_s, sc_s)
        hs_flat = hs.reshape(1, Ns, D)
        if kind == 0:
            H = NH_A
            dk = state_mlstm_C.shape[2]
            dv = state_mlstm_C.shape[3]
            nmain = 2 * H * dk + 2 * H * dv
            w_main = mlstm_w_in[:, :nmain].astype(BF16)
            w_gate = mlstm_w_in[:, nmain:].astype(BF16)
            zp = matmul(hp, w_main)
            gp = matmul(hp, w_gate)
            zs = matmul(hs_flat, w_main).reshape(Bs, T, nmain)
            gs = matmul(hs_flat, w_gate).reshape(Bs, T, 2 * H)
            chunk_p = _pick(S, (CHUNK_A, 128))
            op, mC_p, mn_p, mm_p = mlstm_core(
                zp, gp, mlstm_b_gates, jnp.zeros((Bp, H, dk, dv), F32), jnp.zeros((Bp, H, dk), F32),
                jnp.full((Bp, H), M_INIT, F32), mlstm_g_out, chunk_p)
            Tp = 128
            zs_pad = _pad_rows(zs, Tp)
            gs_pad = jnp.concatenate([
                _pad_rows(gs[..., :H], Tp, NEG), _pad_rows(gs[..., H:], Tp, -NEG)], axis=-1)
            os_, mC_s, mn_s, mm_s = mlstm_core(
                zs_pad, gs_pad, mlstm_b_gates, state_mlstm_C.astype(F32), state_mlstm_n.astype(F32),
                state_mlstm_m.astype(F32), mlstm_g_out, Tp)
            os_ = os_[:, :T]
            w_out = mlstm_w_out
        elif kind == 1:
            H = NH_B
            dh = D // H
            wb = swa_w_in.astype(BF16)
            zp = matmul(hp, wb)
            zs = matmul(hs_flat, wb).reshape(Bs, T, 3 * D)
            gq = jnp.tile(swa_g_q, H)
            gk = jnp.tile(swa_g_k, H)
            qn_p, kn_p = headnorm(zp, gq, gk, dh, dh ** -0.5)
            qn_s, kn_s = headnorm(zs, gq, gk, dh, dh ** -0.5)
            t = _pick(S, (512, 256, 128))
            bm_tab, ok = _swa_bias(rel_bias, _toeplitz_dist_np(S // t, t))
            assert ok.any(axis=2).all()
            vblk = 2 * H
            op = flash_attention(qn_p, kn_p, zp, bm_tab, n_units=H, dqk=dh, dv=dh,
                                 qcol=lambda a: a, kcol=lambda a: a, vcol=lambda a: vblk + a, bmap=lambda a: a,
                                 t=t, out_dtype=BF16)
            swa_k_p = kn_p.reshape(Bp, S, H, dh)
            swa_v_p = zp[:, :, 2 * D:].reshape(Bp, S, H, dh)
            Wb = cache_swa_k.shape[1]
            vs_new = zs[:, :, 2 * D:]
            os_ = swa_decode(qn_s, kn_s, vs_new, cache_swa_k, cache_swa_v, rel_bias)
            swa_k_s = jnp.concatenate([cache_swa_k[:, T:].astype(F32), kn_s.reshape(Bs, T, H, dh)], axis=1)
            swa_v_s = jnp.concatenate([cache_swa_v[:, T:].astype(F32), vs_new.reshape(Bs, T, H, dh)], axis=1)
            w_out = swa_w_out
        elif kind == 2:
            H = NH_C
            dh = D // (2 * H)
            dv = 2 * dh
            lam_init = 0.8 - 0.6 * math.exp(-0.3 * i)
            lam4 = jnp.stack([diff_lam_q1, diff_lam_k1, diff_lam_q2, diff_lam_k2]).astype(F32)
            wb = diff_w_in.astype(BF16)
            zp = matmul(hp, wb)
            zs = matmul(hs_flat, wb).reshape(Bs, T, 3 * D)
            gq = jnp.tile(diff_g_q.reshape(-1), H)
            gk = jnp.tile(diff_g_k.reshape(-1), H)
            qn_p, kn_p = headnorm(zp, gq, gk, dh, dh ** -0.5)
            qn_s, kn_s = headnorm(zs, gq, gk, dh, dh ** -0.5)
            t = _pick(S, (512, 256, 128))
            bm_tab, _ = _causal_bias(rel_bias, _toeplitz_dist_np(S // t, t))
            vblk = (2 * D) // dv
            o2 = flash_attention(qn_p, kn_p, zp, bm_tab, n_units=2 * H, dqk=dh, dv=dv,
                                 qcol=lambda a: a, kcol=lambda a: a, vcol=lambda a: vblk + a // 2,
                                 bmap=lambda a: (a % 2) * H + a // 2, t=t, out_dtype=F32)
            op = diff_combine(o2, lam4, diff_g_out, lam_init)
            diff_k_p = kn_p.reshape(Bp, S, H, 2 * dh)
            diff_v_p = zp[:, :, 2 * D:].reshape(Bp, S, H, dv)
            diff_k_s = kn_s.reshape(Bs, T, H, 2 * dh)
            diff_v_s = zs[:, :, 2 * D:].reshape(Bs, T, H, dv)
            os_ = diff_paged(qn_s.reshape(Bs, T, H, 2, dh), diff_k_s, diff_v_s, cache_diff_k, cache_diff_v,
                             page_table, rel_bias, lam4, diff_g_out, lam_init)
            w_out = diff_w_out
        else:
            wb = rglru_w_in.astype(BF16)
            zp = matmul(hp, wb)
            zs = matmul(hs_flat, wb).reshape(Bs, T, -1)
            tblk = _pick(S, (256, 128))
            op, conv_p, h_p = rglru_core(zp, jnp.zeros((Bp, CONV_W - 1, D), F32), jnp.zeros((Bp, D), F32),
                                         rglru_conv_w, rglru_conv_b, rglru_w_a, rglru_b_a, rglru_w_x, rglru_b_x,
                                         rglru_lambda, tblk, tblk)
            os_, conv_s, h_s = rglru_core(_pad_rows(zs, 8), state_rglru_conv, state_rglru_h,
                                          rglru_conv_w, rglru_conv_b, rglru_w_a, rglru_b_a, rglru_w_x, rglru_b_x,
                                          rglru_lambda, T, 8)
            os_ = os_[:, :T]
            w_out = rglru_w_out
        out_proj(op, os_, w_out, gt_p, gt_s)

        sh_p, sh_s = mods(i, 3)
        sc_p, sc_s = mods(i, 4)
        gt_p, gt_s = mods(i, 5)
        hp = modulate(xp, g_norm_ffn[i], sh_p, sc_p).reshape(Bp * S, D)
        hs = modulate(xs, g_norm_ffn[i], sh_s, sc_s).reshape(Ns, D)
        wq = peer_w_q[i].astype(BF16)
        ut = jnp.transpose(peer_u[i]).astype(BF16)
        vb = peer_v[i].astype(BF16)
        nk = peer_keys.shape[3]
        Gp = peer_build(*peer_route(hp, wq, peer_keys[i]), nk)
        Ns_pad = -(-Ns // LANES) * LANES
        Gs = peer_build(*peer_route(_pad_rows(hs, Ns_pad, axis=0), wq, peer_keys[i]), nk)[:Ns]
        xp = peer_dense(hp, ut, vb, Gp, xp.reshape(Bp * S, D), gt_p.reshape(Bp, D), S).reshape(Bp, S, D)
        gs_rows = jnp.broadcast_to(gt_s, (Bs, T, D)).reshape(Ns, D)
        xs = peer_dense(hs, ut, vb, Gs, xs.reshape(Ns, D), gs_rows, 1).reshape(Bs, T, D)

    return (xp, xs, mC_p, mC_s, mn_p, mn_s, mm_p, mm_s, swa_k_p, swa_k_s, swa_v_p, swa_v_s,
            diff_k_p, diff_k_s, diff_v_p, diff_v_s, conv_p, conv_s, h_p, h_s)
```

```python
import functools
import math

import numpy as np
import jax
import jax.numpy as jnp
from jax import lax
from jax.experimental import pallas as pl
from jax.experimental.pallas import tpu as pltpu

F32 = jnp.float32
BF16 = jnp.bfloat16
HIGHEST = lax.Precision.HIGHEST

EPS = 1e-6
NEG = -1e30
VMEM_LIMIT = 56 * 1024 * 1024
LANES = 128

NH_A, CHUNK_A, M_INIT = 8, 256, -1e30
NH_B = 16
DIL_PAIRS = ((128, 1), (512, 4), (2048, 16))
NH_C = 8
NBLK_D, CONV_W, LRU_C = 16, 4, 8.0
NH_P, N_KEYS, TOPK_P = 8, 128, 16
REL_BUCKETS, REL_MAX_DIST = 32, 2048
PAGE_SIZE = 128


def _cparams(sem):
    return pltpu.CompilerParams(dimension_semantics=sem, vmem_limit_bytes=VMEM_LIMIT)


def _bdot(a, b):
    return jnp.dot(a.astype(BF16), b.astype(BF16), preferred_element_type=F32)


def _bdot_nt(a, b):
    return lax.dot_general(a.astype(BF16), b.astype(BF16), (((1,), (1,)), ((), ())),
                           preferred_element_type=F32)


def _gelu(x):
    return 0.5 * x * (1.0 + lax.erf(x * (1.0 / math.sqrt(2.0))))


def _log_sigmoid(x):
    return jnp.minimum(x, 0.0) - jnp.log1p(jnp.exp(-jnp.abs(x)))


def _softplus(x):
    return jnp.maximum(x, 0.0) + jnp.log1p(jnp.exp(-jnp.abs(x)))


def _pick(n, pref):
    for t in pref:
        if n % t == 0:
            return t
    return n


def _ada_kernel(c_ref, w_ref, b_ref, o_ref):
    c = c_ref[...]
    a = c * jax.nn.sigmoid(c)
    o_ref[0] = jnp.dot(a, w_ref[0], preferred_element_type=F32, precision=HIGHEST) + b_ref[0]


def ada_all(c_all, w_ada, b_ada):
    R, D = c_all.shape
    nl, _, N = w_ada.shape
    tn = _pick(N, (1024, 512, 256, 128))
    return pl.pallas_call(
        _ada_kernel,
        grid=(nl, N // tn),
        in_specs=[pl.BlockSpec((R, D), lambda l, j: (0, 0)),
                  pl.BlockSpec((1, D, tn), lambda l, j: (l, 0, j)),
                  pl.BlockSpec((1, 1, tn), lambda l, j: (l, 0, j))],
        out_specs=pl.BlockSpec((1, R, tn), lambda l, j: (l, 0, j)),
        out_shape=jax.ShapeDtypeStruct((nl, R, N), F32),
        compiler_params=_cparams(("arbitrary", "arbitrary")),
        name="ada",
    )(c_all, w_ada, b_ada.reshape(nl, 1, N))


def _modulate_kernel(x_ref, g_ref, sh_ref, sc_ref, o_ref):
    x = x_ref[0]
    ms = jnp.mean(x * x, axis=-1, keepdims=True)
    y = x * lax.rsqrt(ms + EPS) * g_ref[...]
    o_ref[0] = (y * (1.0 + sc_ref[0]) + sh_ref[0]).astype(o_ref.dtype)


def modulate(x, g, shift, scale):
    B, L, D = x.shape
    tl = _pick(L, (512, 256, 128))
    return pl.pallas_call(
        _modulate_kernel,
        grid=(B, L // tl),
        in_specs=[pl.BlockSpec((1, tl, D), lambda b, i: (b, i, 0)),
                  pl.BlockSpec((1, D), lambda b, i: (0, 0)),
                  pl.BlockSpec((1, 1, D), lambda b, i: (b, 0, 0)),
                  pl.BlockSpec((1, 1, D), lambda b, i: (b, 0, 0))],
        out_specs=pl.BlockSpec((1, tl, D), lambda b, i: (b, i, 0)),
        out_shape=jax.ShapeDtypeStruct((B, L, D), BF16),
        compiler_params=_cparams(("arbitrary", "arbitrary")),
        name="modulate",
    )(x, g.reshape(1, D), shift, scale)


def _mm_kernel(x_ref, w_ref, o_ref):
    o_ref[0] = jnp.dot(x_ref[0], w_ref[...], preferred_element_type=F32).astype(o_ref.dtype)


def _mm_res_kernel(x_ref, w_ref, r_ref, gt_ref, o_ref):
    acc = jnp.dot(x_ref[0], w_ref[...], preferred_element_type=F32)
    o_ref[0] = r_ref[0] + gt_ref[0] * acc


def matmul(x, w, res=None, gate=None, out_dtype=F32):
    B, L, K = x.shape
    N = w.shape[1]
    tm = _pick(L, (512, 256, 128))
    tn = _pick(N, (1024, 512, 256, 128))
    grid = (N // tn, B, L // tm)
    x_spec = pl.BlockSpec((1, tm, K), lambda j, b, i: (b, i, 0))
    w_spec = pl.BlockSpec((K, tn), lambda j, b, i: (0, j))
    o_spec = pl.BlockSpec((1, tm, tn), lambda j, b, i: (b, i, j))
    if res is None:
        return pl.pallas_call(
            _mm_kernel, grid=grid, in_specs=[x_spec, w_spec], out_specs=o_spec,
            out_shape=jax.ShapeDtypeStruct((B, L, N), out_dtype),
            compiler_params=_cparams(("arbitrary",) * 3), name="matmul",
        )(x, w)
    if gate.shape[1] == 1:
        g_spec = pl.BlockSpec((1, 1, tn), lambda j, b, i: (b, 0, j))
    else:
        g_spec = pl.BlockSpec((1, tm, tn), lambda j, b, i: (b, i, j))
    return pl.pallas_call(
        _mm_res_kernel, grid=grid, in_specs=[x_spec, w_spec, o_spec, g_spec], out_specs=o_spec,
        out_shape=jax.ShapeDtypeStruct((B, L, N), F32),
        compiler_params=_cparams(("arbitrary",) * 3), name="matmul_res",
    )(x, w, res, gate)


def _mlstm_kernel(q_ref, k_ref, v_ref, o_ref, gc_ref, gr_ref, bc_ref, br_ref,
                  C0_ref, n0_ref, m0_ref, gout_ref,
                  hs_ref, C_ref, n_ref, m_ref, C_s, n_s, m_s, *, dk):
    ci = pl.program_id(2)

    @pl.when(ci == 0)
    def _():
        C_s[...] = C0_ref[0, 0]
        n_s[...] = n0_ref[0, 0]
        m_s[...] = m0_ref[0, 0]

    q = q_ref[0]
    k = k_ref[0] * (dk ** -0.5)
    v = v_ref[0]
    c = q.shape[0]
    gc = gc_ref[0, 0] + bc_ref[0]
    gr = gr_ref[0, 0] + br_ref[0]
    li_c, lf_c = gc[:, 0:1], _log_sigmoid(gc[:, 1:2])
    li_r, lf_r = gr[0:1, :], _log_sigmoid(gr[1:2, :])
    row = lax.broadcasted_iota(jnp.int32, (c, c), 0)
    col = lax.broadcasted_iota(jnp.int32, (c, c), 1)
    causal = col <= row
    b_c = jnp.sum(jnp.where(causal, lf_r, 0.0), axis=1, keepdims=True)
    b_r = jnp.sum(jnp.where(row <= col, lf_c, 0.0), axis=0, keepdims=True)
    m_prev = m_s[...]
    Dm = jnp.where(causal, b_c - b_r + li_r, NEG)
    m_t = jnp.maximum(b_c + m_prev, jnp.max(Dm, axis=1, keepdims=True))
    S = _bdot_nt(q, k) * jnp.exp(Dm - m_t)
    inter = jnp.exp(b_c + m_prev - m_t)
    C = C_s[...]
    n = n_s[...]
    num = _bdot(S, v) + inter * _bdot(q, C)
    den = jnp.sum(S, axis=1, keepdims=True) + inter * jnp.sum(q * n, axis=1, keepdims=True)
    h = num / jnp.maximum(jnp.abs(den), jnp.exp(-m_t))
    hh = jax.nn.sigmoid(o_ref[0]) * h
    ms = jnp.mean(hh * hh, axis=-1, keepdims=True)
    hs_ref[0] = (hh * lax.rsqrt(ms + EPS) * gout_ref[...]).astype(hs_ref.dtype)
    m_new = m_t[c - 1:c, :]
    b_last = b_c[c - 1:c, :]
    w_r = jnp.exp(b_last - b_r + li_r - m_new)
    w_c = jnp.exp(b_last - b_c + li_c - m_new)
    decay = jnp.exp(b_last + m_prev - m_new)
    C_new = decay * C + _bdot(k.T, w_c * v)
    n_new = decay * n + jnp.dot(w_r, k, preferred_element_type=F32, precision=HIGHEST)
    C_s[...] = C_new
    n_s[...] = n_new
    m_s[...] = m_new

    @pl.when(ci == pl.num_programs(2) - 1)
    def _():
        C_ref[0, 0] = C_new
        n_ref[0, 0] = n_new
        m_ref[0, 0] = m_new


def mlstm_core(z, g, b_gates, C0, n0, m0, g_out, chunk):
    B, L, _ = z.shape
    H = NH_A
    dk = C0.shape[2]
    dv = C0.shape[3]
    nc = L // chunk
    g4 = g.reshape(B, L, 2, H)
    gcol = jnp.transpose(g4, (0, 3, 1, 2))
    grow = jnp.transpose(g4, (0, 3, 2, 1))
    bg = b_gates.reshape(2, H)
    bcol = jnp.transpose(bg, (1, 0)).reshape(H, 1, 2)
    brow = jnp.transpose(bg, (1, 0)).reshape(H, 2, 1)
    kq = (H * dk) // dk
    vo = (2 * H * dk) // dv
    oo = vo + H
    outs = pl.pallas_call(
        functools.partial(_mlstm_kernel, dk=dk),
        grid=(B, H, nc),
        in_specs=[
            pl.BlockSpec((1, chunk, dk), lambda b, h, c: (b, c, h)),
            pl.BlockSpec((1, chunk, dk), lambda b, h, c: (b, c, kq + h)),
            pl.BlockSpec((1, chunk, dv), lambda b, h, c: (b, c, vo + h)),
            pl.BlockSpec((1, chunk, dv), lambda b, h, c: (b, c, oo + h)),
            pl.BlockSpec((1, 1, chunk, 2), lambda b, h, c: (b, h, c, 0)),
            pl.BlockSpec((1, 1, 2, chunk), lambda b, h, c: (b, h, 0, c)),
            pl.BlockSpec((1, 1, 2), lambda b, h, c: (h, 0, 0)),
            pl.BlockSpec((1, 2, 1), lambda b, h, c: (h, 0, 0)),
            pl.BlockSpec((1, 1, dk, dv), lambda b, h, c: (b, h, 0, 0)),
            pl.BlockSpec((1, 1, 1, dk), lambda b, h, c: (b, h, 0, 0)),
            pl.BlockSpec((1, 1, 1, 1), lambda b, h, c: (b, h, 0, 0)),
            pl.BlockSpec((1, dv), lambda b, h, c: (0, h)),
        ],
        out_specs=[
            pl.BlockSpec((1, chunk, dv), lambda b, h, c: (b, c, h)),
            pl.BlockSpec((1, 1, dk, dv), lambda b, h, c: (b, h, 0, 0)),
            pl.BlockSpec((1, 1, 1, dk), lambda b, h, c: (b, h, 0, 0)),
            pl.BlockSpec((1, 1, 1, 1), lambda b, h, c: (b, h, 0, 0)),
        ],
        out_shape=[
            jax.ShapeDtypeStruct((B, L, H * dv), BF16),
            jax.ShapeDtypeStruct((B, H, dk, dv), F32),
            jax.ShapeDtypeStruct((B, H, 1, dk), F32),
            jax.ShapeDtypeStruct((B, H, 1, 1), F32),
        ],
        scratch_shapes=[pltpu.VMEM((dk, dv), F32), pltpu.VMEM((1, dk), F32), pltpu.VMEM((1, 1), F32)],
        compiler_params=_cparams(("arbitrary",) * 3),
        name="mlstm",
    )(z, z, z, z, gcol, grow, bcol, brow, C0, n0.reshape(B, H, 1, dk), m0.reshape(B, H, 1, 1),
      g_out.reshape(1, H * dv))
    hs, C, n, m = outs
    return hs, C, n.reshape(B, H, dk), m.reshape(B, H)


def _headnorm_kernel(q_ref, k_ref, gq_ref, gk_ref, qo_ref, ko_ref, *, dh, q_scale):
    nh = q_ref.shape[2] // dh
    for h in range(nh):
        sl = slice(h * dh, (h + 1) * dh)
        for src, g, dst, sc in ((q_ref, gq_ref, qo_ref, q_scale), (k_ref, gk_ref, ko_ref, 1.0)):
            x = src[0, :, sl]
            ms = jnp.mean(x * x, axis=-1, keepdims=True)
            y = x * lax.rsqrt(ms + EPS) * g[:, sl]
            dst[0, :, sl] = (y * sc if sc != 1.0 else y).astype(dst.dtype)


def headnorm(z, gq_full, gk_full, dh, q_scale):
    B, L, W2 = z.shape
    W = W2 // 2
    tl = _pick(L, (256, 128))
    return pl.pallas_call(
        functools.partial(_headnorm_kernel, dh=dh, q_scale=q_scale),
        grid=(B, L // tl),
        in_specs=[pl.BlockSpec((1, tl, W), lambda b, i: (b, i, 0)),
                  pl.BlockSpec((1, tl, W), lambda b, i: (b, i, 1)),
                  pl.BlockSpec((1, W), lambda b, i: (0, 0)),
                  pl.BlockSpec((1, W), lambda b, i: (0, 0))],
        out_specs=[pl.BlockSpec((1, tl, W), lambda b, i: (b, i, 0)),
                   pl.BlockSpec((1, tl, W), lambda b, i: (b, i, 0))],
        out_shape=[jax.ShapeDtypeStruct((B, L, W), BF16), jax.ShapeDtypeStruct((B, L, W), F32)],
        compiler_params=_cparams(("arbitrary", "arbitrary")),
        name="headnorm",
    )(z, z, gq_full.reshape(1, W), gk_full.reshape(1, W))


def _rel_bucket(dist):
    exact = REL_BUCKETS // 2
    d_f = jnp.maximum(dist, 1).astype(F32)
    large = exact + (jnp.log(d_f / exact) / math.log(REL_MAX_DIST / exact) * (REL_BUCKETS - exact)).astype(jnp.int32)
    large = jnp.minimum(large, REL_BUCKETS - 1)
    return jnp.where(dist < exact, dist, large)


def _bias_of_dist(rel_bias, dist):
    bucket = _rel_bucket(jnp.maximum(dist, 0))
    out = jnp.zeros((rel_bias.shape[1],) + dist.shape, F32)
    for b in range(REL_BUCKETS):
        out = jnp.where(bucket[None] == b, rel_bias[b].reshape((-1,) + (1,) * dist.ndim), out)
    return out


def _np_multiplicity(dist):
    cnt = np.zeros(dist.shape, np.int64)
    for w, d in DIL_PAIRS:
        cnt = cnt + ((dist % d == 0) & (dist <= w) & (dist >= 0))
    return cnt


def _swa_bias(rel_bias, dist_np):
    cnt = _np_multiplicity(dist_np)
    logc = jnp.log(jnp.asarray(np.maximum(cnt, 1), F32))
    bias = _bias_of_dist(rel_bias, jnp.asarray(np.maximum(dist_np, 0), jnp.int32))
    return jnp.where(jnp.asarray(cnt > 0)[None], bias + logc[None], NEG), cnt > 0


def _causal_bias(rel_bias, dist_np):
    bias = _bias_of_dist(rel_bias, jnp.asarray(np.maximum(dist_np, 0), jnp.int32))
    return jnp.where(jnp.asarray(dist_np >= 0)[None], bias, NEG), dist_np >= 0


def _toeplitz_dist_np(n_off, t):
    o = np.arange(n_off)[:, None, None]
    i = np.arange(t)[None, :, None]
    j = np.arange(t)[None, None, :]
    return o * t + i - j


def _toeplitz_bias(rel_bias, n_off, t, bias_fn):
    dist = np.arange(-(t - 1), n_off * t)
    f, _ = bias_fn(rel_bias, dist)
    maps = f.shape[0]
    idx = np.arange(n_off)[:, None] * t + np.arange(2 * t - 1)[None, :]
    b = jnp.pad(f[:, idx][..., ::-1], ((0, 0), (0, 0), (0, 1)))
    r = jnp.tile(b, (1, 1, t))[..., :t * (2 * t - 1)].reshape(maps, n_off, t, 2 * t - 1)
    return r[..., t - 1:]


def _flash_kernel(qi_ref, ki_ref, q_ref, k_ref, v_ref, bm_ref, o_ref, m_s, l_s, acc_s):
    p = pl.program_id(1)
    qi = qi_ref[p]
    ki = ki_ref[p]
    nb, tq, dv = acc_s.shape
    tk = k_ref.shape[1]

    @pl.when(ki == 0)
    def _():
        m_s[...] = jnp.full(m_s.shape, NEG, F32)
        l_s[...] = jnp.zeros(l_s.shape, F32)
        acc_s[...] = jnp.zeros(acc_s.shape, F32)

    bm = bm_ref[0, 0]
    for b in range(nb):
        s = _bdot_nt(q_ref[b], k_ref[b]) + bm
        m_prev = m_s[b]
        m_next = jnp.maximum(m_prev, jnp.max(s, axis=1, keepdims=True))
        pr = jnp.exp(s - jnp.tile(m_next, (1, tk // LANES)))
        alpha = jnp.exp(m_prev - m_next)
        l_s[b] = alpha * l_s[b] + jnp.sum(pr, axis=1, keepdims=True)
        acc_s[b] = acc_s[b] * jnp.tile(alpha, (1, dv // LANES)) + _bdot(pr, v_ref[b])
        m_s[b] = m_next

    @pl.when(ki == qi)
    def _():
        for b in range(nb):
            o_ref[b] = (acc_s[b] / jnp.tile(l_s[b], (1, dv // LANES))).astype(o_ref.dtype)


def flash_attention(q, k, v, bm_tab, *, n_units, dqk, dv, qcol, kcol, vcol, bmap, t, out_dtype):
    B, L = q.shape[:2]
    nq = L // t
    pairs = [(qi, ki) for qi in range(nq) for ki in range(qi + 1)]
    qi_arr = jnp.asarray(np.array([p[0] for p in pairs], np.int32))
    ki_arr = jnp.asarray(np.array([p[1] for p in pairs], np.int32))
    grid_spec = pltpu.PrefetchScalarGridSpec(
        num_scalar_prefetch=2,
        grid=(n_units, len(pairs)),
        in_specs=[
            pl.BlockSpec((B, t, dqk), lambda a, p, qa, ka: (0, qa[p], qcol(a))),
            pl.BlockSpec((B, t, dqk), lambda a, p, qa, ka: (0, ka[p], kcol(a))),
            pl.BlockSpec((B, t, dv), lambda a, p, qa, ka: (0, ka[p], vcol(a))),
            pl.BlockSpec((1, 1, t, t), lambda a, p, qa, ka: (bmap(a), qa[p] - ka[p], 0, 0)),
        ],
        out_specs=pl.BlockSpec((B, t, dv), lambda a, p, qa, ka: (0, qa[p], a)),
        scratch_shapes=[pltpu.VMEM((B, t, LANES), F32), pltpu.VMEM((B, t, LANES), F32),
                        pltpu.VMEM((B, t, dv), F32)],
    )
    return pl.pallas_call(
        _flash_kernel,
        grid_spec=grid_spec,
        out_shape=jax.ShapeDtypeStruct((B, L, n_units * dv), out_dtype),
        compiler_params=_cparams(("arbitrary", "arbitrary")),
        name="flash_attention",
    )(qi_arr, ki_arr, q, k, v, bm_tab)


def _diff_lambda(lam_ref, lam_init):
    lq1, lk1, lq2, lk2 = lam_ref[0:1, :], lam_ref[1:2, :], lam_ref[2:3, :], lam_ref[3:4, :]
    return (jnp.exp(jnp.sum(lq1 * lk1, axis=-1, keepdims=True))
            - jnp.exp(jnp.sum(lq2 * lk2, axis=-1, keepdims=True)) + lam_init)


def _diff_combine_kernel(o_ref, lam_ref, g_ref, out_ref, *, dv, lam_init):
    lam = _diff_lambda(lam_ref, lam_init)
    nh = out_ref.shape[2] // dv
    for h in range(nh):
        o0 = o_ref[0, :, (2 * h) * dv:(2 * h + 1) * dv]
        o1 = o_ref[0, :, (2 * h + 1) * dv:(2 * h + 2) * dv]
        d = o0 - lam * o1
        ms = jnp.mean(d * d, axis=-1, keepdims=True)
        out_ref[0, :, h * dv:(h + 1) * dv] = (d * lax.rsqrt(ms + EPS) * g_ref[...] * (1.0 - lam_init)
                                              ).astype(out_ref.dtype)


def diff_combine(o, lam4, g_out, lam_init):
    B, L, W2 = o.shape
    dv = g_out.shape[0]
    W = W2 // 2
    tl = _pick(L, (256, 128))
    return pl.pallas_call(
        functools.partial(_diff_combine_kernel, dv=dv, lam_init=lam_init),
        grid=(B, L // tl),
        in_specs=[pl.BlockSpec((1, tl, W2), lambda b, i: (b, i, 0)),
                  pl.BlockSpec(lam4.shape, lambda b, i: (0, 0)),
                  pl.BlockSpec((1, dv), lambda b, i: (0, 0))],
        out_specs=pl.BlockSpec((1, tl, W), lambda b, i: (b, i, 0)),
        out_shape=jax.ShapeDtypeStruct((B, L, W), BF16),
        compiler_params=_cparams(("arbitrary", "arbitrary")),
        name="diff_combine",
    )(o, lam4, g_out.reshape(1, dv))


def _decode_kernel(*refs, n_scalar, n_chunks, diff, lam_init, n_heads, n_new):
    refs = refs[n_scalar:]
    q_ref = refs[0]
    k_refs = refs[1:1 + n_chunks]
    v_refs = refs[1 + n_chunks:1 + 2 * n_chunks]
    bm_refs = refs[1 + 2 * n_chunks:1 + 3 * n_chunks]
    rest = refs[1 + 3 * n_chunks:]
    if diff:
        kn_ref, vn_ref, bmn_ref, lam_ref, g_ref, out_ref, m_s, l_s, acc_s = rest
    else:
        kn_ref, vn_ref, bmn_ref, out_ref, m_s, l_s, acc_s = rest
    p = pl.program_id(1)

    @pl.when(p == 0)
    def _():
        m_s[...] = jnp.full(m_s.shape, NEG, F32)
        l_s[...] = jnp.zeros(l_s.shape, F32)
        acc_s[...] = jnp.zeros(acc_s.shape, F32)

    q = q_ref[0]

    def absorb(k2s, v2s, bms):
        ss = [_bdot_nt(q, k2) + bm for k2, bm in zip(k2s, bms)]
        m_old = m_s[...]
        m_new = m_old
        for s in ss:
            m_new = jnp.maximum(m_new, jnp.max(s, axis=-1, keepdims=True))
        alpha = jnp.exp(m_old - m_new)
        l_new = alpha * l_s[...]
        acc = alpha * acc_s[...]
        for s, v2 in zip(ss, v2s):
            pr = jnp.exp(s - m_new)
            l_new = l_new + jnp.sum(pr, axis=-1, keepdims=True)
            acc = acc + _bdot(pr, v2)
        l_s[...] = l_new
        acc_s[...] = acc
        m_s[...] = m_new

    def rows2d(ref):
        x = ref[0]
        return x.reshape(x.shape[0] * x.shape[1], x.shape[2])

    absorb([rows2d(r) for r in k_refs], [rows2d(r) for r in v_refs], [r[0] for r in bm_refs])

    @pl.when(p == pl.num_programs(1) - 1)
    def _():
        absorb([kn_ref[0]], [vn_ref[0]], [bmn_ref[...]])
        o = acc_s[...] / l_s[...]
        if diff:
            half = n_heads * n_new
            lam = _diff_lambda(lam_ref, lam_init)
            d = o[0:half] - lam * o[half:2 * half]
            ms = jnp.mean(d * d, axis=-1, keepdims=True)
            o = d * lax.rsqrt(ms + EPS) * g_ref[...] * (1.0 - lam_init)
        dv = o.shape[1]
        for h in range(n_heads):
            out_ref[0, :, h * dv:(h + 1) * dv] = o[h * n_new:(h + 1) * n_new].astype(out_ref.dtype)


def _head_expand(bias_rows, row_head, n_heads):
    ok = jnp.asarray(row_head[:, None] == np.arange(n_heads)[None, :])
    out = jnp.where(ok[:, None, :], bias_rows[..., None], NEG)
    return out.reshape(bias_rows.shape[:-1] + (bias_rows.shape[-1] * n_heads,))


def swa_decode(qn, k_new, v_new, cache_k, cache_v, rel_bias):
    B, Wb, H, dh = cache_k.shape
    T = qn.shape[1]
    R = H * T
    tk = _pick(Wb, (512, 256, 128))
    nblk = Wb // tk
    q2 = jnp.transpose(qn.reshape(B, T, H, dh), (0, 2, 1, 3)).reshape(B, R, dh)
    row_head = np.arange(R) // T
    dist_np = (Wb + np.arange(T))[:, None] - np.arange(Wb)[None, :]
    bias, ok = _swa_bias(rel_bias, dist_np)
    assert ok[:, :tk].any(axis=1).all()
    bm = _head_expand(bias.reshape(R, Wb), row_head, H)
    bm = jnp.transpose(bm.reshape(R, nblk, tk * H), (1, 0, 2))
    dn = np.arange(T)[:, None] - np.arange(T)[None, :]
    bias_n, _ = _swa_bias(rel_bias, dn)
    bmn = _head_expand(bias_n.reshape(R, T), row_head, H)
    kn2 = k_new.reshape(B, T * H, dh)
    vn2 = v_new.reshape(B, T * H, dh)
    return pl.pallas_call(
        functools.partial(_decode_kernel, n_scalar=0, n_chunks=1, diff=False, lam_init=0.0, n_heads=H, n_new=T),
        grid=(B, nblk),
        in_specs=[
            pl.BlockSpec((1, R, dh), lambda b, p: (b, 0, 0)),
            pl.BlockSpec((1, tk, H, dh), lambda b, p: (b, p, 0, 0)),
            pl.BlockSpec((1, tk, H, dh), lambda b, p: (b, p, 0, 0)),
            pl.BlockSpec((1, R, tk * H), lambda b, p: (p, 0, 0)),
            pl.BlockSpec((1, T * H, dh), lambda b, p: (b, 0, 0)),
            pl.BlockSpec((1, T * H, dh), lambda b, p: (b, 0, 0)),
            pl.BlockSpec((R, T * H), lambda b, p: (0, 0)),
        ],
        out_specs=pl.BlockSpec((1, T, H * dh), lambda b, p: (b, 0, 0)),
        out_shape=jax.ShapeDtypeStruct((B, T, H * dh), BF16),
        scratch_shapes=[pltpu.VMEM((R, 1), F32), pltpu.VMEM((R, 1), F32), pltpu.VMEM((R, dh), F32)],
        compiler_params=_cparams(("arbitrary", "arbitrary")),
        name="swa_decode",
    )(q2, cache_k, cache_v, bm, kn2, vn2, bmn)


def _np_rel_bucket(dist):
    exact = REL_BUCKETS // 2
    d_f = np.maximum(dist, 1).astype(np.float32)
    large = exact + (np.log(d_f / np.float32(exact)) / np.float32(math.log(REL_MAX_DIST / exact))
                     * np.float32(REL_BUCKETS - exact)).astype(np.int32)
    large = np.minimum(large, REL_BUCKETS - 1)
    return np.where(dist < exact, dist, large)


def diff_paged(qn, k_new, v_new, cache_k, cache_v, page_table, rel_bias, lam4, g_out, lam_init):
    B, T, H, _, dh = qn.shape
    dv = v_new.shape[-1]
    n_pages = page_table.shape[1]
    page = cache_k.shape[1]
    P = n_pages * page
    R = 2 * H * T
    pp = next(c for c in (4, 2, 1) if n_pages % c == 0)
    qt = jnp.transpose(qn, (0, 3, 2, 1, 4))
    zeros = jnp.zeros_like(qt[:, 0])
    qm = jnp.concatenate([jnp.concatenate([qt[:, 0], zeros], axis=-1),
                          jnp.concatenate([zeros, qt[:, 1]], axis=-1)], axis=1)
    qm = qm.reshape(B, R, 2 * dh)
    pg = np.arange(n_pages)[:, None, None]
    tt = np.arange(T)[None, :, None]
    ii = np.arange(page)[None, None, :]
    dist_np = P + tt - pg * page - ii
    bucket_np = _np_rel_bucket(dist_np)
    far = np.all(bucket_np == bucket_np[0:1], axis=(1, 2))
    near_pages = [int(x) for x in np.nonzero(~far)[0]]
    tile_pages = [0] + near_pages
    tile_of_page = np.zeros((n_pages,), np.int32)
    for ti, pgi in enumerate(near_pages):
        tile_of_page[pgi] = ti + 1
    nt = len(tile_pages)
    bias_t = _bias_of_dist(rel_bias, jnp.asarray(dist_np[tile_pages], jnp.int32))
    bias_rows = jnp.transpose(bias_t.reshape(2, H, nt, T, page), (2, 0, 1, 3, 4)).reshape(nt, R, page)
    row_head = (np.arange(R) // T) % H
    bm = _head_expand(bias_rows, row_head, H)
    dn = np.arange(T)[:, None] - np.arange(T)[None, :]
    bias_n, _ = _causal_bias(rel_bias, dn)
    bmn = _head_expand(bias_n.reshape(R, T), row_head, H)
    kn2 = k_new.reshape(B, T * H, 2 * dh)
    vn2 = v_new.reshape(B, T * H, dv)

    def kv_spec(j, d):
        return pl.BlockSpec((1, page, H, d), lambda b, p, pt, tl: (pt[b * n_pages + p * pp + j], 0, 0, 0))

    def bm_spec(j):
        return pl.BlockSpec((1, R, page * H), lambda b, p, pt, tl: (tl[p * pp + j], 0, 0))

    grid_spec = pltpu.PrefetchScalarGridSpec(
        num_scalar_prefetch=2,
        grid=(B, n_pages // pp),
        in_specs=(
            [pl.BlockSpec((1, R, 2 * dh), lambda b, p, pt, tl: (b, 0, 0))]
            + [kv_spec(j, 2 * dh) for j in range(pp)]
            + [kv_spec(j, dv) for j in range(pp)]
            + [bm_spec(j) for j in range(pp)]
            + [pl.BlockSpec((1, T * H, 2 * dh), lambda b, p, pt, tl: (b, 0, 0)),
               pl.BlockSpec((1, T * H, dv), lambda b, p, pt, tl: (b, 0, 0)),
               pl.BlockSpec((R, T * H), lambda b, p, pt, tl: (0, 0)),
               pl.BlockSpec(lam4.shape, lambda b, p, pt, tl: (0, 0)),
               pl.BlockSpec((1, dv), lambda b, p, pt, tl: (0, 0))]),
        out_specs=pl.BlockSpec((1, T, H * dv), lambda b, p, pt, tl: (b, 0, 0)),
        scratch_shapes=[pltpu.VMEM((R, 1), F32), pltpu.VMEM((R, 1), F32), pltpu.VMEM((R, dv), F32)],
    )
    return pl.pallas_call(
        functools.partial(_decode_kernel, n_scalar=2, n_chunks=pp, diff=True, lam_init=lam_init,
                          n_heads=H, n_new=T),
        grid_spec=grid_spec,
        out_shape=jax.ShapeDtypeStruct((B, T, H * dv), BF16),
        compiler_params=_cparams(("arbitrary", "arbitrary")),
        name="diff_paged",
    )(page_table.reshape(-1), jnp.asarray(tile_of_page), qm, *([cache_k] * pp), *([cache_v] * pp),
      *([bm] * pp), kn2, vn2, bmn, lam4, g_out.reshape(1, dv))


def _rglru_kernel(y_ref, x_ref, cs_ref, h0_ref, cw_ref, cb_ref, wa_ref, ba_ref, wx_ref, bx_ref, lam_ref,
                  out_ref, conv_ref, hl_ref, xpad, hcar, *, l_valid):
    i = pl.program_id(1)
    T, D = x_ref.shape[1], x_ref.shape[2]

    @pl.when(i == 0)
    def _():
        xpad[0:8, :] = cs_ref[0]
        hcar[...] = h0_ref[0]

    xpad[8:8 + T, :] = x_ref[0]
    conv = cb_ref[...] + cw_ref[CONV_W - 1:CONV_W, :] * xpad[8:8 + T, :]
    for j in range(CONV_W - 1):
        s = CONV_W - 1 - j
        conv = conv + cw_ref[j:j + 1, :] * xpad[8 - s:8 - s + T, :]
    nb = wa_ref.shape[0]
    bs = D // nb
    r_parts, i_parts = [], []
    for n in range(nb):
        xb = conv[:, n * bs:(n + 1) * bs]
        r_parts.append(_bdot(xb, wa_ref[n]))
        i_parts.append(_bdot(xb, wx_ref[n]))
    r = jax.nn.sigmoid(jnp.concatenate(r_parts, axis=-1) + ba_ref[...])
    ig = jax.nn.sigmoid(jnp.concatenate(i_parts, axis=-1) + bx_ref[...])
    log_a = -LRU_C * r * _softplus(-lam_ref[...])
    a = jnp.exp(log_a)
    bb = jnp.sqrt(-jnp.tanh(log_a) * (1.0 + a * a)) * ig * conv
    rowid = lax.broadcasted_iota(jnp.int32, (T, D), 0)
    s = 1
    while s < T:
        keep = rowid >= s
        a_sh = pltpu.roll(a, s, 0)
        b_sh = pltpu.roll(bb, s, 0)
        bb = jnp.where(keep, a * b_sh + bb, bb)
        a = jnp.where(keep, a * a_sh, a)
        s *= 2
    hs = a * hcar[...] + bb
    out_ref[0] = (hs * _gelu(y_ref[0])).astype(out_ref.dtype)
    hcar[...] = hs[T - 1:T, :]
    tail = xpad[T:T + 8, :]
    xpad[0:8, :] = tail

    @pl.when(i == pl.num_programs(1) - 1)
    def _():
        hl_ref[0] = hs[l_valid - 1:l_valid, :]
        conv_ref[0] = xpad[8 + l_valid - (CONV_W - 1):8 + l_valid, :]


def rglru_core(z, conv_state, h0, conv_w, conv_b, w_a, b_a, w_x, b_x, lam, l_valid_last, tblk):
    B, L, D2 = z.shape
    D = D2 // 2
    cs8 = jnp.concatenate([jnp.zeros((B, 8 - (CONV_W - 1), D), F32), conv_state.astype(F32)], axis=1)
    vec = lambda a: a.reshape(1, D)
    out, conv_new, h_last = pl.pallas_call(
        functools.partial(_rglru_kernel, l_valid=l_valid_last),
        grid=(B, L // tblk),
        in_specs=[
            pl.BlockSpec((1, tblk, D), lambda b, i: (b, i, 0)),
            pl.BlockSpec((1, tblk, D), lambda b, i: (b, i, 1)),
            pl.BlockSpec((1, 8, D), lambda b, i: (b, 0, 0)),
            pl.BlockSpec((1, 1, D), lambda b, i: (b, 0, 0)),
            pl.BlockSpec((CONV_W, D), lambda b, i: (0, 0)),
            pl.BlockSpec((1, D), lambda b, i: (0, 0)),
            pl.BlockSpec(w_a.shape, lambda b, i: (0, 0, 0)),
            pl.BlockSpec((1, D), lambda b, i: (0, 0)),
            pl.BlockSpec(w_x.shape, lambda b, i: (0, 0, 0)),
            pl.BlockSpec((1, D), lambda b, i: (0, 0)),
            pl.BlockSpec((1, D), lambda b, i: (0, 0)),
        ],
        out_specs=[
            pl.BlockSpec((1, tblk, D), lambda b, i: (b, i, 0)),
            pl.BlockSpec((1, CONV_W - 1, D), lambda b, i: (b, 0, 0)),
            pl.BlockSpec((1, 1, D), lambda b, i: (b, 0, 0)),
        ],
        out_shape=[
            jax.ShapeDtypeStruct((B, L, D), BF16),
            jax.ShapeDtypeStruct((B, CONV_W - 1, D), F32),
            jax.ShapeDtypeStruct((B, 1, D), F32),
        ],
        scratch_shapes=[pltpu.VMEM((tblk + 8, D), F32), pltpu.VMEM((1, D), F32)],
        compiler_params=_cparams(("arbitrary", "arbitrary")),
        name="rglru",
    )(z, z, cs8, h0.astype(F32).reshape(B, 1, D), conv_w, vec(conv_b), w_a.astype(BF16), vec(b_a),
      w_x.astype(BF16), vec(b_x), vec(lam))
    return out, conv_new, h_last.reshape(B, D)


def _topk_rows(s, k):
    n, tb = s.shape
    rowf = lax.broadcasted_iota(jnp.int32, (n, tb), 0).astype(F32)
    vals, idxs = [], []
    for _ in range(k):
        m = jnp.max(s, axis=0, keepdims=True)
        idx = jnp.min(jnp.where(s == m, rowf, float(n)), axis=0, keepdims=True)
        vals.append(m)
        idxs.append(idx)
        s = jnp.where(rowf == idx, -jnp.inf, s)
    return jnp.concatenate(vals, axis=0), jnp.concatenate(idxs, axis=0)


def _pair_plan(k):
    full, small = [], []
    a = 0
    while a < k and k // (a + 1) > 1:
        nb, b0 = k // (a + 1), 0
        while nb - b0 >= 8:
            full.append((a, b0))
            b0 += 8
        if nb > b0:
            small.append((a, b0, nb - b0))
        a += 1
    a0 = a
    assert k % 8 == 0 and (k - a0) % 8 == 0 and (k & (k - 1)) == 0
    bins = []
    for a_, b0, n in sorted(small, key=lambda p: -p[2]):
        for bn in bins:
            used = sum(p[3] for p in bn)
            if used + n <= 8:
                bn.append((a_, b0, used, n))
                break
        else:
            bins.append([(a_, b0, 0, n)])
    code = []
    for a_, b0 in full:
        code += [a_ * k + b0 + r for r in range(8)]
    for bn in bins:
        rows = [k * k] * 8
        for a_, b0, off, n in bn:
            for r in range(n):
                rows[off + r] = a_ * k + b0 + r
        code += rows
    code += [a_ * k for a_ in range(a0, k)]
    return full, bins, a0, np.array(code, np.float32)


def _route_kernel(h_ref, wq_ref, k1_ref, k2_ref, code_ref, e1_ref, e2_ref, g_ref, *, topk):
    k = topk
    q = jnp.dot(h_ref[...], wq_ref[...], preferred_element_type=F32)
    dk2 = k1_ref.shape[2]
    tb = q.shape[0]
    nt = (((1,), (1,)), ((), ()))
    s1 = lax.dot_general(k1_ref[0], q[:, :dk2], nt, precision=HIGHEST, preferred_element_type=F32)
    s2 = lax.dot_general(k2_ref[0], q[:, dk2:], nt, precision=HIGHEST, preferred_element_type=F32)
    sv1, si1 = _topk_rows(s1, k)
    sv2, si2 = _topk_rows(s2, k)
    full, bins, a0, _ = _pair_plan(k)
    row8 = lax.broadcasted_iota(jnp.int32, (8, tb), 0)
    groups = [sv1[a:a + 1] + sv2[b0:b0 + 8] for a, b0 in full]
    for bn in bins:
        v = jnp.full((8, tb), -jnp.inf, F32)
        for a, b0, off, n in bn:
            piece = sv1[a:a + 1] + (pltpu.roll(sv2[b0:b0 + 8], off, 0) if off else sv2[b0:b0 + 8])
            v = jnp.where((row8 >= off) & (row8 < off + n), piece, v)
        groups.append(v)
    for c in range(a0, k, 8):
        groups.append(sv1[c:c + 8] + sv2[0:1])
    cand = jnp.concatenate(groups, axis=0)
    code = code_ref[...]
    vals, poss = [], []
    for _ in range(k):
        m = jnp.max(cand, axis=0, keepdims=True)
        pos = jnp.min(jnp.where(cand == m, code, float(k * k)), axis=0, keepdims=True)
        vals.append(m)
        poss.append(pos)
        cand = jnp.where(code == pos, -jnp.inf, cand)
    val = jnp.concatenate(vals, axis=0)
    pos = jnp.concatenate(poss, axis=0)
    ra = jnp.floor(pos * (1.0 / k))
    rb = pos - k * ra
    e1 = jnp.zeros((k, tb), F32)
    e2 = jnp.zeros((k, tb), F32)
    for r in range(k):
        e1 = jnp.where(ra == float(r), si1[r:r + 1], e1)
        e2 = jnp.where(rb == float(r), si2[r:r + 1], e2)
    e1_ref[0] = e1
    e2_ref[0] = e2
    ex = jnp.exp(val - val[0:1])
    g_ref[0] = ex / jnp.sum(ex, axis=0, keepdims=True)


def peer_route(h, wq, keys):
    N, D = h.shape
    nh, _, nk, dk2 = keys.shape
    tb = _pick(N, (512, 256, 128))
    code = _pair_plan(TOPK_P)[3]
    code = jnp.asarray(np.broadcast_to(code[:, None], (code.shape[0], tb)))
    shp = jax.ShapeDtypeStruct((nh, TOPK_P, N), F32)
    o_spec = pl.BlockSpec((1, TOPK_P, tb), lambda hh, i: (hh, 0, i))
    return pl.pallas_call(
        functools.partial(_route_kernel, topk=TOPK_P),
        grid=(nh, N // tb),
        in_specs=[pl.BlockSpec((tb, D), lambda hh, i: (i, 0)),
                  pl.BlockSpec((D, 2 * dk2), lambda hh, i: (0, hh)),
                  pl.BlockSpec((1, nk, dk2), lambda hh, i: (2 * hh, 0, 0)),
                  pl.BlockSpec((1, nk, dk2), lambda hh, i: (2 * hh + 1, 0, 0)),
                  pl.BlockSpec(code.shape, lambda hh, i: (0, 0))],
        out_specs=[o_spec, o_spec, o_spec],
        out_shape=[shp, shp, shp],
        compiler_params=_cparams(("arbitrary", "arbitrary")),
        name="peer_route",
    )(h, wq, keys.reshape(nh * 2, nk, dk2), keys.reshape(nh * 2, nk, dk2), code)


def _route_build_kernel(e1_ref, e2_ref, g_ref, sub_ref, o_ref, *, nkeys):
    sub = sub_ref[...][None]
    one = jnp.ones((), BF16)
    zero = jnp.zeros((), BF16)
    at = jnp.where(e1_ref[...].astype(BF16) == sub, one, zero)
    bt = jnp.where(e2_ref[...].astype(BF16) == sub, g_ref[...].astype(BF16), zero)
    g3 = jnp.einsum('tik,tjk->tij', at, bt, preferred_element_type=F32)
    gt = pltpu.einshape('tij->itj', g3)
    for i in range(nkeys):
        o_ref[:, i * nkeys:(i + 1) * nkeys] = gt[i].astype(o_ref.dtype)


def peer_build(e1, e2, g, nkeys):
    nh, k, N = e1.shape
    ks = nh * k
    slot = lambda a: jnp.transpose(a, (2, 0, 1)).reshape(N, 1, ks)
    tb = _pick(N, (64, 32))
    spec = pl.BlockSpec((tb, 1, ks), lambda i: (i, 0, 0))
    sub = jnp.asarray(np.broadcast_to(np.arange(nkeys, dtype=np.float32)[:, None], (nkeys, ks)), BF16)
    return pl.pallas_call(
        functools.partial(_route_build_kernel, nkeys=nkeys),
        grid=(N // tb,),
        in_specs=[spec, spec, spec, pl.BlockSpec((nkeys, ks), lambda i: (0, 0))],
        out_specs=pl.BlockSpec((tb, nkeys * nkeys), lambda i: (i, 0)),
        out_shape=jax.ShapeDtypeStruct((N, nkeys * nkeys), BF16),
        compiler_params=_cparams(("arbitrary",)),
        name="peer_build",
    )(slot(e1), slot(e2), slot(g), sub)


def _peer_kernel(x_ref, ut_ref, v_ref, g_ref, r_ref, gt_ref, o_ref, acc_s):
    e = pl.program_id(1)

    @pl.when(e == 0)
    def _():
        acc_s[...] = jnp.zeros(acc_s.shape, F32)

    s = jnp.dot(x_ref[...], ut_ref[...], preferred_element_type=F32)
    p = (_gelu(s) * g_ref[...].astype(F32)).astype(BF16)
    acc_s[...] += jnp.dot(p, v_ref[...], preferred_element_type=F32)

    @pl.when(e == pl.num_programs(1) - 1)
    def _():
        o_ref[...] = r_ref[...] + gt_ref[...] * acc_s[...]


def _peer_kernel_shared_gate(x_ref, ut_ref, v_ref, g_ref, r_ref, gt_ref, o_ref, acc_s):
    _peer_kernel(x_ref, ut_ref, v_ref, g_ref, r_ref, gt_ref.at[0], o_ref, acc_s)


def peer_dense(h, ut, v, G, res, gate, rows_per_gate):
    N, D = h.shape
    E = ut.shape[1]
    tb = _pick(N, (512, 256, 128))
    eb = _pick(E, (1024, 512, 256, 128))
    if rows_per_gate == 1:
        kern = _peer_kernel
        g_spec = pl.BlockSpec((tb, D), lambda i, e: (i, 0))
        gate_arr = gate
    else:
        assert rows_per_gate % tb == 0
        per = rows_per_gate // tb
        kern = _peer_kernel_shared_gate
        g_spec = pl.BlockSpec((1, 1, D), lambda i, e: (i // per, 0, 0))
        gate_arr = gate.reshape(gate.shape[0], 1, D)
    return pl.pallas_call(
        kern,
        grid=(N // tb, E // eb),
        in_specs=[
            pl.BlockSpec((tb, D), lambda i, e: (i, 0)),
            pl.BlockSpec((D, eb), lambda i, e: (0, e)),
            pl.BlockSpec((eb, D), lambda i, e: (e, 0)),
            pl.BlockSpec((tb, eb), lambda i, e: (i, e)),
            pl.BlockSpec((tb, D), lambda i, e: (i, 0)),
            g_spec,
        ],
        out_specs=pl.BlockSpec((tb, D), lambda i, e: (i, 0)),
        out_shape=jax.ShapeDtypeStruct((N, D), F32),
        scratch_shapes=[pltpu.VMEM((tb, D), F32)],
        compiler_params=_cparams(("arbitrary", "arbitrary")),
        name="peer_dense",
    )(h, ut, v, G, res, gate_arr)


def _pad_rows(a, n, value=0.0, axis=1):
    pad = [(0, 0)] * a.ndim
    pad[axis] = (0, n - a.shape[axis])
    return jnp.pad(a, pad, constant_values=value)


def kernel(x_prompt, x_sample, state_mlstm_C, state_mlstm_n, state_mlstm_m, cache_swa_k, cache_swa_v,
           cache_diff_k, cache_diff_v, state_rglru_conv, state_rglru_h, page_table, c_prompt, c_sample,
           w_ada, b_ada, g_norm_mix, g_norm_ffn, rel_bias,
           mlstm_w_in, mlstm_b_gates, mlstm_g_out, mlstm_w_out,
           swa_w_in, swa_g_q, swa_g_k, swa_w_out,
           diff_w_in, diff_g_q, diff_g_k, diff_lam_q1, diff_lam_k1, diff_lam_q2, diff_lam_k2, diff_g_out, diff_w_out,
           rglru_w_in, rglru_conv_w, rglru_conv_b, rglru_w_a, rglru_b_a, rglru_w_x, rglru_b_x, rglru_lambda, rglru_w_out,
           peer_w_q, peer_keys, peer_u, peer_v):
    xp, xs = x_prompt, x_sample
    Bp, S, D = xp.shape
    Bs, T, _ = xs.shape
    depth = w_ada.shape[0]
    Ns = Bs * T

    c_all = jnp.concatenate([c_prompt, c_sample], axis=0)
    mod = ada_all(c_all, w_ada, b_ada)
    mod = mod.reshape(depth, Bp + Bs, 6, 1, D)

    def mods(i, which):
        m = mod[i, :, which]
        return m[:Bp], m[Bp:]

    def out_proj(hp, hs, w, gate_p, gate_s):
        nonlocal xp, xs
        wb = w.astype(BF16)
        xp = matmul(hp, wb, res=xp, gate=gate_p)
        gs_rows = jnp.broadcast_to(gate_s, (Bs, T, D)).reshape(1, Ns, D)
        xs = matmul(hs.reshape(1, Ns, -1), wb, res=xs.reshape(1, Ns, D), gate=gs_rows).reshape(Bs, T, D)

    for i in range(depth):
        kind = i % 4
        sh_p, sh_s = mods(i, 0)
        sc_p, sc_s = mods(i, 1)
        gt_p, gt_s = mods(i, 2)
        hp = modulate(xp, g_norm_mix[i], sh_p, sc_p)
        hs = modulate(xs, g_norm_mix[i], sh_s, sc_s)
        hs_flat = hs.reshape(1, Ns, D)
        if kind == 0:
            H = NH_A
            dk = state_mlstm_C.shape[2]
            dv = state_mlstm_C.shape[3]
            nmain = 2 * H * dk + 2 * H * dv
            w_main = mlstm_w_in[:, :nmain].astype(BF16)
            w_gate = mlstm_w_in[:, nmain:].astype(BF16)
            zp = matmul(hp, w_main)
            gp = matmul(hp, w_gate)
            zs = matmul(hs_flat, w_main).reshape(Bs, T, nmain)
            gs = matmul(hs_flat, w_gate).reshape(Bs, T, 2 * H)
            chunk_p = _pick(S, (CHUNK_A, 128))
            op, mC_p, mn_p, mm_p = mlstm_core(
                zp, gp, mlstm_b_gates, jnp.zeros((Bp, H, dk, dv), F32), jnp.zeros((Bp, H, dk), F32),
                jnp.full((Bp, H), M_INIT, F32), mlstm_g_out, chunk_p)
            Tp = 128
            zs_pad = _pad_rows(zs, Tp)
            gs_pad = jnp.concatenate([
                _pad_rows(gs[..., :H], Tp, NEG), _pad_rows(gs[..., H:], Tp, -NEG)], axis=-1)
            os_, mC_s, mn_s, mm_s = mlstm_core(
                zs_pad, gs_pad, mlstm_b_gates, state_mlstm_C.astype(F32), state_mlstm_n.astype(F32),
                state_mlstm_m.astype(F32), mlstm_g_out, Tp)
            os_ = os_[:, :T]
            w_out = mlstm_w_out
        elif kind == 1:
            H = NH_B
            dh = D // H
            w_qk = swa_w_in[:, :2 * D].astype(BF16)
            w_v = swa_w_in[:, 2 * D:].astype(BF16)
            zp = matmul(hp, w_qk)
            vp = matmul(hp, w_v)
            zs = matmul(hs_flat, w_qk).reshape(Bs, T, 2 * D)
            vs_new = matmul(hs_flat, w_v).reshape(Bs, T, D)
            gq = jnp.tile(swa_g_q, H)
            gk = jnp.tile(swa_g_k, H)
            qn_p, kn_p = headnorm(zp, gq, gk, dh, dh ** -0.5)
            qn_s, kn_s = headnorm(zs, gq, gk, dh, dh ** -0.5)
            t = _pick(S, (512, 256, 128))
            assert (_np_multiplicity(_toeplitz_dist_np(S // t, t)) > 0).any(axis=2).all()
            bm_tab = _toeplitz_bias(rel_bias, S // t, t, _swa_bias)
            op = flash_attention(qn_p, kn_p, vp, bm_tab, n_units=H, dqk=dh, dv=dh,
                                 qcol=lambda a: a, kcol=lambda a: a, vcol=lambda a: a, bmap=lambda a: a,
                                 t=t, out_dtype=BF16)
            swa_k_p = kn_p.reshape(Bp, S, H, dh)
            swa_v_p = vp.reshape(Bp, S, H, dh)
            os_ = swa_decode(qn_s, kn_s, vs_new, cache_swa_k, cache_swa_v, rel_bias)
            swa_k_s = jnp.concatenate([cache_swa_k[:, T:].astype(F32), kn_s.reshape(Bs, T, H, dh)], axis=1)
            swa_v_s = jnp.concatenate([cache_swa_v[:, T:].astype(F32), vs_new.reshape(Bs, T, H, dh)], axis=1)
            w_out = swa_w_out
        elif kind == 2:
            H = NH_C
            dh = D // (2 * H)
            dv = 2 * dh
            lam_init = 0.8 - 0.6 * math.exp(-0.3 * i)
            lam4 = jnp.stack([diff_lam_q1, diff_lam_k1, diff_lam_q2, diff_lam_k2]).astype(F32)
            w_qk = diff_w_in[:, :2 * D].astype(BF16)
            w_v = diff_w_in[:, 2 * D:].astype(BF16)
            zp = matmul(hp, w_qk)
            vp = matmul(hp, w_v)
            zs = matmul(hs_flat, w_qk).reshape(Bs, T, 2 * D)
            vs_new = matmul(hs_flat, w_v).reshape(Bs, T, D)
            gq = jnp.tile(diff_g_q.reshape(-1), H)
            gk = jnp.tile(diff_g_k.reshape(-1), H)
            qn_p, kn_p = headnorm(zp, gq, gk, dh, dh ** -0.5)
            qn_s, kn_s = headnorm(zs, gq, gk, dh, dh ** -0.5)
            t = _pick(S, (512, 256, 128))
            bm_tab = _toeplitz_bias(rel_bias, S // t, t, _causal_bias)
            o2 = flash_attention(qn_p, kn_p, vp, bm_tab, n_units=2 * H, dqk=dh, dv=dv,
                                 qcol=lambda a: a, kcol=lambda a: a, vcol=lambda a: a // 2,
                                 bmap=lambda a: (a % 2) * H + a // 2, t=t, out_dtype=F32)
            op = diff_combine(o2, lam4, diff_g_out, lam_init)
            diff_k_p = kn_p.reshape(Bp, S, H, 2 * dh)
            diff_v_p = vp.reshape(Bp, S, H, dv)
            diff_k_s = kn_s.reshape(Bs, T, H, 2 * dh)
            diff_v_s = vs_new.reshape(Bs, T, H, dv)
            os_ = diff_paged(qn_s.reshape(Bs, T, H, 2, dh), diff_k_s, diff_v_s, cache_diff_k, cache_diff_v,
                             page_table, rel_bias, lam4, diff_g_out, lam_init)
            w_out = diff_w_out
        else:
            wb = rglru_w_in.astype(BF16)
            zp = matmul(hp, wb)
            zs = matmul(hs_flat, wb).reshape(Bs, T, -1)
            tblk = _pick(S, (256, 128))
            op, conv_p, h_p = rglru_core(zp, jnp.zeros((Bp, CONV_W - 1, D), F32), jnp.zeros((Bp, D), F32),
                                         rglru_conv_w, rglru_conv_b, rglru_w_a, rglru_b_a, rglru_w_x, rglru_b_x,
                                         rglru_lambda, tblk, tblk)
            os_, conv_s, h_s = rglru_core(_pad_rows(zs, 8), state_rglru_conv, state_rglru_h,
                                          rglru_conv_w, rglru_conv_b, rglru_w_a, rglru_b_a, rglru_w_x, rglru_b_x,
                                          rglru_lambda, T, 8)
            os_ = os_[:, :T]
            w_out = rglru_w_out
        out_proj(op, os_, w_out, gt_p, gt_s)

        sh_p, sh_s = mods(i, 3)
        sc_p, sc_s = mods(i, 4)
        gt_p, gt_s = mods(i, 5)
        hp = modulate(xp, g_norm_ffn[i], sh_p, sc_p).reshape(Bp * S, D)
        hs = modulate(xs, g_norm_ffn[i], sh_s, sc_s).reshape(Ns, D)
        wq = peer_w_q[i].astype(BF16)
        ut = jnp.transpose(peer_u[i]).astype(BF16)
        vb = peer_v[i].astype(BF16)
        nk = peer_keys.shape[3]
        Gp = peer_build(*peer_route(hp, wq, peer_keys[i]), nk)
        Ns_pad = -(-Ns // LANES) * LANES
        Gs = peer_build(*peer_route(_pad_rows(hs, Ns_pad, axis=0), wq, peer_keys[i]), nk)[:Ns]
        xp = peer_dense(hp, ut, vb, Gp, xp.reshape(Bp * S, D), gt_p.reshape(Bp, D), S).reshape(Bp, S, D)
        gs_rows = jnp.broadcast_to(gt_s, (Bs, T, D)).reshape(Ns, D)
        xs = peer_dense(hs, ut, vb, Gs, xs.reshape(Ns, D), gs_rows, 1).reshape(Bs, T, D)

    return (xp, xs, mC_p, mC_s, mn_p, mn_s, mm_p, mm_s, swa_k_p, swa_k_s, swa_v_p, swa_v_s,
            diff_k_p, diff_k_s, diff_v_p, diff_v_s, conv_p, conv_s, h_p, h_s)
```

```python
import functools
import math

import numpy as np
import jax
import jax.numpy as jnp
from jax import lax
from jax.experimental import pallas as pl
from jax.experimental.pallas import tpu as pltpu

F32 = jnp.float32
BF16 = jnp.bfloat16
HIGHEST = lax.Precision.HIGHEST

EPS = 1e-6
NEG = -1e30
VMEM_LIMIT = 56 * 1024 * 1024
LANES = 128

NH_A, CHUNK_A, M_INIT = 8, 256, -1e30
NH_B = 16
DIL_PAIRS = ((128, 1), (512, 4), (2048, 16))
NH_C = 8
NBLK_D, CONV_W, LRU_C = 16, 4, 8.0
NH_P, N_KEYS, TOPK_P = 8, 128, 16
REL_BUCKETS, REL_MAX_DIST = 32, 2048
PAGE_SIZE = 128


def _cparams(sem):
    return pltpu.CompilerParams(dimension_semantics=sem, vmem_limit_bytes=VMEM_LIMIT)


def _bdot(a, b):
    return jnp.dot(a.astype(BF16), b.astype(BF16), preferred_element_type=F32)


def _bdot_nt(a, b):
    return lax.dot_general(a.astype(BF16), b.astype(BF16), (((1,), (1,)), ((), ())),
                           preferred_element_type=F32)


def _gelu(x):
    return 0.5 * x * (1.0 + lax.erf(x * (1.0 / math.sqrt(2.0))))


def _log_sigmoid(x):
    return jnp.minimum(x, 0.0) - jnp.log1p(jnp.exp(-jnp.abs(x)))


def _softplus(x):
    return jnp.maximum(x, 0.0) + jnp.log1p(jnp.exp(-jnp.abs(x)))


def _pick(n, pref):
    for t in pref:
        if n % t == 0:
            return t
    return n


def _ada_kernel(c_ref, w_ref, b_ref, o_ref):
    c = c_ref[...]
    a = c * jax.nn.sigmoid(c)
    o_ref[0] = jnp.dot(a, w_ref[0], preferred_element_type=F32, precision=HIGHEST) + b_ref[0]


def ada_all(c_all, w_ada, b_ada):
    R, D = c_all.shape
    nl, _, N = w_ada.shape
    tn = _pick(N, (1024, 512, 256, 128))
    return pl.pallas_call(
        _ada_kernel,
        grid=(nl, N // tn),
        in_specs=[pl.BlockSpec((R, D), lambda l, j: (0, 0)),
                  pl.BlockSpec((1, D, tn), lambda l, j: (l, 0, j)),
                  pl.BlockSpec((1, 1, tn), lambda l, j: (l, 0, j))],
        out_specs=pl.BlockSpec((1, R, tn), lambda l, j: (l, 0, j)),
        out_shape=jax.ShapeDtypeStruct((nl, R, N), F32),
        compiler_params=_cparams(("arbitrary", "arbitrary")),
        name="ada",
    )(c_all, w_ada, b_ada.reshape(nl, 1, N))


def _modulate_kernel(x_ref, g_ref, sh_ref, sc_ref, o_ref):
    x = x_ref[0]
    ms = jnp.mean(x * x, axis=-1, keepdims=True)
    y = x * lax.rsqrt(ms + EPS) * g_ref[...]
    o_ref[0] = (y * (1.0 + sc_ref[0]) + sh_ref[0]).astype(o_ref.dtype)


def modulate(x, g, shift, scale):
    B, L, D = x.shape
    tl = _pick(L, (512, 256, 128))
    return pl.pallas_call(
        _modulate_kernel,
        grid=(B, L // tl),
        in_specs=[pl.BlockSpec((1, tl, D), lambda b, i: (b, i, 0)),
                  pl.BlockSpec((1, D), lambda b, i: (0, 0)),
                  pl.BlockSpec((1, 1, D), lambda b, i: (b, 0, 0)),
                  pl.BlockSpec((1, 1, D), lambda b, i: (b, 0, 0))],
        out_specs=pl.BlockSpec((1, tl, D), lambda b, i: (b, i, 0)),
        out_shape=jax.ShapeDtypeStruct((B, L, D), BF16),
        compiler_params=_cparams(("arbitrary", "arbitrary")),
        name="modulate",
    )(x, g.reshape(1, D), shift, scale)


def _mm_kernel(x_ref, w_ref, o_ref):
    o_ref[0] = jnp.dot(x_ref[0], w_ref[...], preferred_element_type=F32).astype(o_ref.dtype)


def _mm_res_kernel(x_ref, w_ref, r_ref, gt_ref, o_ref):
    acc = jnp.dot(x_ref[0], w_ref[...], preferred_element_type=F32)
    o_ref[0] = r_ref[0] + gt_ref[0] * acc


def matmul(x, w, res=None, gate=None, out_dtype=F32, cols=None):
    B, L, K = x.shape
    c0, N = cols if cols is not None else (0, w.shape[1])
    tm = _pick(L, (512, 256, 128))
    tn = next((c for c in (1024, 512, 256, 128) if N % c == 0 and c0 % c == 0), N)
    assert c0 % tn == 0 and (tn == w.shape[1] or tn % LANES == 0)
    j0 = c0 // tn
    grid = (N // tn, B, L // tm)
    x_spec = pl.BlockSpec((1, tm, K), lambda j, b, i: (b, i, 0))
    w_spec = pl.BlockSpec((K, tn), lambda j, b, i: (0, j0 + j))
    o_spec = pl.BlockSpec((1, tm, tn), lambda j, b, i: (b, i, j))
    if res is None:
        return pl.pallas_call(
            _mm_kernel, grid=grid, in_specs=[x_spec, w_spec], out_specs=o_spec,
            out_shape=jax.ShapeDtypeStruct((B, L, N), out_dtype),
            compiler_params=_cparams(("arbitrary",) * 3), name="matmul",
        )(x, w)
    if gate.shape[1] == 1:
        g_spec = pl.BlockSpec((1, 1, tn), lambda j, b, i: (b, 0, j))
    else:
        g_spec = pl.BlockSpec((1, tm, tn), lambda j, b, i: (b, i, j))
    return pl.pallas_call(
        _mm_res_kernel, grid=grid, in_specs=[x_spec, w_spec, o_spec, g_spec], out_specs=o_spec,
        out_shape=jax.ShapeDtypeStruct((B, L, N), F32),
        compiler_params=_cparams(("arbitrary",) * 3), name="matmul_res",
    )(x, w, res, gate)


def _mlstm_kernel(q_ref, k_ref, v_ref, o_ref, gc_ref, gr_ref, bc_ref, br_ref,
                  C0_ref, n0_ref, m0_ref, gout_ref,
                  hs_ref, C_ref, n_ref, m_ref, C_s, n_s, m_s, *, dk):
    ci = pl.program_id(2)

    @pl.when(ci == 0)
    def _():
        C_s[...] = C0_ref[0, 0]
        n_s[...] = n0_ref[0, 0]
        m_s[...] = m0_ref[0, 0]

    q = q_ref[0]
    k = k_ref[0] * (dk ** -0.5)
    v = v_ref[0]
    c = q.shape[0]
    gc = gc_ref[0, 0] + bc_ref[0]
    gr = gr_ref[0, 0] + br_ref[0]
    li_c, lf_c = gc[:, 0:1], _log_sigmoid(gc[:, 1:2])
    li_r, lf_r = gr[0:1, :], _log_sigmoid(gr[1:2, :])
    row = lax.broadcasted_iota(jnp.int32, (c, c), 0)
    col = lax.broadcasted_iota(jnp.int32, (c, c), 1)
    causal = col <= row
    b_c = jnp.sum(jnp.where(causal, lf_r, 0.0), axis=1, keepdims=True)
    b_r = jnp.sum(jnp.where(row <= col, lf_c, 0.0), axis=0, keepdims=True)
    m_prev = m_s[...]
    Dm = jnp.where(causal, b_c - b_r + li_r, NEG)
    m_t = jnp.maximum(b_c + m_prev, jnp.max(Dm, axis=1, keepdims=True))
    S = _bdot_nt(q, k) * jnp.exp(Dm - m_t)
    inter = jnp.exp(b_c + m_prev - m_t)
    C = C_s[...]
    n = n_s[...]
    num = _bdot(S, v) + inter * _bdot(q, C)
    den = jnp.sum(S, axis=1, keepdims=True) + inter * jnp.sum(q * n, axis=1, keepdims=True)
    h = num / jnp.maximum(jnp.abs(den), jnp.exp(-m_t))
    hh = jax.nn.sigmoid(o_ref[0]) * h
    ms = jnp.mean(hh * hh, axis=-1, keepdims=True)
    hs_ref[0] = (hh * lax.rsqrt(ms + EPS) * gout_ref[...]).astype(hs_ref.dtype)
    m_new = m_t[c - 1:c, :]
    b_last = b_c[c - 1:c, :]
    w_r = jnp.exp(b_last - b_r + li_r - m_new)
    w_c = jnp.exp(b_last - b_c + li_c - m_new)
    decay = jnp.exp(b_last + m_prev - m_new)
    C_new = decay * C + _bdot(k.T, w_c * v)
    n_new = decay * n + jnp.dot(w_r, k, preferred_element_type=F32, precision=HIGHEST)
    C_s[...] = C_new
    n_s[...] = n_new
    m_s[...] = m_new

    @pl.when(ci == pl.num_programs(2) - 1)
    def _():
        C_ref[0, 0] = C_new
        n_ref[0, 0] = n_new
        m_ref[0, 0] = m_new


def mlstm_core(z, g, b_gates, C0, n0, m0, g_out, chunk):
    B, L, _ = z.shape
    H = NH_A
    dk = C0.shape[2]
    dv = C0.shape[3]
    nc = L // chunk
    g4 = g.reshape(B, L, 2, H)
    gcol = jnp.transpose(g4, (0, 3, 1, 2))
    grow = jnp.transpose(g4, (0, 3, 2, 1))
    bg = b_gates.reshape(2, H)
    bcol = jnp.transpose(bg, (1, 0)).reshape(H, 1, 2)
    brow = jnp.transpose(bg, (1, 0)).reshape(H, 2, 1)
    kq = (H * dk) // dk
    vo = (2 * H * dk) // dv
    oo = vo + H
    outs = pl.pallas_call(
        functools.partial(_mlstm_kernel, dk=dk),
        grid=(B, H, nc),
        in_specs=[
            pl.BlockSpec((1, chunk, dk), lambda b, h, c: (b, c, h)),
            pl.BlockSpec((1, chunk, dk), lambda b, h, c: (b, c, kq + h)),
            pl.BlockSpec((1, chunk, dv), lambda b, h, c: (b, c, vo + h)),
            pl.BlockSpec((1, chunk, dv), lambda b, h, c: (b, c, oo + h)),
            pl.BlockSpec((1, 1, chunk, 2), lambda b, h, c: (b, h, c, 0)),
            pl.BlockSpec((1, 1, 2, chunk), lambda b, h, c: (b, h, 0, c)),
            pl.BlockSpec((1, 1, 2), lambda b, h, c: (h, 0, 0)),
            pl.BlockSpec((1, 2, 1), lambda b, h, c: (h, 0, 0)),
            pl.BlockSpec((1, 1, dk, dv), lambda b, h, c: (b, h, 0, 0)),
            pl.BlockSpec((1, 1, 1, dk), lambda b, h, c: (b, h, 0, 0)),
            pl.BlockSpec((1, 1, 1, 1), lambda b, h, c: (b, h, 0, 0)),
            pl.BlockSpec((1, dv), lambda b, h, c: (0, h)),
        ],
        out_specs=[
            pl.BlockSpec((1, chunk, dv), lambda b, h, c: (b, c, h)),
            pl.BlockSpec((1, 1, dk, dv), lambda b, h, c: (b, h, 0, 0)),
            pl.BlockSpec((1, 1, 1, dk), lambda b, h, c: (b, h, 0, 0)),
            pl.BlockSpec((1, 1, 1, 1), lambda b, h, c: (b, h, 0, 0)),
        ],
        out_shape=[
            jax.ShapeDtypeStruct((B, L, H * dv), BF16),
            jax.ShapeDtypeStruct((B, H, dk, dv), F32),
            jax.ShapeDtypeStruct((B, H, 1, dk), F32),
            jax.ShapeDtypeStruct((B, H, 1, 1), F32),
        ],
        scratch_shapes=[pltpu.VMEM((dk, dv), F32), pltpu.VMEM((1, dk), F32), pltpu.VMEM((1, 1), F32)],
        compiler_params=_cparams(("arbitrary",) * 3),
        name="mlstm",
    )(z, z, z, z, gcol, grow, bcol, brow, C0, n0.reshape(B, H, 1, dk), m0.reshape(B, H, 1, 1),
      g_out.reshape(1, H * dv))
    hs, C, n, m = outs
    return hs, C, n.reshape(B, H, dk), m.reshape(B, H)


def _headnorm_kernel(q_ref, k_ref, gq_ref, gk_ref, qo_ref, ko_ref, *, dh, q_scale):
    nh = q_ref.shape[2] // dh
    for h in range(nh):
        sl = slice(h * dh, (h + 1) * dh)
        for src, g, dst, sc in ((q_ref, gq_ref, qo_ref, q_scale), (k_ref, gk_ref, ko_ref, 1.0)):
            x = src[0, :, sl]
            ms = jnp.mean(x * x, axis=-1, keepdims=True)
            y = x * lax.rsqrt(ms + EPS) * g[:, sl]
            dst[0, :, sl] = (y * sc if sc != 1.0 else y).astype(dst.dtype)


def headnorm(z, gq_full, gk_full, dh, q_scale):
    B, L, W2 = z.shape
    W = W2 // 2
    tl = _pick(L, (256, 128))
    return pl.pallas_call(
        functools.partial(_headnorm_kernel, dh=dh, q_scale=q_scale),
        grid=(B, L // tl),
        in_specs=[pl.BlockSpec((1, tl, W), lambda b, i: (b, i, 0)),
                  pl.BlockSpec((1, tl, W), lambda b, i: (b, i, 1)),
                  pl.BlockSpec((1, W), lambda b, i: (0, 0)),
                  pl.BlockSpec((1, W), lambda b, i: (0, 0))],
        out_specs=[pl.BlockSpec((1, tl, W), lambda b, i: (b, i, 0)),
                   pl.BlockSpec((1, tl, W), lambda b, i: (b, i, 0))],
        out_shape=[jax.ShapeDtypeStruct((B, L, W), BF16), jax.ShapeDtypeStruct((B, L, W), F32)],
        compiler_params=_cparams(("arbitrary", "arbitrary")),
        name="headnorm",
    )(z, z, gq_full.reshape(1, W), gk_full.reshape(1, W))


def _rel_bucket(dist):
    exact = REL_BUCKETS // 2
    d_f = jnp.maximum(dist, 1).astype(F32)
    large = exact + (jnp.log(d_f / exact) / math.log(REL_MAX_DIST / exact) * (REL_BUCKETS - exact)).astype(jnp.int32)
    large = jnp.minimum(large, REL_BUCKETS - 1)
    return jnp.where(dist < exact, dist, large)


def _bias_of_dist(rel_bias, dist):
    bucket = _rel_bucket(jnp.maximum(dist, 0))
    out = jnp.zeros((rel_bias.shape[1],) + dist.shape, F32)
    for b in range(REL_BUCKETS):
        out = jnp.where(bucket[None] == b, rel_bias[b].reshape((-1,) + (1,) * dist.ndim), out)
    return out


def _np_multiplicity(dist):
    cnt = np.zeros(dist.shape, np.int64)
    for w, d in DIL_PAIRS:
        cnt = cnt + ((dist % d == 0) & (dist <= w) & (dist >= 0))
    return cnt


def _swa_bias(rel_bias, dist_np):
    cnt = _np_multiplicity(dist_np)
    logc = jnp.log(jnp.asarray(np.maximum(cnt, 1), F32))
    bias = _bias_of_dist(rel_bias, jnp.asarray(np.maximum(dist_np, 0), jnp.int32))
    return jnp.where(jnp.asarray(cnt > 0)[None], bias + logc[None], NEG), cnt > 0


def _causal_bias(rel_bias, dist_np):
    bias = _bias_of_dist(rel_bias, jnp.asarray(np.maximum(dist_np, 0), jnp.int32))
    return jnp.where(jnp.asarray(dist_np >= 0)[None], bias, NEG), dist_np >= 0


def _toeplitz_dist_np(n_off, t):
    o = np.arange(n_off)[:, None, None]
    i = np.arange(t)[None, :, None]
    j = np.arange(t)[None, None, :]
    return o * t + i - j


def _toeplitz_kernel(b_ref, o_ref):
    t = o_ref.shape[2]
    rows = jnp.broadcast_to(b_ref[0, 0], (t, 2 * t))
    skew = pltpu.roll(rows, 0, 1, stride=1, stride_axis=0)
    o_ref[0, 0] = skew[:, t:]


def _toeplitz_bias(rel_bias, n_off, t, bias_fn):
    dist = np.arange(-(t - 1), n_off * t)
    f, _ = bias_fn(rel_bias, dist)
    maps = f.shape[0]
    want = np.arange(n_off)[:, None] * t + t - np.arange(2 * t)[None, :]
    idx = np.minimum(want + (t - 1), len(dist) - 1)
    b = f[:, idx].reshape(maps, n_off, 1, 2 * t)
    return pl.pallas_call(
        _toeplitz_kernel,
        grid=(maps, n_off),
        in_specs=[pl.BlockSpec((1, 1, 1, 2 * t), lambda m, o: (m, o, 0, 0))],
        out_specs=pl.BlockSpec((1, 1, t, t), lambda m, o: (m, o, 0, 0)),
        out_shape=jax.ShapeDtypeStruct((maps, n_off, t, t), F32),
        compiler_params=_cparams(("arbitrary", "arbitrary")),
        name="toeplitz_bias",
    )(b)


def _flash_kernel(qi_ref, ki_ref, q_ref, k_ref, v_ref, bm_ref, o_ref, m_s, l_s, acc_s):
    p = pl.program_id(1)
    qi = qi_ref[p]
    ki = ki_ref[p]
    nb, tq, dv = acc_s.shape
    tk = k_ref.shape[1]

    @pl.when(ki == 0)
    def _():
        m_s[...] = jnp.full(m_s.shape, NEG, F32)
        l_s[...] = jnp.zeros(l_s.shape, F32)
        acc_s[...] = jnp.zeros(acc_s.shape, F32)

    bm = bm_ref[0, 0]
    for b in range(nb):
        s = _bdot_nt(q_ref[b], k_ref[b]) + bm
        m_prev = m_s[b]
        m_next = jnp.maximum(m_prev, jnp.max(s, axis=1, keepdims=True))
        pr = jnp.exp(s - jnp.tile(m_next, (1, tk // LANES)))
        alpha = jnp.exp(m_prev - m_next)
        l_s[b] = alpha * l_s[b] + jnp.sum(pr, axis=1, keepdims=True)
        acc_s[b] = acc_s[b] * jnp.tile(alpha, (1, dv // LANES)) + _bdot(pr, v_ref[b])
        m_s[b] = m_next

    @pl.when(ki == qi)
    def _():
        for b in range(nb):
            o_ref[b] = (acc_s[b] / jnp.tile(l_s[b], (1, dv // LANES))).astype(o_ref.dtype)


def flash_attention(q, k, v, bm_tab, *, n_units, dqk, dv, qcol, kcol, vcol, bmap, t, out_dtype):
    B, L = q.shape[:2]
    nq = L // t
    pairs = [(qi, ki) for qi in range(nq) for ki in range(qi + 1)]
    qi_arr = jnp.asarray(np.array([p[0] for p in pairs], np.int32))
    ki_arr = jnp.asarray(np.array([p[1] for p in pairs], np.int32))
    grid_spec = pltpu.PrefetchScalarGridSpec(
        num_scalar_prefetch=2,
        grid=(n_units, len(pairs)),
        in_specs=[
            pl.BlockSpec((B, t, dqk), lambda a, p, qa, ka: (0, qa[p], qcol(a))),
            pl.BlockSpec((B, t, dqk), lambda a, p, qa, ka: (0, ka[p], kcol(a))),
            pl.BlockSpec((B, t, dv), lambda a, p, qa, ka: (0, ka[p], vcol(a))),
            pl.BlockSpec((1, 1, t, t), lambda a, p, qa, ka: (bmap(a), qa[p] - ka[p], 0, 0)),
        ],
        out_specs=pl.BlockSpec((B, t, dv), lambda a, p, qa, ka: (0, qa[p], a)),
        scratch_shapes=[pltpu.VMEM((B, t, LANES), F32), pltpu.VMEM((B, t, LANES), F32),
                        pltpu.VMEM((B, t, dv), F32)],
    )
    return pl.pallas_call(
        _flash_kernel,
        grid_spec=grid_spec,
        out_shape=jax.ShapeDtypeStruct((B, L, n_units * dv), out_dtype),
        compiler_params=_cparams(("arbitrary", "arbitrary")),
        name="flash_attention",
    )(qi_arr, ki_arr, q, k, v, bm_tab)


def _diff_lambda(lam_ref, lam_init):
    lq1, lk1, lq2, lk2 = lam_ref[0:1, :], lam_ref[1:2, :], lam_ref[2:3, :], lam_ref[3:4, :]
    return (jnp.exp(jnp.sum(lq1 * lk1, axis=-1, keepdims=True))
            - jnp.exp(jnp.sum(lq2 * lk2, axis=-1, keepdims=True)) + lam_init)


def _diff_combine_kernel(o_ref, lam_ref, g_ref, out_ref, *, dv, lam_init):
    lam = _diff_lambda(lam_ref, lam_init)
    nh = out_ref.shape[2] // dv
    for h in range(nh):
        o0 = o_ref[0, :, (2 * h) * dv:(2 * h + 1) * dv]
        o1 = o_ref[0, :, (2 * h + 1) * dv:(2 * h + 2) * dv]
        d = o0 - lam * o1
        ms = jnp.mean(d * d, axis=-1, keepdims=True)
        out_ref[0, :, h * dv:(h + 1) * dv] = (d * lax.rsqrt(ms + EPS) * g_ref[...] * (1.0 - lam_init)
                                              ).astype(out_ref.dtype)


def diff_combine(o, lam4, g_out, lam_init):
    B, L, W2 = o.shape
    dv = g_out.shape[0]
    W = W2 // 2
    tl = _pick(L, (256, 128))
    return pl.pallas_call(
        functools.partial(_diff_combine_kernel, dv=dv, lam_init=lam_init),
        grid=(B, L // tl),
        in_specs=[pl.BlockSpec((1, tl, W2), lambda b, i: (b, i, 0)),
                  pl.BlockSpec(lam4.shape, lambda b, i: (0, 0)),
                  pl.BlockSpec((1, dv), lambda b, i: (0, 0))],
        out_specs=pl.BlockSpec((1, tl, W), lambda b, i: (b, i, 0)),
        out_shape=jax.ShapeDtypeStruct((B, L, W), BF16),
        compiler_params=_cparams(("arbitrary", "arbitrary")),
        name="diff_combine",
    )(o, lam4, g_out.reshape(1, dv))


def _decode_kernel(*refs, n_scalar, n_chunks, diff, lam_init, n_heads, n_new):
    refs = refs[n_scalar:]
    q_ref = refs[0]
    k_refs = refs[1:1 + n_chunks]
    v_refs = refs[1 + n_chunks:1 + 2 * n_chunks]
    bm_refs = refs[1 + 2 * n_chunks:1 + 3 * n_chunks]
    rest = refs[1 + 3 * n_chunks:]
    if diff:
        kn_ref, vn_ref, bmn_ref, lam_ref, g_ref, out_ref, m_s, l_s, acc_s = rest
    else:
        kn_ref, vn_ref, bmn_ref, out_ref, m_s, l_s, acc_s = rest
    p = pl.program_id(1)

    @pl.when(p == 0)
    def _():
        m_s[...] = jnp.full(m_s.shape, NEG, F32)
        l_s[...] = jnp.zeros(l_s.shape, F32)
        acc_s[...] = jnp.zeros(acc_s.shape, F32)

    q = q_ref[0]

    def absorb(k2s, v2s, bms):
        ss = [_bdot_nt(q, k2) + bm for k2, bm in zip(k2s, bms)]
        m_old = m_s[...]
        m_new = m_old
        for s in ss:
            m_new = jnp.maximum(m_new, jnp.max(s, axis=-1, keepdims=True))
        alpha = jnp.exp(m_old - m_new)
        l_new = alpha * l_s[...]
        acc = alpha * acc_s[...]
        for s, v2 in zip(ss, v2s):
            pr = jnp.exp(s - m_new)
            l_new = l_new + jnp.sum(pr, axis=-1, keepdims=True)
            acc = acc + _bdot(pr, v2)
        l_s[...] = l_new
        acc_s[...] = acc
        m_s[...] = m_new

    def rows2d(ref):
        x = ref[0]
        return x.reshape(x.shape[0] * x.shape[1], x.shape[2])

    absorb([rows2d(r) for r in k_refs], [rows2d(r) for r in v_refs], [r[0] for r in bm_refs])

    @pl.when(p == pl.num_programs(1) - 1)
    def _():
        absorb([kn_ref[0]], [vn_ref[0]], [bmn_ref[...]])
        o = acc_s[...] / l_s[...]
        if diff:
            half = n_heads * n_new
            lam = _diff_lambda(lam_ref, lam_init)
            d = o[0:half] - lam * o[half:2 * half]
            ms = jnp.mean(d * d, axis=-1, keepdims=True)
            o = d * lax.rsqrt(ms + EPS) * g_ref[...] * (1.0 - lam_init)
        dv = o.shape[1]
        for h in range(n_heads):
            out_ref[0, :, h * dv:(h + 1) * dv] = o[h * n_new:(h + 1) * n_new].astype(out_ref.dtype)


def _head_expand(bias_rows, row_head, n_heads):
    ok = jnp.asarray(row_head[:, None] == np.arange(n_heads)[None, :])
    out = jnp.where(ok[:, None, :], bias_rows[..., None], NEG)
    return out.reshape(bias_rows.shape[:-1] + (bias_rows.shape[-1] * n_heads,))


def swa_decode(qn, k_new, v_new, cache_k, cache_v, rel_bias):
    B, Wb, H, dh = cache_k.shape
    T = qn.shape[1]
    R = H * T
    tk = _pick(Wb, (512, 256, 128))
    nblk = Wb // tk
    q2 = jnp.transpose(qn.reshape(B, T, H, dh), (0, 2, 1, 3)).reshape(B, R, dh)
    row_head = np.arange(R) // T
    dist_np = (Wb + np.arange(T))[:, None] - np.arange(Wb)[None, :]
    bias, ok = _swa_bias(rel_bias, dist_np)
    assert ok[:, :tk].any(axis=1).all()
    bm = _head_expand(bias.reshape(R, Wb), row_head, H)
    bm = jnp.transpose(bm.reshape(R, nblk, tk * H), (1, 0, 2))
    dn = np.arange(T)[:, None] - np.arange(T)[None, :]
    bias_n, _ = _swa_bias(rel_bias, dn)
    bmn = _head_expand(bias_n.reshape(R, T), row_head, H)
    kn2 = k_new.reshape(B, T * H, dh)
    vn2 = v_new.reshape(B, T * H, dh)
    return pl.pallas_call(
        functools.partial(_decode_kernel, n_scalar=0, n_chunks=1, diff=False, lam_init=0.0, n_heads=H, n_new=T),
        grid=(B, nblk),
        in_specs=[
            pl.BlockSpec((1, R, dh), lambda b, p: (b, 0, 0)),
            pl.BlockSpec((1, tk, H, dh), lambda b, p: (b, p, 0, 0)),
            pl.BlockSpec((1, tk, H, dh), lambda b, p: (b, p, 0, 0)),
            pl.BlockSpec((1, R, tk * H), lambda b, p: (p, 0, 0)),
            pl.BlockSpec((1, T * H, dh), lambda b, p: (b, 0, 0)),
            pl.BlockSpec((1, T * H, dh), lambda b, p: (b, 0, 0)),
            pl.BlockSpec((R, T * H), lambda b, p: (0, 0)),
        ],
        out_specs=pl.BlockSpec((1, T, H * dh), lambda b, p: (b, 0, 0)),
        out_shape=jax.ShapeDtypeStruct((B, T, H * dh), BF16),
        scratch_shapes=[pltpu.VMEM((R, 1), F32), pltpu.VMEM((R, 1), F32), pltpu.VMEM((R, dh), F32)],
        compiler_params=_cparams(("arbitrary", "arbitrary")),
        name="swa_decode",
    )(q2, cache_k, cache_v, bm, kn2, vn2, bmn)


def _np_rel_bucket(dist):
    exact = REL_BUCKETS // 2
    d_f = np.maximum(dist, 1).astype(np.float32)
    large = exact + (np.log(d_f / np.float32(exact)) / np.float32(math.log(REL_MAX_DIST / exact))
                     * np.float32(REL_BUCKETS - exact)).astype(np.int32)
    large = np.minimum(large, REL_BUCKETS - 1)
    return np.where(dist < exact, dist, large)


def diff_paged(qn, k_new, v_new, cache_k, cache_v, page_table, rel_bias, lam4, g_out, lam_init):
    B, T, H, _, dh = qn.shape
    dv = v_new.shape[-1]
    n_pages = page_table.shape[1]
    page = cache_k.shape[1]
    P = n_pages * page
    R = 2 * H * T
    pp = next(c for c in (4, 2, 1) if n_pages % c == 0)
    qt = jnp.transpose(qn, (0, 3, 2, 1, 4))
    zeros = jnp.zeros_like(qt[:, 0])
    qm = jnp.concatenate([jnp.concatenate([qt[:, 0], zeros], axis=-1),
                          jnp.concatenate([zeros, qt[:, 1]], axis=-1)], axis=1)
    qm = qm.reshape(B, R, 2 * dh)
    pg = np.arange(n_pages)[:, None, None]
    tt = np.arange(T)[None, :, None]
    ii = np.arange(page)[None, None, :]
    dist_np = P + tt - pg * page - ii
    bucket_np = _np_rel_bucket(dist_np)
    far = np.all(bucket_np == bucket_np[0:1], axis=(1, 2))
    near_pages = [int(x) for x in np.nonzero(~far)[0]]
    tile_pages = [0] + near_pages
    tile_of_page = np.zeros((n_pages,), np.int32)
    for ti, pgi in enumerate(near_pages):
        tile_of_page[pgi] = ti + 1
    nt = len(tile_pages)
    bias_t = _bias_of_dist(rel_bias, jnp.asarray(dist_np[tile_pages], jnp.int32))
    bias_rows = jnp.transpose(bias_t.reshape(2, H, nt, T, page), (2, 0, 1, 3, 4)).reshape(nt, R, page)
    row_head = (np.arange(R) // T) % H
    bm = _head_expand(bias_rows, row_head, H)
    dn = np.arange(T)[:, None] - np.arange(T)[None, :]
    bias_n, _ = _causal_bias(rel_bias, dn)
    bmn = _head_expand(bias_n.reshape(R, T), row_head, H)
    kn2 = k_new.reshape(B, T * H, 2 * dh)
    vn2 = v_new.reshape(B, T * H, dv)

    def kv_spec(j, d):
        return pl.BlockSpec((1, page, H, d), lambda b, p, pt, tl: (pt[b * n_pages + p * pp + j], 0, 0, 0))

    def bm_spec(j):
        return pl.BlockSpec((1, R, page * H), lambda b, p, pt, tl: (tl[p * pp + j], 0, 0))

    grid_spec = pltpu.PrefetchScalarGridSpec(
        num_scalar_prefetch=2,
        grid=(B, n_pages // pp),
        in_specs=(
            [pl.BlockSpec((1, R, 2 * dh), lambda b, p, pt, tl: (b, 0, 0))]
            + [kv_spec(j, 2 * dh) for j in range(pp)]
            + [kv_spec(j, dv) for j in range(pp)]
            + [bm_spec(j) for j in range(pp)]
            + [pl.BlockSpec((1, T * H, 2 * dh), lambda b, p, pt, tl: (b, 0, 0)),
               pl.BlockSpec((1, T * H, dv), lambda b, p, pt, tl: (b, 0, 0)),
               pl.BlockSpec((R, T * H), lambda b, p, pt, tl: (0, 0)),
               pl.BlockSpec(lam4.shape, lambda b, p, pt, tl: (0, 0)),
               pl.BlockSpec((1, dv), lambda b, p, pt, tl: (0, 0))]),
        out_specs=pl.BlockSpec((1, T, H * dv), lambda b, p, pt, tl: (b, 0, 0)),
        scratch_shapes=[pltpu.VMEM((R, 1), F32), pltpu.VMEM((R, 1), F32), pltpu.VMEM((R, dv), F32)],
    )
    return pl.pallas_call(
        functools.partial(_decode_kernel, n_scalar=2, n_chunks=pp, diff=True, lam_init=lam_init,
                          n_heads=H, n_new=T),
        grid_spec=grid_spec,
        out_shape=jax.ShapeDtypeStruct((B, T, H * dv), BF16),
        compiler_params=_cparams(("arbitrary", "arbitrary")),
        name="diff_paged",
    )(page_table.reshape(-1), jnp.asarray(tile_of_page), qm, *([cache_k] * pp), *([cache_v] * pp),
      *([bm] * pp), kn2, vn2, bmn, lam4, g_out.reshape(1, dv))


def _rglru_kernel(y_ref, x_ref, cs_ref, h0_ref, cw_ref, cb_ref, wa_ref, ba_ref, wx_ref, bx_ref, lam_ref,
                  out_ref, conv_ref, hl_ref, xpad, hcar, *, l_valid):
    i = pl.program_id(1)
    T, D = x_ref.shape[1], x_ref.shape[2]

    @pl.when(i == 0)
    def _():
        xpad[0:8, :] = cs_ref[0]
        hcar[...] = h0_ref[0]

    xpad[8:8 + T, :] = x_ref[0]
    conv = cb_ref[...] + cw_ref[CONV_W - 1:CONV_W, :] * xpad[8:8 + T, :]
    for j in range(CONV_W - 1):
        s = CONV_W - 1 - j
        conv = conv + cw_ref[j:j + 1, :] * xpad[8 - s:8 - s + T, :]
    nb = wa_ref.shape[0]
    bs = D // nb
    r_parts, i_parts = [], []
    for n in range(nb):
        xb = conv[:, n * bs:(n + 1) * bs]
        r_parts.append(_bdot(xb, wa_ref[n]))
        i_parts.append(_bdot(xb, wx_ref[n]))
    r = jax.nn.sigmoid(jnp.concatenate(r_parts, axis=-1) + ba_ref[...])
    ig = jax.nn.sigmoid(jnp.concatenate(i_parts, axis=-1) + bx_ref[...])
    log_a = -LRU_C * r * _softplus(-lam_ref[...])
    a = jnp.exp(log_a)
    bb = jnp.sqrt(-jnp.tanh(log_a) * (1.0 + a * a)) * ig * conv
    rowid = lax.broadcasted_iota(jnp.int32, (T, D), 0)
    s = 1
    while s < T:
        keep = rowid >= s
        a_sh = pltpu.roll(a, s, 0)
        b_sh = pltpu.roll(bb, s, 0)
        bb = jnp.where(keep, a * b_sh + bb, bb)
        a = jnp.where(keep, a * a_sh, a)
        s *= 2
    hs = a * hcar[...] + bb
    out_ref[0] = (hs * _gelu(y_ref[0])).astype(out_ref.dtype)
    hcar[...] = hs[T - 1:T, :]
    tail = xpad[T:T + 8, :]
    xpad[0:8, :] = tail

    @pl.when(i == pl.num_programs(1) - 1)
    def _():
        hl_ref[0] = hs[l_valid - 1:l_valid, :]
        conv_ref[0] = xpad[8 + l_valid - (CONV_W - 1):8 + l_valid, :]


def rglru_core(z, conv_state, h0, conv_w, conv_b, w_a, b_a, w_x, b_x, lam, l_valid_last, tblk):
    B, L, D2 = z.shape
    D = D2 // 2
    cs8 = jnp.concatenate([jnp.zeros((B, 8 - (CONV_W - 1), D), F32), conv_state.astype(F32)], axis=1)
    vec = lambda a: a.reshape(1, D)
    out, conv_new, h_last = pl.pallas_call(
        functools.partial(_rglru_kernel, l_valid=l_valid_last),
        grid=(B, L // tblk),
        in_specs=[
            pl.BlockSpec((1, tblk, D), lambda b, i: (b, i, 0)),
            pl.BlockSpec((1, tblk, D), lambda b, i: (b, i, 1)),
            pl.BlockSpec((1, 8, D), lambda b, i: (b, 0, 0)),
            pl.BlockSpec((1, 1, D), lambda b, i: (b, 0, 0)),
            pl.BlockSpec((CONV_W, D), lambda b, i: (0, 0)),
            pl.BlockSpec((1, D), lambda b, i: (0, 0)),
            pl.BlockSpec(w_a.shape, lambda b, i: (0, 0, 0)),
            pl.BlockSpec((1, D), lambda b, i: (0, 0)),
            pl.BlockSpec(w_x.shape, lambda b, i: (0, 0, 0)),
            pl.BlockSpec((1, D), lambda b, i: (0, 0)),
            pl.BlockSpec((1, D), lambda b, i: (0, 0)),
        ],
        out_specs=[
            pl.BlockSpec((1, tblk, D), lambda b, i: (b, i, 0)),
            pl.BlockSpec((1, CONV_W - 1, D), lambda b, i: (b, 0, 0)),
            pl.BlockSpec((1, 1, D), lambda b, i: (b, 0, 0)),
        ],
        out_shape=[
            jax.ShapeDtypeStruct((B, L, D), BF16),
            jax.ShapeDtypeStruct((B, CONV_W - 1, D), F32),
            jax.ShapeDtypeStruct((B, 1, D), F32),
        ],
        scratch_shapes=[pltpu.VMEM((tblk + 8, D), F32), pltpu.VMEM((1, D), F32)],
        compiler_params=_cparams(("arbitrary", "arbitrary")),
        name="rglru",
    )(z, z, cs8, h0.astype(F32).reshape(B, 1, D), conv_w, vec(conv_b), w_a.astype(BF16), vec(b_a),
      w_x.astype(BF16), vec(b_x), vec(lam))
    return out, conv_new, h_last.reshape(B, D)


def _topk_rows(s, k):
    n, tb = s.shape
    rowf = lax.broadcasted_iota(jnp.int32, (n, tb), 0).astype(F32)
    vals, idxs = [], []
    for _ in range(k):
        m = jnp.max(s, axis=0, keepdims=True)
        idx = jnp.min(jnp.where(s == m, rowf, float(n)), axis=0, keepdims=True)
        vals.append(m)
        idxs.append(idx)
        s = jnp.where(rowf == idx, -jnp.inf, s)
    return jnp.concatenate(vals, axis=0), jnp.concatenate(idxs, axis=0)


def _pair_plan(k):
    full, small = [], []
    a = 0
    while a < k and k // (a + 1) > 1:
        nb, b0 = k // (a + 1), 0
        while nb - b0 >= 8:
            full.append((a, b0))
            b0 += 8
        if nb > b0:
            small.append((a, b0, nb - b0))
        a += 1
    a0 = a
    assert k % 8 == 0 and (k - a0) % 8 == 0 and (k & (k - 1)) == 0
    bins = []
    for a_, b0, n in sorted(small, key=lambda p: -p[2]):
        for bn in bins:
            used = sum(p[3] for p in bn)
            if used + n <= 8:
                bn.append((a_, b0, used, n))
                break
        else:
            bins.append([(a_, b0, 0, n)])
    code = []
    for a_, b0 in full:
        code += [a_ * k + b0 + r for r in range(8)]
    for bn in bins:
        rows = [k * k] * 8
        for a_, b0, off, n in bn:
            for r in range(n):
                rows[off + r] = a_ * k + b0 + r
        code += rows
    code += [a_ * k for a_ in range(a0, k)]
    return full, bins, a0, np.array(code, np.float32)


def _route_kernel(h_ref, wq_ref, k1_ref, k2_ref, code_ref, e1_ref, e2_ref, g_ref, *, topk):
    k = topk
    q = jnp.dot(h_ref[...], wq_ref[...], preferred_element_type=F32)
    dk2 = k1_ref.shape[2]
    tb = q.shape[0]
    nt = (((1,), (1,)), ((), ()))
    s1 = lax.dot_general(k1_ref[0], q[:, :dk2], nt, precision=HIGHEST, preferred_element_type=F32)
    s2 = lax.dot_general(k2_ref[0], q[:, dk2:], nt, precision=HIGHEST, preferred_element_type=F32)
    sv1, si1 = _topk_rows(s1, k)
    sv2, si2 = _topk_rows(s2, k)
    full, bins, a0, _ = _pair_plan(k)
    row8 = lax.broadcasted_iota(jnp.int32, (8, tb), 0)
    groups = [sv1[a:a + 1] + sv2[b0:b0 + 8] for a, b0 in full]
    for bn in bins:
        v = jnp.full((8, tb), -jnp.inf, F32)
        for a, b0, off, n in bn:
            piece = sv1[a:a + 1] + (pltpu.roll(sv2[b0:b0 + 8], off, 0) if off else sv2[b0:b0 + 8])
            v = jnp.where((row8 >= off) & (row8 < off + n), piece, v)
        groups.append(v)
    for c in range(a0, k, 8):
        groups.append(sv1[c:c + 8] + sv2[0:1])
    cand = jnp.concatenate(groups, axis=0)
    code = code_ref[...]
    vals, poss = [], []
    for _ in range(k):
        m = jnp.max(cand, axis=0, keepdims=True)
        pos = jnp.min(jnp.where(cand == m, code, float(k * k)), axis=0, keepdims=True)
        vals.append(m)
        poss.append(pos)
        cand = jnp.where(code == pos, -jnp.inf, cand)
    val = jnp.concatenate(vals, axis=0)
    pos = jnp.concatenate(poss, axis=0)
    ra = jnp.floor(pos * (1.0 / k))
    rb = pos - k * ra
    e1 = jnp.zeros((k, tb), F32)
    e2 = jnp.zeros((k, tb), F32)
    for r in range(k):
        e1 = jnp.where(ra == float(r), si1[r:r + 1], e1)
        e2 = jnp.where(rb == float(r), si2[r:r + 1], e2)
    e1_ref[0] = e1
    e2_ref[0] = e2
    ex = jnp.exp(val - val[0:1])
    g_ref[0] = ex / jnp.sum(ex, axis=0, keepdims=True)


def peer_route(h, wq, keys):
    N, D = h.shape
    nh, _, nk, dk2 = keys.shape
    tb = _pick(N, (512, 256, 128))
    code = _pair_plan(TOPK_P)[3]
    code = jnp.asarray(np.broadcast_to(code[:, None], (code.shape[0], tb)))
    shp = jax.ShapeDtypeStruct((nh, TOPK_P, N), F32)
    o_spec = pl.BlockSpec((1, TOPK_P, tb), lambda hh, i: (hh, 0, i))
    return pl.pallas_call(
        functools.partial(_route_kernel, topk=TOPK_P),
        grid=(nh, N // tb),
        in_specs=[pl.BlockSpec((tb, D), lambda hh, i: (i, 0)),
                  pl.BlockSpec((D, 2 * dk2), lambda hh, i: (0, hh)),
                  pl.BlockSpec((1, nk, dk2), lambda hh, i: (2 * hh, 0, 0)),
                  pl.BlockSpec((1, nk, dk2), lambda hh, i: (2 * hh + 1, 0, 0)),
                  pl.BlockSpec(code.shape, lambda hh, i: (0, 0))],
        out_specs=[o_spec, o_spec, o_spec],
        out_shape=[shp, shp, shp],
        compiler_params=_cparams(("arbitrary", "arbitrary")),
        name="peer_route",
    )(h, wq, keys.reshape(nh * 2, nk, dk2), keys.reshape(nh * 2, nk, dk2), code)


def _route_build_kernel(e1_ref, e2_ref, g_ref, sub_ref, o_ref, *, nkeys):
    sub = sub_ref[...][None]
    one = jnp.ones((), BF16)
    zero = jnp.zeros((), BF16)
    at = jnp.where(e1_ref[...].astype(BF16) == sub, one, zero)
    bt = jnp.where(e2_ref[...].astype(BF16) == sub, g_ref[...].astype(BF16), zero)
    g3 = jnp.einsum('tik,tjk->tij', at, bt, preferred_element_type=F32)
    gt = pltpu.einshape('tij->itj', g3)
    for i in range(nkeys):
        o_ref[:, i * nkeys:(i + 1) * nkeys] = gt[i].astype(o_ref.dtype)


def peer_build(e1, e2, g, nkeys):
    nh, k, N = e1.shape
    ks = nh * k
    slot = lambda a: jnp.transpose(a, (2, 0, 1)).reshape(N, 1, ks)
    tb = _pick(N, (64, 32))
    spec = pl.BlockSpec((tb, 1, ks), lambda i: (i, 0, 0))
    sub = jnp.asarray(np.broadcast_to(np.arange(nkeys, dtype=np.float32)[:, None], (nkeys, ks)), BF16)
    return pl.pallas_call(
        functools.partial(_route_build_kernel, nkeys=nkeys),
        grid=(N // tb,),
        in_specs=[spec, spec, spec, pl.BlockSpec((nkeys, ks), lambda i: (0, 0))],
        out_specs=pl.BlockSpec((tb, nkeys * nkeys), lambda i: (i, 0)),
        out_shape=jax.ShapeDtypeStruct((N, nkeys * nkeys), BF16),
        compiler_params=_cparams(("arbitrary",)),
        name="peer_build",
    )(slot(e1), slot(e2), slot(g), sub)


def _peer_kernel(x_ref, ut_ref, v_ref, g_ref, r_ref, gt_ref, o_ref, acc_s):
    e = pl.program_id(1)

    @pl.when(e == 0)
    def _():
        acc_s[...] = jnp.zeros(acc_s.shape, F32)

    s = jnp.dot(x_ref[...], ut_ref[...], preferred_element_type=F32)
    p = (_gelu(s) * g_ref[...].astype(F32)).astype(BF16)
    acc_s[...] += jnp.dot(p, v_ref[...], preferred_element_type=F32)

    @pl.when(e == pl.num_programs(1) - 1)
    def _():
        o_ref[...] = r_ref[...] + gt_ref[...] * acc_s[...]


def _peer_kernel_shared_gate(x_ref, ut_ref, v_ref, g_ref, r_ref, gt_ref, o_ref, acc_s):
    _peer_kernel(x_ref, ut_ref, v_ref, g_ref, r_ref, gt_ref.at[0], o_ref, acc_s)


def peer_dense(h, ut, v, G, res, gate, rows_per_gate):
    N, D = h.shape
    E = ut.shape[1]
    tb = _pick(N, (512, 256, 128))
    eb = _pick(E, (1024, 512, 256, 128))
    if rows_per_gate == 1:
        kern = _peer_kernel
        g_spec = pl.BlockSpec((tb, D), lambda i, e: (i, 0))
        gate_arr = gate
    else:
        assert rows_per_gate % tb == 0
        per = rows_per_gate // tb
        kern = _peer_kernel_shared_gate
        g_spec = pl.BlockSpec((1, 1, D), lambda i, e: (i // per, 0, 0))
        gate_arr = gate.reshape(gate.shape[0], 1, D)
    return pl.pallas_call(
        kern,
        grid=(N // tb, E // eb),
        in_specs=[
            pl.BlockSpec((tb, D), lambda i, e: (i, 0)),
            pl.BlockSpec((D, eb), lambda i, e: (0, e)),
            pl.BlockSpec((eb, D), lambda i, e: (e, 0)),
            pl.BlockSpec((tb, eb), lambda i, e: (i, e)),
            pl.BlockSpec((tb, D), lambda i, e: (i, 0)),
            g_spec,
        ],
        out_specs=pl.BlockSpec((tb, D), lambda i, e: (i, 0)),
        out_shape=jax.ShapeDtypeStruct((N, D), F32),
        scratch_shapes=[pltpu.VMEM((tb, D), F32)],
        compiler_params=_cparams(("arbitrary", "arbitrary")),
        name="peer_dense",
    )(h, ut, v, G, res, gate_arr)


def _pad_rows(a, n, value=0.0, axis=1):
    pad = [(0, 0)] * a.ndim
    pad[axis] = (0, n - a.shape[axis])
    return jnp.pad(a, pad, constant_values=value)


def kernel(x_prompt, x_sample, state_mlstm_C, state_mlstm_n, state_mlstm_m, cache_swa_k, cache_swa_v,
           cache_diff_k, cache_diff_v, state_rglru_conv, state_rglru_h, page_table, c_prompt, c_sample,
           w_ada, b_ada, g_norm_mix, g_norm_ffn, rel_bias,
           mlstm_w_in, mlstm_b_gates, mlstm_g_out, mlstm_w_out,
           swa_w_in, swa_g_q, swa_g_k, swa_w_out,
           diff_w_in, diff_g_q, diff_g_k, diff_lam_q1, diff_lam_k1, diff_lam_q2, diff_lam_k2, diff_g_out, diff_w_out,
           rglru_w_in, rglru_conv_w, rglru_conv_b, rglru_w_a, rglru_b_a, rglru_w_x, rglru_b_x, rglru_lambda, rglru_w_out,
           peer_w_q, peer_keys, peer_u, peer_v):
    xp, xs = x_prompt, x_sample
    Bp, S, D = xp.shape
    Bs, T, _ = xs.shape
    depth = w_ada.shape[0]
    Ns = Bs * T

    c_all = jnp.concatenate([c_prompt, c_sample], axis=0)
    mod = ada_all(c_all, w_ada, b_ada)
    mod = mod.reshape(depth, Bp + Bs, 6, 1, D)

    def mods(i, which):
        m = mod[i, :, which]
        return m[:Bp], m[Bp:]

    def out_proj(hp, hs, w, gate_p, gate_s):
        nonlocal xp, xs
        wb = w.astype(BF16)
        xp = matmul(hp, wb, res=xp, gate=gate_p)
        gs_rows = jnp.broadcast_to(gate_s, (Bs, T, D)).reshape(1, Ns, D)
        xs = matmul(hs.reshape(1, Ns, -1), wb, res=xs.reshape(1, Ns, D), gate=gs_rows).reshape(Bs, T, D)

    for i in range(depth):
        kind = i % 4
        sh_p, sh_s = mods(i, 0)
        sc_p, sc_s = mods(i, 1)
        gt_p, gt_s = mods(i, 2)
        hp = modulate(xp, g_norm_mix[i], sh_p, sc_p)
        hs = modulate(xs, g_norm_mix[i], sh_s, sc_s)
        hs_flat = hs.reshape(1, Ns, D)
        if kind == 0:
            H = NH_A
            dk = state_mlstm_C.shape[2]
            dv = state_mlstm_C.shape[3]
            nmain = 2 * H * dk + 2 * H * dv
            wb = mlstm_w_in.astype(BF16)
            w_gate = mlstm_w_in[:, nmain:].astype(BF16)
            zp = matmul(hp, wb, cols=(0, nmain))
            gp = matmul(hp, w_gate)
            zs = matmul(hs_flat, wb, cols=(0, nmain)).reshape(Bs, T, nmain)
            gs = matmul(hs_flat, w_gate).reshape(Bs, T, 2 * H)
            chunk_p = _pick(S, (CHUNK_A, 128))
            op, mC_p, mn_p, mm_p = mlstm_core(
                zp, gp, mlstm_b_gates, jnp.zeros((Bp, H, dk, dv), F32), jnp.zeros((Bp, H, dk), F32),
                jnp.full((Bp, H), M_INIT, F32), mlstm_g_out, chunk_p)
            Tp = 128
            zs_pad = _pad_rows(zs, Tp)
            gs_pad = jnp.concatenate([
                _pad_rows(gs[..., :H], Tp, NEG), _pad_rows(gs[..., H:], Tp, -NEG)], axis=-1)
            os_, mC_s, mn_s, mm_s = mlstm_core(
                zs_pad, gs_pad, mlstm_b_gates, state_mlstm_C.astype(F32), state_mlstm_n.astype(F32),
                state_mlstm_m.astype(F32), mlstm_g_out, Tp)
            os_ = os_[:, :T]
            w_out = mlstm_w_out
        elif kind == 1:
            H = NH_B
            dh = D // H
            wb = swa_w_in.astype(BF16)
            zp = matmul(hp, wb, cols=(0, 2 * D))
            vp = matmul(hp, wb, cols=(2 * D, D))
            zs = matmul(hs_flat, wb, cols=(0, 2 * D)).reshape(Bs, T, 2 * D)
            vs_new = matmul(hs_flat, wb, cols=(2 * D, D)).reshape(Bs, T, D)
            gq = jnp.tile(swa_g_q, H)
            gk = jnp.tile(swa_g_k, H)
            qn_p, kn_p = headnorm(zp, gq, gk, dh, dh ** -0.5)
            qn_s, kn_s = headnorm(zs, gq, gk, dh, dh ** -0.5)
            t = _pick(S, (512, 256, 128))
            assert (_np_multiplicity(_toeplitz_dist_np(S // t, t)) > 0).any(axis=2).all()
            bm_tab = _toeplitz_bias(rel_bias, S // t, t, _swa_bias)
            op = flash_attention(qn_p, kn_p, vp, bm_tab, n_units=H, dqk=dh, dv=dh,
                                 qcol=lambda a: a, kcol=lambda a: a, vcol=lambda a: a, bmap=lambda a: a,
                                 t=t, out_dtype=BF16)
            swa_k_p = kn_p.reshape(Bp, S, H, dh)
            swa_v_p = vp.reshape(Bp, S, H, dh)
            os_ = swa_decode(qn_s, kn_s, vs_new, cache_swa_k, cache_swa_v, rel_bias)
            def shifted(cache, new):
                Wb = cache.shape[1]
                buf = lax.pad(cache.astype(F32), jnp.zeros((), F32), ((0, 0, 0), (-T, T, 0), (0, 0, 0), (0, 0, 0)))
                return lax.dynamic_update_slice(buf, new.reshape(Bs, T, H, dh), (0, Wb - T, 0, 0))

            swa_k_s = shifted(cache_swa_k, kn_s)
            swa_v_s = shifted(cache_swa_v, vs_new)
            w_out = swa_w_out
        elif kind == 2:
            H = NH_C
            dh = D // (2 * H)
            dv = 2 * dh
            lam_init = 0.8 - 0.6 * math.exp(-0.3 * i)
            lam4 = jnp.stack([diff_lam_q1, diff_lam_k1, diff_lam_q2, diff_lam_k2]).astype(F32)
            wb = diff_w_in.astype(BF16)
            zp = matmul(hp, wb, cols=(0, 2 * D))
            vp = matmul(hp, wb, cols=(2 * D, D))
            zs = matmul(hs_flat, wb, cols=(0, 2 * D)).reshape(Bs, T, 2 * D)
            vs_new = matmul(hs_flat, wb, cols=(2 * D, D)).reshape(Bs, T, D)
            gq = jnp.tile(diff_g_q.reshape(-1), H)
            gk = jnp.tile(diff_g_k.reshape(-1), H)
            qn_p, kn_p = headnorm(zp, gq, gk, dh, dh ** -0.5)
            qn_s, kn_s = headnorm(zs, gq, gk, dh, dh ** -0.5)
            t = _pick(S, (512, 256, 128))
            bm_tab = _toeplitz_bias(rel_bias, S // t, t, _causal_bias)
            o2 = flash_attention(qn_p, kn_p, vp, bm_tab, n_units=2 * H, dqk=dh, dv=dv,
                                 qcol=lambda a: a, kcol=lambda a: a, vcol=lambda a: a // 2,
                                 bmap=lambda a: (a % 2) * H + a // 2, t=t, out_dtype=F32)
            op = diff_combine(o2, lam4, diff_g_out, lam_init)
            diff_k_p = kn_p.reshape(Bp, S, H, 2 * dh)
            diff_v_p = vp.reshape(Bp, S, H, dv)
            diff_k_s = kn_s.reshape(Bs, T, H, 2 * dh)
            diff_v_s = vs_new.reshape(Bs, T, H, dv)
            os_ = diff_paged(qn_s.reshape(Bs, T, H, 2, dh), diff_k_s, diff_v_s, cache_diff_k, cache_diff_v,
                             page_table, rel_bias, lam4, diff_g_out, lam_init)
            w_out = diff_w_out
        else:
            wb = rglru_w_in.astype(BF16)
            zp = matmul(hp, wb)
            zs = matmul(hs_flat, wb).reshape(Bs, T, -1)
            tblk = _pick(S, (256, 128))
            op, conv_p, h_p = rglru_core(zp, jnp.zeros((Bp, CONV_W - 1, D), F32), jnp.zeros((Bp, D), F32),
                                         rglru_conv_w, rglru_conv_b, rglru_w_a, rglru_b_a, rglru_w_x, rglru_b_x,
                                         rglru_lambda, tblk, tblk)
            os_, conv_s, h_s = rglru_core(_pad_rows(zs, 8), state_rglru_conv, state_rglru_h,
                                          rglru_conv_w, rglru_conv_b, rglru_w_a, rglru_b_a, rglru_w_x, rglru_b_x,
                                          rglru_lambda, T, 8)
            os_ = os_[:, :T]
            w_out = rglru_w_out
        out_proj(op, os_, w_out, gt_p, gt_s)

        sh_p, sh_s = mods(i, 3)
        sc_p, sc_s = mods(i, 4)
        gt_p, gt_s = mods(i, 5)
        hp = modulate(xp, g_norm_ffn[i], sh_p, sc_p).reshape(Bp * S, D)
        hs = modulate(xs, g_norm_ffn[i], sh_s, sc_s).reshape(Ns, D)
        wq = peer_w_q[i].astype(BF16)
        ut = jnp.transpose(peer_u[i]).astype(BF16)
        vb = peer_v[i].astype(BF16)
        nk = peer_keys.shape[3]
        Gp = peer_build(*peer_route(hp, wq, peer_keys[i]), nk)
        Ns_pad = -(-Ns // LANES) * LANES
        Gs = peer_build(*peer_route(_pad_rows(hs, Ns_pad, axis=0), wq, peer_keys[i]), nk)[:Ns]
        xp = peer_dense(hp, ut, vb, Gp, xp.reshape(Bp * S, D), gt_p.reshape(Bp, D), S).reshape(Bp, S, D)
        gs_rows = jnp.broadcast_to(gt_s, (Bs, T, D)).reshape(Ns, D)
        xs = peer_dense(hs, ut, vb, Gs, xs.reshape(Ns, D), gs_rows, 1).reshape(Bs, T, D)

    return (xp, xs, mC_p, mC_s, mn_p, mn_s, mm_p, mm_s, swa_k_p, swa_k_s, swa_v_p, swa_v_s,
            diff_k_p, diff_k_s, diff_v_p, diff_v_s, conv_p, conv_s, h_p, h_s)
```

```python
import functools
import math

import numpy as np
import jax
import jax.numpy as jnp
from jax import lax
from jax.experimental import pallas as pl
from jax.experimental.pallas import tpu as pltpu

F32 = jnp.float32
BF16 = jnp.bfloat16
HIGHEST = lax.Precision.HIGHEST

EPS = 1e-6
NEG = -1e30
VMEM_LIMIT = 56 * 1024 * 1024
LANES = 128

NH_A, CHUNK_A, M_INIT = 8, 256, -1e30
NH_B = 16
DIL_PAIRS = ((128, 1), (512, 4), (2048, 16))
NH_C = 8
NBLK_D, CONV_W, LRU_C = 16, 4, 8.0
NH_P, N_KEYS, TOPK_P = 8, 128, 16
REL_BUCKETS, REL_MAX_DIST = 32, 2048
PAGE_SIZE = 128


def _cparams(sem):
    return pltpu.CompilerParams(dimension_semantics=sem, vmem_limit_bytes=VMEM_LIMIT)


def _bdot(a, b):
    return jnp.dot(a.astype(BF16), b.astype(BF16), preferred_element_type=F32)


def _bdot_nt(a, b):
    return lax.dot_general(a.astype(BF16), b.astype(BF16), (((1,), (1,)), ((), ())),
                           preferred_element_type=F32)


def _gelu(x):
    return 0.5 * x * (1.0 + lax.erf(x * (1.0 / math.sqrt(2.0))))


def _log_sigmoid(x):
    return jnp.minimum(x, 0.0) - jnp.log1p(jnp.exp(-jnp.abs(x)))


def _softplus(x):
    return jnp.maximum(x, 0.0) + jnp.log1p(jnp.exp(-jnp.abs(x)))


def _pick(n, pref):
    for t in pref:
        if n % t == 0:
            return t
    return n


def _ada_kernel(c_ref, w_ref, b_ref, o_ref):
    c = c_ref[...]
    a = c * jax.nn.sigmoid(c)
    o_ref[0] = jnp.dot(a, w_ref[0], preferred_element_type=F32, precision=HIGHEST) + b_ref[0]


def ada_all(c_all, w_ada, b_ada):
    R, D = c_all.shape
    nl, _, N = w_ada.shape
    tn = _pick(N, (1024, 512, 256, 128))
    return pl.pallas_call(
        _ada_kernel,
        grid=(nl, N // tn),
        in_specs=[pl.BlockSpec((R, D), lambda l, j: (0, 0)),
                  pl.BlockSpec((1, D, tn), lambda l, j: (l, 0, j)),
                  pl.BlockSpec((1, 1, tn), lambda l, j: (l, 0, j))],
        out_specs=pl.BlockSpec((1, R, tn), lambda l, j: (l, 0, j)),
        out_shape=jax.ShapeDtypeStruct((nl, R, N), F32),
        compiler_params=_cparams(("arbitrary", "arbitrary")),
        name="ada",
    )(c_all, w_ada, b_ada.reshape(nl, 1, N))


def _modulate_kernel(x_ref, g_ref, sh_ref, sc_ref, o_ref):
    x = x_ref[0]
    ms = jnp.mean(x * x, axis=-1, keepdims=True)
    y = x * lax.rsqrt(ms + EPS) * g_ref[...]
    o_ref[0] = (y * (1.0 + sc_ref[0]) + sh_ref[0]).astype(o_ref.dtype)


def modulate(x, g, shift, scale):
    B, L, D = x.shape
    tl = _pick(L, (512, 256, 128))
    return pl.pallas_call(
        _modulate_kernel,
        grid=(B, L // tl),
        in_specs=[pl.BlockSpec((1, tl, D), lambda b, i: (b, i, 0)),
                  pl.BlockSpec((1, D), lambda b, i: (0, 0)),
                  pl.BlockSpec((1, 1, D), lambda b, i: (b, 0, 0)),
                  pl.BlockSpec((1, 1, D), lambda b, i: (b, 0, 0))],
        out_specs=pl.BlockSpec((1, tl, D), lambda b, i: (b, i, 0)),
        out_shape=jax.ShapeDtypeStruct((B, L, D), BF16),
        compiler_params=_cparams(("arbitrary", "arbitrary")),
        name="modulate",
    )(x, g.reshape(1, D), shift, scale)


def _mm_kernel(x_ref, w_ref, o_ref):
    o_ref[0] = jnp.dot(x_ref[0], w_ref[...], preferred_element_type=F32).astype(o_ref.dtype)


def _mm_res_kernel(x_ref, w_ref, r_ref, gt_ref, o_ref):
    acc = jnp.dot(x_ref[0], w_ref[...], preferred_element_type=F32)
    o_ref[0] = r_ref[0] + gt_ref[0] * acc


def matmul(x, w, res=None, gate=None, out_dtype=F32, cols=None):
    B, L, K = x.shape
    c0, N = cols if cols is not None else (0, w.shape[1])
    tm = _pick(L, (512, 256, 128))
    tn = next((c for c in (1024, 512, 256, 128) if N % c == 0 and c0 % c == 0), N)
    assert c0 % tn == 0 and (tn == w.shape[1] or tn % LANES == 0)
    j0 = c0 // tn
    grid = (N // tn, B, L // tm)
    x_spec = pl.BlockSpec((1, tm, K), lambda j, b, i: (b, i, 0))
    w_spec = pl.BlockSpec((K, tn), lambda j, b, i: (0, j0 + j))
    o_spec = pl.BlockSpec((1, tm, tn), lambda j, b, i: (b, i, j))
    if res is None:
        return pl.pallas_call(
            _mm_kernel, grid=grid, in_specs=[x_spec, w_spec], out_specs=o_spec,
            out_shape=jax.ShapeDtypeStruct((B, L, N), out_dtype),
            compiler_params=_cparams(("arbitrary",) * 3), name="matmul",
        )(x, w)
    if gate.shape[1] == 1:
        g_spec = pl.BlockSpec((1, 1, tn), lambda j, b, i: (b, 0, j))
    else:
        g_spec = pl.BlockSpec((1, tm, tn), lambda j, b, i: (b, i, j))
    return pl.pallas_call(
        _mm_res_kernel, grid=grid, in_specs=[x_spec, w_spec, o_spec, g_spec], out_specs=o_spec,
        out_shape=jax.ShapeDtypeStruct((B, L, N), F32),
        compiler_params=_cparams(("arbitrary",) * 3), name="matmul_res",
    )(x, w, res, gate)


def _modulated(xn, gn_ref, sh_ref, sc_ref):
    ms = jnp.mean(xn * xn, axis=-1, keepdims=True)
    return xn * lax.rsqrt(ms + EPS) * gn_ref[...] * (1.0 + sc_ref[0]) + sh_ref[0]


def _mm_res_mod_kernel(x_ref, w_ref, r_ref, gt_ref, gn_ref, sh_ref, sc_ref, o_ref, h_ref):
    acc = jnp.dot(x_ref[0], w_ref[...], preferred_element_type=F32)
    xn = r_ref[0] + gt_ref[0] * acc
    o_ref[0] = xn
    h_ref[0] = _modulated(xn, gn_ref, sh_ref, sc_ref).astype(h_ref.dtype)


def _row_param_spec(p, tm, n):
    if p.shape[1] == 1:
        return pl.BlockSpec((1, 1, n), lambda b, i: (b, 0, 0))
    return pl.BlockSpec((1, tm, n), lambda b, i: (b, i, 0))


def matmul_res_mod(x, w, res, gate, g_norm, shift, scale):
    B, L, K = x.shape
    N = w.shape[1]
    tm = _pick(L, (256, 128))
    row = pl.BlockSpec((1, tm, N), lambda b, i: (b, i, 0))
    return pl.pallas_call(
        _mm_res_mod_kernel,
        grid=(B, L // tm),
        in_specs=[pl.BlockSpec((1, tm, K), lambda b, i: (b, i, 0)),
                  pl.BlockSpec((K, N), lambda b, i: (0, 0)),
                  row, _row_param_spec(gate, tm, N),
                  pl.BlockSpec((1, N), lambda b, i: (0, 0)),
                  _row_param_spec(shift, tm, N), _row_param_spec(scale, tm, N)],
        out_specs=[row, row],
        out_shape=[jax.ShapeDtypeStruct((B, L, N), F32), jax.ShapeDtypeStruct((B, L, N), BF16)],
        compiler_params=_cparams(("arbitrary", "arbitrary")),
        name="matmul_res_mod",
    )(x, w, res, gate, g_norm.reshape(1, N), shift, scale)


def _mm_headnorm_kernel(x_ref, w_ref, g_ref, o_ref, *, dh, scale):
    acc = jnp.dot(x_ref[0], w_ref[...], preferred_element_type=F32)
    for h in range(acc.shape[1] // dh):
        sl = slice(h * dh, (h + 1) * dh)
        a = acc[:, sl]
        ms = jnp.mean(a * a, axis=-1, keepdims=True)
        y = a * lax.rsqrt(ms + EPS) * g_ref[:, sl]
        o_ref[0, :, sl] = (y * scale if scale != 1.0 else y).astype(o_ref.dtype)


def matmul_headnorm(x, w, cols, gain_full, dh, scale, out_dtype):
    B, L, K = x.shape
    c0, N = cols
    tm = _pick(L, (512, 256, 128))
    tn = next(c for c in (1024, 512, 256, 128) if N % c == 0 and c0 % c == 0)
    j0 = c0 // tn
    return pl.pallas_call(
        functools.partial(_mm_headnorm_kernel, dh=dh, scale=scale),
        grid=(N // tn, B, L // tm),
        in_specs=[pl.BlockSpec((1, tm, K), lambda j, b, i: (b, i, 0)),
                  pl.BlockSpec((K, tn), lambda j, b, i: (0, j0 + j)),
                  pl.BlockSpec((1, tn), lambda j, b, i: (0, j))],
        out_specs=pl.BlockSpec((1, tm, tn), lambda j, b, i: (b, i, j)),
        out_shape=jax.ShapeDtypeStruct((B, L, N), out_dtype),
        compiler_params=_cparams(("arbitrary",) * 3),
        name="matmul_headnorm",
    )(x, w, gain_full.reshape(1, N))


def _mlstm_kernel(q_ref, k_ref, v_ref, o_ref, gc_ref, gr_ref, bc_ref, br_ref,
                  C0_ref, n0_ref, m0_ref, gout_ref,
                  hs_ref, C_ref, n_ref, m_ref, C_s, n_s, m_s, *, dk):
    ci = pl.program_id(2)

    @pl.when(ci == 0)
    def _():
        C_s[...] = C0_ref[0, 0]
        n_s[...] = n0_ref[0, 0]
        m_s[...] = m0_ref[0, 0]

    q = q_ref[0]
    k = k_ref[0] * (dk ** -0.5)
    v = v_ref[0]
    c = q.shape[0]
    gc = gc_ref[0, 0] + bc_ref[0]
    gr = gr_ref[0, 0] + br_ref[0]
    li_c, lf_c = gc[:, 0:1], _log_sigmoid(gc[:, 1:2])
    li_r, lf_r = gr[0:1, :], _log_sigmoid(gr[1:2, :])
    row = lax.broadcasted_iota(jnp.int32, (c, c), 0)
    col = lax.broadcasted_iota(jnp.int32, (c, c), 1)
    causal = col <= row
    b_c = jnp.sum(jnp.where(causal, lf_r, 0.0), axis=1, keepdims=True)
    b_r = jnp.sum(jnp.where(row <= col, lf_c, 0.0), axis=0, keepdims=True)
    m_prev = m_s[...]
    Dm = jnp.where(causal, b_c - b_r + li_r, NEG)
    m_t = jnp.maximum(b_c + m_prev, jnp.max(Dm, axis=1, keepdims=True))
    S = _bdot_nt(q, k) * jnp.exp(Dm - m_t)
    inter = jnp.exp(b_c + m_prev - m_t)
    C = C_s[...]
    n = n_s[...]
    num = _bdot(S, v) + inter * _bdot(q, C)
    den = jnp.sum(S, axis=1, keepdims=True) + inter * jnp.sum(q * n, axis=1, keepdims=True)
    h = num / jnp.maximum(jnp.abs(den), jnp.exp(-m_t))
    hh = jax.nn.sigmoid(o_ref[0]) * h
    ms = jnp.mean(hh * hh, axis=-1, keepdims=True)
    hs_ref[0] = (hh * lax.rsqrt(ms + EPS) * gout_ref[...]).astype(hs_ref.dtype)
    m_new = m_t[c - 1:c, :]
    b_last = b_c[c - 1:c, :]
    w_r = jnp.exp(b_last - b_r + li_r - m_new)
    w_c = jnp.exp(b_last - b_c + li_c - m_new)
    decay = jnp.exp(b_last + m_prev - m_new)
    C_new = decay * C + _bdot(k.T, w_c * v)
    n_new = decay * n + jnp.dot(w_r, k, preferred_element_type=F32, precision=HIGHEST)
    C_s[...] = C_new
    n_s[...] = n_new
    m_s[...] = m_new

    @pl.when(ci == pl.num_programs(2) - 1)
    def _():
        C_ref[0, 0] = C_new
        n_ref[0, 0] = n_new
        m_ref[0, 0] = m_new


def mlstm_core(z, g, b_gates, C0, n0, m0, g_out, chunk):
    B, L, _ = z.shape
    H = NH_A
    dk = C0.shape[2]
    dv = C0.shape[3]
    nc = L // chunk
    g4 = g.reshape(B, L, 2, H)
    gcol = jnp.transpose(g4, (0, 3, 1, 2))
    grow = jnp.transpose(g4, (0, 3, 2, 1))
    bg = b_gates.reshape(2, H)
    bcol = jnp.transpose(bg, (1, 0)).reshape(H, 1, 2)
    brow = jnp.transpose(bg, (1, 0)).reshape(H, 2, 1)
    kq = (H * dk) // dk
    vo = (2 * H * dk) // dv
    oo = vo + H
    outs = pl.pallas_call(
        functools.partial(_mlstm_kernel, dk=dk),
        grid=(B, H, nc),
        in_specs=[
            pl.BlockSpec((1, chunk, dk), lambda b, h, c: (b, c, h)),
            pl.BlockSpec((1, chunk, dk), lambda b, h, c: (b, c, kq + h)),
            pl.BlockSpec((1, chunk, dv), lambda b, h, c: (b, c, vo + h)),
            pl.BlockSpec((1, chunk, dv), lambda b, h, c: (b, c, oo + h)),
            pl.BlockSpec((1, 1, chunk, 2), lambda b, h, c: (b, h, c, 0)),
            pl.BlockSpec((1, 1, 2, chunk), lambda b, h, c: (b, h, 0, c)),
            pl.BlockSpec((1, 1, 2), lambda b, h, c: (h, 0, 0)),
            pl.BlockSpec((1, 2, 1), lambda b, h, c: (h, 0, 0)),
            pl.BlockSpec((1, 1, dk, dv), lambda b, h, c: (b, h, 0, 0)),
            pl.BlockSpec((1, 1, 1, dk), lambda b, h, c: (b, h, 0, 0)),
            pl.BlockSpec((1, 1, 1, 1), lambda b, h, c: (b, h, 0, 0)),
            pl.BlockSpec((1, dv), lambda b, h, c: (0, h)),
        ],
        out_specs=[
            pl.BlockSpec((1, chunk, dv), lambda b, h, c: (b, c, h)),
            pl.BlockSpec((1, 1, dk, dv), lambda b, h, c: (b, h, 0, 0)),
            pl.BlockSpec((1, 1, 1, dk), lambda b, h, c: (b, h, 0, 0)),
            pl.BlockSpec((1, 1, 1, 1), lambda b, h, c: (b, h, 0, 0)),
        ],
        out_shape=[
            jax.ShapeDtypeStruct((B, L, H * dv), BF16),
            jax.ShapeDtypeStruct((B, H, dk, dv), F32),
            jax.ShapeDtypeStruct((B, H, 1, dk), F32),
            jax.ShapeDtypeStruct((B, H, 1, 1), F32),
        ],
        scratch_shapes=[pltpu.VMEM((dk, dv), F32), pltpu.VMEM((1, dk), F32), pltpu.VMEM((1, 1), F32)],
        compiler_params=_cparams(("arbitrary",) * 3),
        name="mlstm",
    )(z, z, z, z, gcol, grow, bcol, brow, C0, n0.reshape(B, H, 1, dk), m0.reshape(B, H, 1, 1),
      g_out.reshape(1, H * dv))
    hs, C, n, m = outs
    return hs, C, n.reshape(B, H, dk), m.reshape(B, H)


def _headnorm_kernel(q_ref, k_ref, gq_ref, gk_ref, qo_ref, ko_ref, *, dh, q_scale):
    nh = q_ref.shape[2] // dh
    for h in range(nh):
        sl = slice(h * dh, (h + 1) * dh)
        for src, g, dst, sc in ((q_ref, gq_ref, qo_ref, q_scale), (k_ref, gk_ref, ko_ref, 1.0)):
            x = src[0, :, sl]
            ms = jnp.mean(x * x, axis=-1, keepdims=True)
            y = x * lax.rsqrt(ms + EPS) * g[:, sl]
            dst[0, :, sl] = (y * sc if sc != 1.0 else y).astype(dst.dtype)


def headnorm(z, gq_full, gk_full, dh, q_scale):
    B, L, W2 = z.shape
    W = W2 // 2
    tl = _pick(L, (256, 128))
    return pl.pallas_call(
        functools.partial(_headnorm_kernel, dh=dh, q_scale=q_scale),
        grid=(B, L // tl),
        in_specs=[pl.BlockSpec((1, tl, W), lambda b, i: (b, i, 0)),
                  pl.BlockSpec((1, tl, W), lambda b, i: (b, i, 1)),
                  pl.BlockSpec((1, W), lambda b, i: (0, 0)),
                  pl.BlockSpec((1, W), lambda b, i: (0, 0))],
        out_specs=[pl.BlockSpec((1, tl, W), lambda b, i: (b, i, 0)),
                   pl.BlockSpec((1, tl, W), lambda b, i: (b, i, 0))],
        out_shape=[jax.ShapeDtypeStruct((B, L, W), BF16), jax.ShapeDtypeStruct((B, L, W), F32)],
        compiler_params=_cparams(("arbitrary", "arbitrary")),
        name="headnorm",
    )(z, z, gq_full.reshape(1, W), gk_full.reshape(1, W))


def _rel_bucket(dist):
    exact = REL_BUCKETS // 2
    d_f = jnp.maximum(dist, 1).astype(F32)
    large = exact + (jnp.log(d_f / exact) / math.log(REL_MAX_DIST / exact) * (REL_BUCKETS - exact)).astype(jnp.int32)
    large = jnp.minimum(large, REL_BUCKETS - 1)
    return jnp.where(dist < exact, dist, large)


def _bias_of_dist(rel_bias, dist):
    bucket = _rel_bucket(jnp.maximum(dist, 0))
    out = jnp.zeros((rel_bias.shape[1],) + dist.shape, F32)
    for b in range(REL_BUCKETS):
        out = jnp.where(bucket[None] == b, rel_bias[b].reshape((-1,) + (1,) * dist.ndim), out)
    return out


def _np_multiplicity(dist):
    cnt = np.zeros(dist.shape, np.int64)
    for w, d in DIL_PAIRS:
        cnt = cnt + ((dist % d == 0) & (dist <= w) & (dist >= 0))
    return cnt


def _swa_bias(rel_bias, dist_np):
    cnt = _np_multiplicity(dist_np)
    logc = jnp.log(jnp.asarray(np.maximum(cnt, 1), F32))
    bias = _bias_of_dist(rel_bias, jnp.asarray(np.maximum(dist_np, 0), jnp.int32))
    return jnp.where(jnp.asarray(cnt > 0)[None], bias + logc[None], NEG), cnt > 0


def _causal_bias(rel_bias, dist_np):
    bias = _bias_of_dist(rel_bias, jnp.asarray(np.maximum(dist_np, 0), jnp.int32))
    return jnp.where(jnp.asarray(dist_np >= 0)[None], bias, NEG), dist_np >= 0


def _toeplitz_dist_np(n_off, t):
    o = np.arange(n_off)[:, None, None]
    i = np.arange(t)[None, :, None]
    j = np.arange(t)[None, None, :]
    return o * t + i - j


def _toeplitz_kernel(b_ref, o_ref):
    t = o_ref.shape[2]
    rows = jnp.broadcast_to(b_ref[0, 0], (t, 2 * t))
    skew = pltpu.roll(rows, 0, 1, stride=1, stride_axis=0)
    o_ref[0, 0] = skew[:, t:]


def _toeplitz_bias(rel_bias, n_off, t, bias_fn):
    dist = np.arange(-(t - 1), n_off * t)
    f, _ = bias_fn(rel_bias, dist)
    maps = f.shape[0]
    want = np.arange(n_off)[:, None] * t + t - np.arange(2 * t)[None, :]
    idx = np.minimum(want + (t - 1), len(dist) - 1)
    b = f[:, idx].reshape(maps, n_off, 1, 2 * t)
    return pl.pallas_call(
        _toeplitz_kernel,
        grid=(maps, n_off),
        in_specs=[pl.BlockSpec((1, 1, 1, 2 * t), lambda m, o: (m, o, 0, 0))],
        out_specs=pl.BlockSpec((1, 1, t, t), lambda m, o: (m, o, 0, 0)),
        out_shape=jax.ShapeDtypeStruct((maps, n_off, t, t), F32),
        compiler_params=_cparams(("arbitrary", "arbitrary")),
        name="toeplitz_bias",
    )(b)


def _flash_kernel(qi_ref, ki_ref, q_ref, k_ref, v_ref, bm_ref, o_ref, m_s, l_s, acc_s):
    p = pl.program_id(1)
    qi = qi_ref[p]
    ki = ki_ref[p]
    nb, tq, dv = acc_s.shape
    tk = k_ref.shape[1]

    @pl.when(ki == 0)
    def _():
        m_s[...] = jnp.full(m_s.shape, NEG, F32)
        l_s[...] = jnp.zeros(l_s.shape, F32)
        acc_s[...] = jnp.zeros(acc_s.shape, F32)

    bm = bm_ref[0, 0]
    for b in range(nb):
        s = _bdot_nt(q_ref[b], k_ref[b]) + bm
        m_prev = m_s[b]
        m_next = jnp.maximum(m_prev, jnp.max(s, axis=1, keepdims=True))
        pr = jnp.exp(s - jnp.tile(m_next, (1, tk // LANES)))
        alpha = jnp.exp(m_prev - m_next)
        l_s[b] = alpha * l_s[b] + jnp.sum(pr, axis=1, keepdims=True)
        acc_s[b] = acc_s[b] * jnp.tile(alpha, (1, dv // LANES)) + _bdot(pr, v_ref[b])
        m_s[b] = m_next

    @pl.when(ki == qi)
    def _():
        for b in range(nb):
            o_ref[b] = (acc_s[b] / jnp.tile(l_s[b], (1, dv // LANES))).astype(o_ref.dtype)


def flash_attention(q, k, v, bm_tab, *, n_units, dqk, dv, qcol, kcol, vcol, bmap, t, out_dtype):
    B, L = q.shape[:2]
    nq = L // t
    pairs = [(qi, ki) for qi in range(nq) for ki in range(qi + 1)]
    qi_arr = jnp.asarray(np.array([p[0] for p in pairs], np.int32))
    ki_arr = jnp.asarray(np.array([p[1] for p in pairs], np.int32))
    grid_spec = pltpu.PrefetchScalarGridSpec(
        num_scalar_prefetch=2,
        grid=(n_units, len(pairs)),
        in_specs=[
            pl.BlockSpec((B, t, dqk), lambda a, p, qa, ka: (0, qa[p], qcol(a))),
            pl.BlockSpec((B, t, dqk), lambda a, p, qa, ka: (0, ka[p], kcol(a))),
            pl.BlockSpec((B, t, dv), lambda a, p, qa, ka: (0, ka[p], vcol(a))),
            pl.BlockSpec((1, 1, t, t), lambda a, p, qa, ka: (bmap(a), qa[p] - ka[p], 0, 0)),
        ],
        out_specs=pl.BlockSpec((B, t, dv), lambda a, p, qa, ka: (0, qa[p], a)),
        scratch_shapes=[pltpu.VMEM((B, t, LANES), F32), pltpu.VMEM((B, t, LANES), F32),
                        pltpu.VMEM((B, t, dv), F32)],
    )
    return pl.pallas_call(
        _flash_kernel,
        grid_spec=grid_spec,
        out_shape=jax.ShapeDtypeStruct((B, L, n_units * dv), out_dtype),
        compiler_params=_cparams(("arbitrary", "arbitrary")),
        name="flash_attention",
    )(qi_arr, ki_arr, q, k, v, bm_tab)


def _diff_lambda(lam_ref, lam_init):
    lq1, lk1, lq2, lk2 = lam_ref[0:1, :], lam_ref[1:2, :], lam_ref[2:3, :], lam_ref[3:4, :]
    return (jnp.exp(jnp.sum(lq1 * lk1, axis=-1, keepdims=True))
            - jnp.exp(jnp.sum(lq2 * lk2, axis=-1, keepdims=True)) + lam_init)


def _diff_combine_kernel(o_ref, lam_ref, g_ref, out_ref, *, dv, lam_init):
    lam = _diff_lambda(lam_ref, lam_init)
    nh = out_ref.shape[2] // dv
    for h in range(nh):
        o0 = o_ref[0, :, (2 * h) * dv:(2 * h + 1) * dv]
        o1 = o_ref[0, :, (2 * h + 1) * dv:(2 * h + 2) * dv]
        d = o0 - lam * o1
        ms = jnp.mean(d * d, axis=-1, keepdims=True)
        out_ref[0, :, h * dv:(h + 1) * dv] = (d * lax.rsqrt(ms + EPS) * g_ref[...] * (1.0 - lam_init)
                                              ).astype(out_ref.dtype)


def diff_combine(o, lam4, g_out, lam_init):
    B, L, W2 = o.shape
    dv = g_out.shape[0]
    W = W2 // 2
    tl = _pick(L, (256, 128))
    return pl.pallas_call(
        functools.partial(_diff_combine_kernel, dv=dv, lam_init=lam_init),
        grid=(B, L // tl),
        in_specs=[pl.BlockSpec((1, tl, W2), lambda b, i: (b, i, 0)),
                  pl.BlockSpec(lam4.shape, lambda b, i: (0, 0)),
                  pl.BlockSpec((1, dv), lambda b, i: (0, 0))],
        out_specs=pl.BlockSpec((1, tl, W), lambda b, i: (b, i, 0)),
        out_shape=jax.ShapeDtypeStruct((B, L, W), BF16),
        compiler_params=_cparams(("arbitrary", "arbitrary")),
        name="diff_combine",
    )(o, lam4, g_out.reshape(1, dv))


def _decode_kernel(*refs, n_scalar, n_chunks, diff, lam_init, n_heads, n_new):
    refs = refs[n_scalar:]
    q_ref = refs[0]
    k_refs = refs[1:1 + n_chunks]
    v_refs = refs[1 + n_chunks:1 + 2 * n_chunks]
    bm_refs = refs[1 + 2 * n_chunks:1 + 3 * n_chunks]
    rest = refs[1 + 3 * n_chunks:]
    if diff:
        kn_ref, vn_ref, bmn_ref, lam_ref, g_ref, out_ref, m_s, l_s, acc_s = rest
    else:
        kn_ref, vn_ref, bmn_ref, out_ref, m_s, l_s, acc_s = rest
    p = pl.program_id(1)

    @pl.when(p == 0)
    def _():
        m_s[...] = jnp.full(m_s.shape, NEG, F32)
        l_s[...] = jnp.zeros(l_s.shape, F32)
        acc_s[...] = jnp.zeros(acc_s.shape, F32)

    q = q_ref[0]

    def absorb(k2s, v2s, bms):
        ss = [_bdot_nt(q, k2) + bm for k2, bm in zip(k2s, bms)]
        m_old = m_s[...]
        m_new = m_old
        for s in ss:
            m_new = jnp.maximum(m_new, jnp.max(s, axis=-1, keepdims=True))
        alpha = jnp.exp(m_old - m_new)
        l_new = alpha * l_s[...]
        acc = alpha * acc_s[...]
        for s, v2 in zip(ss, v2s):
            pr = jnp.exp(s - m_new)
            l_new = l_new + jnp.sum(pr, axis=-1, keepdims=True)
            acc = acc + _bdot(pr, v2)
        l_s[...] = l_new
        acc_s[...] = acc
        m_s[...] = m_new

    def rows2d(ref):
        x = ref[0]
        return x.reshape(x.shape[0] * x.shape[1], x.shape[2])

    absorb([rows2d(r) for r in k_refs], [rows2d(r) for r in v_refs], [r[0] for r in bm_refs])

    @pl.when(p == pl.num_programs(1) - 1)
    def _():
        absorb([kn_ref[0]], [vn_ref[0]], [bmn_ref[...]])
        o = acc_s[...] / l_s[...]
        if diff:
            half = n_heads * n_new
            lam = _diff_lambda(lam_ref, lam_init)
            d = o[0:half] - lam * o[half:2 * half]
            ms = jnp.mean(d * d, axis=-1, keepdims=True)
            o = d * lax.rsqrt(ms + EPS) * g_ref[...] * (1.0 - lam_init)
        dv = o.shape[1]
        for h in range(n_heads):
            out_ref[0, :, h * dv:(h + 1) * dv] = o[h * n_new:(h + 1) * n_new].astype(out_ref.dtype)


def _head_expand(bias_rows, row_head, n_heads):
    ok = jnp.asarray(row_head[:, None] == np.arange(n_heads)[None, :])
    out = jnp.where(ok[:, None, :], bias_rows[..., None], NEG)
    return out.reshape(bias_rows.shape[:-1] + (bias_rows.shape[-1] * n_heads,))


def swa_decode(qn, k_new, v_new, cache_k, cache_v, rel_bias):
    B, Wb, H, dh = cache_k.shape
    T = qn.shape[1]
    R = H * T
    tk = _pick(Wb, (512, 256, 128))
    nblk = Wb // tk
    q2 = jnp.transpose(qn.reshape(B, T, H, dh), (0, 2, 1, 3)).reshape(B, R, dh)
    row_head = np.arange(R) // T
    dist_np = (Wb + np.arange(T))[:, None] - np.arange(Wb)[None, :]
    bias, ok = _swa_bias(rel_bias, dist_np)
    assert ok[:, :tk].any(axis=1).all()
    bm = _head_expand(bias.reshape(R, Wb), row_head, H)
    bm = jnp.transpose(bm.reshape(R, nblk, tk * H), (1, 0, 2))
    dn = np.arange(T)[:, None] - np.arange(T)[None, :]
    bias_n, _ = _swa_bias(rel_bias, dn)
    bmn = _head_expand(bias_n.reshape(R, T), row_head, H)
    kn2 = k_new.reshape(B, T * H, dh)
    vn2 = v_new.reshape(B, T * H, dh)
    return pl.pallas_call(
        functools.partial(_decode_kernel, n_scalar=0, n_chunks=1, diff=False, lam_init=0.0, n_heads=H, n_new=T),
        grid=(B, nblk),
        in_specs=[
            pl.BlockSpec((1, R, dh), lambda b, p: (b, 0, 0)),
            pl.BlockSpec((1, tk, H, dh), lambda b, p: (b, p, 0, 0)),
            pl.BlockSpec((1, tk, H, dh), lambda b, p: (b, p, 0, 0)),
            pl.BlockSpec((1, R, tk * H), lambda b, p: (p, 0, 0)),
            pl.BlockSpec((1, T * H, dh), lambda b, p: (b, 0, 0)),
            pl.BlockSpec((1, T * H, dh), lambda b, p: (b, 0, 0)),
            pl.BlockSpec((R, T * H), lambda b, p: (0, 0)),
        ],
        out_specs=pl.BlockSpec((1, T, H * dh), lambda b, p: (b, 0, 0)),
        out_shape=jax.ShapeDtypeStruct((B, T, H * dh), BF16),
        scratch_shapes=[pltpu.VMEM((R, 1), F32), pltpu.VMEM((R, 1), F32), pltpu.VMEM((R, dh), F32)],
        compiler_params=_cparams(("arbitrary", "arbitrary")),
        name="swa_decode",
    )(q2, cache_k, cache_v, bm, kn2, vn2, bmn)


def _np_rel_bucket(dist):
    exact = REL_BUCKETS // 2
    d_f = np.maximum(dist, 1).astype(np.float32)
    large = exact + (np.log(d_f / np.float32(exact)) / np.float32(math.log(REL_MAX_DIST / exact))
                     * np.float32(REL_BUCKETS - exact)).astype(np.int32)
    large = np.minimum(large, REL_BUCKETS - 1)
    return np.where(dist < exact, dist, large)


def diff_paged(qn, k_new, v_new, cache_k, cache_v, page_table, rel_bias, lam4, g_out, lam_init):
    B, T, H, _, dh = qn.shape
    dv = v_new.shape[-1]
    n_pages = page_table.shape[1]
    page = cache_k.shape[1]
    P = n_pages * page
    R = 2 * H * T
    pp = next(c for c in (8, 4, 2, 1) if n_pages % c == 0)
    qt = jnp.transpose(qn, (0, 3, 2, 1, 4))
    zeros = jnp.zeros_like(qt[:, 0])
    qm = jnp.concatenate([jnp.concatenate([qt[:, 0], zeros], axis=-1),
                          jnp.concatenate([zeros, qt[:, 1]], axis=-1)], axis=1)
    qm = qm.reshape(B, R, 2 * dh)
    pg = np.arange(n_pages)[:, None, None]
    tt = np.arange(T)[None, :, None]
    ii = np.arange(page)[None, None, :]
    dist_np = P + tt - pg * page - ii
    bucket_np = _np_rel_bucket(dist_np)
    far = np.all(bucket_np == bucket_np[0:1], axis=(1, 2))
    near_pages = [int(x) for x in np.nonzero(~far)[0]]
    tile_pages = [0] + near_pages
    tile_of_page = np.zeros((n_pages,), np.int32)
    for ti, pgi in enumerate(near_pages):
        tile_of_page[pgi] = ti + 1
    nt = len(tile_pages)
    bias_t = _bias_of_dist(rel_bias, jnp.asarray(dist_np[tile_pages], jnp.int32))
    bias_rows = jnp.transpose(bias_t.reshape(2, H, nt, T, page), (2, 0, 1, 3, 4)).reshape(nt, R, page)
    row_head = (np.arange(R) // T) % H
    bm = _head_expand(bias_rows, row_head, H)
    dn = np.arange(T)[:, None] - np.arange(T)[None, :]
    bias_n, _ = _causal_bias(rel_bias, dn)
    bmn = _head_expand(bias_n.reshape(R, T), row_head, H)
    kn2 = k_new.reshape(B, T * H, 2 * dh)
    vn2 = v_new.reshape(B, T * H, dv)

    def kv_spec(j, d):
        return pl.BlockSpec((1, page, H, d), lambda b, p, pt, tl: (pt[b * n_pages + p * pp + j], 0, 0, 0))

    def bm_spec(j):
        return pl.BlockSpec((1, R, page * H), lambda b, p, pt, tl: (tl[p * pp + j], 0, 0))

    grid_spec = pltpu.PrefetchScalarGridSpec(
        num_scalar_prefetch=2,
        grid=(B, n_pages // pp),
        in_specs=(
            [pl.BlockSpec((1, R, 2 * dh), lambda b, p, pt, tl: (b, 0, 0))]
            + [kv_spec(j, 2 * dh) for j in range(pp)]
            + [kv_spec(j, dv) for j in range(pp)]
            + [bm_spec(j) for j in range(pp)]
            + [pl.BlockSpec((1, T * H, 2 * dh), lambda b, p, pt, tl: (b, 0, 0)),
               pl.BlockSpec((1, T * H, dv), lambda b, p, pt, tl: (b, 0, 0)),
               pl.BlockSpec((R, T * H), lambda b, p, pt, tl: (0, 0)),
               pl.BlockSpec(lam4.shape, lambda b, p, pt, tl: (0, 0)),
               pl.BlockSpec((1, dv), lambda b, p, pt, tl: (0, 0))]),
        out_specs=pl.BlockSpec((1, T, H * dv), lambda b, p, pt, tl: (b, 0, 0)),
        scratch_shapes=[pltpu.VMEM((R, 1), F32), pltpu.VMEM((R, 1), F32), pltpu.VMEM((R, dv), F32)],
    )
    return pl.pallas_call(
        functools.partial(_decode_kernel, n_scalar=2, n_chunks=pp, diff=True, lam_init=lam_init,
                          n_heads=H, n_new=T),
        grid_spec=grid_spec,
        out_shape=jax.ShapeDtypeStruct((B, T, H * dv), BF16),
        compiler_params=_cparams(("arbitrary", "arbitrary")),
        name="diff_paged",
    )(page_table.reshape(-1), jnp.asarray(tile_of_page), qm, *([cache_k] * pp), *([cache_v] * pp),
      *([bm] * pp), kn2, vn2, bmn, lam4, g_out.reshape(1, dv))


def _rglru_kernel(y_ref, x_ref, cs_ref, h0_ref, cw_ref, cb_ref, wa_ref, ba_ref, wx_ref, bx_ref, lam_ref,
                  out_ref, conv_ref, hl_ref, xpad, hcar, *, l_valid):
    i = pl.program_id(1)
    T, D = x_ref.shape[1], x_ref.shape[2]

    @pl.when(i == 0)
    def _():
        xpad[0:8, :] = cs_ref[0]
        hcar[...] = h0_ref[0]

    xpad[8:8 + T, :] = x_ref[0]
    conv = cb_ref[...] + cw_ref[CONV_W - 1:CONV_W, :] * xpad[8:8 + T, :]
    for j in range(CONV_W - 1):
        s = CONV_W - 1 - j
        conv = conv + cw_ref[j:j + 1, :] * xpad[8 - s:8 - s + T, :]
    nb = wa_ref.shape[0]
    bs = D // nb
    r_parts, i_parts = [], []
    for n in range(nb):
        xb = conv[:, n * bs:(n + 1) * bs]
        r_parts.append(_bdot(xb, wa_ref[n]))
        i_parts.append(_bdot(xb, wx_ref[n]))
    r = jax.nn.sigmoid(jnp.concatenate(r_parts, axis=-1) + ba_ref[...])
    ig = jax.nn.sigmoid(jnp.concatenate(i_parts, axis=-1) + bx_ref[...])
    log_a = -LRU_C * r * _softplus(-lam_ref[...])
    a = jnp.exp(log_a)
    bb = jnp.sqrt(-jnp.tanh(log_a) * (1.0 + a * a)) * ig * conv
    rowid = lax.broadcasted_iota(jnp.int32, (T, D), 0)
    s = 1
    while s < T:
        keep = rowid >= s
        a_sh = pltpu.roll(a, s, 0)
        b_sh = pltpu.roll(bb, s, 0)
        bb = jnp.where(keep, a * b_sh + bb, bb)
        a = jnp.where(keep, a * a_sh, a)
        s *= 2
    hs = a * hcar[...] + bb
    out_ref[0] = (hs * _gelu(y_ref[0])).astype(out_ref.dtype)
    hcar[...] = hs[T - 1:T, :]
    tail = xpad[T:T + 8, :]
    xpad[0:8, :] = tail

    @pl.when(i == pl.num_programs(1) - 1)
    def _():
        hl_ref[0] = hs[l_valid - 1:l_valid, :]
        conv_ref[0] = xpad[8 + l_valid - (CONV_W - 1):8 + l_valid, :]


def rglru_core(z, conv_state, h0, conv_w, conv_b, w_a, b_a, w_x, b_x, lam, l_valid_last, tblk):
    B, L, D2 = z.shape
    D = D2 // 2
    cs8 = jnp.concatenate([jnp.zeros((B, 8 - (CONV_W - 1), D), F32), conv_state.astype(F32)], axis=1)
    vec = lambda a: a.reshape(1, D)
    out, conv_new, h_last = pl.pallas_call(
        functools.partial(_rglru_kernel, l_valid=l_valid_last),
        grid=(B, L // tblk),
        in_specs=[
            pl.BlockSpec((1, tblk, D), lambda b, i: (b, i, 0)),
            pl.BlockSpec((1, tblk, D), lambda b, i: (b, i, 1)),
            pl.BlockSpec((1, 8, D), lambda b, i: (b, 0, 0)),
            pl.BlockSpec((1, 1, D), lambda b, i: (b, 0, 0)),
            pl.BlockSpec((CONV_W, D), lambda b, i: (0, 0)),
            pl.BlockSpec((1, D), lambda b, i: (0, 0)),
            pl.BlockSpec(w_a.shape, lambda b, i: (0, 0, 0)),
            pl.BlockSpec((1, D), lambda b, i: (0, 0)),
            pl.BlockSpec(w_x.shape, lambda b, i: (0, 0, 0)),
            pl.BlockSpec((1, D), lambda b, i: (0, 0)),
            pl.BlockSpec((1, D), lambda b, i: (0, 0)),
        ],
        out_specs=[
            pl.BlockSpec((1, tblk, D), lambda b, i: (b, i, 0)),
            pl.BlockSpec((1, CONV_W - 1, D), lambda b, i: (b, 0, 0)),
            pl.BlockSpec((1, 1, D), lambda b, i: (b, 0, 0)),
        ],
        out_shape=[
            jax.ShapeDtypeStruct((B, L, D), BF16),
            jax.ShapeDtypeStruct((B, CONV_W - 1, D), F32),
            jax.ShapeDtypeStruct((B, 1, D), F32),
        ],
        scratch_shapes=[pltpu.VMEM((tblk + 8, D), F32), pltpu.VMEM((1, D), F32)],
        compiler_params=_cparams(("arbitrary", "arbitrary")),
        name="rglru",
    )(z, z, cs8, h0.astype(F32).reshape(B, 1, D), conv_w, vec(conv_b), w_a.astype(BF16), vec(b_a),
      w_x.astype(BF16), vec(b_x), vec(lam))
    return out, conv_new, h_last.reshape(B, D)


def _topk_rows(s, k):
    n, tb = s.shape
    rowf = lax.broadcasted_iota(jnp.int32, (n, tb), 0).astype(F32)
    vals, idxs = [], []
    for _ in range(k):
        m = jnp.max(s, axis=0, keepdims=True)
        idx = jnp.min(jnp.where(s == m, rowf, float(n)), axis=0, keepdims=True)
        vals.append(m)
        idxs.append(idx)
        s = jnp.where(rowf == idx, -jnp.inf, s)
    return jnp.concatenate(vals, axis=0), jnp.concatenate(idxs, axis=0)


def _pair_plan(k):
    full, small = [], []
    a = 0
    while a < k and k // (a + 1) > 1:
        nb, b0 = k // (a + 1), 0
        while nb - b0 >= 8:
            full.append((a, b0))
            b0 += 8
        if nb > b0:
            small.append((a, b0, nb - b0))
        a += 1
    a0 = a
    assert k % 8 == 0 and (k - a0) % 8 == 0 and (k & (k - 1)) == 0
    bins = []
    for a_, b0, n in sorted(small, key=lambda p: -p[2]):
        for bn in bins:
            used = sum(p[3] for p in bn)
            if used + n <= 8:
                bn.append((a_, b0, used, n))
                break
        else:
            bins.append([(a_, b0, 0, n)])
    code = []
    for a_, b0 in full:
        code += [a_ * k + b0 + r for r in range(8)]
    for bn in bins:
        rows = [k * k] * 8
        for a_, b0, off, n in bn:
            for r in range(n):
                rows[off + r] = a_ * k + b0 + r
        code += rows
    code += [a_ * k for a_ in range(a0, k)]
    return full, bins, a0, np.array(code, np.float32)


def _route_kernel(h_ref, wq_ref, k1_ref, k2_ref, code_ref, e1_ref, e2_ref, g_ref, *, topk):
    k = topk
    q = jnp.dot(h_ref[...], wq_ref[...], preferred_element_type=F32)
    dk2 = k1_ref.shape[2]
    tb = q.shape[0]
    nt = (((1,), (1,)), ((), ()))
    s1 = lax.dot_general(k1_ref[0], q[:, :dk2], nt, precision=HIGHEST, preferred_element_type=F32)
    s2 = lax.dot_general(k2_ref[0], q[:, dk2:], nt, precision=HIGHEST, preferred_element_type=F32)
    sv1, si1 = _topk_rows(s1, k)
    sv2, si2 = _topk_rows(s2, k)
    full, bins, a0, _ = _pair_plan(k)
    row8 = lax.broadcasted_iota(jnp.int32, (8, tb), 0)
    groups = [sv1[a:a + 1] + sv2[b0:b0 + 8] for a, b0 in full]
    for bn in bins:
        v = jnp.full((8, tb), -jnp.inf, F32)
        for a, b0, off, n in bn:
            piece = sv1[a:a + 1] + (pltpu.roll(sv2[b0:b0 + 8], off, 0) if off else sv2[b0:b0 + 8])
            v = jnp.where((row8 >= off) & (row8 < off + n), piece, v)
        groups.append(v)
    for c in range(a0, k, 8):
        groups.append(sv1[c:c + 8] + sv2[0:1])
    cand = jnp.concatenate(groups, axis=0)
    code = code_ref[...]
    vals, poss = [], []
    for _ in range(k):
        m = jnp.max(cand, axis=0, keepdims=True)
        pos = jnp.min(jnp.where(cand == m, code, float(k * k)), axis=0, keepdims=True)
        vals.append(m)
        poss.append(pos)
        cand = jnp.where(code == pos, -jnp.inf, cand)
    val = jnp.concatenate(vals, axis=0)
    pos = jnp.concatenate(poss, axis=0)
    ra = jnp.floor(pos * (1.0 / k))
    rb = pos - k * ra
    e1 = jnp.zeros((k, tb), F32)
    e2 = jnp.zeros((k, tb), F32)
    for r in range(k):
        e1 = jnp.where(ra == float(r), si1[r:r + 1], e1)
        e2 = jnp.where(rb == float(r), si2[r:r + 1], e2)
    e1_ref[0] = e1
    e2_ref[0] = e2
    ex = jnp.exp(val - val[0:1])
    g_ref[0] = ex / jnp.sum(ex, axis=0, keepdims=True)


def peer_route(h, wq, keys):
    N, D = h.shape
    nh, _, nk, dk2 = keys.shape
    tb = _pick(N, (512, 256, 128))
    code = _pair_plan(TOPK_P)[3]
    code = jnp.asarray(np.broadcast_to(code[:, None], (code.shape[0], tb)))
    shp = jax.ShapeDtypeStruct((nh, TOPK_P, N), F32)
    o_spec = pl.BlockSpec((1, TOPK_P, tb), lambda hh, i: (hh, 0, i))
    return pl.pallas_call(
        functools.partial(_route_kernel, topk=TOPK_P),
        grid=(nh, N // tb),
        in_specs=[pl.BlockSpec((tb, D), lambda hh, i: (i, 0)),
                  pl.BlockSpec((D, 2 * dk2), lambda hh, i: (0, hh)),
                  pl.BlockSpec((1, nk, dk2), lambda hh, i: (2 * hh, 0, 0)),
                  pl.BlockSpec((1, nk, dk2), lambda hh, i: (2 * hh + 1, 0, 0)),
                  pl.BlockSpec(code.shape, lambda hh, i: (0, 0))],
        out_specs=[o_spec, o_spec, o_spec],
        out_shape=[shp, shp, shp],
        compiler_params=_cparams(("arbitrary", "arbitrary")),
        name="peer_route",
    )(h, wq, keys.reshape(nh * 2, nk, dk2), keys.reshape(nh * 2, nk, dk2), code)


def _route_build_kernel(e1_ref, e2_ref, g_ref, sub_ref, o_ref, *, nkeys):
    sub = sub_ref[...][None]
    one = jnp.ones((), BF16)
    zero = jnp.zeros((), BF16)
    at = jnp.where(e1_ref[...].astype(BF16) == sub, one, zero)
    bt = jnp.where(e2_ref[...].astype(BF16) == sub, g_ref[...].astype(BF16), zero)
    g3 = jnp.einsum('tik,tjk->tij', at, bt, preferred_element_type=F32)
    gt = pltpu.einshape('tij->itj', g3)
    for i in range(nkeys):
        o_ref[:, i * nkeys:(i + 1) * nkeys] = gt[i].astype(o_ref.dtype)


def peer_build(e1, e2, g, nkeys):
    nh, k, N = e1.shape
    ks = nh * k
    slot = lambda a: jnp.transpose(a, (2, 0, 1)).reshape(N, 1, ks)
    tb = _pick(N, (64, 32))
    spec = pl.BlockSpec((tb, 1, ks), lambda i: (i, 0, 0))
    sub = jnp.asarray(np.broadcast_to(np.arange(nkeys, dtype=np.float32)[:, None], (nkeys, ks)), BF16)
    return pl.pallas_call(
        functools.partial(_route_build_kernel, nkeys=nkeys),
        grid=(N // tb,),
        in_specs=[spec, spec, spec, pl.BlockSpec((nkeys, ks), lambda i: (0, 0))],
        out_specs=pl.BlockSpec((tb, nkeys * nkeys), lambda i: (i, 0)),
        out_shape=jax.ShapeDtypeStruct((N, nkeys * nkeys), BF16),
        compiler_params=_cparams(("arbitrary",)),
        name="peer_build",
    )(slot(e1), slot(e2), slot(g), sub)


def _peer_kernel(*refs, with_next):
    if with_next:
        x_ref, ut_ref, v_ref, g_ref, r_ref, gt_ref, gn_ref, sh_ref, sc_ref, o_ref, h_ref, acc_s = refs
    else:
        x_ref, ut_ref, v_ref, g_ref, r_ref, gt_ref, o_ref, acc_s = refs
    e = pl.program_id(1)

    @pl.when(e == 0)
    def _():
        acc_s[...] = jnp.zeros(acc_s.shape, F32)

    s = jnp.dot(x_ref[...], ut_ref[0], preferred_element_type=F32)
    p = (_gelu(s) * g_ref[...].astype(F32)).astype(BF16)
    acc_s[...] += jnp.dot(p, v_ref[0], preferred_element_type=F32)

    @pl.when(e == pl.num_programs(1) - 1)
    def _():
        xn = r_ref[...] + gt_ref[0] * acc_s[...]
        o_ref[...] = xn
        if with_next:
            h_ref[...] = _modulated(xn, gn_ref, sh_ref, sc_ref).astype(h_ref.dtype)


def peer_dense(h, ut_all, v_all, layer, G, res, gate, next_mod=None):
    N, D = h.shape
    E = ut_all.shape[2]
    tb = _pick(N, (512, 256, 128))
    eb = _pick(E, (1024, 512, 256, 128))

    def param_spec(p):
        if p.shape[1] == 1:
            rows = N // p.shape[0]
            assert rows % tb == 0
            return pl.BlockSpec((1, 1, D), lambda i, e: (i // (rows // tb), 0, 0))
        return pl.BlockSpec((1, tb, D), lambda i, e: (0, i, 0))

    row = pl.BlockSpec((tb, D), lambda i, e: (i, 0))
    in_specs = [row,
                pl.BlockSpec((1, D, eb), lambda i, e: (layer, 0, e)),
                pl.BlockSpec((1, eb, D), lambda i, e: (layer, e, 0)),
                pl.BlockSpec((tb, eb), lambda i, e: (i, e)),
                row, param_spec(gate)]
    args = [h, ut_all, v_all, G, res, gate]
    out_specs, out_shape = [row], [jax.ShapeDtypeStruct((N, D), F32)]
    if next_mod is not None:
        g_norm, shift, scale = next_mod
        in_specs += [pl.BlockSpec((1, D), lambda i, e: (0, 0)), param_spec(shift), param_spec(scale)]
        args += [g_norm.reshape(1, D), shift, scale]
        out_specs.append(row)
        out_shape.append(jax.ShapeDtypeStruct((N, D), BF16))
    outs = pl.pallas_call(
        functools.partial(_peer_kernel, with_next=next_mod is not None),
        grid=(N // tb, E // eb),
        in_specs=in_specs,
        out_specs=out_specs,
        out_shape=out_shape,
        scratch_shapes=[pltpu.VMEM((tb, D), F32)],
        compiler_params=_cparams(("arbitrary", "arbitrary")),
        name="peer_dense",
    )(*args)
    return (outs[0], outs[1]) if next_mod is not None else (outs[0], None)


def _pad_rows(a, n, value=0.0, axis=1):
    pad = [(0, 0)] * a.ndim
    pad[axis] = (0, n - a.shape[axis])
    return jnp.pad(a, pad, constant_values=value)


def kernel(x_prompt, x_sample, state_mlstm_C, state_mlstm_n, state_mlstm_m, cache_swa_k, cache_swa_v,
           cache_diff_k, cache_diff_v, state_rglru_conv, state_rglru_h, page_table, c_prompt, c_sample,
           w_ada, b_ada, g_norm_mix, g_norm_ffn, rel_bias,
           mlstm_w_in, mlstm_b_gates, mlstm_g_out, mlstm_w_out,
           swa_w_in, swa_g_q, swa_g_k, swa_w_out,
           diff_w_in, diff_g_q, diff_g_k, diff_lam_q1, diff_lam_k1, diff_lam_q2, diff_lam_k2, diff_g_out, diff_w_out,
           rglru_w_in, rglru_conv_w, rglru_conv_b, rglru_w_a, rglru_b_a, rglru_w_x, rglru_b_x, rglru_lambda, rglru_w_out,
           peer_w_q, peer_keys, peer_u, peer_v):
    xp, xs = x_prompt, x_sample
    Bp, S, D = xp.shape
    Bs, T, _ = xs.shape
    depth = w_ada.shape[0]
    Ns = Bs * T

    c_all = jnp.concatenate([c_prompt, c_sample], axis=0)
    mod = ada_all(c_all, w_ada, b_ada)
    mod = mod.reshape(depth, Bp + Bs, 6, 1, D)

    def mods(i, which):
        m = mod[i, :, which]
        return m[:Bp], m[Bp:]

    def rows(p):
        return jnp.broadcast_to(p, (Bs, T, D)).reshape(1, Ns, D)

    ut_all = jnp.transpose(peer_u, (0, 2, 1)).astype(BF16)
    v_all = peer_v.astype(BF16)
    nk = peer_keys.shape[3]
    Ns_pad = -(-Ns // LANES) * LANES

    sh_p, sh_s = mods(0, 0)
    sc_p, sc_s = mods(0, 1)
    hp = modulate(xp, g_norm_mix[0], sh_p, sc_p)
    hs = modulate(xs, g_norm_mix[0], sh_s, sc_s)
    for i in range(depth):
        kind = i % 4
        gt_p, gt_s = mods(i, 2)
        hs_flat = hs.reshape(1, Ns, D)
        if kind == 0:
            H = NH_A
            dk = state_mlstm_C.shape[2]
            dv = state_mlstm_C.shape[3]
            nmain = 2 * H * dk + 2 * H * dv
            wb = mlstm_w_in.astype(BF16)
            w_gate = mlstm_w_in[:, nmain:].astype(BF16)
            zp = matmul(hp, wb, cols=(0, nmain))
            gp = matmul(hp, w_gate)
            zs = matmul(hs_flat, wb, cols=(0, nmain)).reshape(Bs, T, nmain)
            gs = matmul(hs_flat, w_gate).reshape(Bs, T, 2 * H)
            chunk_p = _pick(S, (CHUNK_A, 128))
            op, mC_p, mn_p, mm_p = mlstm_core(
                zp, gp, mlstm_b_gates, jnp.zeros((Bp, H, dk, dv), F32), jnp.zeros((Bp, H, dk), F32),
                jnp.full((Bp, H), M_INIT, F32), mlstm_g_out, chunk_p)
            Tp = 128
            zs_pad = _pad_rows(zs, Tp)
            gs_pad = jnp.concatenate([
                _pad_rows(gs[..., :H], Tp, NEG), _pad_rows(gs[..., H:], Tp, -NEG)], axis=-1)
            os_, mC_s, mn_s, mm_s = mlstm_core(
                zs_pad, gs_pad, mlstm_b_gates, state_mlstm_C.astype(F32), state_mlstm_n.astype(F32),
                state_mlstm_m.astype(F32), mlstm_g_out, Tp)
            os_ = os_[:, :T]
            w_out = mlstm_w_out
        elif kind == 1:
            H = NH_B
            dh = D // H
            wb = swa_w_in.astype(BF16)
            gq = jnp.tile(swa_g_q, H)
            gk = jnp.tile(swa_g_k, H)
            qn_p = matmul_headnorm(hp, wb, (0, D), gq, dh, dh ** -0.5, BF16)
            kn_p = matmul_headnorm(hp, wb, (D, D), gk, dh, 1.0, F32)
            vp = matmul(hp, wb, cols=(2 * D, D))
            qn_s = matmul_headnorm(hs_flat, wb, (0, D), gq, dh, dh ** -0.5, BF16).reshape(Bs, T, D)
            kn_s = matmul_headnorm(hs_flat, wb, (D, D), gk, dh, 1.0, F32).reshape(Bs, T, D)
            vs_new = matmul(hs_flat, wb, cols=(2 * D, D)).reshape(Bs, T, D)
            t = _pick(S, (512, 256, 128))
            assert (_np_multiplicity(_toeplitz_dist_np(S // t, t)) > 0).any(axis=2).all()
            bm_tab = _toeplitz_bias(rel_bias, S // t, t, _swa_bias)
            op = flash_attention(qn_p, kn_p, vp, bm_tab, n_units=H, dqk=dh, dv=dh,
                                 qcol=lambda a: a, kcol=lambda a: a, vcol=lambda a: a, bmap=lambda a: a,
                                 t=t, out_dtype=BF16)
            swa_k_p = kn_p.reshape(Bp, S, H, dh)
            swa_v_p = vp.reshape(Bp, S, H, dh)
            os_ = swa_decode(qn_s, kn_s, vs_new, cache_swa_k, cache_swa_v, rel_bias)
            def shifted(cache, new):
                Wb = cache.shape[1]
                buf = lax.pad(cache.astype(F32), jnp.zeros((), F32), ((0, 0, 0), (-T, T, 0), (0, 0, 0), (0, 0, 0)))
                return lax.dynamic_update_slice(buf, new.reshape(Bs, T, H, dh), (0, Wb - T, 0, 0))

            swa_k_s = shifted(cache_swa_k, kn_s)
            swa_v_s = shifted(cache_swa_v, vs_new)
            w_out = swa_w_out
        elif kind == 2:
            H = NH_C
            dh = D // (2 * H)
            dv = 2 * dh
            lam_init = 0.8 - 0.6 * math.exp(-0.3 * i)
            lam4 = jnp.stack([diff_lam_q1, diff_lam_k1, diff_lam_q2, diff_lam_k2]).astype(F32)
            wb = diff_w_in.astype(BF16)
            gq = jnp.tile(diff_g_q.reshape(-1), H)
            gk = jnp.tile(diff_g_k.reshape(-1), H)
            qn_p = matmul_headnorm(hp, wb, (0, D), gq, dh, dh ** -0.5, BF16)
            kn_p = matmul_headnorm(hp, wb, (D, D), gk, dh, 1.0, F32)
            vp = matmul(hp, wb, cols=(2 * D, D))
            qn_s = matmul_headnorm(hs_flat, wb, (0, D), gq, dh, dh ** -0.5, BF16).reshape(Bs, T, D)
            kn_s = matmul_headnorm(hs_flat, wb, (D, D), gk, dh, 1.0, F32).reshape(Bs, T, D)
            vs_new = matmul(hs_flat, wb, cols=(2 * D, D)).reshape(Bs, T, D)
            t = _pick(S, (512, 256, 128))
            bm_tab = _toeplitz_bias(rel_bias, S // t, t, _causal_bias)
            o2 = flash_attention(qn_p, kn_p, vp, bm_tab, n_units=2 * H, dqk=dh, dv=dv,
                                 qcol=lambda a: a, kcol=lambda a: a, vcol=lambda a: a // 2,
                                 bmap=lambda a: (a % 2) * H + a // 2, t=t, out_dtype=F32)
            op = diff_combine(o2, lam4, diff_g_out, lam_init)
            diff_k_p = kn_p.reshape(Bp, S, H, 2 * dh)
            diff_v_p = vp.reshape(Bp, S, H, dv)
            diff_k_s = kn_s.reshape(Bs, T, H, 2 * dh)
            diff_v_s = vs_new.reshape(Bs, T, H, dv)
            os_ = diff_paged(qn_s.reshape(Bs, T, H, 2, dh), diff_k_s, diff_v_s, cache_diff_k, cache_diff_v,
                             page_table, rel_bias, lam4, diff_g_out, lam_init)
            w_out = diff_w_out
        else:
            wb = rglru_w_in.astype(BF16)
            zp = matmul(hp, wb)
            zs = matmul(hs_flat, wb).reshape(Bs, T, -1)
            tblk = _pick(S, (256, 128))
            op, conv_p, h_p = rglru_core(zp, jnp.zeros((Bp, CONV_W - 1, D), F32), jnp.zeros((Bp, D), F32),
                                         rglru_conv_w, rglru_conv_b, rglru_w_a, rglru_b_a, rglru_w_x, rglru_b_x,
                                         rglru_lambda, tblk, tblk)
            os_, conv_s, h_s = rglru_core(_pad_rows(zs, 8), state_rglru_conv, state_rglru_h,
                                          rglru_conv_w, rglru_conv_b, rglru_w_a, rglru_b_a, rglru_w_x, rglru_b_x,
                                          rglru_lambda, T, 8)
            os_ = os_[:, :T]
            w_out = rglru_w_out
        sh_p, sh_s = mods(i, 3)
        sc_p, sc_s = mods(i, 4)
        wb = w_out.astype(BF16)
        xp, hp = matmul_res_mod(op, wb, xp, gt_p, g_norm_ffn[i], sh_p, sc_p)
        xs, hs = matmul_res_mod(os_.reshape(1, Ns, -1), wb, xs.reshape(1, Ns, D), rows(gt_s), g_norm_ffn[i],
                                rows(sh_s), rows(sc_s))
        hp = hp.reshape(Bp * S, D)
        hs = hs.reshape(Ns, D)

        gt_p, gt_s = mods(i, 5)
        wq = peer_w_q[i].astype(BF16)
        Gp = peer_build(*peer_route(hp, wq, peer_keys[i]), nk)
        Gs = peer_build(*peer_route(_pad_rows(hs, Ns_pad, axis=0), wq, peer_keys[i]), nk)[:Ns]
        if i + 1 < depth:
            sh_p, sh_s = mods(i + 1, 0)
            sc_p, sc_s = mods(i + 1, 1)
            next_p = (g_norm_mix[i + 1], sh_p, sc_p)
            next_s = (g_norm_mix[i + 1], rows(sh_s), rows(sc_s))
        else:
            next_p = next_s = None
        xp, hp = peer_dense(hp, ut_all, v_all, i, Gp, xp.reshape(Bp * S, D), gt_p, next_p)
        xs, hs = peer_dense(hs, ut_all, v_all, i, Gs, xs.reshape(Ns, D), rows(gt_s), next_s)
        xp = xp.reshape(Bp, S, D)
        xs = xs.reshape(Bs, T, D)
        if hp is not None:
            hp = hp.reshape(Bp, S, D)
            hs = hs.reshape(Bs, T, D)

    return (xp, xs, mC_p, mC_s, mn_p, mn_s, mm_p, mm_s, swa_k_p, swa_k_s, swa_v_p, swa_v_s,
            diff_k_p, diff_k_s, diff_v_p, diff_v_s, conv_p, conv_s, h_p, h_s)
```

```python
import functools
import math

import numpy as np
import jax
import jax.numpy as jnp
from jax import lax
from jax.experimental import pallas as pl
from jax.experimental.pallas import tpu as pltpu

F32 = jnp.float32
BF16 = jnp.bfloat16
HIGHEST = lax.Precision.HIGHEST

EPS = 1e-6
NEG = -1e30
VMEM_LIMIT = 56 * 1024 * 1024
LANES = 128

NH_A, CHUNK_A, M_INIT = 8, 256, -1e30
NH_B = 16
DIL_PAIRS = ((128, 1), (512, 4), (2048, 16))
NH_C = 8
NBLK_D, CONV_W, LRU_C = 16, 4, 8.0
NH_P, N_KEYS, TOPK_P = 8, 128, 16
REL_BUCKETS, REL_MAX_DIST = 32, 2048
PAGE_SIZE = 128


def _cparams(sem):
    return pltpu.CompilerParams(dimension_semantics=sem, vmem_limit_bytes=VMEM_LIMIT)


def _bdot(a, b):
    return jnp.dot(a.astype(BF16), b.astype(BF16), preferred_element_type=F32)


def _bdot_nt(a, b):
    return lax.dot_general(a.astype(BF16), b.astype(BF16), (((1,), (1,)), ((), ())),
                           preferred_element_type=F32)


def _gelu(x):
    return 0.5 * x * (1.0 + lax.erf(x * (1.0 / math.sqrt(2.0))))


def _log_sigmoid(x):
    return jnp.minimum(x, 0.0) - jnp.log1p(jnp.exp(-jnp.abs(x)))


def _softplus(x):
    return jnp.maximum(x, 0.0) + jnp.log1p(jnp.exp(-jnp.abs(x)))


def _pick(n, pref):
    for t in pref:
        if n % t == 0:
            return t
    return n


def _ada_kernel(c_ref, w_ref, b_ref, o_ref):
    c = c_ref[...]
    a = c * jax.nn.sigmoid(c)
    o_ref[0] = jnp.dot(a, w_ref[0], preferred_element_type=F32, precision=HIGHEST) + b_ref[0]


def ada_all(c_all, w_ada, b_ada):
    R, D = c_all.shape
    nl, _, N = w_ada.shape
    tn = _pick(N, (2048, 1024, 512, 256, 128))
    return pl.pallas_call(
        _ada_kernel,
        grid=(nl, N // tn),
        in_specs=[pl.BlockSpec((R, D), lambda l, j: (0, 0)),
                  pl.BlockSpec((1, D, tn), lambda l, j: (l, 0, j)),
                  pl.BlockSpec((1, 1, tn), lambda l, j: (l, 0, j))],
        out_specs=pl.BlockSpec((1, R, tn), lambda l, j: (l, 0, j)),
        out_shape=jax.ShapeDtypeStruct((nl, R, N), F32),
        compiler_params=_cparams(("arbitrary", "arbitrary")),
        name="ada",
    )(c_all, w_ada, b_ada.reshape(nl, 1, N))


def _modulate_kernel(x_ref, g_ref, sh_ref, sc_ref, o_ref):
    x = x_ref[0]
    ms = jnp.mean(x * x, axis=-1, keepdims=True)
    y = x * lax.rsqrt(ms + EPS) * g_ref[...]
    o_ref[0] = (y * (1.0 + sc_ref[0]) + sh_ref[0]).astype(o_ref.dtype)


def modulate(x, g, shift, scale):
    B, L, D = x.shape
    tl = _pick(L, (512, 256, 128))
    return pl.pallas_call(
        _modulate_kernel,
        grid=(B, L // tl),
        in_specs=[pl.BlockSpec((1, tl, D), lambda b, i: (b, i, 0)),
                  pl.BlockSpec((1, D), lambda b, i: (0, 0)),
                  pl.BlockSpec((1, 1, D), lambda b, i: (b, 0, 0)),
                  pl.BlockSpec((1, 1, D), lambda b, i: (b, 0, 0))],
        out_specs=pl.BlockSpec((1, tl, D), lambda b, i: (b, i, 0)),
        out_shape=jax.ShapeDtypeStruct((B, L, D), BF16),
        compiler_params=_cparams(("arbitrary", "arbitrary")),
        name="modulate",
    )(x, g.reshape(1, D), shift, scale)


def _mm_kernel(x_ref, w_ref, o_ref):
    o_ref[0] = jnp.dot(x_ref[0], w_ref[...], preferred_element_type=F32).astype(o_ref.dtype)


def matmul(x, w, out_dtype=F32, cols=None):
    B, L, K = x.shape
    c0, N = cols if cols is not None else (0, w.shape[1])
    tm = _pick(L, (512, 256, 128))
    tn = next((c for c in (1024, 512, 256, 128) if N % c == 0 and c0 % c == 0), N)
    assert c0 % tn == 0 and (tn == w.shape[1] or tn % LANES == 0)
    j0 = c0 // tn
    grid = (N // tn, B, L // tm)
    x_spec = pl.BlockSpec((1, tm, K), lambda j, b, i: (b, i, 0))
    w_spec = pl.BlockSpec((K, tn), lambda j, b, i: (0, j0 + j))
    o_spec = pl.BlockSpec((1, tm, tn), lambda j, b, i: (b, i, j))
    return pl.pallas_call(
        _mm_kernel, grid=grid, in_specs=[x_spec, w_spec], out_specs=o_spec,
        out_shape=jax.ShapeDtypeStruct((B, L, N), out_dtype),
        compiler_params=_cparams(("arbitrary",) * 3), name="matmul",
    )(x, w)


def _modulated(xn, gn_ref, sh_ref, sc_ref):
    ms = jnp.mean(xn * xn, axis=-1, keepdims=True)
    return xn * lax.rsqrt(ms + EPS) * gn_ref[...] * (1.0 + sc_ref[0]) + sh_ref[0]


def _mm_res_mod_kernel(x_ref, w_ref, r_ref, gt_ref, gn_ref, sh_ref, sc_ref, o_ref, h_ref):
    acc = jnp.dot(x_ref[0], w_ref[...], preferred_element_type=F32)
    xn = r_ref[0] + gt_ref[0] * acc
    o_ref[0] = xn
    h_ref[0] = _modulated(xn, gn_ref, sh_ref, sc_ref).astype(h_ref.dtype)


def _row_param_spec(p, tm, n):
    if p.shape[1] == 1:
        return pl.BlockSpec((1, 1, n), lambda b, i: (b, 0, 0))
    return pl.BlockSpec((1, tm, n), lambda b, i: (b, i, 0))


def matmul_res_mod(x, w, res, gate, g_norm, shift, scale):
    B, L, K = x.shape
    N = w.shape[1]
    tm = _pick(L, (256, 128))
    row = pl.BlockSpec((1, tm, N), lambda b, i: (b, i, 0))
    return pl.pallas_call(
        _mm_res_mod_kernel,
        grid=(B, L // tm),
        in_specs=[pl.BlockSpec((1, tm, K), lambda b, i: (b, i, 0)),
                  pl.BlockSpec((K, N), lambda b, i: (0, 0)),
                  row, _row_param_spec(gate, tm, N),
                  pl.BlockSpec((1, N), lambda b, i: (0, 0)),
                  _row_param_spec(shift, tm, N), _row_param_spec(scale, tm, N)],
        out_specs=[row, row],
        out_shape=[jax.ShapeDtypeStruct((B, L, N), F32), jax.ShapeDtypeStruct((B, L, N), BF16)],
        compiler_params=_cparams(("arbitrary", "arbitrary")),
        name="matmul_res_mod",
    )(x, w, res, gate, g_norm.reshape(1, N), shift, scale)


def _mm_headnorm_kernel(x_ref, w_ref, g_ref, o_ref, *, dh, scale):
    acc = jnp.dot(x_ref[0], w_ref[...], preferred_element_type=F32)
    for h in range(acc.shape[1] // dh):
        sl = slice(h * dh, (h + 1) * dh)
        a = acc[:, sl]
        ms = jnp.mean(a * a, axis=-1, keepdims=True)
        y = a * lax.rsqrt(ms + EPS) * g_ref[:, sl]
        o_ref[0, :, sl] = (y * scale if scale != 1.0 else y).astype(o_ref.dtype)


def matmul_headnorm(x, w, cols, gain_full, dh, scale, out_dtype):
    B, L, K = x.shape
    c0, N = cols
    tm = _pick(L, (512, 256, 128))
    tn = next(c for c in (1024, 512, 256, 128) if N % c == 0 and c0 % c == 0)
    j0 = c0 // tn
    return pl.pallas_call(
        functools.partial(_mm_headnorm_kernel, dh=dh, scale=scale),
        grid=(N // tn, B, L // tm),
        in_specs=[pl.BlockSpec((1, tm, K), lambda j, b, i: (b, i, 0)),
                  pl.BlockSpec((K, tn), lambda j, b, i: (0, j0 + j)),
                  pl.BlockSpec((1, tn), lambda j, b, i: (0, j))],
        out_specs=pl.BlockSpec((1, tm, tn), lambda j, b, i: (b, i, j)),
        out_shape=jax.ShapeDtypeStruct((B, L, N), out_dtype),
        compiler_params=_cparams(("arbitrary",) * 3),
        name="matmul_headnorm",
    )(x, w, gain_full.reshape(1, N))


def _mlstm_kernel(q_ref, k_ref, v_ref, o_ref, gc_ref, gr_ref, bc_ref, br_ref,
                  C0_ref, n0_ref, m0_ref, gout_ref,
                  hs_ref, C_ref, n_ref, m_ref, C_s, n_s, m_s, *, dk):
    ci = pl.program_id(2)

    @pl.when(ci == 0)
    def _():
        C_s[...] = C0_ref[0, 0]
        n_s[...] = n0_ref[0, 0]
        m_s[...] = m0_ref[0, 0]

    q = q_ref[0]
    k = k_ref[0] * (dk ** -0.5)
    v = v_ref[0]
    c = q.shape[0]
    gc = gc_ref[0, 0] + bc_ref[0]
    gr = gr_ref[0, 0] + br_ref[0]
    li_c, lf_c = gc[:, 0:1], _log_sigmoid(gc[:, 1:2])
    li_r, lf_r = gr[0:1, :], _log_sigmoid(gr[1:2, :])
    row = lax.broadcasted_iota(jnp.int32, (c, c), 0)
    col = lax.broadcasted_iota(jnp.int32, (c, c), 1)
    causal = col <= row
    b_c = jnp.sum(jnp.where(causal, lf_r, 0.0), axis=1, keepdims=True)
    b_r = jnp.sum(jnp.where(row <= col, lf_c, 0.0), axis=0, keepdims=True)
    m_prev = m_s[...]
    Dm = jnp.where(causal, b_c - b_r + li_r, NEG)
    m_t = jnp.maximum(b_c + m_prev, jnp.max(Dm, axis=1, keepdims=True))
    S = _bdot_nt(q, k) * jnp.exp(Dm - m_t)
    inter = jnp.exp(b_c + m_prev - m_t)
    C = C_s[...]
    n = n_s[...]
    num = _bdot(S, v) + inter * _bdot(q, C)
    den = jnp.sum(S, axis=1, keepdims=True) + inter * jnp.sum(q * n, axis=1, keepdims=True)
    h = num / jnp.maximum(jnp.abs(den), jnp.exp(-m_t))
    hh = jax.nn.sigmoid(o_ref[0]) * h
    ms = jnp.mean(hh * hh, axis=-1, keepdims=True)
    hs_ref[0] = (hh * lax.rsqrt(ms + EPS) * gout_ref[...]).astype(hs_ref.dtype)
    m_new = m_t[c - 1:c, :]
    b_last = b_c[c - 1:c, :]
    w_r = jnp.exp(b_last - b_r + li_r - m_new)
    w_c = jnp.exp(b_last - b_c + li_c - m_new)
    decay = jnp.exp(b_last + m_prev - m_new)
    C_new = decay * C + _bdot(k.T, w_c * v)
    n_new = decay * n + jnp.dot(w_r, k, preferred_element_type=F32, precision=HIGHEST)
    C_s[...] = C_new
    n_s[...] = n_new
    m_s[...] = m_new

    @pl.when(ci == pl.num_programs(2) - 1)
    def _():
        C_ref[0, 0] = C_new
        n_ref[0, 0] = n_new
        m_ref[0, 0] = m_new


def mlstm_core(z, g, b_gates, C0, n0, m0, g_out, chunk):
    B, L, _ = z.shape
    H = NH_A
    dk = C0.shape[2]
    dv = C0.shape[3]
    nc = L // chunk
    g4 = g.reshape(B, L, 2, H)
    gcol = jnp.transpose(g4, (0, 3, 1, 2))
    grow = jnp.transpose(g4, (0, 3, 2, 1))
    bg = b_gates.reshape(2, H)
    bcol = jnp.transpose(bg, (1, 0)).reshape(H, 1, 2)
    brow = jnp.transpose(bg, (1, 0)).reshape(H, 2, 1)
    kq = (H * dk) // dk
    vo = (2 * H * dk) // dv
    oo = vo + H
    outs = pl.pallas_call(
        functools.partial(_mlstm_kernel, dk=dk),
        grid=(B, H, nc),
        in_specs=[
            pl.BlockSpec((1, chunk, dk), lambda b, h, c: (b, c, h)),
            pl.BlockSpec((1, chunk, dk), lambda b, h, c: (b, c, kq + h)),
            pl.BlockSpec((1, chunk, dv), lambda b, h, c: (b, c, vo + h)),
            pl.BlockSpec((1, chunk, dv), lambda b, h, c: (b, c, oo + h)),
            pl.BlockSpec((1, 1, chunk, 2), lambda b, h, c: (b, h, c, 0)),
            pl.BlockSpec((1, 1, 2, chunk), lambda b, h, c: (b, h, 0, c)),
            pl.BlockSpec((1, 1, 2), lambda b, h, c: (h, 0, 0)),
            pl.BlockSpec((1, 2, 1), lambda b, h, c: (h, 0, 0)),
            pl.BlockSpec((1, 1, dk, dv), lambda b, h, c: (b, h, 0, 0)),
            pl.BlockSpec((1, 1, 1, dk), lambda b, h, c: (b, h, 0, 0)),
            pl.BlockSpec((1, 1, 1, 1), lambda b, h, c: (b, h, 0, 0)),
            pl.BlockSpec((1, dv), lambda b, h, c: (0, h)),
        ],
        out_specs=[
            pl.BlockSpec((1, chunk, dv), lambda b, h, c: (b, c, h)),
            pl.BlockSpec((1, 1, dk, dv), lambda b, h, c: (b, h, 0, 0)),
            pl.BlockSpec((1, 1, 1, dk), lambda b, h, c: (b, h, 0, 0)),
            pl.BlockSpec((1, 1, 1, 1), lambda b, h, c: (b, h, 0, 0)),
        ],
        out_shape=[
            jax.ShapeDtypeStruct((B, L, H * dv), BF16),
            jax.ShapeDtypeStruct((B, H, dk, dv), F32),
            jax.ShapeDtypeStruct((B, H, 1, dk), F32),
            jax.ShapeDtypeStruct((B, H, 1, 1), F32),
        ],
        scratch_shapes=[pltpu.VMEM((dk, dv), F32), pltpu.VMEM((1, dk), F32), pltpu.VMEM((1, 1), F32)],
        compiler_params=_cparams(("arbitrary",) * 3),
        name="mlstm",
    )(z, z, z, z, gcol, grow, bcol, brow, C0, n0.reshape(B, H, 1, dk), m0.reshape(B, H, 1, 1),
      g_out.reshape(1, H * dv))
    hs, C, n, m = outs
    return hs, C, n.reshape(B, H, dk), m.reshape(B, H)


def _rel_bucket(dist):
    exact = REL_BUCKETS // 2
    d_f = jnp.maximum(dist, 1).astype(F32)
    large = exact + (jnp.log(d_f / exact) / math.log(REL_MAX_DIST / exact) * (REL_BUCKETS - exact)).astype(jnp.int32)
    large = jnp.minimum(large, REL_BUCKETS - 1)
    return jnp.where(dist < exact, dist, large)


def _bias_of_dist(rel_bias, dist):
    bucket = _rel_bucket(jnp.maximum(dist, 0))
    out = jnp.zeros((rel_bias.shape[1],) + dist.shape, F32)
    for b in range(REL_BUCKETS):
        out = jnp.where(bucket[None] == b, rel_bias[b].reshape((-1,) + (1,) * dist.ndim), out)
    return out


def _np_multiplicity(dist):
    cnt = np.zeros(dist.shape, np.int64)
    for w, d in DIL_PAIRS:
        cnt = cnt + ((dist % d == 0) & (dist <= w) & (dist >= 0))
    return cnt


def _swa_bias(rel_bias, dist_np):
    cnt = _np_multiplicity(dist_np)
    logc = jnp.log(jnp.asarray(np.maximum(cnt, 1), F32))
    bias = _bias_of_dist(rel_bias, jnp.asarray(np.maximum(dist_np, 0), jnp.int32))
    return jnp.where(jnp.asarray(cnt > 0)[None], bias + logc[None], NEG), cnt > 0


def _causal_bias(rel_bias, dist_np):
    bias = _bias_of_dist(rel_bias, jnp.asarray(np.maximum(dist_np, 0), jnp.int32))
    return jnp.where(jnp.asarray(dist_np >= 0)[None], bias, NEG), dist_np >= 0


def _toeplitz_dist_np(n_off, t):
    o = np.arange(n_off)[:, None, None]
    i = np.arange(t)[None, :, None]
    j = np.arange(t)[None, None, :]
    return o * t + i - j


def _toeplitz_kernel(b_ref, o_ref):
    t = o_ref.shape[2]
    rows = jnp.broadcast_to(b_ref[0, 0], (t, 2 * t))
    skew = pltpu.roll(rows, 0, 1, stride=1, stride_axis=0)
    o_ref[0, 0] = skew[:, t:]


def _toeplitz_bias(rel_bias, n_off, t, bias_fn):
    dist = np.arange(-(t - 1), n_off * t)
    f, _ = bias_fn(rel_bias, dist)
    maps = f.shape[0]
    want = np.arange(n_off)[:, None] * t + t - np.arange(2 * t)[None, :]
    idx = np.minimum(want + (t - 1), len(dist) - 1)
    b = f[:, idx].reshape(maps, n_off, 1, 2 * t)
    return pl.pallas_call(
        _toeplitz_kernel,
        grid=(maps, n_off),
        in_specs=[pl.BlockSpec((1, 1, 1, 2 * t), lambda m, o: (m, o, 0, 0))],
        out_specs=pl.BlockSpec((1, 1, t, t), lambda m, o: (m, o, 0, 0)),
        out_shape=jax.ShapeDtypeStruct((maps, n_off, t, t), F32),
        compiler_params=_cparams(("arbitrary", "arbitrary")),
        name="toeplitz_bias",
    )(b)


def _flash_kernel(qi_ref, ki_ref, q_ref, k_ref, v_ref, bm_ref, o_ref, m_s, l_s, acc_s):
    p = pl.program_id(1)
    qi = qi_ref[p]
    ki = ki_ref[p]
    nb, tq, dv = acc_s.shape
    tk = k_ref.shape[1]

    @pl.when(ki == 0)
    def _():
        m_s[...] = jnp.full(m_s.shape, NEG, F32)
        l_s[...] = jnp.zeros(l_s.shape, F32)
        acc_s[...] = jnp.zeros(acc_s.shape, F32)

    bm = bm_ref[0, 0]
    for b in range(nb):
        s = _bdot_nt(q_ref[b], k_ref[b]) + bm
        m_prev = m_s[b]
        m_next = jnp.maximum(m_prev, jnp.max(s, axis=1, keepdims=True))
        pr = jnp.exp(s - jnp.tile(m_next, (1, tk // LANES)))
        alpha = jnp.exp(m_prev - m_next)
        l_s[b] = alpha * l_s[b] + jnp.sum(pr, axis=1, keepdims=True)
        acc_s[b] = acc_s[b] * jnp.tile(alpha, (1, dv // LANES)) + _bdot(pr, v_ref[b])
        m_s[b] = m_next

    @pl.when(ki == qi)
    def _():
        for b in range(nb):
            o_ref[b] = (acc_s[b] / jnp.tile(l_s[b], (1, dv // LANES))).astype(o_ref.dtype)


def flash_attention(q, k, v, bm_tab, *, n_units, dqk, dv, qcol, kcol, vcol, bmap, t, out_dtype):
    B, L = q.shape[:2]
    nq = L // t
    pairs = [(qi, ki) for qi in range(nq) for ki in range(qi + 1)]
    qi_arr = jnp.asarray(np.array([p[0] for p in pairs], np.int32))
    ki_arr = jnp.asarray(np.array([p[1] for p in pairs], np.int32))
    grid_spec = pltpu.PrefetchScalarGridSpec(
        num_scalar_prefetch=2,
        grid=(n_units, len(pairs)),
        in_specs=[
            pl.BlockSpec((B, t, dqk), lambda a, p, qa, ka: (0, qa[p], qcol(a))),
            pl.BlockSpec((B, t, dqk), lambda a, p, qa, ka: (0, ka[p], kcol(a))),
            pl.BlockSpec((B, t, dv), lambda a, p, qa, ka: (0, ka[p], vcol(a))),
            pl.BlockSpec((1, 1, t, t), lambda a, p, qa, ka: (bmap(a), qa[p] - ka[p], 0, 0)),
        ],
        out_specs=pl.BlockSpec((B, t, dv), lambda a, p, qa, ka: (0, qa[p], a)),
        scratch_shapes=[pltpu.VMEM((B, t, LANES), F32), pltpu.VMEM((B, t, LANES), F32),
                        pltpu.VMEM((B, t, dv), F32)],
    )
    return pl.pallas_call(
        _flash_kernel,
        grid_spec=grid_spec,
        out_shape=jax.ShapeDtypeStruct((B, L, n_units * dv), out_dtype),
        compiler_params=_cparams(("arbitrary", "arbitrary")),
        name="flash_attention",
    )(qi_arr, ki_arr, q, k, v, bm_tab)


def _diff_lambda(lam_ref, lam_init):
    lq1, lk1, lq2, lk2 = lam_ref[0:1, :], lam_ref[1:2, :], lam_ref[2:3, :], lam_ref[3:4, :]
    return (jnp.exp(jnp.sum(lq1 * lk1, axis=-1, keepdims=True))
            - jnp.exp(jnp.sum(lq2 * lk2, axis=-1, keepdims=True)) + lam_init)


def _diff_combine_kernel(o_ref, lam_ref, g_ref, out_ref, *, dv, lam_init):
    lam = _diff_lambda(lam_ref, lam_init)
    nh = out_ref.shape[2] // dv
    for h in range(nh):
        o0 = o_ref[0, :, (2 * h) * dv:(2 * h + 1) * dv]
        o1 = o_ref[0, :, (2 * h + 1) * dv:(2 * h + 2) * dv]
        d = o0 - lam * o1
        ms = jnp.mean(d * d, axis=-1, keepdims=True)
        out_ref[0, :, h * dv:(h + 1) * dv] = (d * lax.rsqrt(ms + EPS) * g_ref[...] * (1.0 - lam_init)
                                              ).astype(out_ref.dtype)


def diff_combine(o, lam4, g_out, lam_init):
    B, L, W2 = o.shape
    dv = g_out.shape[0]
    W = W2 // 2
    tl = _pick(L, (256, 128))
    return pl.pallas_call(
        functools.partial(_diff_combine_kernel, dv=dv, lam_init=lam_init),
        grid=(B, L // tl),
        in_specs=[pl.BlockSpec((1, tl, W2), lambda b, i: (b, i, 0)),
                  pl.BlockSpec(lam4.shape, lambda b, i: (0, 0)),
                  pl.BlockSpec((1, dv), lambda b, i: (0, 0))],
        out_specs=pl.BlockSpec((1, tl, W), lambda b, i: (b, i, 0)),
        out_shape=jax.ShapeDtypeStruct((B, L, W), BF16),
        compiler_params=_cparams(("arbitrary", "arbitrary")),
        name="diff_combine",
    )(o, lam4, g_out.reshape(1, dv))


def _decode_kernel(*refs, n_scalar, n_chunks, diff, lam_init, n_heads, n_new):
    refs = refs[n_scalar:]
    q_ref = refs[0]
    k_refs = refs[1:1 + n_chunks]
    v_refs = refs[1 + n_chunks:1 + 2 * n_chunks]
    bm_refs = refs[1 + 2 * n_chunks:1 + 3 * n_chunks]
    rest = refs[1 + 3 * n_chunks:]
    if diff:
        kn_ref, vn_ref, bmn_ref, lam_ref, g_ref, out_ref, m_s, l_s, acc_s = rest
    else:
        kn_ref, vn_ref, bmn_ref, out_ref, m_s, l_s, acc_s = rest
    p = pl.program_id(1)

    @pl.when(p == 0)
    def _():
        m_s[...] = jnp.full(m_s.shape, NEG, F32)
        l_s[...] = jnp.zeros(l_s.shape, F32)
        acc_s[...] = jnp.zeros(acc_s.shape, F32)

    q = q_ref[0]

    def absorb(k2s, v2s, bms):
        ss = [_bdot_nt(q, k2) + bm for k2, bm in zip(k2s, bms)]
        m_old = m_s[...]
        m_new = m_old
        for s in ss:
            m_new = jnp.maximum(m_new, jnp.max(s, axis=-1, keepdims=True))
        alpha = jnp.exp(m_old - m_new)
        l_new = alpha * l_s[...]
        acc = alpha * acc_s[...]
        for s, v2 in zip(ss, v2s):
            pr = jnp.exp(s - m_new)
            l_new = l_new + jnp.sum(pr, axis=-1, keepdims=True)
            acc = acc + _bdot(pr, v2)
        l_s[...] = l_new
        acc_s[...] = acc
        m_s[...] = m_new

    def rows2d(ref):
        x = ref[0]
        return x.reshape(x.shape[0] * x.shape[1], x.shape[2])

    absorb([rows2d(r) for r in k_refs], [rows2d(r) for r in v_refs], [r[0] for r in bm_refs])

    @pl.when(p == pl.num_programs(1) - 1)
    def _():
        absorb([kn_ref[0]], [vn_ref[0]], [bmn_ref[...]])
        o = acc_s[...] / l_s[...]
        if diff:
            half = n_heads * n_new
            lam = _diff_lambda(lam_ref, lam_init)
            d = o[0:half] - lam * o[half:2 * half]
            ms = jnp.mean(d * d, axis=-1, keepdims=True)
            o = d * lax.rsqrt(ms + EPS) * g_ref[...] * (1.0 - lam_init)
        dv = o.shape[1]
        for h in range(n_heads):
            out_ref[0, :, h * dv:(h + 1) * dv] = o[h * n_new:(h + 1) * n_new].astype(out_ref.dtype)


def _head_expand(bias_rows, row_head, n_heads):
    ok = jnp.asarray(row_head[:, None] == np.arange(n_heads)[None, :])
    out = jnp.where(ok[:, None, :], bias_rows[..., None], NEG)
    return out.reshape(bias_rows.shape[:-1] + (bias_rows.shape[-1] * n_heads,))


def swa_decode(qn, k_new, v_new, cache_k, cache_v, rel_bias):
    B, Wb, H, dh = cache_k.shape
    T = qn.shape[1]
    R = H * T
    tk = _pick(Wb, (512, 256, 128))
    nblk = Wb // tk
    q2 = jnp.transpose(qn.reshape(B, T, H, dh), (0, 2, 1, 3)).reshape(B, R, dh)
    row_head = np.arange(R) // T
    dist_np = (Wb + np.arange(T))[:, None] - np.arange(Wb)[None, :]
    bias, ok = _swa_bias(rel_bias, dist_np)
    assert ok[:, :tk].any(axis=1).all()
    bm = _head_expand(bias.reshape(R, Wb), row_head, H)
    bm = jnp.transpose(bm.reshape(R, nblk, tk * H), (1, 0, 2))
    dn = np.arange(T)[:, None] - np.arange(T)[None, :]
    bias_n, _ = _swa_bias(rel_bias, dn)
    bmn = _head_expand(bias_n.reshape(R, T), row_head, H)
    kn2 = k_new.reshape(B, T * H, dh)
    vn2 = v_new.reshape(B, T * H, dh)
    return pl.pallas_call(
        functools.partial(_decode_kernel, n_scalar=0, n_chunks=1, diff=False, lam_init=0.0, n_heads=H, n_new=T),
        grid=(B, nblk),
        in_specs=[
            pl.BlockSpec((1, R, dh), lambda b, p: (b, 0, 0)),
            pl.BlockSpec((1, tk, H, dh), lambda b, p: (b, p, 0, 0)),
            pl.BlockSpec((1, tk, H, dh), lambda b, p: (b, p, 0, 0)),
            pl.BlockSpec((1, R, tk * H), lambda b, p: (p, 0, 0)),
            pl.BlockSpec((1, T * H, dh), lambda b, p: (b, 0, 0)),
            pl.BlockSpec((1, T * H, dh), lambda b, p: (b, 0, 0)),
            pl.BlockSpec((R, T * H), lambda b, p: (0, 0)),
        ],
        out_specs=pl.BlockSpec((1, T, H * dh), lambda b, p: (b, 0, 0)),
        out_shape=jax.ShapeDtypeStruct((B, T, H * dh), BF16),
        scratch_shapes=[pltpu.VMEM((R, 1), F32), pltpu.VMEM((R, 1), F32), pltpu.VMEM((R, dh), F32)],
        compiler_params=_cparams(("arbitrary", "arbitrary")),
        name="swa_decode",
    )(q2, cache_k, cache_v, bm, kn2, vn2, bmn)


def _np_rel_bucket(dist):
    exact = REL_BUCKETS // 2
    d_f = np.maximum(dist, 1).astype(np.float32)
    large = exact + (np.log(d_f / np.float32(exact)) / np.float32(math.log(REL_MAX_DIST / exact))
                     * np.float32(REL_BUCKETS - exact)).astype(np.int32)
    large = np.minimum(large, REL_BUCKETS - 1)
    return np.where(dist < exact, dist, large)


def diff_paged(qn, k_new, v_new, cache_k, cache_v, page_table, rel_bias, lam4, g_out, lam_init):
    B, T, H, _, dh = qn.shape
    dv = v_new.shape[-1]
    n_pages = page_table.shape[1]
    page = cache_k.shape[1]
    P = n_pages * page
    R = 2 * H * T
    pp = next(c for c in (8, 4, 2, 1) if n_pages % c == 0)
    qt = jnp.transpose(qn, (0, 3, 2, 1, 4))
    zeros = jnp.zeros_like(qt[:, 0])
    qm = jnp.concatenate([jnp.concatenate([qt[:, 0], zeros], axis=-1),
                          jnp.concatenate([zeros, qt[:, 1]], axis=-1)], axis=1)
    qm = qm.reshape(B, R, 2 * dh)
    pg = np.arange(n_pages)[:, None, None]
    tt = np.arange(T)[None, :, None]
    ii = np.arange(page)[None, None, :]
    dist_np = P + tt - pg * page - ii
    bucket_np = _np_rel_bucket(dist_np)
    far = np.all(bucket_np == bucket_np[0:1], axis=(1, 2))
    near_pages = [int(x) for x in np.nonzero(~far)[0]]
    tile_pages = [0] + near_pages
    tile_of_page = np.zeros((n_pages,), np.int32)
    for ti, pgi in enumerate(near_pages):
        tile_of_page[pgi] = ti + 1
    nt = len(tile_pages)
    bias_t = _bias_of_dist(rel_bias, jnp.asarray(dist_np[tile_pages], jnp.int32))
    bias_rows = jnp.transpose(bias_t.reshape(2, H, nt, T, page), (2, 0, 1, 3, 4)).reshape(nt, R, page)
    row_head = (np.arange(R) // T) % H
    bm = _head_expand(bias_rows, row_head, H)
    dn = np.arange(T)[:, None] - np.arange(T)[None, :]
    bias_n, _ = _causal_bias(rel_bias, dn)
    bmn = _head_expand(bias_n.reshape(R, T), row_head, H)
    kn2 = k_new.reshape(B, T * H, 2 * dh)
    vn2 = v_new.reshape(B, T * H, dv)

    def kv_spec(j, d):
        return pl.BlockSpec((1, page, H, d), lambda b, p, pt, tl: (pt[b * n_pages + p * pp + j], 0, 0, 0))

    def bm_spec(j):
        return pl.BlockSpec((1, R, page * H), lambda b, p, pt, tl: (tl[p * pp + j], 0, 0))

    grid_spec = pltpu.PrefetchScalarGridSpec(
        num_scalar_prefetch=2,
        grid=(B, n_pages // pp),
        in_specs=(
            [pl.BlockSpec((1, R, 2 * dh), lambda b, p, pt, tl: (b, 0, 0))]
            + [kv_spec(j, 2 * dh) for j in range(pp)]
            + [kv_spec(j, dv) for j in range(pp)]
            + [bm_spec(j) for j in range(pp)]
            + [pl.BlockSpec((1, T * H, 2 * dh), lambda b, p, pt, tl: (b, 0, 0)),
               pl.BlockSpec((1, T * H, dv), lambda b, p, pt, tl: (b, 0, 0)),
               pl.BlockSpec((R, T * H), lambda b, p, pt, tl: (0, 0)),
               pl.BlockSpec(lam4.shape, lambda b, p, pt, tl: (0, 0)),
               pl.BlockSpec((1, dv), lambda b, p, pt, tl: (0, 0))]),
        out_specs=pl.BlockSpec((1, T, H * dv), lambda b, p, pt, tl: (b, 0, 0)),
        scratch_shapes=[pltpu.VMEM((R, 1), F32), pltpu.VMEM((R, 1), F32), pltpu.VMEM((R, dv), F32)],
    )
    return pl.pallas_call(
        functools.partial(_decode_kernel, n_scalar=2, n_chunks=pp, diff=True, lam_init=lam_init,
                          n_heads=H, n_new=T),
        grid_spec=grid_spec,
        out_shape=jax.ShapeDtypeStruct((B, T, H * dv), BF16),
        compiler_params=_cparams(("arbitrary", "arbitrary")),
        name="diff_paged",
    )(page_table.reshape(-1), jnp.asarray(tile_of_page), qm, *([cache_k] * pp), *([cache_v] * pp),
      *([bm] * pp), kn2, vn2, bmn, lam4, g_out.reshape(1, dv))


def _rglru_kernel(y_ref, x_ref, cs_ref, h0_ref, cw_ref, cb_ref, wa_ref, ba_ref, wx_ref, bx_ref, lam_ref,
                  out_ref, conv_ref, hl_ref, xpad, hcar, *, l_valid):
    i = pl.program_id(1)
    T, D = x_ref.shape[1], x_ref.shape[2]

    @pl.when(i == 0)
    def _():
        xpad[0:8, :] = cs_ref[0]
        hcar[...] = h0_ref[0]

    xpad[8:8 + T, :] = x_ref[0]
    conv = cb_ref[...] + cw_ref[CONV_W - 1:CONV_W, :] * xpad[8:8 + T, :]
    for j in range(CONV_W - 1):
        s = CONV_W - 1 - j
        conv = conv + cw_ref[j:j + 1, :] * xpad[8 - s:8 - s + T, :]
    nb = wa_ref.shape[0]
    bs = D // nb
    r_parts, i_parts = [], []
    for n in range(nb):
        xb = conv[:, n * bs:(n + 1) * bs]
        r_parts.append(_bdot(xb, wa_ref[n]))
        i_parts.append(_bdot(xb, wx_ref[n]))
    r = jax.nn.sigmoid(jnp.concatenate(r_parts, axis=-1) + ba_ref[...])
    ig = jax.nn.sigmoid(jnp.concatenate(i_parts, axis=-1) + bx_ref[...])
    log_a = -LRU_C * r * _softplus(-lam_ref[...])
    a = jnp.exp(log_a)
    bb = jnp.sqrt(-jnp.tanh(log_a) * (1.0 + a * a)) * ig * conv
    rowid = lax.broadcasted_iota(jnp.int32, (T, D), 0)
    s = 1
    while s < T:
        keep = rowid >= s
        a_sh = pltpu.roll(a, s, 0)
        b_sh = pltpu.roll(bb, s, 0)
        bb = jnp.where(keep, a * b_sh + bb, bb)
        a = jnp.where(keep, a * a_sh, a)
        s *= 2
    hs = a * hcar[...] + bb
    out_ref[0] = (hs * _gelu(y_ref[0])).astype(out_ref.dtype)
    hcar[...] = hs[T - 1:T, :]
    tail = xpad[T:T + 8, :]
    xpad[0:8, :] = tail

    @pl.when(i == pl.num_programs(1) - 1)
    def _():
        hl_ref[0] = hs[l_valid - 1:l_valid, :]
        conv_ref[0] = xpad[8 + l_valid - (CONV_W - 1):8 + l_valid, :]


def rglru_core(z, conv_state, h0, conv_w, conv_b, w_a, b_a, w_x, b_x, lam, l_valid_last, tblk):
    B, L, D2 = z.shape
    D = D2 // 2
    cs8 = jnp.concatenate([jnp.zeros((B, 8 - (CONV_W - 1), D), F32), conv_state.astype(F32)], axis=1)
    vec = lambda a: a.reshape(1, D)
    out, conv_new, h_last = pl.pallas_call(
        functools.partial(_rglru_kernel, l_valid=l_valid_last),
        grid=(B, L // tblk),
        in_specs=[
            pl.BlockSpec((1, tblk, D), lambda b, i: (b, i, 0)),
            pl.BlockSpec((1, tblk, D), lambda b, i: (b, i, 1)),
            pl.BlockSpec((1, 8, D), lambda b, i: (b, 0, 0)),
            pl.BlockSpec((1, 1, D), lambda b, i: (b, 0, 0)),
            pl.BlockSpec((CONV_W, D), lambda b, i: (0, 0)),
            pl.BlockSpec((1, D), lambda b, i: (0, 0)),
            pl.BlockSpec(w_a.shape, lambda b, i: (0, 0, 0)),
            pl.BlockSpec((1, D), lambda b, i: (0, 0)),
            pl.BlockSpec(w_x.shape, lambda b, i: (0, 0, 0)),
            pl.BlockSpec((1, D), lambda b, i: (0, 0)),
            pl.BlockSpec((1, D), lambda b, i: (0, 0)),
        ],
        out_specs=[
            pl.BlockSpec((1, tblk, D), lambda b, i: (b, i, 0)),
            pl.BlockSpec((1, CONV_W - 1, D), lambda b, i: (b, 0, 0)),
            pl.BlockSpec((1, 1, D), lambda b, i: (b, 0, 0)),
        ],
        out_shape=[
            jax.ShapeDtypeStruct((B, L, D), BF16),
            jax.ShapeDtypeStruct((B, CONV_W - 1, D), F32),
            jax.ShapeDtypeStruct((B, 1, D), F32),
        ],
        scratch_shapes=[pltpu.VMEM((tblk + 8, D), F32), pltpu.VMEM((1, D), F32)],
        compiler_params=_cparams(("arbitrary", "arbitrary")),
        name="rglru",
    )(z, z, cs8, h0.astype(F32).reshape(B, 1, D), conv_w, vec(conv_b), w_a.astype(BF16), vec(b_a),
      w_x.astype(BF16), vec(b_x), vec(lam))
    return out, conv_new, h_last.reshape(B, D)


def _argmax_rows(s):
    n, tb = s.shape
    row8 = lax.broadcasted_iota(jnp.int32, (8, tb), 0).astype(F32)
    vals = [s[i:i + 8] for i in range(0, n, 8)]
    rows = [row8 + float(i) for i in range(0, n, 8)]
    while len(vals) > 1:
        nv, nr = [], []
        for a in range(0, len(vals) - 1, 2):
            first = vals[a] >= vals[a + 1]
            nv.append(jnp.maximum(vals[a], vals[a + 1]))
            nr.append(jnp.where(first, rows[a], rows[a + 1]))
        if len(vals) % 2:
            nv.append(vals[-1])
            nr.append(rows[-1])
        vals, rows = nv, nr
    v, r = vals[0], rows[0]
    m = jnp.max(v, axis=0, keepdims=True)
    return m, jnp.min(jnp.where(v == m, r, float(n)), axis=0, keepdims=True)


def _topk_rows(s, k):
    n, tb = s.shape
    rowf = lax.broadcasted_iota(jnp.int32, (n, tb), 0).astype(F32)
    vals, idxs = [], []
    for _ in range(k):
        m, idx = _argmax_rows(s)
        vals.append(m)
        idxs.append(idx)
        s = jnp.where(rowf == idx, -jnp.inf, s)
    return jnp.concatenate(vals, axis=0), jnp.concatenate(idxs, axis=0)


def _pair_plan(k):
    full, small = [], []
    a = 0
    while a < k and k // (a + 1) > 1:
        nb, b0 = k // (a + 1), 0
        while nb - b0 >= 8:
            full.append((a, b0))
            b0 += 8
        if nb > b0:
            small.append((a, b0, nb - b0))
        a += 1
    a0 = a
    assert k % 8 == 0 and (k - a0) % 8 == 0 and (k & (k - 1)) == 0
    bins = []
    for a_, b0, n in sorted(small, key=lambda p: -p[2]):
        for bn in bins:
            used = sum(p[3] for p in bn)
            if used + n <= 8:
                bn.append((a_, b0, used, n))
                break
        else:
            bins.append([(a_, b0, 0, n)])
    code = []
    for a_, b0 in full:
        code += [a_ * k + b0 + r for r in range(8)]
    for bn in bins:
        rows = [k * k] * 8
        for a_, b0, off, n in bn:
            for r in range(n):
                rows[off + r] = a_ * k + b0 + r
        code += rows
    code += [a_ * k for a_ in range(a0, k)]
    return full, bins, a0, np.array(code, np.float32)


def _route_kernel(h_ref, wq_ref, k1_ref, k2_ref, code_ref, e1_ref, e2_ref, g_ref, *, topk):
    k = topk
    q = jnp.dot(h_ref[...], wq_ref[...], preferred_element_type=F32)
    dk2 = k1_ref.shape[2]
    tb = q.shape[0]
    nt = (((1,), (1,)), ((), ()))
    s1 = lax.dot_general(k1_ref[0], q[:, :dk2], nt, precision=HIGHEST, preferred_element_type=F32)
    s2 = lax.dot_general(k2_ref[0], q[:, dk2:], nt, precision=HIGHEST, preferred_element_type=F32)
    sv1, si1 = _topk_rows(s1, k)
    sv2, si2 = _topk_rows(s2, k)
    full, bins, a0, _ = _pair_plan(k)
    row8 = lax.broadcasted_iota(jnp.int32, (8, tb), 0)
    groups = [sv1[a:a + 1] + sv2[b0:b0 + 8] for a, b0 in full]
    for bn in bins:
        v = jnp.full((8, tb), -jnp.inf, F32)
        for a, b0, off, n in bn:
            piece = sv1[a:a + 1] + (pltpu.roll(sv2[b0:b0 + 8], off, 0) if off else sv2[b0:b0 + 8])
            v = jnp.where((row8 >= off) & (row8 < off + n), piece, v)
        groups.append(v)
    for c in range(a0, k, 8):
        groups.append(sv1[c:c + 8] + sv2[0:1])
    cand = jnp.concatenate(groups, axis=0)
    code = code_ref[...]
    vals, poss = [], []
    for _ in range(k):
        m = jnp.max(cand, axis=0, keepdims=True)
        pos = jnp.min(jnp.where(cand == m, code, float(k * k)), axis=0, keepdims=True)
        vals.append(m)
        poss.append(pos)
        cand = jnp.where(code == pos, -jnp.inf, cand)
    val = jnp.concatenate(vals, axis=0)
    pos = jnp.concatenate(poss, axis=0)
    ra = jnp.floor(pos * (1.0 / k))
    rb = pos - k * ra
    e1 = jnp.zeros((k, tb), F32)
    e2 = jnp.zeros((k, tb), F32)
    for r in range(k):
        e1 = jnp.where(ra == float(r), si1[r:r + 1], e1)
        e2 = jnp.where(rb == float(r), si2[r:r + 1], e2)
    e1_ref[0] = e1
    e2_ref[0] = e2
    ex = jnp.exp(val - val[0:1])
    g_ref[0] = ex / jnp.sum(ex, axis=0, keepdims=True)


def peer_route(h, wq, keys):
    N, D = h.shape
    nh, _, nk, dk2 = keys.shape
    tb = _pick(N, (1024, 512, 256, 128))
    code = _pair_plan(TOPK_P)[3]
    code = jnp.asarray(np.broadcast_to(code[:, None], (code.shape[0], tb)))
    shp = jax.ShapeDtypeStruct((nh, TOPK_P, N), F32)
    o_spec = pl.BlockSpec((1, TOPK_P, tb), lambda hh, i: (hh, 0, i))
    return pl.pallas_call(
        functools.partial(_route_kernel, topk=TOPK_P),
        grid=(nh, N // tb),
        in_specs=[pl.BlockSpec((tb, D), lambda hh, i: (i, 0)),
                  pl.BlockSpec((D, 2 * dk2), lambda hh, i: (0, hh)),
                  pl.BlockSpec((1, nk, dk2), lambda hh, i: (2 * hh, 0, 0)),
                  pl.BlockSpec((1, nk, dk2), lambda hh, i: (2 * hh + 1, 0, 0)),
                  pl.BlockSpec(code.shape, lambda hh, i: (0, 0))],
        out_specs=[o_spec, o_spec, o_spec],
        out_shape=[shp, shp, shp],
        compiler_params=_cparams(("arbitrary", "arbitrary")),
        name="peer_route",
    )(h, wq, keys.reshape(nh * 2, nk, dk2), keys.reshape(nh * 2, nk, dk2), code)


def _route_build_kernel(e1_ref, e2_ref, g_ref, sub_ref, o_ref, *, nkeys):
    sub = sub_ref[...][None]
    one = jnp.ones((), BF16)
    zero = jnp.zeros((), BF16)
    at = jnp.where(e1_ref[...].astype(BF16) == sub, one, zero)
    bt = jnp.where(e2_ref[...].astype(BF16) == sub, g_ref[...].astype(BF16), zero)
    g3 = jnp.einsum('tik,tjk->tij', at, bt, preferred_element_type=F32)
    gt = pltpu.einshape('tij->itj', g3)
    for i in range(nkeys):
        o_ref[:, i * nkeys:(i + 1) * nkeys] = gt[i].astype(o_ref.dtype)


def peer_build(e1, e2, g, nkeys):
    nh, k, N = e1.shape
    ks = nh * k
    slot = lambda a: jnp.transpose(a, (2, 0, 1)).reshape(N, 1, ks)
    tb = _pick(N, (64, 32))
    spec = pl.BlockSpec((tb, 1, ks), lambda i: (i, 0, 0))
    sub = jnp.asarray(np.broadcast_to(np.arange(nkeys, dtype=np.float32)[:, None], (nkeys, ks)), BF16)
    return pl.pallas_call(
        functools.partial(_route_build_kernel, nkeys=nkeys),
        grid=(N // tb,),
        in_specs=[spec, spec, spec, pl.BlockSpec((nkeys, ks), lambda i: (0, 0))],
        out_specs=pl.BlockSpec((tb, nkeys * nkeys), lambda i: (i, 0)),
        out_shape=jax.ShapeDtypeStruct((N, nkeys * nkeys), BF16),
        compiler_params=_cparams(("arbitrary",)),
        name="peer_build",
    )(slot(e1), slot(e2), slot(g), sub)


def _peer_kernel(*refs, with_next):
    if with_next:
        x_ref, ut_ref, v_ref, g_ref, r_ref, gt_ref, gn_ref, sh_ref, sc_ref, o_ref, h_ref, acc_s = refs
    else:
        x_ref, ut_ref, v_ref, g_ref, r_ref, gt_ref, o_ref, acc_s = refs
    e = pl.program_id(1)

    @pl.when(e == 0)
    def _():
        acc_s[...] = jnp.zeros(acc_s.shape, F32)

    s = jnp.dot(x_ref[...], ut_ref[0], preferred_element_type=F32)
    p = (_gelu(s) * g_ref[...].astype(F32)).astype(BF16)
    acc_s[...] += jnp.dot(p, v_ref[0], preferred_element_type=F32)

    @pl.when(e == pl.num_programs(1) - 1)
    def _():
        xn = r_ref[...] + gt_ref[0] * acc_s[...]
        o_ref[...] = xn
        if with_next:
            h_ref[...] = _modulated(xn, gn_ref, sh_ref, sc_ref).astype(h_ref.dtype)


def peer_dense(h, ut_all, v_all, layer, G, res, gate, next_mod=None):
    N, D = h.shape
    E = ut_all.shape[2]
    tb = _pick(N, (512, 256, 128))
    eb = _pick(E, (1024, 512, 256, 128))

    def param_spec(p):
        if p.shape[1] == 1:
            rows = N // p.shape[0]
            assert rows % tb == 0
            return pl.BlockSpec((1, 1, D), lambda i, e: (i // (rows // tb), 0, 0))
        return pl.BlockSpec((1, tb, D), lambda i, e: (0, i, 0))

    row = pl.BlockSpec((tb, D), lambda i, e: (i, 0))
    in_specs = [row,
                pl.BlockSpec((1, D, eb), lambda i, e: (layer, 0, e)),
                pl.BlockSpec((1, eb, D), lambda i, e: (layer, e, 0)),
                pl.BlockSpec((tb, eb), lambda i, e: (i, e)),
                row, param_spec(gate)]
    args = [h, ut_all, v_all, G, res, gate]
    out_specs, out_shape = [row], [jax.ShapeDtypeStruct((N, D), F32)]
    if next_mod is not None:
        g_norm, shift, scale = next_mod
        in_specs += [pl.BlockSpec((1, D), lambda i, e: (0, 0)), param_spec(shift), param_spec(scale)]
        args += [g_norm.reshape(1, D), shift, scale]
        out_specs.append(row)
        out_shape.append(jax.ShapeDtypeStruct((N, D), BF16))
    outs = pl.pallas_call(
        functools.partial(_peer_kernel, with_next=next_mod is not None),
        grid=(N // tb, E // eb),
        in_specs=in_specs,
        out_specs=out_specs,
        out_shape=out_shape,
        scratch_shapes=[pltpu.VMEM((tb, D), F32)],
        compiler_params=_cparams(("arbitrary", "arbitrary")),
        name="peer_dense",
    )(*args)
    return (outs[0], outs[1]) if next_mod is not None else (outs[0], None)


def _pad_rows(a, n, value=0.0, axis=1):
    pad = [(0, 0)] * a.ndim
    pad[axis] = (0, n - a.shape[axis])
    return jnp.pad(a, pad, constant_values=value)


def kernel(x_prompt, x_sample, state_mlstm_C, state_mlstm_n, state_mlstm_m, cache_swa_k, cache_swa_v,
           cache_diff_k, cache_diff_v, state_rglru_conv, state_rglru_h, page_table, c_prompt, c_sample,
           w_ada, b_ada, g_norm_mix, g_norm_ffn, rel_bias,
           mlstm_w_in, mlstm_b_gates, mlstm_g_out, mlstm_w_out,
           swa_w_in, swa_g_q, swa_g_k, swa_w_out,
           diff_w_in, diff_g_q, diff_g_k, diff_lam_q1, diff_lam_k1, diff_lam_q2, diff_lam_k2, diff_g_out, diff_w_out,
           rglru_w_in, rglru_conv_w, rglru_conv_b, rglru_w_a, rglru_b_a, rglru_w_x, rglru_b_x, rglru_lambda, rglru_w_out,
           peer_w_q, peer_keys, peer_u, peer_v):
    xp, xs = x_prompt, x_sample
    Bp, S, D = xp.shape
    Bs, T, _ = xs.shape
    depth = w_ada.shape[0]
    Ns = Bs * T

    c_all = jnp.concatenate([c_prompt, c_sample], axis=0)
    mod = ada_all(c_all, w_ada, b_ada)
    mod = mod.reshape(depth, Bp + Bs, 6, 1, D)

    def mods(i, which):
        m = mod[i, :, which]
        return m[:Bp], m[Bp:]

    def rows(p):
        return jnp.broadcast_to(p, (Bs, T, D)).reshape(1, Ns, D)

    ut_all = jnp.transpose(peer_u, (0, 2, 1)).astype(BF16)
    v_all = peer_v.astype(BF16)
    nk = peer_keys.shape[3]
    Ns_pad = -(-Ns // LANES) * LANES

    sh_p, sh_s = mods(0, 0)
    sc_p, sc_s = mods(0, 1)
    hp = modulate(xp, g_norm_mix[0], sh_p, sc_p)
    hs = modulate(xs, g_norm_mix[0], sh_s, sc_s)
    for i in range(depth):
        kind = i % 4
        gt_p, gt_s = mods(i, 2)
        hs_flat = hs.reshape(1, Ns, D)
        if kind == 0:
            H = NH_A
            dk = state_mlstm_C.shape[2]
            dv = state_mlstm_C.shape[3]
            nmain = 2 * H * dk + 2 * H * dv
            wb = mlstm_w_in.astype(BF16)
            w_gate = mlstm_w_in[:, nmain:].astype(BF16)
            zp = matmul(hp, wb, cols=(0, nmain))
            gp = matmul(hp, w_gate)
            zs = matmul(hs_flat, wb, cols=(0, nmain)).reshape(Bs, T, nmain)
            gs = matmul(hs_flat, w_gate).reshape(Bs, T, 2 * H)
            chunk_p = _pick(S, (CHUNK_A, 128))
            op, mC_p, mn_p, mm_p = mlstm_core(
                zp, gp, mlstm_b_gates, jnp.zeros((Bp, H, dk, dv), F32), jnp.zeros((Bp, H, dk), F32),
                jnp.full((Bp, H), M_INIT, F32), mlstm_g_out, chunk_p)
            Tp = 128
            zs_pad = _pad_rows(zs, Tp)
            gs_pad = jnp.concatenate([
                _pad_rows(gs[..., :H], Tp, NEG), _pad_rows(gs[..., H:], Tp, -NEG)], axis=-1)
            os_, mC_s, mn_s, mm_s = mlstm_core(
                zs_pad, gs_pad, mlstm_b_gates, state_mlstm_C.astype(F32), state_mlstm_n.astype(F32),
                state_mlstm_m.astype(F32), mlstm_g_out, Tp)
            os_ = os_[:, :T]
            w_out = mlstm_w_out
        elif kind == 1:
            H = NH_B
            dh = D // H
            wb = swa_w_in.astype(BF16)
            gq = jnp.tile(swa_g_q, H)
            gk = jnp.tile(swa_g_k, H)
            qn_p = matmul_headnorm(hp, wb, (0, D), gq, dh, dh ** -0.5, BF16)
            kn_p = matmul_headnorm(hp, wb, (D, D), gk, dh, 1.0, F32)
            vp = matmul(hp, wb, cols=(2 * D, D))
            qn_s = matmul_headnorm(hs_flat, wb, (0, D), gq, dh, dh ** -0.5, BF16).reshape(Bs, T, D)
            kn_s = matmul_headnorm(hs_flat, wb, (D, D), gk, dh, 1.0, F32).reshape(Bs, T, D)
            vs_new = matmul(hs_flat, wb, cols=(2 * D, D)).reshape(Bs, T, D)
            t = _pick(S, (512, 256, 128))
            assert (_np_multiplicity(_toeplitz_dist_np(S // t, t)) > 0).any(axis=2).all()
            bm_tab = _toeplitz_bias(rel_bias, S // t, t, _swa_bias)
            op = flash_attention(qn_p, kn_p, vp, bm_tab, n_units=H, dqk=dh, dv=dh,
                                 qcol=lambda a: a, kcol=lambda a: a, vcol=lambda a: a, bmap=lambda a: a,
                                 t=t, out_dtype=BF16)
            swa_k_p = kn_p.reshape(Bp, S, H, dh)
            swa_v_p = vp.reshape(Bp, S, H, dh)
            os_ = swa_decode(qn_s, kn_s, vs_new, cache_swa_k, cache_swa_v, rel_bias)
            def shifted(cache, new):
                Wb = cache.shape[1]
                buf = lax.pad(cache.astype(F32), jnp.zeros((), F32), ((0, 0, 0), (-T, T, 0), (0, 0, 0), (0, 0, 0)))
                return lax.dynamic_update_slice(buf, new.reshape(Bs, T, H, dh), (0, Wb - T, 0, 0))

            swa_k_s = shifted(cache_swa_k, kn_s)
            swa_v_s = shifted(cache_swa_v, vs_new)
            w_out = swa_w_out
        elif kind == 2:
            H = NH_C
            dh = D // (2 * H)
            dv = 2 * dh
            lam_init = 0.8 - 0.6 * math.exp(-0.3 * i)
            lam4 = jnp.stack([diff_lam_q1, diff_lam_k1, diff_lam_q2, diff_lam_k2]).astype(F32)
            wb = diff_w_in.astype(BF16)
            gq = jnp.tile(diff_g_q.reshape(-1), H)
            gk = jnp.tile(diff_g_k.reshape(-1), H)
            qn_p = matmul_headnorm(hp, wb, (0, D), gq, dh, dh ** -0.5, BF16)
            kn_p = matmul_headnorm(hp, wb, (D, D), gk, dh, 1.0, F32)
            vp = matmul(hp, wb, cols=(2 * D, D))
            qn_s = matmul_headnorm(hs_flat, wb, (0, D), gq, dh, dh ** -0.5, BF16).reshape(Bs, T, D)
            kn_s = matmul_headnorm(hs_flat, wb, (D, D), gk, dh, 1.0, F32).reshape(Bs, T, D)
            vs_new = matmul(hs_flat, wb, cols=(2 * D, D)).reshape(Bs, T, D)
            t = _pick(S, (512, 256, 128))
            bm_tab = _toeplitz_bias(rel_bias, S // t, t, _causal_bias)
            o2 = flash_attention(qn_p, kn_p, vp, bm_tab, n_units=2 * H, dqk=dh, dv=dv,
                                 qcol=lambda a: a, kcol=lambda a: a, vcol=lambda a: a // 2,
                                 bmap=lambda a: (a % 2) * H + a // 2, t=t, out_dtype=F32)
            op = diff_combine(o2, lam4, diff_g_out, lam_init)
            diff_k_p = kn_p.reshape(Bp, S, H, 2 * dh)
            diff_v_p = vp.reshape(Bp, S, H, dv)
            diff_k_s = kn_s.reshape(Bs, T, H, 2 * dh)
            diff_v_s = vs_new.reshape(Bs, T, H, dv)
            os_ = diff_paged(qn_s.reshape(Bs, T, H, 2, dh), diff_k_s, diff_v_s, cache_diff_k, cache_diff_v,
                             page_table, rel_bias, lam4, diff_g_out, lam_init)
            w_out = diff_w_out
        else:
            wb = rglru_w_in.astype(BF16)
            zp = matmul(hp, wb)
            zs = matmul(hs_flat, wb).reshape(Bs, T, -1)
            tblk = _pick(S, (256, 128))
            op, conv_p, h_p = rglru_core(zp, jnp.zeros((Bp, CONV_W - 1, D), F32), jnp.zeros((Bp, D), F32),
                                         rglru_conv_w, rglru_conv_b, rglru_w_a, rglru_b_a, rglru_w_x, rglru_b_x,
                                         rglru_lambda, tblk, tblk)
            os_, conv_s, h_s = rglru_core(_pad_rows(zs, 8), state_rglru_conv, state_rglru_h,
                                          rglru_conv_w, rglru_conv_b, rglru_w_a, rglru_b_a, rglru_w_x, rglru_b_x,
                                          rglru_lambda, T, 8)
            os_ = os_[:, :T]
            w_out = rglru_w_out
        sh_p, sh_s = mods(i, 3)
        sc_p, sc_s = mods(i, 4)
        wb = w_out.astype(BF16)
        xp, hp = matmul_res_mod(op, wb, xp, gt_p, g_norm_ffn[i], sh_p, sc_p)
        xs, hs = matmul_res_mod(os_.reshape(1, Ns, -1), wb, xs.reshape(1, Ns, D), rows(gt_s), g_norm_ffn[i],
                                rows(sh_s), rows(sc_s))
        hp = hp.reshape(Bp * S, D)
        hs = hs.reshape(Ns, D)

        gt_p, gt_s = mods(i, 5)
        wq = peer_w_q[i].astype(BF16)
        Gp = peer_build(*peer_route(hp, wq, peer_keys[i]), nk)
        Gs = peer_build(*peer_route(_pad_rows(hs, Ns_pad, axis=0), wq, peer_keys[i]), nk)[:Ns]
        if i + 1 < depth:
            sh_p, sh_s = mods(i + 1, 0)
            sc_p, sc_s = mods(i + 1, 1)
            next_p = (g_norm_mix[i + 1], sh_p, sc_p)
            next_s = (g_norm_mix[i + 1], rows(sh_s), rows(sc_s))
        else:
            next_p = next_s = None
        xp, hp = peer_dense(hp, ut_all, v_all, i, Gp, xp.reshape(Bp * S, D), gt_p, next_p)
        xs, hs = peer_dense(hs, ut_all, v_all, i, Gs, xs.reshape(Ns, D), rows(gt_s), next_s)
        xp = xp.reshape(Bp, S, D)
        xs = xs.reshape(Bs, T, D)
        if hp is not None:
            hp = hp.reshape(Bp, S, D)
            hs = hs.reshape(Bs, T, D)

    return (xp, xs, mC_p, mC_s, mn_p, mn_s, mm_p, mm_s, swa_k_p, swa_k_s, swa_v_p, swa_v_s,
            diff_k_p, diff_k_s, diff_v_p, diff_v_s, conv_p, conv_s, h_p, h_s)
```

```python
import functools
import math

import numpy as np
import jax
import jax.numpy as jnp
from jax import lax
from jax.experimental import pallas as pl
from jax.experimental.pallas import tpu as pltpu

F32 = jnp.float32
BF16 = jnp.bfloat16
HIGHEST = lax.Precision.HIGHEST

EPS = 1e-6
NEG = -1e30
VMEM_LIMIT = 56 * 1024 * 1024
LANES = 128
LOG2E = math.log2(math.e)

NH_A, CHUNK_A, M_INIT = 8, 256, -1e30
NH_B = 16
DIL_PAIRS = ((128, 1), (512, 4), (2048, 16))
NH_C = 8
NBLK_D, CONV_W, LRU_C = 16, 4, 8.0
NH_P, N_KEYS, TOPK_P = 8, 128, 16
REL_BUCKETS, REL_MAX_DIST = 32, 2048
PAGE_SIZE = 128


def _cparams(sem):
    return pltpu.CompilerParams(dimension_semantics=sem, vmem_limit_bytes=VMEM_LIMIT)


def _bdot(a, b):
    return jnp.dot(a.astype(BF16), b.astype(BF16), preferred_element_type=F32)


def _bdot_nt(a, b):
    return lax.dot_general(a.astype(BF16), b.astype(BF16), (((1,), (1,)), ((), ())),
                           preferred_element_type=F32)


def _gelu(x):
    return 0.5 * x * (1.0 + lax.erf(x * (1.0 / math.sqrt(2.0))))


def _log_sigmoid(x):
    return jnp.minimum(x, 0.0) - jnp.log1p(jnp.exp(-jnp.abs(x)))


def _softplus(x):
    return jnp.maximum(x, 0.0) + jnp.log1p(jnp.exp(-jnp.abs(x)))


def _pick(n, pref):
    for t in pref:
        if n % t == 0:
            return t
    return n


def _ada_kernel(c_ref, w_ref, b_ref, o_ref):
    c = c_ref[...]
    a = c * jax.nn.sigmoid(c)
    o_ref[0] = jnp.dot(a, w_ref[0], preferred_element_type=F32, precision=HIGHEST) + b_ref[0]


def ada_all(c_all, w_ada, b_ada):
    R, D = c_all.shape
    nl, _, N = w_ada.shape
    tn = _pick(N, (2048, 1024, 512, 256, 128))
    return pl.pallas_call(
        _ada_kernel,
        grid=(nl, N // tn),
        in_specs=[pl.BlockSpec((R, D), lambda l, j: (0, 0)),
                  pl.BlockSpec((1, D, tn), lambda l, j: (l, 0, j)),
                  pl.BlockSpec((1, 1, tn), lambda l, j: (l, 0, j))],
        out_specs=pl.BlockSpec((1, R, tn), lambda l, j: (l, 0, j)),
        out_shape=jax.ShapeDtypeStruct((nl, R, N), F32),
        compiler_params=_cparams(("arbitrary", "arbitrary")),
        name="ada",
    )(c_all, w_ada, b_ada.reshape(nl, 1, N))


def _modulate_kernel(x_ref, g_ref, sh_ref, sc_ref, o_ref):
    x = x_ref[0]
    ms = jnp.mean(x * x, axis=-1, keepdims=True)
    y = x * lax.rsqrt(ms + EPS) * g_ref[...]
    o_ref[0] = (y * (1.0 + sc_ref[0]) + sh_ref[0]).astype(o_ref.dtype)


def modulate(x, g, shift, scale):
    B, L, D = x.shape
    tl = _pick(L, (512, 256, 128))
    return pl.pallas_call(
        _modulate_kernel,
        grid=(B, L // tl),
        in_specs=[pl.BlockSpec((1, tl, D), lambda b, i: (b, i, 0)),
                  pl.BlockSpec((1, D), lambda b, i: (0, 0)),
                  pl.BlockSpec((1, 1, D), lambda b, i: (b, 0, 0)),
                  pl.BlockSpec((1, 1, D), lambda b, i: (b, 0, 0))],
        out_specs=pl.BlockSpec((1, tl, D), lambda b, i: (b, i, 0)),
        out_shape=jax.ShapeDtypeStruct((B, L, D), BF16),
        compiler_params=_cparams(("arbitrary", "arbitrary")),
        name="modulate",
    )(x, g.reshape(1, D), shift, scale)


def _mm_kernel(x_ref, w_ref, o_ref):
    o_ref[0] = jnp.dot(x_ref[0], w_ref[...], preferred_element_type=F32).astype(o_ref.dtype)


def matmul(x, w, out_dtype=F32, cols=None):
    B, L, K = x.shape
    c0, N = cols if cols is not None else (0, w.shape[1])
    tm = _pick(L, (512, 256, 128))
    tn = next((c for c in (1024, 512, 256, 128) if N % c == 0 and c0 % c == 0), N)
    assert c0 % tn == 0 and (tn == w.shape[1] or tn % LANES == 0)
    j0 = c0 // tn
    grid = (N // tn, B, L // tm)
    x_spec = pl.BlockSpec((1, tm, K), lambda j, b, i: (b, i, 0))
    w_spec = pl.BlockSpec((K, tn), lambda j, b, i: (0, j0 + j))
    o_spec = pl.BlockSpec((1, tm, tn), lambda j, b, i: (b, i, j))
    return pl.pallas_call(
        _mm_kernel, grid=grid, in_specs=[x_spec, w_spec], out_specs=o_spec,
        out_shape=jax.ShapeDtypeStruct((B, L, N), out_dtype),
        compiler_params=_cparams(("arbitrary",) * 3), name="matmul",
    )(x, w)


def _modulated(xn, gn_ref, sh_ref, sc_ref):
    ms = jnp.mean(xn * xn, axis=-1, keepdims=True)
    return xn * lax.rsqrt(ms + EPS) * gn_ref[...] * (1.0 + sc_ref[0]) + sh_ref[0]


def _mm_res_mod_kernel(x_ref, w_ref, r_ref, gt_ref, gn_ref, sh_ref, sc_ref, o_ref, h_ref):
    acc = jnp.dot(x_ref[0], w_ref[...], preferred_element_type=F32)
    xn = r_ref[0] + gt_ref[0] * acc
    o_ref[0] = xn
    h_ref[0] = _modulated(xn, gn_ref, sh_ref, sc_ref).astype(h_ref.dtype)


def _row_param_spec(p, tm, n):
    if p.shape[1] == 1:
        return pl.BlockSpec((1, 1, n), lambda b, i: (b, 0, 0))
    return pl.BlockSpec((1, tm, n), lambda b, i: (b, i, 0))


def matmul_res_mod(x, w, res, gate, g_norm, shift, scale):
    B, L, K = x.shape
    N = w.shape[1]
    tm = _pick(L, (256, 128))
    row = pl.BlockSpec((1, tm, N), lambda b, i: (b, i, 0))
    return pl.pallas_call(
        _mm_res_mod_kernel,
        grid=(B, L // tm),
        in_specs=[pl.BlockSpec((1, tm, K), lambda b, i: (b, i, 0)),
                  pl.BlockSpec((K, N), lambda b, i: (0, 0)),
                  row, _row_param_spec(gate, tm, N),
                  pl.BlockSpec((1, N), lambda b, i: (0, 0)),
                  _row_param_spec(shift, tm, N), _row_param_spec(scale, tm, N)],
        out_specs=[row, row],
        out_shape=[jax.ShapeDtypeStruct((B, L, N), F32), jax.ShapeDtypeStruct((B, L, N), BF16)],
        compiler_params=_cparams(("arbitrary", "arbitrary")),
        name="matmul_res_mod",
    )(x, w, res, gate, g_norm.reshape(1, N), shift, scale)


def _mm_headnorm_kernel(x_ref, w_ref, g_ref, o_ref, *, dh, scale):
    acc = jnp.dot(x_ref[0], w_ref[...], preferred_element_type=F32)
    for h in range(acc.shape[1] // dh):
        sl = slice(h * dh, (h + 1) * dh)
        a = acc[:, sl]
        ms = jnp.mean(a * a, axis=-1, keepdims=True)
        y = a * lax.rsqrt(ms + EPS) * g_ref[:, sl]
        o_ref[0, :, sl] = (y * scale if scale != 1.0 else y).astype(o_ref.dtype)


def matmul_headnorm(x, w, cols, gain_full, dh, scale, out_dtype):
    B, L, K = x.shape
    c0, N = cols
    tm = _pick(L, (512, 256, 128))
    tn = next(c for c in (1024, 512, 256, 128) if N % c == 0 and c0 % c == 0)
    j0 = c0 // tn
    return pl.pallas_call(
        functools.partial(_mm_headnorm_kernel, dh=dh, scale=scale),
        grid=(N // tn, B, L // tm),
        in_specs=[pl.BlockSpec((1, tm, K), lambda j, b, i: (b, i, 0)),
                  pl.BlockSpec((K, tn), lambda j, b, i: (0, j0 + j)),
                  pl.BlockSpec((1, tn), lambda j, b, i: (0, j))],
        out_specs=pl.BlockSpec((1, tm, tn), lambda j, b, i: (b, i, j)),
        out_shape=jax.ShapeDtypeStruct((B, L, N), out_dtype),
        compiler_params=_cparams(("arbitrary",) * 3),
        name="matmul_headnorm",
    )(x, w, gain_full.reshape(1, N))


def _mlstm_kernel(q_ref, k_ref, v_ref, o_ref, gc_ref, gr_ref, bc_ref, br_ref,
                  C0_ref, n0_ref, m0_ref, gout_ref,
                  hs_ref, C_ref, n_ref, m_ref, C_s, n_s, m_s, *, dk):
    ci = pl.program_id(2)

    @pl.when(ci == 0)
    def _():
        C_s[...] = C0_ref[0, 0]
        n_s[...] = n0_ref[0, 0]
        m_s[...] = m0_ref[0, 0]

    q = q_ref[0]
    k = k_ref[0] * (dk ** -0.5)
    v = v_ref[0]
    c = q.shape[0]
    gc = gc_ref[0, 0] + bc_ref[0]
    gr = gr_ref[0, 0] + br_ref[0]
    li_c, lf_c = gc[:, 0:1], _log_sigmoid(gc[:, 1:2])
    li_r, lf_r = gr[0:1, :], _log_sigmoid(gr[1:2, :])
    row = lax.broadcasted_iota(jnp.int32, (c, c), 0)
    col = lax.broadcasted_iota(jnp.int32, (c, c), 1)
    causal = col <= row
    b_c = jnp.sum(jnp.where(causal, lf_r, 0.0), axis=1, keepdims=True)
    b_r = jnp.sum(jnp.where(row <= col, lf_c, 0.0), axis=0, keepdims=True)
    m_prev = m_s[...]
    Dm = jnp.where(causal, b_c - b_r + li_r, NEG)
    m_t = jnp.maximum(b_c + m_prev, jnp.max(Dm, axis=1, keepdims=True))
    S = _bdot_nt(q, k) * jnp.exp(Dm - m_t)
    inter = jnp.exp(b_c + m_prev - m_t)
    C = C_s[...]
    n = n_s[...]
    num = _bdot(S, v) + inter * _bdot(q, C)
    den = jnp.sum(S, axis=1, keepdims=True) + inter * jnp.sum(q * n, axis=1, keepdims=True)
    h = num / jnp.maximum(jnp.abs(den), jnp.exp(-m_t))
    hh = jax.nn.sigmoid(o_ref[0]) * h
    ms = jnp.mean(hh * hh, axis=-1, keepdims=True)
    hs_ref[0] = (hh * lax.rsqrt(ms + EPS) * gout_ref[...]).astype(hs_ref.dtype)
    m_new = m_t[c - 1:c, :]
    b_last = b_c[c - 1:c, :]
    w_r = jnp.exp(b_last - b_r + li_r - m_new)
    w_c = jnp.exp(b_last - b_c + li_c - m_new)
    decay = jnp.exp(b_last + m_prev - m_new)
    C_new = decay * C + _bdot(k.T, w_c * v)
    n_new = decay * n + jnp.dot(w_r, k, preferred_element_type=F32, precision=HIGHEST)
    C_s[...] = C_new
    n_s[...] = n_new
    m_s[...] = m_new

    @pl.when(ci == pl.num_programs(2) - 1)
    def _():
        C_ref[0, 0] = C_new
        n_ref[0, 0] = n_new
        m_ref[0, 0] = m_new


def mlstm_core(z, g, b_gates, C0, n0, m0, g_out, chunk):
    B, L, _ = z.shape
    H = NH_A
    dk = C0.shape[2]
    dv = C0.shape[3]
    nc = L // chunk
    g4 = g.reshape(B, L, 2, H)
    gcol = jnp.transpose(g4, (0, 3, 1, 2))
    grow = jnp.transpose(g4, (0, 3, 2, 1))
    bg = b_gates.reshape(2, H)
    bcol = jnp.transpose(bg, (1, 0)).reshape(H, 1, 2)
    brow = jnp.transpose(bg, (1, 0)).reshape(H, 2, 1)
    kq = (H * dk) // dk
    vo = (2 * H * dk) // dv
    oo = vo + H
    outs = pl.pallas_call(
        functools.partial(_mlstm_kernel, dk=dk),
        grid=(B, H, nc),
        in_specs=[
            pl.BlockSpec((1, chunk, dk), lambda b, h, c: (b, c, h)),
            pl.BlockSpec((1, chunk, dk), lambda b, h, c: (b, c, kq + h)),
            pl.BlockSpec((1, chunk, dv), lambda b, h, c: (b, c, vo + h)),
            pl.BlockSpec((1, chunk, dv), lambda b, h, c: (b, c, oo + h)),
            pl.BlockSpec((1, 1, chunk, 2), lambda b, h, c: (b, h, c, 0)),
            pl.BlockSpec((1, 1, 2, chunk), lambda b, h, c: (b, h, 0, c)),
            pl.BlockSpec((1, 1, 2), lambda b, h, c: (h, 0, 0)),
            pl.BlockSpec((1, 2, 1), lambda b, h, c: (h, 0, 0)),
            pl.BlockSpec((1, 1, dk, dv), lambda b, h, c: (b, h, 0, 0)),
            pl.BlockSpec((1, 1, 1, dk), lambda b, h, c: (b, h, 0, 0)),
            pl.BlockSpec((1, 1, 1, 1), lambda b, h, c: (b, h, 0, 0)),
            pl.BlockSpec((1, dv), lambda b, h, c: (0, h)),
        ],
        out_specs=[
            pl.BlockSpec((1, chunk, dv), lambda b, h, c: (b, c, h)),
            pl.BlockSpec((1, 1, dk, dv), lambda b, h, c: (b, h, 0, 0)),
            pl.BlockSpec((1, 1, 1, dk), lambda b, h, c: (b, h, 0, 0)),
            pl.BlockSpec((1, 1, 1, 1), lambda b, h, c: (b, h, 0, 0)),
        ],
        out_shape=[
            jax.ShapeDtypeStruct((B, L, H * dv), BF16),
            jax.ShapeDtypeStruct((B, H, dk, dv), F32),
            jax.ShapeDtypeStruct((B, H, 1, dk), F32),
            jax.ShapeDtypeStruct((B, H, 1, 1), F32),
        ],
        scratch_shapes=[pltpu.VMEM((dk, dv), F32), pltpu.VMEM((1, dk), F32), pltpu.VMEM((1, 1), F32)],
        compiler_params=_cparams(("arbitrary",) * 3),
        name="mlstm",
    )(z, z, z, z, gcol, grow, bcol, brow, C0, n0.reshape(B, H, 1, dk), m0.reshape(B, H, 1, 1),
      g_out.reshape(1, H * dv))
    hs, C, n, m = outs
    return hs, C, n.reshape(B, H, dk), m.reshape(B, H)


def _rel_bucket(dist):
    exact = REL_BUCKETS // 2
    d_f = jnp.maximum(dist, 1).astype(F32)
    large = exact + (jnp.log(d_f / exact) / math.log(REL_MAX_DIST / exact) * (REL_BUCKETS - exact)).astype(jnp.int32)
    large = jnp.minimum(large, REL_BUCKETS - 1)
    return jnp.where(dist < exact, dist, large)


def _bias_of_dist(rel_bias, dist):
    bucket = _rel_bucket(jnp.maximum(dist, 0))
    out = jnp.zeros((rel_bias.shape[1],) + dist.shape, F32)
    for b in range(REL_BUCKETS):
        out = jnp.where(bucket[None] == b, rel_bias[b].reshape((-1,) + (1,) * dist.ndim), out)
    return out


def _np_multiplicity(dist):
    cnt = np.zeros(dist.shape, np.int64)
    for w, d in DIL_PAIRS:
        cnt = cnt + ((dist % d == 0) & (dist <= w) & (dist >= 0))
    return cnt


def _swa_bias(rel_bias, dist_np):
    cnt = _np_multiplicity(dist_np)
    logc = jnp.log(jnp.asarray(np.maximum(cnt, 1), F32))
    bias = _bias_of_dist(rel_bias, jnp.asarray(np.maximum(dist_np, 0), jnp.int32))
    return jnp.where(jnp.asarray(cnt > 0)[None], bias + logc[None], NEG), cnt > 0


def _causal_bias(rel_bias, dist_np):
    bias = _bias_of_dist(rel_bias, jnp.asarray(np.maximum(dist_np, 0), jnp.int32))
    return jnp.where(jnp.asarray(dist_np >= 0)[None], bias, NEG), dist_np >= 0


def _toeplitz_dist_np(n_off, t):
    o = np.arange(n_off)[:, None, None]
    i = np.arange(t)[None, :, None]
    j = np.arange(t)[None, None, :]
    return o * t + i - j


def _toeplitz_kernel(b_ref, o_ref):
    t = o_ref.shape[2]
    rows = jnp.broadcast_to(b_ref[0, 0], (t, 2 * t))
    skew = pltpu.roll(rows, 0, 1, stride=1, stride_axis=0)
    o_ref[0, 0] = skew[:, t:]


def _toeplitz_bias(rel_bias, n_off, t, bias_fn, mult):
    dist = np.arange(-(t - 1), n_off * t)
    f, _ = bias_fn(rel_bias, dist)
    f = f * mult
    maps = f.shape[0]
    want = np.arange(n_off)[:, None] * t + t - np.arange(2 * t)[None, :]
    idx = np.minimum(want + (t - 1), len(dist) - 1)
    b = f[:, idx].reshape(maps, n_off, 1, 2 * t)
    return pl.pallas_call(
        _toeplitz_kernel,
        grid=(maps, n_off),
        in_specs=[pl.BlockSpec((1, 1, 1, 2 * t), lambda m, o: (m, o, 0, 0))],
        out_specs=pl.BlockSpec((1, 1, t, t), lambda m, o: (m, o, 0, 0)),
        out_shape=jax.ShapeDtypeStruct((maps, n_off, t, t), F32),
        compiler_params=_cparams(("arbitrary", "arbitrary")),
        name="toeplitz_bias",
    )(b)


def _flash_kernel(qi_ref, ki_ref, q_ref, k_ref, v_ref, bm_ref, o_ref, m_s, l_s, acc_s):
    p = pl.program_id(1)
    qi = qi_ref[p]
    ki = ki_ref[p]
    nb, tq, dv = acc_s.shape
    tk = k_ref.shape[1]

    @pl.when(ki == 0)
    def _():
        m_s[...] = jnp.full(m_s.shape, NEG, F32)
        l_s[...] = jnp.zeros(l_s.shape, F32)
        acc_s[...] = jnp.zeros(acc_s.shape, F32)

    bm = bm_ref[0, 0]
    for b in range(nb):
        s = _bdot_nt(q_ref[b], k_ref[b]) + bm
        m_prev = m_s[b]
        m_next = jnp.maximum(m_prev, jnp.max(s, axis=1, keepdims=True))
        pr = jnp.exp2(s - jnp.tile(m_next, (1, tk // LANES)))
        alpha = jnp.exp2(m_prev - m_next)
        l_s[b] = alpha * l_s[b] + jnp.sum(pr, axis=1, keepdims=True)
        acc_s[b] = acc_s[b] * jnp.tile(alpha, (1, dv // LANES)) + _bdot(pr, v_ref[b])
        m_s[b] = m_next

    @pl.when(ki == qi)
    def _():
        for b in range(nb):
            o_ref[b] = (acc_s[b] / jnp.tile(l_s[b], (1, dv // LANES))).astype(o_ref.dtype)


def flash_attention(q, k, v, bm_tab, *, n_units, dqk, dv, qcol, kcol, vcol, bmap, t, out_dtype):
    B, L = q.shape[:2]
    nq = L // t
    pairs = [(qi, ki) for qi in range(nq) for ki in range(qi + 1)]
    qi_arr = jnp.asarray(np.array([p[0] for p in pairs], np.int32))
    ki_arr = jnp.asarray(np.array([p[1] for p in pairs], np.int32))
    grid_spec = pltpu.PrefetchScalarGridSpec(
        num_scalar_prefetch=2,
        grid=(n_units, len(pairs)),
        in_specs=[
            pl.BlockSpec((B, t, dqk), lambda a, p, qa, ka: (0, qa[p], qcol(a))),
            pl.BlockSpec((B, t, dqk), lambda a, p, qa, ka: (0, ka[p], kcol(a))),
            pl.BlockSpec((B, t, dv), lambda a, p, qa, ka: (0, ka[p], vcol(a))),
            pl.BlockSpec((1, 1, t, t), lambda a, p, qa, ka: (bmap(a), qa[p] - ka[p], 0, 0)),
        ],
        out_specs=pl.BlockSpec((B, t, dv), lambda a, p, qa, ka: (0, qa[p], a)),
        scratch_shapes=[pltpu.VMEM((B, t, LANES), F32), pltpu.VMEM((B, t, LANES), F32),
                        pltpu.VMEM((B, t, dv), F32)],
    )
    return pl.pallas_call(
        _flash_kernel,
        grid_spec=grid_spec,
        out_shape=jax.ShapeDtypeStruct((B, L, n_units * dv), out_dtype),
        compiler_params=_cparams(("arbitrary", "arbitrary")),
        name="flash_attention",
    )(qi_arr, ki_arr, q, k, v, bm_tab)


def _diff_lambda(lam_ref, lam_init):
    lq1, lk1, lq2, lk2 = lam_ref[0:1, :], lam_ref[1:2, :], lam_ref[2:3, :], lam_ref[3:4, :]
    return (jnp.exp(jnp.sum(lq1 * lk1, axis=-1, keepdims=True))
            - jnp.exp(jnp.sum(lq2 * lk2, axis=-1, keepdims=True)) + lam_init)


def _diff_combine_kernel(o_ref, lam_ref, g_ref, out_ref, *, dv, lam_init):
    lam = _diff_lambda(lam_ref, lam_init)
    nh = out_ref.shape[2] // dv
    for h in range(nh):
        o0 = o_ref[0, :, (2 * h) * dv:(2 * h + 1) * dv]
        o1 = o_ref[0, :, (2 * h + 1) * dv:(2 * h + 2) * dv]
        d = o0 - lam * o1
        ms = jnp.mean(d * d, axis=-1, keepdims=True)
        out_ref[0, :, h * dv:(h + 1) * dv] = (d * lax.rsqrt(ms + EPS) * g_ref[...] * (1.0 - lam_init)
                                              ).astype(out_ref.dtype)


def diff_combine(o, lam4, g_out, lam_init):
    B, L, W2 = o.shape
    dv = g_out.shape[0]
    W = W2 // 2
    tl = _pick(L, (256, 128))
    return pl.pallas_call(
        functools.partial(_diff_combine_kernel, dv=dv, lam_init=lam_init),
        grid=(B, L // tl),
        in_specs=[pl.BlockSpec((1, tl, W2), lambda b, i: (b, i, 0)),
                  pl.BlockSpec(lam4.shape, lambda b, i: (0, 0)),
                  pl.BlockSpec((1, dv), lambda b, i: (0, 0))],
        out_specs=pl.BlockSpec((1, tl, W), lambda b, i: (b, i, 0)),
        out_shape=jax.ShapeDtypeStruct((B, L, W), BF16),
        compiler_params=_cparams(("arbitrary", "arbitrary")),
        name="diff_combine",
    )(o, lam4, g_out.reshape(1, dv))


def _decode_kernel(*refs, n_scalar, n_chunks, diff, lam_init, n_heads, n_new):
    refs = refs[n_scalar:]
    q_ref = refs[0]
    k_refs = refs[1:1 + n_chunks]
    v_refs = refs[1 + n_chunks:1 + 2 * n_chunks]
    bm_refs = refs[1 + 2 * n_chunks:1 + 3 * n_chunks]
    rest = refs[1 + 3 * n_chunks:]
    if diff:
        kn_ref, vn_ref, bmn_ref, lam_ref, g_ref, out_ref, m_s, l_s, acc_s = rest
    else:
        kn_ref, vn_ref, bmn_ref, out_ref, m_s, l_s, acc_s = rest
    p = pl.program_id(1)

    @pl.when(p == 0)
    def _():
        m_s[...] = jnp.full(m_s.shape, NEG, F32)
        l_s[...] = jnp.zeros(l_s.shape, F32)
        acc_s[...] = jnp.zeros(acc_s.shape, F32)

    q = q_ref[0]

    def absorb(k2s, v2s, bms):
        ss = [_bdot_nt(q, k2) + bm for k2, bm in zip(k2s, bms)]
        m_old = m_s[...]
        m_new = m_old
        for s in ss:
            m_new = jnp.maximum(m_new, jnp.max(s, axis=-1, keepdims=True))
        alpha = jnp.exp(m_old - m_new)
        l_new = alpha * l_s[...]
        acc = alpha * acc_s[...]
        for s, v2 in zip(ss, v2s):
            pr = jnp.exp(s - m_new)
            l_new = l_new + jnp.sum(pr, axis=-1, keepdims=True)
            acc = acc + _bdot(pr, v2)
        l_s[...] = l_new
        acc_s[...] = acc
        m_s[...] = m_new

    def rows2d(ref):
        x = ref[0]
        return x.reshape(x.shape[0] * x.shape[1], x.shape[2])

    absorb([rows2d(r) for r in k_refs], [rows2d(r) for r in v_refs], [r[0] for r in bm_refs])

    @pl.when(p == pl.num_programs(1) - 1)
    def _():
        absorb([kn_ref[0]], [vn_ref[0]], [bmn_ref[...]])
        o = acc_s[...] / l_s[...]
        if diff:
            half = n_heads * n_new
            lam = _diff_lambda(lam_ref, lam_init)
            d = o[0:half] - lam * o[half:2 * half]
            ms = jnp.mean(d * d, axis=-1, keepdims=True)
            o = d * lax.rsqrt(ms + EPS) * g_ref[...] * (1.0 - lam_init)
        dv = o.shape[1]
        for h in range(n_heads):
            out_ref[0, :, h * dv:(h + 1) * dv] = o[h * n_new:(h + 1) * n_new].astype(out_ref.dtype)


def _head_expand(bias_rows, row_head, n_heads):
    ok = jnp.asarray(row_head[:, None] == np.arange(n_heads)[None, :])
    out = jnp.where(ok[:, None, :], bias_rows[..., None], NEG)
    return out.reshape(bias_rows.shape[:-1] + (bias_rows.shape[-1] * n_heads,))


def swa_decode(qn, k_new, v_new, cache_k, cache_v, rel_bias):
    B, Wb, H, dh = cache_k.shape
    T = qn.shape[1]
    R = H * T
    tk = _pick(Wb, (512, 256, 128))
    nblk = Wb // tk
    q2 = jnp.transpose(qn.reshape(B, T, H, dh), (0, 2, 1, 3)).reshape(B, R, dh)
    row_head = np.arange(R) // T
    dist_np = (Wb + np.arange(T))[:, None] - np.arange(Wb)[None, :]
    bias, ok = _swa_bias(rel_bias, dist_np)
    assert ok[:, :tk].any(axis=1).all()
    bm = _head_expand(bias.reshape(R, Wb), row_head, H)
    bm = jnp.transpose(bm.reshape(R, nblk, tk * H), (1, 0, 2))
    dn = np.arange(T)[:, None] - np.arange(T)[None, :]
    bias_n, _ = _swa_bias(rel_bias, dn)
    bmn = _head_expand(bias_n.reshape(R, T), row_head, H)
    kn2 = k_new.reshape(B, T * H, dh)
    vn2 = v_new.reshape(B, T * H, dh)
    return pl.pallas_call(
        functools.partial(_decode_kernel, n_scalar=0, n_chunks=1, diff=False, lam_init=0.0, n_heads=H, n_new=T),
        grid=(B, nblk),
        in_specs=[
            pl.BlockSpec((1, R, dh), lambda b, p: (b, 0, 0)),
            pl.BlockSpec((1, tk, H, dh), lambda b, p: (b, p, 0, 0)),
            pl.BlockSpec((1, tk, H, dh), lambda b, p: (b, p, 0, 0)),
            pl.BlockSpec((1, R, tk * H), lambda b, p: (p, 0, 0)),
            pl.BlockSpec((1, T * H, dh), lambda b, p: (b, 0, 0)),
            pl.BlockSpec((1, T * H, dh), lambda b, p: (b, 0, 0)),
            pl.BlockSpec((R, T * H), lambda b, p: (0, 0)),
        ],
        out_specs=pl.BlockSpec((1, T, H * dh), lambda b, p: (b, 0, 0)),
        out_shape=jax.ShapeDtypeStruct((B, T, H * dh), BF16),
        scratch_shapes=[pltpu.VMEM((R, 1), F32), pltpu.VMEM((R, 1), F32), pltpu.VMEM((R, dh), F32)],
        compiler_params=_cparams(("arbitrary", "arbitrary")),
        name="swa_decode",
    )(q2, cache_k, cache_v, bm, kn2, vn2, bmn)


def _np_rel_bucket(dist):
    exact = REL_BUCKETS // 2
    d_f = np.maximum(dist, 1).astype(np.float32)
    large = exact + (np.log(d_f / np.float32(exact)) / np.float32(math.log(REL_MAX_DIST / exact))
                     * np.float32(REL_BUCKETS - exact)).astype(np.int32)
    large = np.minimum(large, REL_BUCKETS - 1)
    return np.where(dist < exact, dist, large)


def diff_paged(qn, k_new, v_new, cache_k, cache_v, page_table, rel_bias, lam4, g_out, lam_init):
    B, T, H, _, dh = qn.shape
    dv = v_new.shape[-1]
    n_pages = page_table.shape[1]
    page = cache_k.shape[1]
    P = n_pages * page
    R = 2 * H * T
    pp = next(c for c in (8, 4, 2, 1) if n_pages % c == 0)
    qt = jnp.transpose(qn, (0, 3, 2, 1, 4))
    zeros = jnp.zeros_like(qt[:, 0])
    qm = jnp.concatenate([jnp.concatenate([qt[:, 0], zeros], axis=-1),
                          jnp.concatenate([zeros, qt[:, 1]], axis=-1)], axis=1)
    qm = qm.reshape(B, R, 2 * dh)
    pg = np.arange(n_pages)[:, None, None]
    tt = np.arange(T)[None, :, None]
    ii = np.arange(page)[None, None, :]
    dist_np = P + tt - pg * page - ii
    bucket_np = _np_rel_bucket(dist_np)
    far = np.all(bucket_np == bucket_np[0:1], axis=(1, 2))
    near_pages = [int(x) for x in np.nonzero(~far)[0]]
    tile_pages = [0] + near_pages
    tile_of_page = np.zeros((n_pages,), np.int32)
    for ti, pgi in enumerate(near_pages):
        tile_of_page[pgi] = ti + 1
    nt = len(tile_pages)
    bias_t = _bias_of_dist(rel_bias, jnp.asarray(dist_np[tile_pages], jnp.int32))
    bias_rows = jnp.transpose(bias_t.reshape(2, H, nt, T, page), (2, 0, 1, 3, 4)).reshape(nt, R, page)
    row_head = (np.arange(R) // T) % H
    bm = _head_expand(bias_rows, row_head, H)
    dn = np.arange(T)[:, None] - np.arange(T)[None, :]
    bias_n, _ = _causal_bias(rel_bias, dn)
    bmn = _head_expand(bias_n.reshape(R, T), row_head, H)
    kn2 = k_new.reshape(B, T * H, 2 * dh)
    vn2 = v_new.reshape(B, T * H, dv)

    def kv_spec(j, d):
        return pl.BlockSpec((1, page, H, d), lambda b, p, pt, tl: (pt[b * n_pages + p * pp + j], 0, 0, 0))

    def bm_spec(j):
        return pl.BlockSpec((1, R, page * H), lambda b, p, pt, tl: (tl[p * pp + j], 0, 0))

    grid_spec = pltpu.PrefetchScalarGridSpec(
        num_scalar_prefetch=2,
        grid=(B, n_pages // pp),
        in_specs=(
            [pl.BlockSpec((1, R, 2 * dh), lambda b, p, pt, tl: (b, 0, 0))]
            + [kv_spec(j, 2 * dh) for j in range(pp)]
            + [kv_spec(j, dv) for j in range(pp)]
            + [bm_spec(j) for j in range(pp)]
            + [pl.BlockSpec((1, T * H, 2 * dh), lambda b, p, pt, tl: (b, 0, 0)),
               pl.BlockSpec((1, T * H, dv), lambda b, p, pt, tl: (b, 0, 0)),
               pl.BlockSpec((R, T * H), lambda b, p, pt, tl: (0, 0)),
               pl.BlockSpec(lam4.shape, lambda b, p, pt, tl: (0, 0)),
               pl.BlockSpec((1, dv), lambda b, p, pt, tl: (0, 0))]),
        out_specs=pl.BlockSpec((1, T, H * dv), lambda b, p, pt, tl: (b, 0, 0)),
        scratch_shapes=[pltpu.VMEM((R, 1), F32), pltpu.VMEM((R, 1), F32), pltpu.VMEM((R, dv), F32)],
    )
    return pl.pallas_call(
        functools.partial(_decode_kernel, n_scalar=2, n_chunks=pp, diff=True, lam_init=lam_init,
                          n_heads=H, n_new=T),
        grid_spec=grid_spec,
        out_shape=jax.ShapeDtypeStruct((B, T, H * dv), BF16),
        compiler_params=_cparams(("arbitrary", "arbitrary")),
        name="diff_paged",
    )(page_table.reshape(-1), jnp.asarray(tile_of_page), qm, *([cache_k] * pp), *([cache_v] * pp),
      *([bm] * pp), kn2, vn2, bmn, lam4, g_out.reshape(1, dv))


def _rglru_kernel(y_ref, x_ref, cs_ref, h0_ref, cw_ref, cb_ref, wa_ref, ba_ref, wx_ref, bx_ref, lam_ref,
                  out_ref, conv_ref, hl_ref, xpad, hcar, *, l_valid):
    i = pl.program_id(1)
    T, D = x_ref.shape[1], x_ref.shape[2]

    @pl.when(i == 0)
    def _():
        xpad[0:8, :] = cs_ref[0]
        hcar[...] = h0_ref[0]

    xpad[8:8 + T, :] = x_ref[0]
    conv = cb_ref[...] + cw_ref[CONV_W - 1:CONV_W, :] * xpad[8:8 + T, :]
    for j in range(CONV_W - 1):
        s = CONV_W - 1 - j
        conv = conv + cw_ref[j:j + 1, :] * xpad[8 - s:8 - s + T, :]
    nb = wa_ref.shape[0]
    bs = D // nb
    r_parts, i_parts = [], []
    for n in range(nb):
        xb = conv[:, n * bs:(n + 1) * bs]
        r_parts.append(_bdot(xb, wa_ref[n]))
        i_parts.append(_bdot(xb, wx_ref[n]))
    r = jax.nn.sigmoid(jnp.concatenate(r_parts, axis=-1) + ba_ref[...])
    ig = jax.nn.sigmoid(jnp.concatenate(i_parts, axis=-1) + bx_ref[...])
    log_a = -LRU_C * r * _softplus(-lam_ref[...])
    a = jnp.exp(log_a)
    bb = jnp.sqrt(-jnp.tanh(log_a) * (1.0 + a * a)) * ig * conv
    rowid = lax.broadcasted_iota(jnp.int32, (T, D), 0)
    s = 1
    while s < T:
        keep = rowid >= s
        a_sh = pltpu.roll(a, s, 0)
        b_sh = pltpu.roll(bb, s, 0)
        bb = jnp.where(keep, a * b_sh + bb, bb)
        a = jnp.where(keep, a * a_sh, a)
        s *= 2
    hs = a * hcar[...] + bb
    out_ref[0] = (hs * _gelu(y_ref[0])).astype(out_ref.dtype)
    hcar[...] = hs[T - 1:T, :]
    tail = xpad[T:T + 8, :]
    xpad[0:8, :] = tail

    @pl.when(i == pl.num_programs(1) - 1)
    def _():
        hl_ref[0] = hs[l_valid - 1:l_valid, :]
        conv_ref[0] = xpad[8 + l_valid - (CONV_W - 1):8 + l_valid, :]


def rglru_core(z, conv_state, h0, conv_w, conv_b, w_a, b_a, w_x, b_x, lam, l_valid_last, tblk):
    B, L, D2 = z.shape
    D = D2 // 2
    cs8 = jnp.concatenate([jnp.zeros((B, 8 - (CONV_W - 1), D), F32), conv_state.astype(F32)], axis=1)
    vec = lambda a: a.reshape(1, D)
    out, conv_new, h_last = pl.pallas_call(
        functools.partial(_rglru_kernel, l_valid=l_valid_last),
        grid=(B, L // tblk),
        in_specs=[
            pl.BlockSpec((1, tblk, D), lambda b, i: (b, i, 0)),
            pl.BlockSpec((1, tblk, D), lambda b, i: (b, i, 1)),
            pl.BlockSpec((1, 8, D), lambda b, i: (b, 0, 0)),
            pl.BlockSpec((1, 1, D), lambda b, i: (b, 0, 0)),
            pl.BlockSpec((CONV_W, D), lambda b, i: (0, 0)),
            pl.BlockSpec((1, D), lambda b, i: (0, 0)),
            pl.BlockSpec(w_a.shape, lambda b, i: (0, 0, 0)),
            pl.BlockSpec((1, D), lambda b, i: (0, 0)),
            pl.BlockSpec(w_x.shape, lambda b, i: (0, 0, 0)),
            pl.BlockSpec((1, D), lambda b, i: (0, 0)),
            pl.BlockSpec((1, D), lambda b, i: (0, 0)),
        ],
        out_specs=[
            pl.BlockSpec((1, tblk, D), lambda b, i: (b, i, 0)),
            pl.BlockSpec((1, CONV_W - 1, D), lambda b, i: (b, 0, 0)),
            pl.BlockSpec((1, 1, D), lambda b, i: (b, 0, 0)),
        ],
        out_shape=[
            jax.ShapeDtypeStruct((B, L, D), BF16),
            jax.ShapeDtypeStruct((B, CONV_W - 1, D), F32),
            jax.ShapeDtypeStruct((B, 1, D), F32),
        ],
        scratch_shapes=[pltpu.VMEM((tblk + 8, D), F32), pltpu.VMEM((1, D), F32)],
        compiler_params=_cparams(("arbitrary", "arbitrary")),
        name="rglru",
    )(z, z, cs8, h0.astype(F32).reshape(B, 1, D), conv_w, vec(conv_b), w_a.astype(BF16), vec(b_a),
      w_x.astype(BF16), vec(b_x), vec(lam))
    return out, conv_new, h_last.reshape(B, D)


def _argmax_rows(s):
    n, tb = s.shape
    row8 = lax.broadcasted_iota(jnp.int32, (8, tb), 0).astype(F32)
    vals = [s[i:i + 8] for i in range(0, n, 8)]
    rows = [row8 + float(i) for i in range(0, n, 8)]
    while len(vals) > 1:
        nv, nr = [], []
        for a in range(0, len(vals) - 1, 2):
            first = vals[a] >= vals[a + 1]
            nv.append(jnp.maximum(vals[a], vals[a + 1]))
            nr.append(jnp.where(first, rows[a], rows[a + 1]))
        if len(vals) % 2:
            nv.append(vals[-1])
            nr.append(rows[-1])
        vals, rows = nv, nr
    v, r = vals[0], rows[0]
    m = jnp.max(v, axis=0, keepdims=True)
    return m, jnp.min(jnp.where(v == m, r, float(n)), axis=0, keepdims=True)


def _topk_rows(s, k):
    n, tb = s.shape
    rowf = lax.broadcasted_iota(jnp.int32, (n, tb), 0).astype(F32)
    vals, idxs = [], []
    for _ in range(k):
        m, idx = _argmax_rows(s)
        vals.append(m)
        idxs.append(idx)
        s = jnp.where(rowf == idx, -jnp.inf, s)
    return jnp.concatenate(vals, axis=0), jnp.concatenate(idxs, axis=0)


def _pair_plan(k):
    full, small = [], []
    a = 0
    while a < k and k // (a + 1) > 1:
        nb, b0 = k // (a + 1), 0
        while nb - b0 >= 8:
            full.append((a, b0))
            b0 += 8
        if nb > b0:
            small.append((a, b0, nb - b0))
        a += 1
    a0 = a
    assert k % 8 == 0 and (k - a0) % 8 == 0 and (k & (k - 1)) == 0
    bins = []
    for a_, b0, n in sorted(small, key=lambda p: -p[2]):
        for bn in bins:
            used = sum(p[3] for p in bn)
            if used + n <= 8:
                bn.append((a_, b0, used, n))
                break
        else:
            bins.append([(a_, b0, 0, n)])
    code = []
    for a_, b0 in full:
        code += [a_ * k + b0 + r for r in range(8)]
    for bn in bins:
        rows = [k * k] * 8
        for a_, b0, off, n in bn:
            for r in range(n):
                rows[off + r] = a_ * k + b0 + r
        code += rows
    code += [a_ * k for a_ in range(a0, k)]
    return full, bins, a0, np.array(code, np.float32)


def _route_kernel(*refs, topk, cast_experts):
    if cast_experts:
        h_ref, wq_ref, k1_ref, k2_ref, code_ref, u_ref, v_ref, e1_ref, e2_ref, g_ref, ut_ref, vb_ref = refs
        ut_ref[...] = u_ref[0].T.astype(BF16)
        vb_ref[...] = v_ref[0].astype(BF16)
    else:
        h_ref, wq_ref, k1_ref, k2_ref, code_ref, e1_ref, e2_ref, g_ref = refs
    k = topk
    q = jnp.dot(h_ref[...], wq_ref[...], preferred_element_type=F32)
    dk2 = k1_ref.shape[2]
    tb = q.shape[0]
    nt = (((1,), (1,)), ((), ()))
    s1 = lax.dot_general(k1_ref[0], q[:, :dk2], nt, precision=HIGHEST, preferred_element_type=F32)
    s2 = lax.dot_general(k2_ref[0], q[:, dk2:], nt, precision=HIGHEST, preferred_element_type=F32)
    sv1, si1 = _topk_rows(s1, k)
    sv2, si2 = _topk_rows(s2, k)
    full, bins, a0, _ = _pair_plan(k)
    row8 = lax.broadcasted_iota(jnp.int32, (8, tb), 0)
    groups = [sv1[a:a + 1] + sv2[b0:b0 + 8] for a, b0 in full]
    for bn in bins:
        v = jnp.full((8, tb), -jnp.inf, F32)
        for a, b0, off, n in bn:
            piece = sv1[a:a + 1] + (pltpu.roll(sv2[b0:b0 + 8], off, 0) if off else sv2[b0:b0 + 8])
            v = jnp.where((row8 >= off) & (row8 < off + n), piece, v)
        groups.append(v)
    for c in range(a0, k, 8):
        groups.append(sv1[c:c + 8] + sv2[0:1])
    cand = jnp.concatenate(groups, axis=0)
    code = code_ref[...]
    vals, poss = [], []
    for _ in range(k):
        m = jnp.max(cand, axis=0, keepdims=True)
        pos = jnp.min(jnp.where(cand == m, code, float(k * k)), axis=0, keepdims=True)
        vals.append(m)
        poss.append(pos)
        cand = jnp.where(code == pos, -jnp.inf, cand)
    val = jnp.concatenate(vals, axis=0)
    pos = jnp.concatenate(poss, axis=0)
    ra = jnp.floor(pos * (1.0 / k))
    rb = pos - k * ra
    e1 = jnp.zeros((k, tb), F32)
    e2 = jnp.zeros((k, tb), F32)
    for r in range(k):
        e1 = jnp.where(ra == float(r), si1[r:r + 1], e1)
        e2 = jnp.where(rb == float(r), si2[r:r + 1], e2)
    e1_ref[0] = e1
    e2_ref[0] = e2
    ex = jnp.exp(val - val[0:1])
    g_ref[0] = ex / jnp.sum(ex, axis=0, keepdims=True)


def peer_route(h, wq, keys, experts=None):
    N, D = h.shape
    nh, _, nk, dk2 = keys.shape
    tb = _pick(N, (1024, 512, 256, 128))
    nblk = N // tb
    code = _pair_plan(TOPK_P)[3]
    code = jnp.asarray(np.broadcast_to(code[:, None], (code.shape[0], tb)))
    shp = jax.ShapeDtypeStruct((nh, TOPK_P, N), F32)
    o_spec = pl.BlockSpec((1, TOPK_P, tb), lambda hh, i: (hh, 0, i))
    in_specs = [pl.BlockSpec((tb, D), lambda hh, i: (i, 0)),
                pl.BlockSpec((D, 2 * dk2), lambda hh, i: (0, hh)),
                pl.BlockSpec((1, nk, dk2), lambda hh, i: (2 * hh, 0, 0)),
                pl.BlockSpec((1, nk, dk2), lambda hh, i: (2 * hh + 1, 0, 0)),
                pl.BlockSpec(code.shape, lambda hh, i: (0, 0))]
    args = [h, wq, keys.reshape(nh * 2, nk, dk2), keys.reshape(nh * 2, nk, dk2), code]
    out_specs, out_shape = [o_spec, o_spec, o_spec], [shp, shp, shp]
    if experts is not None:
        u_all, v_all, layer = experts
        E = u_all.shape[1]
        rb = E // (nh * nblk)
        assert rb * nh * nblk == E and rb % LANES == 0
        in_specs += [pl.BlockSpec((1, rb, D), lambda hh, i: (layer, hh * nblk + i, 0))] * 2
        args += [u_all, v_all]
        out_specs += [pl.BlockSpec((D, rb), lambda hh, i: (0, hh * nblk + i)),
                      pl.BlockSpec((rb, D), lambda hh, i: (hh * nblk + i, 0))]
        out_shape += [jax.ShapeDtypeStruct((D, E), BF16), jax.ShapeDtypeStruct((E, D), BF16)]
    return pl.pallas_call(
        functools.partial(_route_kernel, topk=TOPK_P, cast_experts=experts is not None),
        grid=(nh, nblk),
        in_specs=in_specs,
        out_specs=out_specs,
        out_shape=out_shape,
        compiler_params=_cparams(("arbitrary", "arbitrary")),
        name="peer_route",
    )(*args)


def _route_build_kernel(e1_ref, e2_ref, g_ref, sub_ref, o_ref, *, nkeys):
    sub = sub_ref[...][None]
    one = jnp.ones((), BF16)
    zero = jnp.zeros((), BF16)
    at = jnp.where(e1_ref[...].astype(BF16) == sub, one, zero)
    bt = jnp.where(e2_ref[...].astype(BF16) == sub, g_ref[...].astype(BF16), zero)
    g3 = jnp.einsum('tik,tjk->tij', at, bt, preferred_element_type=F32)
    gt = pltpu.einshape('tij->itj', g3)
    for i in range(nkeys):
        o_ref[:, i * nkeys:(i + 1) * nkeys] = gt[i].astype(o_ref.dtype)


def peer_build(e1, e2, g, nkeys):
    nh, k, N = e1.shape
    ks = nh * k
    slot = lambda a: jnp.transpose(a, (2, 0, 1)).reshape(N, 1, ks)
    tb = _pick(N, (64, 32))
    spec = pl.BlockSpec((tb, 1, ks), lambda i: (i, 0, 0))
    sub = jnp.asarray(np.broadcast_to(np.arange(nkeys, dtype=np.float32)[:, None], (nkeys, ks)), BF16)
    return pl.pallas_call(
        functools.partial(_route_build_kernel, nkeys=nkeys),
        grid=(N // tb,),
        in_specs=[spec, spec, spec, pl.BlockSpec((nkeys, ks), lambda i: (0, 0))],
        out_specs=pl.BlockSpec((tb, nkeys * nkeys), lambda i: (i, 0)),
        out_shape=jax.ShapeDtypeStruct((N, nkeys * nkeys), BF16),
        compiler_params=_cparams(("arbitrary",)),
        name="peer_build",
    )(slot(e1), slot(e2), slot(g), sub)


def _peer_kernel(*refs, with_next):
    if with_next:
        x_ref, ut_ref, v_ref, g_ref, r_ref, gt_ref, gn_ref, sh_ref, sc_ref, o_ref, h_ref, acc_s = refs
    else:
        x_ref, ut_ref, v_ref, g_ref, r_ref, gt_ref, o_ref, acc_s = refs
    e = pl.program_id(1)

    @pl.when(e == 0)
    def _():
        acc_s[...] = jnp.zeros(acc_s.shape, F32)

    s = jnp.dot(x_ref[...], ut_ref[0], preferred_element_type=F32)
    p = (_gelu(s) * g_ref[...].astype(F32)).astype(BF16)
    acc_s[...] += jnp.dot(p, v_ref[0], preferred_element_type=F32)

    @pl.when(e == pl.num_programs(1) - 1)
    def _():
        xn = r_ref[...] + gt_ref[0] * acc_s[...]
        o_ref[...] = xn
        if with_next:
            h_ref[...] = _modulated(xn, gn_ref, sh_ref, sc_ref).astype(h_ref.dtype)


def peer_dense(h, ut_all, v_all, layer, G, res, gate, next_mod=None):
    N, D = h.shape
    E = ut_all.shape[2]
    tb = _pick(N, (512, 256, 128))
    eb = _pick(E, (1024, 512, 256, 128))

    def param_spec(p):
        if p.shape[1] == 1:
            rows = N // p.shape[0]
            assert rows % tb == 0
            return pl.BlockSpec((1, 1, D), lambda i, e: (i // (rows // tb), 0, 0))
        return pl.BlockSpec((1, tb, D), lambda i, e: (0, i, 0))

    row = pl.BlockSpec((tb, D), lambda i, e: (i, 0))
    in_specs = [row,
                pl.BlockSpec((1, D, eb), lambda i, e: (layer, 0, e)),
                pl.BlockSpec((1, eb, D), lambda i, e: (layer, e, 0)),
                pl.BlockSpec((tb, eb), lambda i, e: (i, e)),
                row, param_spec(gate)]
    args = [h, ut_all, v_all, G, res, gate]
    out_specs, out_shape = [row], [jax.ShapeDtypeStruct((N, D), F32)]
    if next_mod is not None:
        g_norm, shift, scale = next_mod
        in_specs += [pl.BlockSpec((1, D), lambda i, e: (0, 0)), param_spec(shift), param_spec(scale)]
        args += [g_norm.reshape(1, D), shift, scale]
        out_specs.append(row)
        out_shape.append(jax.ShapeDtypeStruct((N, D), BF16))
    outs = pl.pallas_call(
        functools.partial(_peer_kernel, with_next=next_mod is not None),
        grid=(N // tb, E // eb),
        in_specs=in_specs,
        out_specs=out_specs,
        out_shape=out_shape,
        scratch_shapes=[pltpu.VMEM((tb, D), F32)],
        compiler_params=_cparams(("arbitrary", "arbitrary")),
        name="peer_dense",
    )(*args)
    return (outs[0], outs[1]) if next_mod is not None else (outs[0], None)


def _pad_rows(a, n, value=0.0, axis=1):
    pad = [(0, 0)] * a.ndim
    pad[axis] = (0, n - a.shape[axis])
    return jnp.pad(a, pad, constant_values=value)


def kernel(x_prompt, x_sample, state_mlstm_C, state_mlstm_n, state_mlstm_m, cache_swa_k, cache_swa_v,
           cache_diff_k, cache_diff_v, state_rglru_conv, state_rglru_h, page_table, c_prompt, c_sample,
           w_ada, b_ada, g_norm_mix, g_norm_ffn, rel_bias,
           mlstm_w_in, mlstm_b_gates, mlstm_g_out, mlstm_w_out,
           swa_w_in, swa_g_q, swa_g_k, swa_w_out,
           diff_w_in, diff_g_q, diff_g_k, diff_lam_q1, diff_lam_k1, diff_lam_q2, diff_lam_k2, diff_g_out, diff_w_out,
           rglru_w_in, rglru_conv_w, rglru_conv_b, rglru_w_a, rglru_b_a, rglru_w_x, rglru_b_x, rglru_lambda, rglru_w_out,
           peer_w_q, peer_keys, peer_u, peer_v):
    xp, xs = x_prompt, x_sample
    Bp, S, D = xp.shape
    Bs, T, _ = xs.shape
    depth = w_ada.shape[0]
    Ns = Bs * T

    c_all = jnp.concatenate([c_prompt, c_sample], axis=0)
    mod = ada_all(c_all, w_ada, b_ada)
    mod = mod.reshape(depth, Bp + Bs, 6, 1, D)

    def mods(i, which):
        m = mod[i, :, which]
        return m[:Bp], m[Bp:]

    def rows(p):
        return jnp.broadcast_to(p, (Bs, T, D)).reshape(1, Ns, D)

    nk = peer_keys.shape[3]
    Ns_pad = -(-Ns // LANES) * LANES

    sh_p, sh_s = mods(0, 0)
    sc_p, sc_s = mods(0, 1)
    hp = modulate(xp, g_norm_mix[0], sh_p, sc_p)
    hs = modulate(xs, g_norm_mix[0], sh_s, sc_s)
    for i in range(depth):
        kind = i % 4
        gt_p, gt_s = mods(i, 2)
        hs_flat = hs.reshape(1, Ns, D)
        if kind == 0:
            H = NH_A
            dk = state_mlstm_C.shape[2]
            dv = state_mlstm_C.shape[3]
            nmain = 2 * H * dk + 2 * H * dv
            wb = mlstm_w_in.astype(BF16)
            w_gate = mlstm_w_in[:, nmain:].astype(BF16)
            zp = matmul(hp, wb, cols=(0, nmain))
            gp = matmul(hp, w_gate)
            zs = matmul(hs_flat, wb, cols=(0, nmain)).reshape(Bs, T, nmain)
            gs = matmul(hs_flat, w_gate).reshape(Bs, T, 2 * H)
            chunk_p = _pick(S, (CHUNK_A, 128))
            op, mC_p, mn_p, mm_p = mlstm_core(
                zp, gp, mlstm_b_gates, jnp.zeros((Bp, H, dk, dv), F32), jnp.zeros((Bp, H, dk), F32),
                jnp.full((Bp, H), M_INIT, F32), mlstm_g_out, chunk_p)
            Tp = 128
            zs_pad = _pad_rows(zs, Tp)
            gs_pad = jnp.concatenate([
                _pad_rows(gs[..., :H], Tp, NEG), _pad_rows(gs[..., H:], Tp, -NEG)], axis=-1)
            os_, mC_s, mn_s, mm_s = mlstm_core(
                zs_pad, gs_pad, mlstm_b_gates, state_mlstm_C.astype(F32), state_mlstm_n.astype(F32),
                state_mlstm_m.astype(F32), mlstm_g_out, Tp)
            os_ = os_[:, :T]
            w_out = mlstm_w_out
        elif kind == 1:
            H = NH_B
            dh = D // H
            wb = swa_w_in.astype(BF16)
            gq = jnp.tile(swa_g_q, H)
            gk = jnp.tile(swa_g_k, H)
            qn_p = matmul_headnorm(hp, wb, (0, D), gq, dh, dh ** -0.5 * LOG2E, BF16)
            kn_p = matmul_headnorm(hp, wb, (D, D), gk, dh, 1.0, F32)
            vp = matmul(hp, wb, cols=(2 * D, D))
            qn_s = matmul_headnorm(hs_flat, wb, (0, D), gq, dh, dh ** -0.5, BF16).reshape(Bs, T, D)
            kn_s = matmul_headnorm(hs_flat, wb, (D, D), gk, dh, 1.0, F32).reshape(Bs, T, D)
            vs_new = matmul(hs_flat, wb, cols=(2 * D, D)).reshape(Bs, T, D)
            t = _pick(S, (512, 256, 128))
            assert (_np_multiplicity(_toeplitz_dist_np(S // t, t)) > 0).any(axis=2).all()
            bm_tab = _toeplitz_bias(rel_bias, S // t, t, _swa_bias, LOG2E)
            op = flash_attention(qn_p, kn_p, vp, bm_tab, n_units=H, dqk=dh, dv=dh,
                                 qcol=lambda a: a, kcol=lambda a: a, vcol=lambda a: a, bmap=lambda a: a,
                                 t=t, out_dtype=BF16)
            swa_k_p = kn_p.reshape(Bp, S, H, dh)
            swa_v_p = vp.reshape(Bp, S, H, dh)
            os_ = swa_decode(qn_s, kn_s, vs_new, cache_swa_k, cache_swa_v, rel_bias)
            def shifted(cache, new):
                Wb = cache.shape[1]
                buf = lax.pad(cache.astype(F32), jnp.zeros((), F32), ((0, 0, 0), (-T, T, 0), (0, 0, 0), (0, 0, 0)))
                return lax.dynamic_update_slice(buf, new.reshape(Bs, T, H, dh), (0, Wb - T, 0, 0))

            swa_k_s = shifted(cache_swa_k, kn_s)
            swa_v_s = shifted(cache_swa_v, vs_new)
            w_out = swa_w_out
        elif kind == 2:
            H = NH_C
            dh = D // (2 * H)
            dv = 2 * dh
            lam_init = 0.8 - 0.6 * math.exp(-0.3 * i)
            lam4 = jnp.stack([diff_lam_q1, diff_lam_k1, diff_lam_q2, diff_lam_k2]).astype(F32)
            wb = diff_w_in.astype(BF16)
            gq = jnp.tile(diff_g_q.reshape(-1), H)
            gk = jnp.tile(diff_g_k.reshape(-1), H)
            qn_p = matmul_headnorm(hp, wb, (0, D), gq, dh, dh ** -0.5 * LOG2E, BF16)
            kn_p = matmul_headnorm(hp, wb, (D, D), gk, dh, 1.0, F32)
            vp = matmul(hp, wb, cols=(2 * D, D))
            qn_s = matmul_headnorm(hs_flat, wb, (0, D), gq, dh, dh ** -0.5, BF16).reshape(Bs, T, D)
            kn_s = matmul_headnorm(hs_flat, wb, (D, D), gk, dh, 1.0, F32).reshape(Bs, T, D)
            vs_new = matmul(hs_flat, wb, cols=(2 * D, D)).reshape(Bs, T, D)
            t = _pick(S, (512, 256, 128))
            bm_tab = _toeplitz_bias(rel_bias, S // t, t, _causal_bias, LOG2E)
            o2 = flash_attention(qn_p, kn_p, vp, bm_tab, n_units=2 * H, dqk=dh, dv=dv,
                                 qcol=lambda a: a, kcol=lambda a: a, vcol=lambda a: a // 2,
                                 bmap=lambda a: (a % 2) * H + a // 2, t=t, out_dtype=F32)
            op = diff_combine(o2, lam4, diff_g_out, lam_init)
            diff_k_p = kn_p.reshape(Bp, S, H, 2 * dh)
            diff_v_p = vp.reshape(Bp, S, H, dv)
            diff_k_s = kn_s.reshape(Bs, T, H, 2 * dh)
            diff_v_s = vs_new.reshape(Bs, T, H, dv)
            os_ = diff_paged(qn_s.reshape(Bs, T, H, 2, dh), diff_k_s, diff_v_s, cache_diff_k, cache_diff_v,
                             page_table, rel_bias, lam4, diff_g_out, lam_init)
            w_out = diff_w_out
        else:
            wb = rglru_w_in.astype(BF16)
            zp = matmul(hp, wb)
            zs = matmul(hs_flat, wb).reshape(Bs, T, -1)
            tblk = _pick(S, (256, 128))
            op, conv_p, h_p = rglru_core(zp, jnp.zeros((Bp, CONV_W - 1, D), F32), jnp.zeros((Bp, D), F32),
                                         rglru_conv_w, rglru_conv_b, rglru_w_a, rglru_b_a, rglru_w_x, rglru_b_x,
                                         rglru_lambda, tblk, tblk)
            os_, conv_s, h_s = rglru_core(_pad_rows(zs, 8), state_rglru_conv, state_rglru_h,
                                          rglru_conv_w, rglru_conv_b, rglru_w_a, rglru_b_a, rglru_w_x, rglru_b_x,
                                          rglru_lambda, T, 8)
            os_ = os_[:, :T]
            w_out = rglru_w_out
        sh_p, sh_s = mods(i, 3)
        sc_p, sc_s = mods(i, 4)
        wb = w_out.astype(BF16)
        xp, hp = matmul_res_mod(op, wb, xp, gt_p, g_norm_ffn[i], sh_p, sc_p)
        xs, hs = matmul_res_mod(os_.reshape(1, Ns, -1), wb, xs.reshape(1, Ns, D), rows(gt_s), g_norm_ffn[i],
                                rows(sh_s), rows(sc_s))
        hp = hp.reshape(Bp * S, D)
        hs = hs.reshape(Ns, D)

        gt_p, gt_s = mods(i, 5)
        wq = peer_w_q[i].astype(BF16)
        e1, e2, gw, ut, vb = peer_route(hp, wq, peer_keys[i], experts=(peer_u, peer_v, i))
        Gp = peer_build(e1, e2, gw, nk)
        Gs = peer_build(*peer_route(_pad_rows(hs, Ns_pad, axis=0), wq, peer_keys[i]), nk)[:Ns]
        if i + 1 < depth:
            sh_p, sh_s = mods(i + 1, 0)
            sc_p, sc_s = mods(i + 1, 1)
            next_p = (g_norm_mix[i + 1], sh_p, sc_p)
            next_s = (g_norm_mix[i + 1], rows(sh_s), rows(sc_s))
        else:
            next_p = next_s = None
        xp, hp = peer_dense(hp, ut[None], vb[None], 0, Gp, xp.reshape(Bp * S, D), gt_p, next_p)
        xs, hs = peer_dense(hs, ut[None], vb[None], 0, Gs, xs.reshape(Ns, D), rows(gt_s), next_s)
        xp = xp.reshape(Bp, S, D)
        xs = xs.reshape(Bs, T, D)
        if hp is not None:
            hp = hp.reshape(Bp, S, D)
            hs = hs.reshape(Bs, T, D)

    return (xp, xs, mC_p, mC_s, mn_p, mn_s, mm_p, mm_s, swa_k_p, swa_k_s, swa_v_p, swa_v_s,
            diff_k_p, diff_k_s, diff_v_p, diff_v_s, conv_p, conv_s, h_p, h_s)
```

```python
import functools
import math

import numpy as np
import jax
import jax.numpy as jnp
from jax import lax
from jax.experimental import pallas as pl
from jax.experimental.pallas import tpu as pltpu

F32 = jnp.float32
BF16 = jnp.bfloat16
HIGHEST = lax.Precision.HIGHEST

EPS = 1e-6
NEG = -1e30
VMEM_LIMIT = 56 * 1024 * 1024
LANES = 128
LOG2E = math.log2(math.e)

NH_A, CHUNK_A, M_INIT = 8, 256, -1e30
NH_B = 16
DIL_PAIRS = ((128, 1), (512, 4), (2048, 16))
NH_C = 8
NBLK_D, CONV_W, LRU_C = 16, 4, 8.0
NH_P, N_KEYS, TOPK_P = 8, 128, 16
REL_BUCKETS, REL_MAX_DIST = 32, 2048
PAGE_SIZE = 128


def _cparams(sem):
    return pltpu.CompilerParams(dimension_semantics=sem, vmem_limit_bytes=VMEM_LIMIT)


def _bdot(a, b):
    return jnp.dot(a.astype(BF16), b.astype(BF16), preferred_element_type=F32)


def _bdot_nt(a, b):
    return lax.dot_general(a.astype(BF16), b.astype(BF16), (((1,), (1,)), ((), ())),
                           preferred_element_type=F32)


def _gelu(x):
    return 0.5 * x * (1.0 + lax.erf(x * (1.0 / math.sqrt(2.0))))


def _log_sigmoid(x):
    return jnp.minimum(x, 0.0) - jnp.log1p(jnp.exp(-jnp.abs(x)))


def _softplus(x):
    return jnp.maximum(x, 0.0) + jnp.log1p(jnp.exp(-jnp.abs(x)))


def _pick(n, pref):
    for t in pref:
        if n % t == 0:
            return t
    return n


def _ada_kernel(c_ref, w_ref, b_ref, o_ref):
    c = c_ref[...]
    a = c * jax.nn.sigmoid(c)
    o_ref[0] = jnp.dot(a, w_ref[0], preferred_element_type=F32, precision=HIGHEST) + b_ref[0]


def ada_all(c_all, w_ada, b_ada):
    R, D = c_all.shape
    nl, _, N = w_ada.shape
    tn = _pick(N, (2048, 1024, 512, 256, 128))
    return pl.pallas_call(
        _ada_kernel,
        grid=(nl, N // tn),
        in_specs=[pl.BlockSpec((R, D), lambda l, j: (0, 0)),
                  pl.BlockSpec((1, D, tn), lambda l, j: (l, 0, j)),
                  pl.BlockSpec((1, 1, tn), lambda l, j: (l, 0, j))],
        out_specs=pl.BlockSpec((1, R, tn), lambda l, j: (l, 0, j)),
        out_shape=jax.ShapeDtypeStruct((nl, R, N), F32),
        compiler_params=_cparams(("arbitrary", "arbitrary")),
        name="ada",
    )(c_all, w_ada, b_ada.reshape(nl, 1, N))


def _modulate_kernel(x_ref, g_ref, sh_ref, sc_ref, o_ref):
    x = x_ref[0]
    ms = jnp.mean(x * x, axis=-1, keepdims=True)
    y = x * lax.rsqrt(ms + EPS) * g_ref[...]
    o_ref[0] = (y * (1.0 + sc_ref[0]) + sh_ref[0]).astype(o_ref.dtype)


def modulate(x, g, shift, scale):
    B, L, D = x.shape
    tl = _pick(L, (512, 256, 128))
    return pl.pallas_call(
        _modulate_kernel,
        grid=(B, L // tl),
        in_specs=[pl.BlockSpec((1, tl, D), lambda b, i: (b, i, 0)),
                  pl.BlockSpec((1, D), lambda b, i: (0, 0)),
                  pl.BlockSpec((1, 1, D), lambda b, i: (b, 0, 0)),
                  pl.BlockSpec((1, 1, D), lambda b, i: (b, 0, 0))],
        out_specs=pl.BlockSpec((1, tl, D), lambda b, i: (b, i, 0)),
        out_shape=jax.ShapeDtypeStruct((B, L, D), BF16),
        compiler_params=_cparams(("arbitrary", "arbitrary")),
        name="modulate",
    )(x, g.reshape(1, D), shift, scale)


def _mm_kernel(x_ref, w_ref, o_ref):
    o_ref[0] = jnp.dot(x_ref[0], w_ref[...], preferred_element_type=F32).astype(o_ref.dtype)


def matmul(x, w, out_dtype=F32, cols=None):
    B, L, K = x.shape
    c0, N = cols if cols is not None else (0, w.shape[1])
    tm = _pick(L, (1024, 512, 256, 128))
    tn = next((c for c in (1024, 512, 256, 128) if N % c == 0 and c0 % c == 0), N)
    assert c0 % tn == 0 and (tn == w.shape[1] or tn % LANES == 0)
    j0 = c0 // tn
    grid = (N // tn, B, L // tm)
    x_spec = pl.BlockSpec((1, tm, K), lambda j, b, i: (b, i, 0))
    w_spec = pl.BlockSpec((K, tn), lambda j, b, i: (0, j0 + j))
    o_spec = pl.BlockSpec((1, tm, tn), lambda j, b, i: (b, i, j))
    return pl.pallas_call(
        _mm_kernel, grid=grid, in_specs=[x_spec, w_spec], out_specs=o_spec,
        out_shape=jax.ShapeDtypeStruct((B, L, N), out_dtype),
        compiler_params=_cparams(("arbitrary",) * 3), name="matmul",
    )(x, w)


def _modulated(xn, gn_ref, sh_ref, sc_ref):
    ms = jnp.mean(xn * xn, axis=-1, keepdims=True)
    return xn * lax.rsqrt(ms + EPS) * gn_ref[...] * (1.0 + sc_ref[0]) + sh_ref[0]


def _mm_res_mod_kernel(x_ref, w_ref, r_ref, gt_ref, gn_ref, sh_ref, sc_ref, o_ref, h_ref):
    acc = jnp.dot(x_ref[0], w_ref[...], preferred_element_type=F32)
    xn = r_ref[0] + gt_ref[0] * acc
    o_ref[0] = xn
    h_ref[0] = _modulated(xn, gn_ref, sh_ref, sc_ref).astype(h_ref.dtype)


def _row_param_spec(p, tm, n):
    if p.shape[1] == 1:
        return pl.BlockSpec((1, 1, n), lambda b, i: (b, 0, 0))
    return pl.BlockSpec((1, tm, n), lambda b, i: (b, i, 0))


def matmul_res_mod(x, w, res, gate, g_norm, shift, scale):
    B, L, K = x.shape
    N = w.shape[1]
    tm = _pick(L, (512, 256, 128))
    row = pl.BlockSpec((1, tm, N), lambda b, i: (b, i, 0))
    return pl.pallas_call(
        _mm_res_mod_kernel,
        grid=(B, L // tm),
        in_specs=[pl.BlockSpec((1, tm, K), lambda b, i: (b, i, 0)),
                  pl.BlockSpec((K, N), lambda b, i: (0, 0)),
                  row, _row_param_spec(gate, tm, N),
                  pl.BlockSpec((1, N), lambda b, i: (0, 0)),
                  _row_param_spec(shift, tm, N), _row_param_spec(scale, tm, N)],
        out_specs=[row, row],
        out_shape=[jax.ShapeDtypeStruct((B, L, N), F32), jax.ShapeDtypeStruct((B, L, N), BF16)],
        compiler_params=_cparams(("arbitrary", "arbitrary")),
        name="matmul_res_mod",
    )(x, w, res, gate, g_norm.reshape(1, N), shift, scale)


def _mm_headnorm_kernel(x_ref, w_ref, g_ref, o_ref, *, dh, scale):
    acc = jnp.dot(x_ref[0], w_ref[...], preferred_element_type=F32)
    for h in range(acc.shape[1] // dh):
        sl = slice(h * dh, (h + 1) * dh)
        a = acc[:, sl]
        ms = jnp.mean(a * a, axis=-1, keepdims=True)
        y = a * lax.rsqrt(ms + EPS) * g_ref[:, sl]
        o_ref[0, :, sl] = (y * scale if scale != 1.0 else y).astype(o_ref.dtype)


def matmul_headnorm(x, w, cols, gain_full, dh, scale, out_dtype):
    B, L, K = x.shape
    c0, N = cols
    tm = _pick(L, (1024, 512, 256, 128))
    tn = next(c for c in (1024, 512, 256, 128) if N % c == 0 and c0 % c == 0)
    j0 = c0 // tn
    return pl.pallas_call(
        functools.partial(_mm_headnorm_kernel, dh=dh, scale=scale),
        grid=(N // tn, B, L // tm),
        in_specs=[pl.BlockSpec((1, tm, K), lambda j, b, i: (b, i, 0)),
                  pl.BlockSpec((K, tn), lambda j, b, i: (0, j0 + j)),
                  pl.BlockSpec((1, tn), lambda j, b, i: (0, j))],
        out_specs=pl.BlockSpec((1, tm, tn), lambda j, b, i: (b, i, j)),
        out_shape=jax.ShapeDtypeStruct((B, L, N), out_dtype),
        compiler_params=_cparams(("arbitrary",) * 3),
        name="matmul_headnorm",
    )(x, w, gain_full.reshape(1, N))


def _mlstm_kernel(q_ref, k_ref, v_ref, o_ref, gc_ref, gr_ref, bc_ref, br_ref,
                  C0_ref, n0_ref, m0_ref, gout_ref,
                  hs_ref, C_ref, n_ref, m_ref, C_s, n_s, m_s, *, dk, dv):
    ci = pl.program_id(2)
    hp = C_s.shape[0]

    @pl.when(ci == 0)
    def _():
        C_s[...] = C0_ref[0]
        n_s[...] = n0_ref[0]
        m_s[...] = m0_ref[0]

    c = q_ref.shape[1]
    row = lax.broadcasted_iota(jnp.int32, (c, c), 0)
    col = lax.broadcasted_iota(jnp.int32, (c, c), 1)
    causal = col <= row
    for hh in range(hp):
        q = q_ref[0, :, hh * dk:(hh + 1) * dk]
        k = k_ref[0, :, hh * dk:(hh + 1) * dk] * (dk ** -0.5)
        v = v_ref[0, :, hh * dv:(hh + 1) * dv]
        gc = gc_ref[0, hh] + bc_ref[hh]
        gr = gr_ref[0, hh] + br_ref[hh]
        li_c, lf_c = gc[:, 0:1], _log_sigmoid(gc[:, 1:2])
        li_r, lf_r = gr[0:1, :], _log_sigmoid(gr[1:2, :])
        b_c = jnp.sum(jnp.where(causal, lf_r, 0.0), axis=1, keepdims=True)
        b_r = jnp.sum(jnp.where(row <= col, lf_c, 0.0), axis=0, keepdims=True)
        m_prev = m_s[hh]
        Dm = jnp.where(causal, b_c - b_r + li_r, NEG)
        m_t = jnp.maximum(b_c + m_prev, jnp.max(Dm, axis=1, keepdims=True))
        S = _bdot_nt(q, k) * jnp.exp(Dm - m_t)
        inter = jnp.exp(b_c + m_prev - m_t)
        C = C_s[hh]
        n = n_s[hh]
        num = _bdot(S, v) + inter * _bdot(q, C)
        den = jnp.sum(S, axis=1, keepdims=True) + inter * jnp.sum(q * n, axis=1, keepdims=True)
        h = num / jnp.maximum(jnp.abs(den), jnp.exp(-m_t))
        sl = slice(hh * dv, (hh + 1) * dv)
        hg = jax.nn.sigmoid(o_ref[0, :, sl]) * h
        ms = jnp.mean(hg * hg, axis=-1, keepdims=True)
        hs_ref[0, :, sl] = (hg * lax.rsqrt(ms + EPS) * gout_ref[:, sl]).astype(hs_ref.dtype)
        m_new = m_t[c - 1:c, :]
        b_last = b_c[c - 1:c, :]
        w_r = jnp.exp(b_last - b_r + li_r - m_new)
        w_c = jnp.exp(b_last - b_c + li_c - m_new)
        decay = jnp.exp(b_last + m_prev - m_new)
        C_s[hh] = decay * C + _bdot(k.T, w_c * v)
        n_s[hh] = decay * n + jnp.dot(w_r, k, preferred_element_type=F32, precision=HIGHEST)
        m_s[hh] = m_new

    @pl.when(ci == pl.num_programs(2) - 1)
    def _():
        C_ref[0] = C_s[...]
        n_ref[0] = n_s[...]
        m_ref[0] = m_s[...]


def mlstm_core(z, g, b_gates, C0, n0, m0, g_out, chunk):
    B, L, _ = z.shape
    H = NH_A
    dk = C0.shape[2]
    dv = C0.shape[3]
    nc = L // chunk
    g4 = g.reshape(B, L, 2, H)
    gcol = jnp.transpose(g4, (0, 3, 1, 2))
    grow = jnp.transpose(g4, (0, 3, 2, 1))
    bg = b_gates.reshape(2, H)
    bcol = jnp.transpose(bg, (1, 0)).reshape(H, 1, 2)
    brow = jnp.transpose(bg, (1, 0)).reshape(H, 2, 1)
    hp = 2 if H % 2 == 0 else 1
    wk, wv = hp * dk, hp * dv
    kq = (H * dk) // wk
    vo = (2 * H * dk) // wv
    oo = vo + H // hp
    outs = pl.pallas_call(
        functools.partial(_mlstm_kernel, dk=dk, dv=dv),
        grid=(B, H // hp, nc),
        in_specs=[
            pl.BlockSpec((1, chunk, wk), lambda b, h, c: (b, c, h)),
            pl.BlockSpec((1, chunk, wk), lambda b, h, c: (b, c, kq + h)),
            pl.BlockSpec((1, chunk, wv), lambda b, h, c: (b, c, vo + h)),
            pl.BlockSpec((1, chunk, wv), lambda b, h, c: (b, c, oo + h)),
            pl.BlockSpec((1, hp, chunk, 2), lambda b, h, c: (b, h, c, 0)),
            pl.BlockSpec((1, hp, 2, chunk), lambda b, h, c: (b, h, 0, c)),
            pl.BlockSpec((hp, 1, 2), lambda b, h, c: (h, 0, 0)),
            pl.BlockSpec((hp, 2, 1), lambda b, h, c: (h, 0, 0)),
            pl.BlockSpec((1, hp, dk, dv), lambda b, h, c: (b, h, 0, 0)),
            pl.BlockSpec((1, hp, 1, dk), lambda b, h, c: (b, h, 0, 0)),
            pl.BlockSpec((1, hp, 1, 1), lambda b, h, c: (b, h, 0, 0)),
            pl.BlockSpec((1, wv), lambda b, h, c: (0, h)),
        ],
        out_specs=[
            pl.BlockSpec((1, chunk, wv), lambda b, h, c: (b, c, h)),
            pl.BlockSpec((1, hp, dk, dv), lambda b, h, c: (b, h, 0, 0)),
            pl.BlockSpec((1, hp, 1, dk), lambda b, h, c: (b, h, 0, 0)),
            pl.BlockSpec((1, hp, 1, 1), lambda b, h, c: (b, h, 0, 0)),
        ],
        out_shape=[
            jax.ShapeDtypeStruct((B, L, H * dv), BF16),
            jax.ShapeDtypeStruct((B, H, dk, dv), F32),
            jax.ShapeDtypeStruct((B, H, 1, dk), F32),
            jax.ShapeDtypeStruct((B, H, 1, 1), F32),
        ],
        scratch_shapes=[pltpu.VMEM((hp, dk, dv), F32), pltpu.VMEM((hp, 1, dk), F32), pltpu.VMEM((hp, 1, 1), F32)],
        compiler_params=_cparams(("arbitrary",) * 3),
        name="mlstm",
    )(z, z, z, z, gcol, grow, bcol, brow, C0, n0.reshape(B, H, 1, dk), m0.reshape(B, H, 1, 1),
      g_out.reshape(1, H * dv))
    hs, C, n, m = outs
    return hs, C, n.reshape(B, H, dk), m.reshape(B, H)


def _rel_bucket(dist):
    exact = REL_BUCKETS // 2
    d_f = jnp.maximum(dist, 1).astype(F32)
    large = exact + (jnp.log(d_f / exact) / math.log(REL_MAX_DIST / exact) * (REL_BUCKETS - exact)).astype(jnp.int32)
    large = jnp.minimum(large, REL_BUCKETS - 1)
    return jnp.where(dist < exact, dist, large)


def _bias_of_dist(rel_bias, dist):
    bucket = _rel_bucket(jnp.maximum(dist, 0))
    out = jnp.zeros((rel_bias.shape[1],) + dist.shape, F32)
    for b in range(REL_BUCKETS):
        out = jnp.where(bucket[None] == b, rel_bias[b].reshape((-1,) + (1,) * dist.ndim), out)
    return out


def _np_multiplicity(dist):
    cnt = np.zeros(dist.shape, np.int64)
    for w, d in DIL_PAIRS:
        cnt = cnt + ((dist % d == 0) & (dist <= w) & (dist >= 0))
    return cnt


def _swa_bias(rel_bias, dist_np):
    cnt = _np_multiplicity(dist_np)
    logc = jnp.log(jnp.asarray(np.maximum(cnt, 1), F32))
    bias = _bias_of_dist(rel_bias, jnp.asarray(np.maximum(dist_np, 0), jnp.int32))
    return jnp.where(jnp.asarray(cnt > 0)[None], bias + logc[None], NEG), cnt > 0


def _causal_bias(rel_bias, dist_np):
    bias = _bias_of_dist(rel_bias, jnp.asarray(np.maximum(dist_np, 0), jnp.int32))
    return jnp.where(jnp.asarray(dist_np >= 0)[None], bias, NEG), dist_np >= 0


def _toeplitz_dist_np(n_off, t):
    o = np.arange(n_off)[:, None, None]
    i = np.arange(t)[None, :, None]
    j = np.arange(t)[None, None, :]
    return o * t + i - j


def _toeplitz_kernel(b_ref, o_ref):
    t = o_ref.shape[2]
    rows = jnp.broadcast_to(b_ref[0, 0], (t, 2 * t))
    skew = pltpu.roll(rows, 0, 1, stride=1, stride_axis=0)
    o_ref[0, 0] = skew[:, t:]


def _toeplitz_bias(rel_bias, n_off, t, bias_fn, mult):
    dist = np.arange(-(t - 1), n_off * t)
    f, _ = bias_fn(rel_bias, dist)
    f = f * mult
    maps = f.shape[0]
    want = np.arange(n_off)[:, None] * t + t - np.arange(2 * t)[None, :]
    idx = np.minimum(want + (t - 1), len(dist) - 1)
    b = f[:, idx].reshape(maps, n_off, 1, 2 * t)
    return pl.pallas_call(
        _toeplitz_kernel,
        grid=(maps, n_off),
        in_specs=[pl.BlockSpec((1, 1, 1, 2 * t), lambda m, o: (m, o, 0, 0))],
        out_specs=pl.BlockSpec((1, 1, t, t), lambda m, o: (m, o, 0, 0)),
        out_shape=jax.ShapeDtypeStruct((maps, n_off, t, t), F32),
        compiler_params=_cparams(("arbitrary", "arbitrary")),
        name="toeplitz_bias",
    )(b)


def _flash_kernel(qi_ref, ki_ref, q_ref, k_ref, v_ref, bm_ref, o_ref, m_s, l_s, acc_s):
    p = pl.program_id(1)
    qi = qi_ref[p]
    ki = ki_ref[p]
    nb, tq, dv = acc_s.shape
    tk = k_ref.shape[1]

    @pl.when(ki == 0)
    def _():
        m_s[...] = jnp.full(m_s.shape, NEG, F32)
        l_s[...] = jnp.zeros(l_s.shape, F32)
        acc_s[...] = jnp.zeros(acc_s.shape, F32)

    bm = bm_ref[0, 0]
    for b in range(nb):
        s = _bdot_nt(q_ref[b], k_ref[b]) + bm
        m_prev = m_s[b]
        m_next = jnp.maximum(m_prev, jnp.max(s, axis=1, keepdims=True))
        pr = jnp.exp2(s - jnp.tile(m_next, (1, tk // LANES)))
        alpha = jnp.exp2(m_prev - m_next)
        l_s[b] = alpha * l_s[b] + jnp.sum(pr, axis=1, keepdims=True)
        acc_s[b] = acc_s[b] * jnp.tile(alpha, (1, dv // LANES)) + _bdot(pr, v_ref[b])
        m_s[b] = m_next

    @pl.when(ki == qi)
    def _():
        for b in range(nb):
            o_ref[b] = (acc_s[b] / jnp.tile(l_s[b], (1, dv // LANES))).astype(o_ref.dtype)


def flash_attention(q, k, v, bm_tab, *, n_units, dqk, dv, qcol, kcol, vcol, bmap, t, out_dtype):
    B, L = q.shape[:2]
    nq = L // t
    pairs = [(qi, ki) for qi in range(nq) for ki in range(qi + 1)]
    qi_arr = jnp.asarray(np.array([p[0] for p in pairs], np.int32))
    ki_arr = jnp.asarray(np.array([p[1] for p in pairs], np.int32))
    grid_spec = pltpu.PrefetchScalarGridSpec(
        num_scalar_prefetch=2,
        grid=(n_units, len(pairs)),
        in_specs=[
            pl.BlockSpec((B, t, dqk), lambda a, p, qa, ka: (0, qa[p], qcol(a))),
            pl.BlockSpec((B, t, dqk), lambda a, p, qa, ka: (0, ka[p], kcol(a))),
            pl.BlockSpec((B, t, dv), lambda a, p, qa, ka: (0, ka[p], vcol(a))),
            pl.BlockSpec((1, 1, t, t), lambda a, p, qa, ka: (bmap(a), qa[p] - ka[p], 0, 0)),
        ],
        out_specs=pl.BlockSpec((B, t, dv), lambda a, p, qa, ka: (0, qa[p], a)),
        scratch_shapes=[pltpu.VMEM((B, t, LANES), F32), pltpu.VMEM((B, t, LANES), F32),
                        pltpu.VMEM((B, t, dv), F32)],
    )
    return pl.pallas_call(
        _flash_kernel,
        grid_spec=grid_spec,
        out_shape=jax.ShapeDtypeStruct((B, L, n_units * dv), out_dtype),
        compiler_params=_cparams(("arbitrary", "arbitrary")),
        name="flash_attention",
    )(qi_arr, ki_arr, q, k, v, bm_tab)


def _diff_lambda(lam_ref, lam_init):
    lq1, lk1, lq2, lk2 = lam_ref[0:1, :], lam_ref[1:2, :], lam_ref[2:3, :], lam_ref[3:4, :]
    return (jnp.exp(jnp.sum(lq1 * lk1, axis=-1, keepdims=True))
            - jnp.exp(jnp.sum(lq2 * lk2, axis=-1, keepdims=True)) + lam_init)


def _diff_combine_kernel(o_ref, lam_ref, g_ref, out_ref, *, dv, lam_init):
    lam = _diff_lambda(lam_ref, lam_init)
    nh = out_ref.shape[2] // dv
    for h in range(nh):
        o0 = o_ref[0, :, (2 * h) * dv:(2 * h + 1) * dv]
        o1 = o_ref[0, :, (2 * h + 1) * dv:(2 * h + 2) * dv]
        d = o0 - lam * o1
        ms = jnp.mean(d * d, axis=-1, keepdims=True)
        out_ref[0, :, h * dv:(h + 1) * dv] = (d * lax.rsqrt(ms + EPS) * g_ref[...] * (1.0 - lam_init)
                                              ).astype(out_ref.dtype)


def diff_combine(o, lam4, g_out, lam_init):
    B, L, W2 = o.shape
    dv = g_out.shape[0]
    W = W2 // 2
    tl = _pick(L, (256, 128))
    return pl.pallas_call(
        functools.partial(_diff_combine_kernel, dv=dv, lam_init=lam_init),
        grid=(B, L // tl),
        in_specs=[pl.BlockSpec((1, tl, W2), lambda b, i: (b, i, 0)),
                  pl.BlockSpec(lam4.shape, lambda b, i: (0, 0)),
                  pl.BlockSpec((1, dv), lambda b, i: (0, 0))],
        out_specs=pl.BlockSpec((1, tl, W), lambda b, i: (b, i, 0)),
        out_shape=jax.ShapeDtypeStruct((B, L, W), BF16),
        compiler_params=_cparams(("arbitrary", "arbitrary")),
        name="diff_combine",
    )(o, lam4, g_out.reshape(1, dv))


def _decode_kernel(*refs, n_scalar, n_chunks, diff, lam_init, n_heads, n_new):
    refs = refs[n_scalar:]
    q_ref = refs[0]
    k_refs = refs[1:1 + n_chunks]
    v_refs = refs[1 + n_chunks:1 + 2 * n_chunks]
    bm_refs = refs[1 + 2 * n_chunks:1 + 3 * n_chunks]
    rest = refs[1 + 3 * n_chunks:]
    if diff:
        kn_ref, vn_ref, bmn_ref, lam_ref, g_ref, out_ref, m_s, l_s, acc_s = rest
    else:
        kn_ref, vn_ref, bmn_ref, out_ref, m_s, l_s, acc_s = rest
    p = pl.program_id(1)

    @pl.when(p == 0)
    def _():
        m_s[...] = jnp.full(m_s.shape, NEG, F32)
        l_s[...] = jnp.zeros(l_s.shape, F32)
        acc_s[...] = jnp.zeros(acc_s.shape, F32)

    q = q_ref[0]

    def absorb(k2s, v2s, bms):
        ss = [_bdot_nt(q, k2) + bm for k2, bm in zip(k2s, bms)]
        m_old = m_s[...]
        m_new = m_old
        for s in ss:
            m_new = jnp.maximum(m_new, jnp.max(s, axis=-1, keepdims=True))
        alpha = jnp.exp(m_old - m_new)
        l_new = alpha * l_s[...]
        acc = alpha * acc_s[...]
        for s, v2 in zip(ss, v2s):
            pr = jnp.exp(s - m_new)
            l_new = l_new + jnp.sum(pr, axis=-1, keepdims=True)
            acc = acc + _bdot(pr, v2)
        l_s[...] = l_new
        acc_s[...] = acc
        m_s[...] = m_new

    def rows2d(ref):
        x = ref[0]
        return x.reshape(x.shape[0] * x.shape[1], x.shape[2])

    absorb([rows2d(r) for r in k_refs], [rows2d(r) for r in v_refs], [r[0] for r in bm_refs])

    @pl.when(p == pl.num_programs(1) - 1)
    def _():
        absorb([kn_ref[0]], [vn_ref[0]], [bmn_ref[...]])
        o = acc_s[...] / l_s[...]
        if diff:
            half = n_heads * n_new
            lam = _diff_lambda(lam_ref, lam_init)
            d = o[0:half] - lam * o[half:2 * half]
            ms = jnp.mean(d * d, axis=-1, keepdims=True)
            o = d * lax.rsqrt(ms + EPS) * g_ref[...] * (1.0 - lam_init)
        dv = o.shape[1]
        for h in range(n_heads):
            out_ref[0, :, h * dv:(h + 1) * dv] = o[h * n_new:(h + 1) * n_new].astype(out_ref.dtype)


def _head_expand(bias_rows, row_head, n_heads):
    ok = jnp.asarray(row_head[:, None] == np.arange(n_heads)[None, :])
    out = jnp.where(ok[:, None, :], bias_rows[..., None], NEG)
    return out.reshape(bias_rows.shape[:-1] + (bias_rows.shape[-1] * n_heads,))


def swa_decode(qn, k_new, v_new, cache_k, cache_v, rel_bias):
    B, Wb, H, dh = cache_k.shape
    T = qn.shape[1]
    R = H * T
    tk = _pick(Wb, (512, 256, 128))
    nblk = Wb // tk
    q2 = jnp.transpose(qn.reshape(B, T, H, dh), (0, 2, 1, 3)).reshape(B, R, dh)
    row_head = np.arange(R) // T
    dist_np = (Wb + np.arange(T))[:, None] - np.arange(Wb)[None, :]
    bias, ok = _swa_bias(rel_bias, dist_np)
    assert ok[:, :tk].any(axis=1).all()
    bm = _head_expand(bias.reshape(R, Wb), row_head, H)
    bm = jnp.transpose(bm.reshape(R, nblk, tk * H), (1, 0, 2))
    dn = np.arange(T)[:, None] - np.arange(T)[None, :]
    bias_n, _ = _swa_bias(rel_bias, dn)
    bmn = _head_expand(bias_n.reshape(R, T), row_head, H)
    kn2 = k_new.reshape(B, T * H, dh)
    vn2 = v_new.reshape(B, T * H, dh)
    return pl.pallas_call(
        functools.partial(_decode_kernel, n_scalar=0, n_chunks=1, diff=False, lam_init=0.0, n_heads=H, n_new=T),
        grid=(B, nblk),
        in_specs=[
            pl.BlockSpec((1, R, dh), lambda b, p: (b, 0, 0)),
            pl.BlockSpec((1, tk, H, dh), lambda b, p: (b, p, 0, 0)),
            pl.BlockSpec((1, tk, H, dh), lambda b, p: (b, p, 0, 0)),
            pl.BlockSpec((1, R, tk * H), lambda b, p: (p, 0, 0)),
            pl.BlockSpec((1, T * H, dh), lambda b, p: (b, 0, 0)),
            pl.BlockSpec((1, T * H, dh), lambda b, p: (b, 0, 0)),
            pl.BlockSpec((R, T * H), lambda b, p: (0, 0)),
        ],
        out_specs=pl.BlockSpec((1, T, H * dh), lambda b, p: (b, 0, 0)),
        out_shape=jax.ShapeDtypeStruct((B, T, H * dh), BF16),
        scratch_shapes=[pltpu.VMEM((R, 1), F32), pltpu.VMEM((R, 1), F32), pltpu.VMEM((R, dh), F32)],
        compiler_params=_cparams(("arbitrary", "arbitrary")),
        name="swa_decode",
    )(q2, cache_k, cache_v, bm, kn2, vn2, bmn)


def _np_rel_bucket(dist):
    exact = REL_BUCKETS // 2
    d_f = np.maximum(dist, 1).astype(np.float32)
    large = exact + (np.log(d_f / np.float32(exact)) / np.float32(math.log(REL_MAX_DIST / exact))
                     * np.float32(REL_BUCKETS - exact)).astype(np.int32)
    large = np.minimum(large, REL_BUCKETS - 1)
    return np.where(dist < exact, dist, large)


def diff_paged(qn, k_new, v_new, cache_k, cache_v, page_table, rel_bias, lam4, g_out, lam_init):
    B, T, H, _, dh = qn.shape
    dv = v_new.shape[-1]
    n_pages = page_table.shape[1]
    page = cache_k.shape[1]
    P = n_pages * page
    R = 2 * H * T
    pp = next(c for c in (8, 4, 2, 1) if n_pages % c == 0)
    qt = jnp.transpose(qn, (0, 3, 2, 1, 4))
    zeros = jnp.zeros_like(qt[:, 0])
    qm = jnp.concatenate([jnp.concatenate([qt[:, 0], zeros], axis=-1),
                          jnp.concatenate([zeros, qt[:, 1]], axis=-1)], axis=1)
    qm = qm.reshape(B, R, 2 * dh)
    pg = np.arange(n_pages)[:, None, None]
    tt = np.arange(T)[None, :, None]
    ii = np.arange(page)[None, None, :]
    dist_np = P + tt - pg * page - ii
    bucket_np = _np_rel_bucket(dist_np)
    far = np.all(bucket_np == bucket_np[0:1], axis=(1, 2))
    near_pages = [int(x) for x in np.nonzero(~far)[0]]
    tile_pages = [0] + near_pages
    tile_of_page = np.zeros((n_pages,), np.int32)
    for ti, pgi in enumerate(near_pages):
        tile_of_page[pgi] = ti + 1
    nt = len(tile_pages)
    bias_t = _bias_of_dist(rel_bias, jnp.asarray(dist_np[tile_pages], jnp.int32))
    bias_rows = jnp.transpose(bias_t.reshape(2, H, nt, T, page), (2, 0, 1, 3, 4)).reshape(nt, R, page)
    row_head = (np.arange(R) // T) % H
    bm = _head_expand(bias_rows, row_head, H)
    dn = np.arange(T)[:, None] - np.arange(T)[None, :]
    bias_n, _ = _causal_bias(rel_bias, dn)
    bmn = _head_expand(bias_n.reshape(R, T), row_head, H)
    kn2 = k_new.reshape(B, T * H, 2 * dh)
    vn2 = v_new.reshape(B, T * H, dv)

    def kv_spec(j, d):
        return pl.BlockSpec((1, page, H, d), lambda b, p, pt, tl: (pt[b * n_pages + p * pp + j], 0, 0, 0))

    def bm_spec(j):
        return pl.BlockSpec((1, R, page * H), lambda b, p, pt, tl: (tl[p * pp + j], 0, 0))

    grid_spec = pltpu.PrefetchScalarGridSpec(
        num_scalar_prefetch=2,
        grid=(B, n_pages // pp),
        in_specs=(
            [pl.BlockSpec((1, R, 2 * dh), lambda b, p, pt, tl: (b, 0, 0))]
            + [kv_spec(j, 2 * dh) for j in range(pp)]
            + [kv_spec(j, dv) for j in range(pp)]
            + [bm_spec(j) for j in range(pp)]
            + [pl.BlockSpec((1, T * H, 2 * dh), lambda b, p, pt, tl: (b, 0, 0)),
               pl.BlockSpec((1, T * H, dv), lambda b, p, pt, tl: (b, 0, 0)),
               pl.BlockSpec((R, T * H), lambda b, p, pt, tl: (0, 0)),
               pl.BlockSpec(lam4.shape, lambda b, p, pt, tl: (0, 0)),
               pl.BlockSpec((1, dv), lambda b, p, pt, tl: (0, 0))]),
        out_specs=pl.BlockSpec((1, T, H * dv), lambda b, p, pt, tl: (b, 0, 0)),
        scratch_shapes=[pltpu.VMEM((R, 1), F32), pltpu.VMEM((R, 1), F32), pltpu.VMEM((R, dv), F32)],
    )
    return pl.pallas_call(
        functools.partial(_decode_kernel, n_scalar=2, n_chunks=pp, diff=True, lam_init=lam_init,
                          n_heads=H, n_new=T),
        grid_spec=grid_spec,
        out_shape=jax.ShapeDtypeStruct((B, T, H * dv), BF16),
        compiler_params=_cparams(("arbitrary", "arbitrary")),
        name="diff_paged",
    )(page_table.reshape(-1), jnp.asarray(tile_of_page), qm, *([cache_k] * pp), *([cache_v] * pp),
      *([bm] * pp), kn2, vn2, bmn, lam4, g_out.reshape(1, dv))


def _rglru_kernel(y_ref, x_ref, cs_ref, h0_ref, cw_ref, cb_ref, wa_ref, ba_ref, wx_ref, bx_ref, lam_ref,
                  out_ref, conv_ref, hl_ref, xpad, hcar, *, l_valid):
    i = pl.program_id(1)
    T, D = x_ref.shape[1], x_ref.shape[2]

    @pl.when(i == 0)
    def _():
        xpad[0:8, :] = cs_ref[0]
        hcar[...] = h0_ref[0]

    xpad[8:8 + T, :] = x_ref[0]
    conv = cb_ref[...] + cw_ref[CONV_W - 1:CONV_W, :] * xpad[8:8 + T, :]
    for j in range(CONV_W - 1):
        s = CONV_W - 1 - j
        conv = conv + cw_ref[j:j + 1, :] * xpad[8 - s:8 - s + T, :]
    nb = wa_ref.shape[0]
    bs = D // nb
    r_parts, i_parts = [], []
    for n in range(nb):
        xb = conv[:, n * bs:(n + 1) * bs]
        r_parts.append(_bdot(xb, wa_ref[n]))
        i_parts.append(_bdot(xb, wx_ref[n]))
    r = jax.nn.sigmoid(jnp.concatenate(r_parts, axis=-1) + ba_ref[...])
    ig = jax.nn.sigmoid(jnp.concatenate(i_parts, axis=-1) + bx_ref[...])
    log_a = -LRU_C * r * _softplus(-lam_ref[...])
    a = jnp.exp(log_a)
    bb = jnp.sqrt(-jnp.tanh(log_a) * (1.0 + a * a)) * ig * conv
    rowid = lax.broadcasted_iota(jnp.int32, (T, D), 0)
    s = 1
    while s < T:
        keep = rowid >= s
        a_sh = pltpu.roll(a, s, 0)
        b_sh = pltpu.roll(bb, s, 0)
        bb = jnp.where(keep, a * b_sh + bb, bb)
        a = jnp.where(keep, a * a_sh, a)
        s *= 2
    hs = a * hcar[...] + bb
    out_ref[0] = (hs * _gelu(y_ref[0])).astype(out_ref.dtype)
    hcar[...] = hs[T - 1:T, :]
    tail = xpad[T:T + 8, :]
    xpad[0:8, :] = tail

    @pl.when(i == pl.num_programs(1) - 1)
    def _():
        hl_ref[0] = hs[l_valid - 1:l_valid, :]
        conv_ref[0] = xpad[8 + l_valid - (CONV_W - 1):8 + l_valid, :]


def rglru_core(z, conv_state, h0, conv_w, conv_b, w_a, b_a, w_x, b_x, lam, l_valid_last, tblk):
    B, L, D2 = z.shape
    D = D2 // 2
    cs8 = jnp.concatenate([jnp.zeros((B, 8 - (CONV_W - 1), D), F32), conv_state.astype(F32)], axis=1)
    vec = lambda a: a.reshape(1, D)
    out, conv_new, h_last = pl.pallas_call(
        functools.partial(_rglru_kernel, l_valid=l_valid_last),
        grid=(B, L // tblk),
        in_specs=[
            pl.BlockSpec((1, tblk, D), lambda b, i: (b, i, 0)),
            pl.BlockSpec((1, tblk, D), lambda b, i: (b, i, 1)),
            pl.BlockSpec((1, 8, D), lambda b, i: (b, 0, 0)),
            pl.BlockSpec((1, 1, D), lambda b, i: (b, 0, 0)),
            pl.BlockSpec((CONV_W, D), lambda b, i: (0, 0)),
            pl.BlockSpec((1, D), lambda b, i: (0, 0)),
            pl.BlockSpec(w_a.shape, lambda b, i: (0, 0, 0)),
            pl.BlockSpec((1, D), lambda b, i: (0, 0)),
            pl.BlockSpec(w_x.shape, lambda b, i: (0, 0, 0)),
            pl.BlockSpec((1, D), lambda b, i: (0, 0)),
            pl.BlockSpec((1, D), lambda b, i: (0, 0)),
        ],
        out_specs=[
            pl.BlockSpec((1, tblk, D), lambda b, i: (b, i, 0)),
            pl.BlockSpec((1, CONV_W - 1, D), lambda b, i: (b, 0, 0)),
            pl.BlockSpec((1, 1, D), lambda b, i: (b, 0, 0)),
        ],
        out_shape=[
            jax.ShapeDtypeStruct((B, L, D), BF16),
            jax.ShapeDtypeStruct((B, CONV_W - 1, D), F32),
            jax.ShapeDtypeStruct((B, 1, D), F32),
        ],
        scratch_shapes=[pltpu.VMEM((tblk + 8, D), F32), pltpu.VMEM((1, D), F32)],
        compiler_params=_cparams(("arbitrary", "arbitrary")),
        name="rglru",
    )(z, z, cs8, h0.astype(F32).reshape(B, 1, D), conv_w, vec(conv_b), w_a.astype(BF16), vec(b_a),
      w_x.astype(BF16), vec(b_x), vec(lam))
    return out, conv_new, h_last.reshape(B, D)


def _argmax_rows(s):
    n, tb = s.shape
    row8 = lax.broadcasted_iota(jnp.int32, (8, tb), 0).astype(F32)
    vals = [s[i:i + 8] for i in range(0, n, 8)]
    rows = [row8 + float(i) for i in range(0, n, 8)]
    while len(vals) > 1:
        nv, nr = [], []
        for a in range(0, len(vals) - 1, 2):
            first = vals[a] >= vals[a + 1]
            nv.append(jnp.maximum(vals[a], vals[a + 1]))
            nr.append(jnp.where(first, rows[a], rows[a + 1]))
        if len(vals) % 2:
            nv.append(vals[-1])
            nr.append(rows[-1])
        vals, rows = nv, nr
    v, r = vals[0], rows[0]
    m = jnp.max(v, axis=0, keepdims=True)
    return m, jnp.min(jnp.where(v == m, r, float(n)), axis=0, keepdims=True)


def _topk_rows(s, k):
    n, tb = s.shape
    rowf = lax.broadcasted_iota(jnp.int32, (n, tb), 0).astype(F32)
    vals, idxs = [], []
    for _ in range(k):
        m, idx = _argmax_rows(s)
        vals.append(m)
        idxs.append(idx)
        s = jnp.where(rowf == idx, -jnp.inf, s)
    return jnp.concatenate(vals, axis=0), jnp.concatenate(idxs, axis=0)


def _pair_plan(k):
    full, small = [], []
    a = 0
    while a < k and k // (a + 1) > 1:
        nb, b0 = k // (a + 1), 0
        while nb - b0 >= 8:
            full.append((a, b0))
            b0 += 8
        if nb > b0:
            small.append((a, b0, nb - b0))
        a += 1
    a0 = a
    assert k % 8 == 0 and (k - a0) % 8 == 0 and (k & (k - 1)) == 0
    bins = []
    for a_, b0, n in sorted(small, key=lambda p: -p[2]):
        for bn in bins:
            used = sum(p[3] for p in bn)
            if used + n <= 8:
                bn.append((a_, b0, used, n))
                break
        else:
            bins.append([(a_, b0, 0, n)])
    code = []
    for a_, b0 in full:
        code += [a_ * k + b0 + r for r in range(8)]
    for bn in bins:
        rows = [k * k] * 8
        for a_, b0, off, n in bn:
            for r in range(n):
                rows[off + r] = a_ * k + b0 + r
        code += rows
    code += [a_ * k for a_ in range(a0, k)]
    return full, bins, a0, np.array(code, np.float32)


def _route_kernel(*refs, topk, cast_experts):
    if cast_experts:
        h_ref, wq_ref, k1_ref, k2_ref, code_ref, u_ref, v_ref, e1_ref, e2_ref, g_ref, ut_ref, vb_ref = refs
        ut_ref[...] = u_ref[0].T.astype(BF16)
        vb_ref[...] = v_ref[0].astype(BF16)
    else:
        h_ref, wq_ref, k1_ref, k2_ref, code_ref, e1_ref, e2_ref, g_ref = refs
    k = topk
    q = jnp.dot(h_ref[...], wq_ref[...], preferred_element_type=F32)
    dk2 = k1_ref.shape[2]
    tb = q.shape[0]
    nt = (((1,), (1,)), ((), ()))
    s1 = lax.dot_general(k1_ref[0], q[:, :dk2], nt, precision=HIGHEST, preferred_element_type=F32)
    s2 = lax.dot_general(k2_ref[0], q[:, dk2:], nt, precision=HIGHEST, preferred_element_type=F32)
    sv1, si1 = _topk_rows(s1, k)
    sv2, si2 = _topk_rows(s2, k)
    full, bins, a0, _ = _pair_plan(k)
    row8 = lax.broadcasted_iota(jnp.int32, (8, tb), 0)
    groups = [sv1[a:a + 1] + sv2[b0:b0 + 8] for a, b0 in full]
    for bn in bins:
        v = jnp.full((8, tb), -jnp.inf, F32)
        for a, b0, off, n in bn:
            piece = sv1[a:a + 1] + (pltpu.roll(sv2[b0:b0 + 8], off, 0) if off else sv2[b0:b0 + 8])
            v = jnp.where((row8 >= off) & (row8 < off + n), piece, v)
        groups.append(v)
    for c in range(a0, k, 8):
        groups.append(sv1[c:c + 8] + sv2[0:1])
    cand = jnp.concatenate(groups, axis=0)
    code = code_ref[...]
    vals, poss = [], []
    for _ in range(k):
        m = jnp.max(cand, axis=0, keepdims=True)
        pos = jnp.min(jnp.where(cand == m, code, float(k * k)), axis=0, keepdims=True)
        vals.append(m)
        poss.append(pos)
        cand = jnp.where(code == pos, -jnp.inf, cand)
    val = jnp.concatenate(vals, axis=0)
    pos = jnp.concatenate(poss, axis=0)
    ra = jnp.floor(pos * (1.0 / k))
    rb = pos - k * ra
    e1 = jnp.zeros((k, tb), F32)
    e2 = jnp.zeros((k, tb), F32)
    for r in range(k):
        e1 = jnp.where(ra == float(r), si1[r:r + 1], e1)
        e2 = jnp.where(rb == float(r), si2[r:r + 1], e2)
    e1_ref[0] = e1
    e2_ref[0] = e2
    ex = jnp.exp(val - val[0:1])
    g_ref[0] = ex / jnp.sum(ex, axis=0, keepdims=True)


def peer_route(h, wq, keys, experts=None):
    N, D = h.shape
    nh, _, nk, dk2 = keys.shape
    tb = _pick(N, (1024, 512, 256, 128))
    nblk = N // tb
    code = _pair_plan(TOPK_P)[3]
    code = jnp.asarray(np.broadcast_to(code[:, None], (code.shape[0], tb)))
    shp = jax.ShapeDtypeStruct((nh, TOPK_P, N), F32)
    o_spec = pl.BlockSpec((1, TOPK_P, tb), lambda hh, i: (hh, 0, i))
    in_specs = [pl.BlockSpec((tb, D), lambda hh, i: (i, 0)),
                pl.BlockSpec((D, 2 * dk2), lambda hh, i: (0, hh)),
                pl.BlockSpec((1, nk, dk2), lambda hh, i: (2 * hh, 0, 0)),
                pl.BlockSpec((1, nk, dk2), lambda hh, i: (2 * hh + 1, 0, 0)),
                pl.BlockSpec(code.shape, lambda hh, i: (0, 0))]
    args = [h, wq, keys.reshape(nh * 2, nk, dk2), keys.reshape(nh * 2, nk, dk2), code]
    out_specs, out_shape = [o_spec, o_spec, o_spec], [shp, shp, shp]
    if experts is not None:
        u_all, v_all, layer = experts
        E = u_all.shape[1]
        rb = E // (nh * nblk)
        assert rb * nh * nblk == E and rb % LANES == 0
        in_specs += [pl.BlockSpec((1, rb, D), lambda hh, i: (layer, hh * nblk + i, 0))] * 2
        args += [u_all, v_all]
        out_specs += [pl.BlockSpec((D, rb), lambda hh, i: (0, hh * nblk + i)),
                      pl.BlockSpec((rb, D), lambda hh, i: (hh * nblk + i, 0))]
        out_shape += [jax.ShapeDtypeStruct((D, E), BF16), jax.ShapeDtypeStruct((E, D), BF16)]
    return pl.pallas_call(
        functools.partial(_route_kernel, topk=TOPK_P, cast_experts=experts is not None),
        grid=(nh, nblk),
        in_specs=in_specs,
        out_specs=out_specs,
        out_shape=out_shape,
        compiler_params=_cparams(("arbitrary", "arbitrary")),
        name="peer_route",
    )(*args)


def _route_build_kernel(e1_ref, e2_ref, g_ref, sub_ref, o_ref, *, nkeys):
    sub = sub_ref[...][None]
    one = jnp.ones((), BF16)
    zero = jnp.zeros((), BF16)
    at = jnp.where(e1_ref[...].astype(BF16) == sub, one, zero)
    bt = jnp.where(e2_ref[...].astype(BF16) == sub, g_ref[...].astype(BF16), zero)
    g3 = jnp.einsum('tik,tjk->tij', at, bt, preferred_element_type=F32)
    gt = pltpu.einshape('tij->itj', g3)
    for i in range(nkeys):
        o_ref[:, i * nkeys:(i + 1) * nkeys] = gt[i].astype(o_ref.dtype)


def peer_build(e1, e2, g, nkeys):
    nh, k, N = e1.shape
    ks = nh * k
    slot = lambda a: jnp.transpose(a, (2, 0, 1)).reshape(N, 1, ks)
    tb = _pick(N, (128, 64, 32))
    spec = pl.BlockSpec((tb, 1, ks), lambda i: (i, 0, 0))
    sub = jnp.asarray(np.broadcast_to(np.arange(nkeys, dtype=np.float32)[:, None], (nkeys, ks)), BF16)
    return pl.pallas_call(
        functools.partial(_route_build_kernel, nkeys=nkeys),
        grid=(N // tb,),
        in_specs=[spec, spec, spec, pl.BlockSpec((nkeys, ks), lambda i: (0, 0))],
        out_specs=pl.BlockSpec((tb, nkeys * nkeys), lambda i: (i, 0)),
        out_shape=jax.ShapeDtypeStruct((N, nkeys * nkeys), BF16),
        compiler_params=_cparams(("arbitrary",)),
        name="peer_build",
    )(slot(e1), slot(e2), slot(g), sub)


def _peer_kernel(*refs, with_next):
    if with_next:
        x_ref, ut_ref, v_ref, g_ref, r_ref, gt_ref, gn_ref, sh_ref, sc_ref, o_ref, h_ref, acc_s = refs
    else:
        x_ref, ut_ref, v_ref, g_ref, r_ref, gt_ref, o_ref, acc_s = refs
    e = pl.program_id(1)

    @pl.when(e == 0)
    def _():
        acc_s[...] = jnp.zeros(acc_s.shape, F32)

    s = jnp.dot(x_ref[...], ut_ref[0], preferred_element_type=F32)
    p = (_gelu(s) * g_ref[...].astype(F32)).astype(BF16)
    acc_s[...] += jnp.dot(p, v_ref[0], preferred_element_type=F32)

    @pl.when(e == pl.num_programs(1) - 1)
    def _():
        xn = r_ref[...] + gt_ref[0] * acc_s[...]
        o_ref[...] = xn
        if with_next:
            h_ref[...] = _modulated(xn, gn_ref, sh_ref, sc_ref).astype(h_ref.dtype)


def peer_dense(h, ut_all, v_all, layer, G, res, gate, next_mod=None):
    N, D = h.shape
    E = ut_all.shape[2]
    tb = _pick(N, (512, 256, 128))
    eb = _pick(E, (1024, 512, 256, 128))

    def param_spec(p):
        if p.shape[1] == 1:
            rows = N // p.shape[0]
            assert rows % tb == 0
            return pl.BlockSpec((1, 1, D), lambda i, e: (i // (rows // tb), 0, 0))
        return pl.BlockSpec((1, tb, D), lambda i, e: (0, i, 0))

    row = pl.BlockSpec((tb, D), lambda i, e: (i, 0))
    in_specs = [row,
                pl.BlockSpec((1, D, eb), lambda i, e: (layer, 0, e)),
                pl.BlockSpec((1, eb, D), lambda i, e: (layer, e, 0)),
                pl.BlockSpec((tb, eb), lambda i, e: (i, e)),
                row, param_spec(gate)]
    args = [h, ut_all, v_all, G, res, gate]
    out_specs, out_shape = [row], [jax.ShapeDtypeStruct((N, D), F32)]
    if next_mod is not None:
        g_norm, shift, scale = next_mod
        in_specs += [pl.BlockSpec((1, D), lambda i, e: (0, 0)), param_spec(shift), param_spec(scale)]
        args += [g_norm.reshape(1, D), shift, scale]
        out_specs.append(row)
        out_shape.append(jax.ShapeDtypeStruct((N, D), BF16))
    outs = pl.pallas_call(
        functools.partial(_peer_kernel, with_next=next_mod is not None),
        grid=(N // tb, E // eb),
        in_specs=in_specs,
        out_specs=out_specs,
        out_shape=out_shape,
        scratch_shapes=[pltpu.VMEM((tb, D), F32)],
        compiler_params=_cparams(("arbitrary", "arbitrary")),
        name="peer_dense",
    )(*args)
    return (outs[0], outs[1]) if next_mod is not None else (outs[0], None)


def _pad_rows(a, n, value=0.0, axis=1):
    pad = [(0, 0)] * a.ndim
    pad[axis] = (0, n - a.shape[axis])
    return jnp.pad(a, pad, constant_values=value)


def kernel(x_prompt, x_sample, state_mlstm_C, state_mlstm_n, state_mlstm_m, cache_swa_k, cache_swa_v,
           cache_diff_k, cache_diff_v, state_rglru_conv, state_rglru_h, page_table, c_prompt, c_sample,
           w_ada, b_ada, g_norm_mix, g_norm_ffn, rel_bias,
           mlstm_w_in, mlstm_b_gates, mlstm_g_out, mlstm_w_out,
           swa_w_in, swa_g_q, swa_g_k, swa_w_out,
           diff_w_in, diff_g_q, diff_g_k, diff_lam_q1, diff_lam_k1, diff_lam_q2, diff_lam_k2, diff_g_out, diff_w_out,
           rglru_w_in, rglru_conv_w, rglru_conv_b, rglru_w_a, rglru_b_a, rglru_w_x, rglru_b_x, rglru_lambda, rglru_w_out,
           peer_w_q, peer_keys, peer_u, peer_v):
    xp, xs = x_prompt, x_sample
    Bp, S, D = xp.shape
    Bs, T, _ = xs.shape
    depth = w_ada.shape[0]
    Ns = Bs * T

    c_all = jnp.concatenate([c_prompt, c_sample], axis=0)
    mod = ada_all(c_all, w_ada, b_ada)
    mod = mod.reshape(depth, Bp + Bs, 6, 1, D)

    def mods(i, which):
        m = mod[i, :, which]
        return m[:Bp], m[Bp:]

    def rows(p):
        return jnp.broadcast_to(p, (Bs, T, D)).reshape(1, Ns, D)

    nk = peer_keys.shape[3]
    Ns_pad = -(-Ns // LANES) * LANES

    sh_p, sh_s = mods(0, 0)
    sc_p, sc_s = mods(0, 1)
    hp = modulate(xp, g_norm_mix[0], sh_p, sc_p)
    hs = modulate(xs, g_norm_mix[0], sh_s, sc_s)
    for i in range(depth):
        kind = i % 4
        gt_p, gt_s = mods(i, 2)
        hs_flat = hs.reshape(1, Ns, D)
        if kind == 0:
            H = NH_A
            dk = state_mlstm_C.shape[2]
            dv = state_mlstm_C.shape[3]
            nmain = 2 * H * dk + 2 * H * dv
            wb = mlstm_w_in.astype(BF16)
            w_gate = mlstm_w_in[:, nmain:].astype(BF16)
            zp = matmul(hp, wb, cols=(0, nmain))
            gp = matmul(hp, w_gate)
            zs = matmul(hs_flat, wb, cols=(0, nmain)).reshape(Bs, T, nmain)
            gs = matmul(hs_flat, w_gate).reshape(Bs, T, 2 * H)
            chunk_p = _pick(S, (CHUNK_A, 128))
            op, mC_p, mn_p, mm_p = mlstm_core(
                zp, gp, mlstm_b_gates, jnp.zeros((Bp, H, dk, dv), F32), jnp.zeros((Bp, H, dk), F32),
                jnp.full((Bp, H), M_INIT, F32), mlstm_g_out, chunk_p)
            Tp = 128
            zs_pad = _pad_rows(zs, Tp)
            gs_pad = jnp.concatenate([
                _pad_rows(gs[..., :H], Tp, NEG), _pad_rows(gs[..., H:], Tp, -NEG)], axis=-1)
            os_, mC_s, mn_s, mm_s = mlstm_core(
                zs_pad, gs_pad, mlstm_b_gates, state_mlstm_C.astype(F32), state_mlstm_n.astype(F32),
                state_mlstm_m.astype(F32), mlstm_g_out, Tp)
            os_ = os_[:, :T]
            w_out = mlstm_w_out
        elif kind == 1:
            H = NH_B
            dh = D // H
            wb = swa_w_in.astype(BF16)
            gq = jnp.tile(swa_g_q, H)
            gk = jnp.tile(swa_g_k, H)
            qn_p = matmul_headnorm(hp, wb, (0, D), gq, dh, dh ** -0.5 * LOG2E, BF16)
            kn_p = matmul_headnorm(hp, wb, (D, D), gk, dh, 1.0, F32)
            vp = matmul(hp, wb, cols=(2 * D, D))
            qn_s = matmul_headnorm(hs_flat, wb, (0, D), gq, dh, dh ** -0.5, BF16).reshape(Bs, T, D)
            kn_s = matmul_headnorm(hs_flat, wb, (D, D), gk, dh, 1.0, F32).reshape(Bs, T, D)
            vs_new = matmul(hs_flat, wb, cols=(2 * D, D)).reshape(Bs, T, D)
            t = _pick(S, (512, 256, 128))
            assert (_np_multiplicity(_toeplitz_dist_np(S // t, t)) > 0).any(axis=2).all()
            bm_tab = _toeplitz_bias(rel_bias, S // t, t, _swa_bias, LOG2E)
            op = flash_attention(qn_p, kn_p, vp, bm_tab, n_units=H, dqk=dh, dv=dh,
                                 qcol=lambda a: a, kcol=lambda a: a, vcol=lambda a: a, bmap=lambda a: a,
                                 t=t, out_dtype=BF16)
            swa_k_p = kn_p.reshape(Bp, S, H, dh)
            swa_v_p = vp.reshape(Bp, S, H, dh)
            os_ = swa_decode(qn_s, kn_s, vs_new, cache_swa_k, cache_swa_v, rel_bias)
            def shifted(cache, new):
                Wb = cache.shape[1]
                buf = lax.pad(cache.astype(F32), jnp.zeros((), F32), ((0, 0, 0), (-T, T, 0), (0, 0, 0), (0, 0, 0)))
                return lax.dynamic_update_slice(buf, new.reshape(Bs, T, H, dh), (0, Wb - T, 0, 0))

            swa_k_s = shifted(cache_swa_k, kn_s)
            swa_v_s = shifted(cache_swa_v, vs_new)
            w_out = swa_w_out
        elif kind == 2:
            H = NH_C
            dh = D // (2 * H)
            dv = 2 * dh
            lam_init = 0.8 - 0.6 * math.exp(-0.3 * i)
            lam4 = jnp.stack([diff_lam_q1, diff_lam_k1, diff_lam_q2, diff_lam_k2]).astype(F32)
            wb = diff_w_in.astype(BF16)
            gq = jnp.tile(diff_g_q.reshape(-1), H)
            gk = jnp.tile(diff_g_k.reshape(-1), H)
            qn_p = matmul_headnorm(hp, wb, (0, D), gq, dh, dh ** -0.5 * LOG2E, BF16)
            kn_p = matmul_headnorm(hp, wb, (D, D), gk, dh, 1.0, F32)
            vp = matmul(hp, wb, cols=(2 * D, D))
            qn_s = matmul_headnorm(hs_flat, wb, (0, D), gq, dh, dh ** -0.5, BF16).reshape(Bs, T, D)
            kn_s = matmul_headnorm(hs_flat, wb, (D, D), gk, dh, 1.0, F32).reshape(Bs, T, D)
            vs_new = matmul(hs_flat, wb, cols=(2 * D, D)).reshape(Bs, T, D)
            t = _pick(S, (512, 256, 128))
            bm_tab = _toeplitz_bias(rel_bias, S // t, t, _causal_bias, LOG2E)
            o2 = flash_attention(qn_p, kn_p, vp, bm_tab, n_units=2 * H, dqk=dh, dv=dv,
                                 qcol=lambda a: a, kcol=lambda a: a, vcol=lambda a: a // 2,
                                 bmap=lambda a: (a % 2) * H + a // 2, t=t, out_dtype=F32)
            op = diff_combine(o2, lam4, diff_g_out, lam_init)
            diff_k_p = kn_p.reshape(Bp, S, H, 2 * dh)
            diff_v_p = vp.reshape(Bp, S, H, dv)
            diff_k_s = kn_s.reshape(Bs, T, H, 2 * dh)
            diff_v_s = vs_new.reshape(Bs, T, H, dv)
            os_ = diff_paged(qn_s.reshape(Bs, T, H, 2, dh), diff_k_s, diff_v_s, cache_diff_k, cache_diff_v,
                             page_table, rel_bias, lam4, diff_g_out, lam_init)
            w_out = diff_w_out
        else:
            wb = rglru_w_in.astype(BF16)
            zp = matmul(hp, wb)
            zs = matmul(hs_flat, wb).reshape(Bs, T, -1)
            tblk = _pick(S, (256, 128))
            op, conv_p, h_p = rglru_core(zp, jnp.zeros((Bp, CONV_W - 1, D), F32), jnp.zeros((Bp, D), F32),
                                         rglru_conv_w, rglru_conv_b, rglru_w_a, rglru_b_a, rglru_w_x, rglru_b_x,
                                         rglru_lambda, tblk, tblk)
            os_, conv_s, h_s = rglru_core(_pad_rows(zs, 8), state_rglru_conv, state_rglru_h,
                                          rglru_conv_w, rglru_conv_b, rglru_w_a, rglru_b_a, rglru_w_x, rglru_b_x,
                                          rglru_lambda, T, 8)
            os_ = os_[:, :T]
            w_out = rglru_w_out
        sh_p, sh_s = mods(i, 3)
        sc_p, sc_s = mods(i, 4)
        wb = w_out.astype(BF16)
        xp, hp = matmul_res_mod(op, wb, xp, gt_p, g_norm_ffn[i], sh_p, sc_p)
        xs, hs = matmul_res_mod(os_.reshape(1, Ns, -1), wb, xs.reshape(1, Ns, D), rows(gt_s), g_norm_ffn[i],
                                rows(sh_s), rows(sc_s))
        hp = hp.reshape(Bp * S, D)
        hs = hs.reshape(Ns, D)

        gt_p, gt_s = mods(i, 5)
        wq = peer_w_q[i].astype(BF16)
        e1, e2, gw, ut, vb = peer_route(hp, wq, peer_keys[i], experts=(peer_u, peer_v, i))
        Gp = peer_build(e1, e2, gw, nk)
        Gs = peer_build(*peer_route(_pad_rows(hs, Ns_pad, axis=0), wq, peer_keys[i]), nk)[:Ns]
        if i + 1 < depth:
            sh_p, sh_s = mods(i + 1, 0)
            sc_p, sc_s = mods(i + 1, 1)
            next_p = (g_norm_mix[i + 1], sh_p, sc_p)
            next_s = (g_norm_mix[i + 1], rows(sh_s), rows(sc_s))
        else:
            next_p = next_s = None
        xp, hp = peer_dense(hp, ut[None], vb[None], 0, Gp, xp.reshape(Bp * S, D), gt_p, next_p)
        xs, hs = peer_dense(hs, ut[None], vb[None], 0, Gs, xs.reshape(Ns, D), rows(gt_s), next_s)
        xp = xp.reshape(Bp, S, D)
        xs = xs.reshape(Bs, T, D)
        if hp is not None:
            hp = hp.reshape(Bp, S, D)
            hs = hs.reshape(Bs, T, D)

    return (xp, xs, mC_p, mC_s, mn_p, mn_s, mm_p, mm_s, swa_k_p, swa_k_s, swa_v_p, swa_v_s,
            diff_k_p, diff_k_s, diff_v_p, diff_v_s, conv_p, conv_s, h_p, h_s)
```

```python
import functools
import math

import numpy as np
import jax
import jax.numpy as jnp
from jax import lax
from jax.experimental import pallas as pl
from jax.experimental.pallas import tpu as pltpu

F32 = jnp.float32
BF16 = jnp.bfloat16
HIGHEST = lax.Precision.HIGHEST

EPS = 1e-6
NEG = -1e30
VMEM_LIMIT = 56 * 1024 * 1024
LANES = 128
LOG2E = math.log2(math.e)

NH_A, CHUNK_A, M_INIT = 8, 256, -1e30
NH_B = 16
DIL_PAIRS = ((128, 1), (512, 4), (2048, 16))
NH_C = 8
NBLK_D, CONV_W, LRU_C = 16, 4, 8.0
NH_P, N_KEYS, TOPK_P = 8, 128, 16
REL_BUCKETS, REL_MAX_DIST = 32, 2048
PAGE_SIZE = 128


def _cparams(sem):
    return pltpu.CompilerParams(dimension_semantics=sem, vmem_limit_bytes=VMEM_LIMIT)


def _bdot(a, b):
    return jnp.dot(a.astype(BF16), b.astype(BF16), preferred_element_type=F32)


def _bdot_nt(a, b):
    return lax.dot_general(a.astype(BF16), b.astype(BF16), (((1,), (1,)), ((), ())),
                           preferred_element_type=F32)


def _gelu(x):
    return 0.5 * x * (1.0 + lax.erf(x * (1.0 / math.sqrt(2.0))))


def _log_sigmoid(x):
    return jnp.minimum(x, 0.0) - jnp.log1p(jnp.exp(-jnp.abs(x)))


def _softplus(x):
    return jnp.maximum(x, 0.0) + jnp.log1p(jnp.exp(-jnp.abs(x)))


def _pick(n, pref):
    for t in pref:
        if n % t == 0:
            return t
    return n


def _ada_kernel(c_ref, w_ref, b_ref, o_ref):
    c = c_ref[...]
    a = c * jax.nn.sigmoid(c)
    o_ref[0] = _bdot(a, w_ref[0]) + b_ref[0]


def ada_all(c_all, w_ada, b_ada):
    R, D = c_all.shape
    nl, _, N = w_ada.shape
    tn = _pick(N, (2048, 1024, 512, 256, 128))
    return pl.pallas_call(
        _ada_kernel,
        grid=(nl, N // tn),
        in_specs=[pl.BlockSpec((R, D), lambda l, j: (0, 0)),
                  pl.BlockSpec((1, D, tn), lambda l, j: (l, 0, j)),
                  pl.BlockSpec((1, 1, tn), lambda l, j: (l, 0, j))],
        out_specs=pl.BlockSpec((1, R, tn), lambda l, j: (l, 0, j)),
        out_shape=jax.ShapeDtypeStruct((nl, R, N), F32),
        compiler_params=_cparams(("arbitrary", "arbitrary")),
        name="ada",
    )(c_all, w_ada, b_ada.reshape(nl, 1, N))


def _modulate_kernel(x_ref, g_ref, sh_ref, sc_ref, o_ref):
    x = x_ref[0]
    ms = jnp.mean(x * x, axis=-1, keepdims=True)
    y = x * lax.rsqrt(ms + EPS) * g_ref[...]
    o_ref[0] = (y * (1.0 + sc_ref[0]) + sh_ref[0]).astype(o_ref.dtype)


def modulate(x, g, shift, scale):
    B, L, D = x.shape
    tl = _pick(L, (512, 256, 128))
    return pl.pallas_call(
        _modulate_kernel,
        grid=(B, L // tl),
        in_specs=[pl.BlockSpec((1, tl, D), lambda b, i: (b, i, 0)),
                  pl.BlockSpec((1, D), lambda b, i: (0, 0)),
                  pl.BlockSpec((1, 1, D), lambda b, i: (b, 0, 0)),
                  pl.BlockSpec((1, 1, D), lambda b, i: (b, 0, 0))],
        out_specs=pl.BlockSpec((1, tl, D), lambda b, i: (b, i, 0)),
        out_shape=jax.ShapeDtypeStruct((B, L, D), BF16),
        compiler_params=_cparams(("arbitrary", "arbitrary")),
        name="modulate",
    )(x, g.reshape(1, D), shift, scale)


def _mm_kernel(x_ref, w_ref, o_ref):
    o_ref[0] = jnp.dot(x_ref[0], w_ref[...], preferred_element_type=F32).astype(o_ref.dtype)


def matmul(x, w, out_dtype=F32, cols=None):
    B, L, K = x.shape
    c0, N = cols if cols is not None else (0, w.shape[1])
    tm = _pick(L, (1024, 512, 256, 128))
    tn = next((c for c in (1024, 512, 256, 128) if N % c == 0 and c0 % c == 0), N)
    assert c0 % tn == 0 and (tn == w.shape[1] or tn % LANES == 0)
    j0 = c0 // tn
    grid = (N // tn, B, L // tm)
    x_spec = pl.BlockSpec((1, tm, K), lambda j, b, i: (b, i, 0))
    w_spec = pl.BlockSpec((K, tn), lambda j, b, i: (0, j0 + j))
    o_spec = pl.BlockSpec((1, tm, tn), lambda j, b, i: (b, i, j))
    return pl.pallas_call(
        _mm_kernel, grid=grid, in_specs=[x_spec, w_spec], out_specs=o_spec,
        out_shape=jax.ShapeDtypeStruct((B, L, N), out_dtype),
        compiler_params=_cparams(("arbitrary",) * 3), name="matmul",
    )(x, w)


def _modulated(xn, gn_ref, sh_ref, sc_ref):
    ms = jnp.mean(xn * xn, axis=-1, keepdims=True)
    return xn * lax.rsqrt(ms + EPS) * gn_ref[...] * (1.0 + sc_ref[0]) + sh_ref[0]


def _mm_res_mod_kernel(x_ref, w_ref, r_ref, gt_ref, gn_ref, sh_ref, sc_ref, o_ref, h_ref):
    acc = jnp.dot(x_ref[0], w_ref[...], preferred_element_type=F32)
    xn = r_ref[0] + gt_ref[0] * acc
    o_ref[0] = xn
    h_ref[0] = _modulated(xn, gn_ref, sh_ref, sc_ref).astype(h_ref.dtype)


def _row_param_spec(p, tm, n):
    if p.shape[1] == 1:
        return pl.BlockSpec((1, 1, n), lambda b, i: (b, 0, 0))
    return pl.BlockSpec((1, tm, n), lambda b, i: (b, i, 0))


def matmul_res_mod(x, w, res, gate, g_norm, shift, scale):
    B, L, K = x.shape
    N = w.shape[1]
    tm = _pick(L, (512, 256, 128))
    row = pl.BlockSpec((1, tm, N), lambda b, i: (b, i, 0))
    return pl.pallas_call(
        _mm_res_mod_kernel,
        grid=(B, L // tm),
        in_specs=[pl.BlockSpec((1, tm, K), lambda b, i: (b, i, 0)),
                  pl.BlockSpec((K, N), lambda b, i: (0, 0)),
                  row, _row_param_spec(gate, tm, N),
                  pl.BlockSpec((1, N), lambda b, i: (0, 0)),
                  _row_param_spec(shift, tm, N), _row_param_spec(scale, tm, N)],
        out_specs=[row, row],
        out_shape=[jax.ShapeDtypeStruct((B, L, N), F32), jax.ShapeDtypeStruct((B, L, N), BF16)],
        compiler_params=_cparams(("arbitrary", "arbitrary")),
        name="matmul_res_mod",
    )(x, w, res, gate, g_norm.reshape(1, N), shift, scale)


def _mm_headnorm_kernel(x_ref, w_ref, g_ref, o_ref, *, dh, scale):
    acc = jnp.dot(x_ref[0], w_ref[...], preferred_element_type=F32)
    for h in range(acc.shape[1] // dh):
        sl = slice(h * dh, (h + 1) * dh)
        a = acc[:, sl]
        ms = jnp.mean(a * a, axis=-1, keepdims=True)
        y = a * lax.rsqrt(ms + EPS) * g_ref[:, sl]
        o_ref[0, :, sl] = (y * scale if scale != 1.0 else y).astype(o_ref.dtype)


def matmul_headnorm(x, w, cols, gain_full, dh, scale, out_dtype):
    B, L, K = x.shape
    c0, N = cols
    tm = _pick(L, (1024, 512, 256, 128))
    tn = next(c for c in (1024, 512, 256, 128) if N % c == 0 and c0 % c == 0)
    j0 = c0 // tn
    return pl.pallas_call(
        functools.partial(_mm_headnorm_kernel, dh=dh, scale=scale),
        grid=(N // tn, B, L // tm),
        in_specs=[pl.BlockSpec((1, tm, K), lambda j, b, i: (b, i, 0)),
                  pl.BlockSpec((K, tn), lambda j, b, i: (0, j0 + j)),
                  pl.BlockSpec((1, tn), lambda j, b, i: (0, j))],
        out_specs=pl.BlockSpec((1, tm, tn), lambda j, b, i: (b, i, j)),
        out_shape=jax.ShapeDtypeStruct((B, L, N), out_dtype),
        compiler_params=_cparams(("arbitrary",) * 3),
        name="matmul_headnorm",
    )(x, w, gain_full.reshape(1, N))


def _mlstm_kernel(q_ref, k_ref, v_ref, o_ref, gc_ref, gr_ref, bc_ref, br_ref,
                  C0_ref, n0_ref, m0_ref, gout_ref,
                  hs_ref, C_ref, n_ref, m_ref, C_s, n_s, m_s, *, dk, dv):
    ci = pl.program_id(2)
    hp = C_s.shape[0]

    @pl.when(ci == 0)
    def _():
        C_s[...] = C0_ref[0]
        n_s[...] = n0_ref[0]
        m_s[...] = m0_ref[0]

    c = q_ref.shape[1]
    row = lax.broadcasted_iota(jnp.int32, (c, c), 0)
    col = lax.broadcasted_iota(jnp.int32, (c, c), 1)
    causal = col <= row
    for hh in range(hp):
        q = q_ref[0, :, hh * dk:(hh + 1) * dk]
        k = k_ref[0, :, hh * dk:(hh + 1) * dk] * (dk ** -0.5)
        v = v_ref[0, :, hh * dv:(hh + 1) * dv]
        gc = gc_ref[0, hh] + bc_ref[hh]
        gr = gr_ref[0, hh] + br_ref[hh]
        li_c, lf_c = gc[:, 0:1], _log_sigmoid(gc[:, 1:2])
        li_r, lf_r = gr[0:1, :], _log_sigmoid(gr[1:2, :])
        b_c = jnp.sum(jnp.where(causal, lf_r, 0.0), axis=1, keepdims=True)
        b_r = jnp.sum(jnp.where(row <= col, lf_c, 0.0), axis=0, keepdims=True)
        m_prev = m_s[hh]
        Dm = jnp.where(causal, b_c - b_r + li_r, NEG)
        m_t = jnp.maximum(b_c + m_prev, jnp.max(Dm, axis=1, keepdims=True))
        S = _bdot_nt(q, k) * jnp.exp(Dm - m_t)
        inter = jnp.exp(b_c + m_prev - m_t)
        C = C_s[hh]
        n = n_s[hh]
        num = _bdot(S, v) + inter * _bdot(q, C)
        den = jnp.sum(S, axis=1, keepdims=True) + inter * jnp.sum(q * n, axis=1, keepdims=True)
        h = num / jnp.maximum(jnp.abs(den), jnp.exp(-m_t))
        sl = slice(hh * dv, (hh + 1) * dv)
        hg = jax.nn.sigmoid(o_ref[0, :, sl]) * h
        ms = jnp.mean(hg * hg, axis=-1, keepdims=True)
        hs_ref[0, :, sl] = (hg * lax.rsqrt(ms + EPS) * gout_ref[:, sl]).astype(hs_ref.dtype)
        m_new = m_t[c - 1:c, :]
        b_last = b_c[c - 1:c, :]
        w_r = jnp.exp(b_last - b_r + li_r - m_new)
        w_c = jnp.exp(b_last - b_c + li_c - m_new)
        decay = jnp.exp(b_last + m_prev - m_new)
        C_s[hh] = decay * C + _bdot(k.T, w_c * v)
        n_s[hh] = decay * n + jnp.dot(w_r, k, preferred_element_type=F32, precision=HIGHEST)
        m_s[hh] = m_new

    @pl.when(ci == pl.num_programs(2) - 1)
    def _():
        C_ref[0] = C_s[...]
        n_ref[0] = n_s[...]
        m_ref[0] = m_s[...]


def mlstm_core(z, g, b_gates, C0, n0, m0, g_out, chunk):
    B, L, _ = z.shape
    H = NH_A
    dk = C0.shape[2]
    dv = C0.shape[3]
    nc = L // chunk
    g4 = g.reshape(B, L, 2, H)
    gcol = jnp.transpose(g4, (0, 3, 1, 2))
    grow = jnp.transpose(g4, (0, 3, 2, 1))
    bg = b_gates.reshape(2, H)
    bcol = jnp.transpose(bg, (1, 0)).reshape(H, 1, 2)
    brow = jnp.transpose(bg, (1, 0)).reshape(H, 2, 1)
    hp = 2 if H % 2 == 0 else 1
    wk, wv = hp * dk, hp * dv
    kq = (H * dk) // wk
    vo = (2 * H * dk) // wv
    oo = vo + H // hp
    outs = pl.pallas_call(
        functools.partial(_mlstm_kernel, dk=dk, dv=dv),
        grid=(B, H // hp, nc),
        in_specs=[
            pl.BlockSpec((1, chunk, wk), lambda b, h, c: (b, c, h)),
            pl.BlockSpec((1, chunk, wk), lambda b, h, c: (b, c, kq + h)),
            pl.BlockSpec((1, chunk, wv), lambda b, h, c: (b, c, vo + h)),
            pl.BlockSpec((1, chunk, wv), lambda b, h, c: (b, c, oo + h)),
            pl.BlockSpec((1, hp, chunk, 2), lambda b, h, c: (b, h, c, 0)),
            pl.BlockSpec((1, hp, 2, chunk), lambda b, h, c: (b, h, 0, c)),
            pl.BlockSpec((hp, 1, 2), lambda b, h, c: (h, 0, 0)),
            pl.BlockSpec((hp, 2, 1), lambda b, h, c: (h, 0, 0)),
            pl.BlockSpec((1, hp, dk, dv), lambda b, h, c: (b, h, 0, 0)),
            pl.BlockSpec((1, hp, 1, dk), lambda b, h, c: (b, h, 0, 0)),
            pl.BlockSpec((1, hp, 1, 1), lambda b, h, c: (b, h, 0, 0)),
            pl.BlockSpec((1, wv), lambda b, h, c: (0, h)),
        ],
        out_specs=[
            pl.BlockSpec((1, chunk, wv), lambda b, h, c: (b, c, h)),
            pl.BlockSpec((1, hp, dk, dv), lambda b, h, c: (b, h, 0, 0)),
            pl.BlockSpec((1, hp, 1, dk), lambda b, h, c: (b, h, 0, 0)),
            pl.BlockSpec((1, hp, 1, 1), lambda b, h, c: (b, h, 0, 0)),
        ],
        out_shape=[
            jax.ShapeDtypeStruct((B, L, H * dv), BF16),
            jax.ShapeDtypeStruct((B, H, dk, dv), F32),
            jax.ShapeDtypeStruct((B, H, 1, dk), F32),
            jax.ShapeDtypeStruct((B, H, 1, 1), F32),
        ],
        scratch_shapes=[pltpu.VMEM((hp, dk, dv), F32), pltpu.VMEM((hp, 1, dk), F32), pltpu.VMEM((hp, 1, 1), F32)],
        compiler_params=_cparams(("arbitrary",) * 3),
        name="mlstm",
    )(z, z, z, z, gcol, grow, bcol, brow, C0, n0.reshape(B, H, 1, dk), m0.reshape(B, H, 1, 1),
      g_out.reshape(1, H * dv))
    hs, C, n, m = outs
    return hs, C, n.reshape(B, H, dk), m.reshape(B, H)


def _rel_bucket(dist):
    exact = REL_BUCKETS // 2
    d_f = jnp.maximum(dist, 1).astype(F32)
    large = exact + (jnp.log(d_f / exact) / math.log(REL_MAX_DIST / exact) * (REL_BUCKETS - exact)).astype(jnp.int32)
    large = jnp.minimum(large, REL_BUCKETS - 1)
    return jnp.where(dist < exact, dist, large)


def _bias_of_dist(rel_bias, dist):
    bucket = _rel_bucket(jnp.maximum(dist, 0))
    out = jnp.zeros((rel_bias.shape[1],) + dist.shape, F32)
    for b in range(REL_BUCKETS):
        out = jnp.where(bucket[None] == b, rel_bias[b].reshape((-1,) + (1,) * dist.ndim), out)
    return out


def _np_multiplicity(dist):
    cnt = np.zeros(dist.shape, np.int64)
    for w, d in DIL_PAIRS:
        cnt = cnt + ((dist % d == 0) & (dist <= w) & (dist >= 0))
    return cnt


def _swa_bias(rel_bias, dist_np):
    cnt = _np_multiplicity(dist_np)
    logc = jnp.log(jnp.asarray(np.maximum(cnt, 1), F32))
    bias = _bias_of_dist(rel_bias, jnp.asarray(np.maximum(dist_np, 0), jnp.int32))
    return jnp.where(jnp.asarray(cnt > 0)[None], bias + logc[None], NEG), cnt > 0


def _causal_bias(rel_bias, dist_np):
    bias = _bias_of_dist(rel_bias, jnp.asarray(np.maximum(dist_np, 0), jnp.int32))
    return jnp.where(jnp.asarray(dist_np >= 0)[None], bias, NEG), dist_np >= 0


def _toeplitz_dist_np(n_off, t):
    o = np.arange(n_off)[:, None, None]
    i = np.arange(t)[None, :, None]
    j = np.arange(t)[None, None, :]
    return o * t + i - j


def _toeplitz_kernel(b_ref, o_ref):
    t = o_ref.shape[2]
    rows = jnp.broadcast_to(b_ref[0, 0], (t, 2 * t))
    skew = pltpu.roll(rows, 0, 1, stride=1, stride_axis=0)
    o_ref[0, 0] = skew[:, t:]


def _toeplitz_bias(rel_bias, n_off, t, bias_fn, mult):
    dist = np.arange(-(t - 1), n_off * t)
    f, _ = bias_fn(rel_bias, dist)
    f = f * mult
    maps = f.shape[0]
    want = np.arange(n_off)[:, None] * t + t - np.arange(2 * t)[None, :]
    idx = np.minimum(want + (t - 1), len(dist) - 1)
    b = f[:, idx].reshape(maps, n_off, 1, 2 * t)
    return pl.pallas_call(
        _toeplitz_kernel,
        grid=(maps, n_off),
        in_specs=[pl.BlockSpec((1, 1, 1, 2 * t), lambda m, o: (m, o, 0, 0))],
        out_specs=pl.BlockSpec((1, 1, t, t), lambda m, o: (m, o, 0, 0)),
        out_shape=jax.ShapeDtypeStruct((maps, n_off, t, t), F32),
        compiler_params=_cparams(("arbitrary", "arbitrary")),
        name="toeplitz_bias",
    )(b)


def _flash_kernel(qi_ref, ki_ref, q_ref, k_ref, v_ref, bm_ref, o_ref, m_s, l_s, acc_s):
    p = pl.program_id(1)
    qi = qi_ref[p]
    ki = ki_ref[p]
    nb, tq, dv = acc_s.shape
    tk = k_ref.shape[1]

    @pl.when(ki == 0)
    def _():
        m_s[...] = jnp.full(m_s.shape, NEG, F32)
        l_s[...] = jnp.zeros(l_s.shape, F32)
        acc_s[...] = jnp.zeros(acc_s.shape, F32)

    bm = bm_ref[0, 0]
    for b in range(nb):
        s = _bdot_nt(q_ref[b], k_ref[b]) + bm
        m_prev = m_s[b]
        m_next = jnp.maximum(m_prev, jnp.max(s, axis=1, keepdims=True))
        pr = jnp.exp2(s - jnp.tile(m_next, (1, tk // LANES)))
        alpha = jnp.exp2(m_prev - m_next)
        l_s[b] = alpha * l_s[b] + jnp.sum(pr, axis=1, keepdims=True)
        acc_s[b] = acc_s[b] * jnp.tile(alpha, (1, dv // LANES)) + _bdot(pr, v_ref[b])
        m_s[b] = m_next

    @pl.when(ki == qi)
    def _():
        for b in range(nb):
            o_ref[b] = (acc_s[b] / jnp.tile(l_s[b], (1, dv // LANES))).astype(o_ref.dtype)


def flash_attention(q, k, v, bm_tab, *, n_units, dqk, dv, qcol, kcol, vcol, bmap, t, out_dtype):
    B, L = q.shape[:2]
    nq = L // t
    pairs = [(qi, ki) for qi in range(nq) for ki in range(qi + 1)]
    qi_arr = jnp.asarray(np.array([p[0] for p in pairs], np.int32))
    ki_arr = jnp.asarray(np.array([p[1] for p in pairs], np.int32))
    grid_spec = pltpu.PrefetchScalarGridSpec(
        num_scalar_prefetch=2,
        grid=(n_units, len(pairs)),
        in_specs=[
            pl.BlockSpec((B, t, dqk), lambda a, p, qa, ka: (0, qa[p], qcol(a))),
            pl.BlockSpec((B, t, dqk), lambda a, p, qa, ka: (0, ka[p], kcol(a))),
            pl.BlockSpec((B, t, dv), lambda a, p, qa, ka: (0, ka[p], vcol(a))),
            pl.BlockSpec((1, 1, t, t), lambda a, p, qa, ka: (bmap(a), qa[p] - ka[p], 0, 0)),
        ],
        out_specs=pl.BlockSpec((B, t, dv), lambda a, p, qa, ka: (0, qa[p], a)),
        scratch_shapes=[pltpu.VMEM((B, t, LANES), F32), pltpu.VMEM((B, t, LANES), F32),
                        pltpu.VMEM((B, t, dv), F32)],
    )
    return pl.pallas_call(
        _flash_kernel,
        grid_spec=grid_spec,
        out_shape=jax.ShapeDtypeStruct((B, L, n_units * dv), out_dtype),
        compiler_params=_cparams(("arbitrary", "arbitrary")),
        name="flash_attention",
    )(qi_arr, ki_arr, q, k, v, bm_tab)


def _diff_lambda(lam_ref, lam_init):
    lq1, lk1, lq2, lk2 = lam_ref[0:1, :], lam_ref[1:2, :], lam_ref[2:3, :], lam_ref[3:4, :]
    return (jnp.exp(jnp.sum(lq1 * lk1, axis=-1, keepdims=True))
            - jnp.exp(jnp.sum(lq2 * lk2, axis=-1, keepdims=True)) + lam_init)


def _diff_combine_kernel(o_ref, lam_ref, g_ref, out_ref, *, dv, lam_init):
    lam = _diff_lambda(lam_ref, lam_init)
    nh = out_ref.shape[2] // dv
    for h in range(nh):
        o0 = o_ref[0, :, (2 * h) * dv:(2 * h + 1) * dv]
        o1 = o_ref[0, :, (2 * h + 1) * dv:(2 * h + 2) * dv]
        d = o0 - lam * o1
        ms = jnp.mean(d * d, axis=-1, keepdims=True)
        out_ref[0, :, h * dv:(h + 1) * dv] = (d * lax.rsqrt(ms + EPS) * g_ref[...] * (1.0 - lam_init)
                                              ).astype(out_ref.dtype)


def diff_combine(o, lam4, g_out, lam_init):
    B, L, W2 = o.shape
    dv = g_out.shape[0]
    W = W2 // 2
    tl = _pick(L, (256, 128))
    return pl.pallas_call(
        functools.partial(_diff_combine_kernel, dv=dv, lam_init=lam_init),
        grid=(B, L // tl),
        in_specs=[pl.BlockSpec((1, tl, W2), lambda b, i: (b, i, 0)),
                  pl.BlockSpec(lam4.shape, lambda b, i: (0, 0)),
                  pl.BlockSpec((1, dv), lambda b, i: (0, 0))],
        out_specs=pl.BlockSpec((1, tl, W), lambda b, i: (b, i, 0)),
        out_shape=jax.ShapeDtypeStruct((B, L, W), BF16),
        compiler_params=_cparams(("arbitrary", "arbitrary")),
        name="diff_combine",
    )(o, lam4, g_out.reshape(1, dv))


def _decode_kernel(*refs, n_scalar, n_chunks, diff, lam_init, n_heads, n_new):
    refs = refs[n_scalar:]
    q_ref = refs[0]
    k_refs = refs[1:1 + n_chunks]
    v_refs = refs[1 + n_chunks:1 + 2 * n_chunks]
    bm_refs = refs[1 + 2 * n_chunks:1 + 3 * n_chunks]
    rest = refs[1 + 3 * n_chunks:]
    if diff:
        kn_ref, vn_ref, bmn_ref, lam_ref, g_ref, out_ref, m_s, l_s, acc_s = rest
    else:
        kn_ref, vn_ref, bmn_ref, out_ref, m_s, l_s, acc_s = rest
    p = pl.program_id(1)

    @pl.when(p == 0)
    def _():
        m_s[...] = jnp.full(m_s.shape, NEG, F32)
        l_s[...] = jnp.zeros(l_s.shape, F32)
        acc_s[...] = jnp.zeros(acc_s.shape, F32)

    q = q_ref[0]

    def absorb(k2s, v2s, bms):
        ss = [_bdot_nt(q, k2) + bm for k2, bm in zip(k2s, bms)]
        m_old = m_s[...]
        m_new = m_old
        for s in ss:
            m_new = jnp.maximum(m_new, jnp.max(s, axis=-1, keepdims=True))
        alpha = jnp.exp(m_old - m_new)
        l_new = alpha * l_s[...]
        acc = alpha * acc_s[...]
        for s, v2 in zip(ss, v2s):
            pr = jnp.exp(s - m_new)
            l_new = l_new + jnp.sum(pr, axis=-1, keepdims=True)
            acc = acc + _bdot(pr, v2)
        l_s[...] = l_new
        acc_s[...] = acc
        m_s[...] = m_new

    def rows2d(ref):
        x = ref[0]
        return x.reshape(x.shape[0] * x.shape[1], x.shape[2])

    absorb([rows2d(r) for r in k_refs], [rows2d(r) for r in v_refs], [r[0] for r in bm_refs])

    @pl.when(p == pl.num_programs(1) - 1)
    def _():
        absorb([kn_ref[0]], [vn_ref[0]], [bmn_ref[...]])
        o = acc_s[...] / l_s[...]
        if diff:
            half = n_heads * n_new
            lam = _diff_lambda(lam_ref, lam_init)
            d = o[0:half] - lam * o[half:2 * half]
            ms = jnp.mean(d * d, axis=-1, keepdims=True)
            o = d * lax.rsqrt(ms + EPS) * g_ref[...] * (1.0 - lam_init)
        dv = o.shape[1]
        for h in range(n_heads):
            out_ref[0, :, h * dv:(h + 1) * dv] = o[h * n_new:(h + 1) * n_new].astype(out_ref.dtype)


def _head_expand(bias_rows, row_head, n_heads):
    ok = jnp.asarray(row_head[:, None] == np.arange(n_heads)[None, :])
    out = jnp.where(ok[:, None, :], bias_rows[..., None], NEG)
    return out.reshape(bias_rows.shape[:-1] + (bias_rows.shape[-1] * n_heads,))


def swa_decode(qn, k_new, v_new, cache_k, cache_v, rel_bias):
    B, Wb, H, dh = cache_k.shape
    T = qn.shape[1]
    R = H * T
    tk = _pick(Wb, (512, 256, 128))
    nblk = Wb // tk
    q2 = jnp.transpose(qn.reshape(B, T, H, dh), (0, 2, 1, 3)).reshape(B, R, dh)
    row_head = np.arange(R) // T
    dist_np = (Wb + np.arange(T))[:, None] - np.arange(Wb)[None, :]
    bias, ok = _swa_bias(rel_bias, dist_np)
    assert ok[:, :tk].any(axis=1).all()
    bm = _head_expand(bias.reshape(R, Wb), row_head, H)
    bm = jnp.transpose(bm.reshape(R, nblk, tk * H), (1, 0, 2))
    dn = np.arange(T)[:, None] - np.arange(T)[None, :]
    bias_n, _ = _swa_bias(rel_bias, dn)
    bmn = _head_expand(bias_n.reshape(R, T), row_head, H)
    kn2 = k_new.reshape(B, T * H, dh)
    vn2 = v_new.reshape(B, T * H, dh)
    return pl.pallas_call(
        functools.partial(_decode_kernel, n_scalar=0, n_chunks=1, diff=False, lam_init=0.0, n_heads=H, n_new=T),
        grid=(B, nblk),
        in_specs=[
            pl.BlockSpec((1, R, dh), lambda b, p: (b, 0, 0)),
            pl.BlockSpec((1, tk, H, dh), lambda b, p: (b, p, 0, 0)),
            pl.BlockSpec((1, tk, H, dh), lambda b, p: (b, p, 0, 0)),
            pl.BlockSpec((1, R, tk * H), lambda b, p: (p, 0, 0)),
            pl.BlockSpec((1, T * H, dh), lambda b, p: (b, 0, 0)),
            pl.BlockSpec((1, T * H, dh), lambda b, p: (b, 0, 0)),
            pl.BlockSpec((R, T * H), lambda b, p: (0, 0)),
        ],
        out_specs=pl.BlockSpec((1, T, H * dh), lambda b, p: (b, 0, 0)),
        out_shape=jax.ShapeDtypeStruct((B, T, H * dh), BF16),
        scratch_shapes=[pltpu.VMEM((R, 1), F32), pltpu.VMEM((R, 1), F32), pltpu.VMEM((R, dh), F32)],
        compiler_params=_cparams(("arbitrary", "arbitrary")),
        name="swa_decode",
    )(q2, cache_k, cache_v, bm, kn2, vn2, bmn)


def _np_rel_bucket(dist):
    exact = REL_BUCKETS // 2
    d_f = np.maximum(dist, 1).astype(np.float32)
    large = exact + (np.log(d_f / np.float32(exact)) / np.float32(math.log(REL_MAX_DIST / exact))
                     * np.float32(REL_BUCKETS - exact)).astype(np.int32)
    large = np.minimum(large, REL_BUCKETS - 1)
    return np.where(dist < exact, dist, large)


def diff_paged(qn, k_new, v_new, cache_k, cache_v, page_table, rel_bias, lam4, g_out, lam_init):
    B, T, H, _, dh = qn.shape
    dv = v_new.shape[-1]
    n_pages = page_table.shape[1]
    page = cache_k.shape[1]
    P = n_pages * page
    R = 2 * H * T
    pp = next(c for c in (8, 4, 2, 1) if n_pages % c == 0)
    qt = jnp.transpose(qn, (0, 3, 2, 1, 4))
    zeros = jnp.zeros_like(qt[:, 0])
    qm = jnp.concatenate([jnp.concatenate([qt[:, 0], zeros], axis=-1),
                          jnp.concatenate([zeros, qt[:, 1]], axis=-1)], axis=1)
    qm = qm.reshape(B, R, 2 * dh)
    pg = np.arange(n_pages)[:, None, None]
    tt = np.arange(T)[None, :, None]
    ii = np.arange(page)[None, None, :]
    dist_np = P + tt - pg * page - ii
    bucket_np = _np_rel_bucket(dist_np)
    far = np.all(bucket_np == bucket_np[0:1], axis=(1, 2))
    near_pages = [int(x) for x in np.nonzero(~far)[0]]
    tile_pages = [0] + near_pages
    tile_of_page = np.zeros((n_pages,), np.int32)
    for ti, pgi in enumerate(near_pages):
        tile_of_page[pgi] = ti + 1
    nt = len(tile_pages)
    bias_t = _bias_of_dist(rel_bias, jnp.asarray(dist_np[tile_pages], jnp.int32))
    bias_rows = jnp.transpose(bias_t.reshape(2, H, nt, T, page), (2, 0, 1, 3, 4)).reshape(nt, R, page)
    row_head = (np.arange(R) // T) % H
    bm = _head_expand(bias_rows, row_head, H)
    dn = np.arange(T)[:, None] - np.arange(T)[None, :]
    bias_n, _ = _causal_bias(rel_bias, dn)
    bmn = _head_expand(bias_n.reshape(R, T), row_head, H)
    kn2 = k_new.reshape(B, T * H, 2 * dh)
    vn2 = v_new.reshape(B, T * H, dv)

    def kv_spec(j, d):
        return pl.BlockSpec((1, page, H, d), lambda b, p, pt, tl: (pt[b * n_pages + p * pp + j], 0, 0, 0))

    def bm_spec(j):
        return pl.BlockSpec((1, R, page * H), lambda b, p, pt, tl: (tl[p * pp + j], 0, 0))

    grid_spec = pltpu.PrefetchScalarGridSpec(
        num_scalar_prefetch=2,
        grid=(B, n_pages // pp),
        in_specs=(
            [pl.BlockSpec((1, R, 2 * dh), lambda b, p, pt, tl: (b, 0, 0))]
            + [kv_spec(j, 2 * dh) for j in range(pp)]
            + [kv_spec(j, dv) for j in range(pp)]
            + [bm_spec(j) for j in range(pp)]
            + [pl.BlockSpec((1, T * H, 2 * dh), lambda b, p, pt, tl: (b, 0, 0)),
               pl.BlockSpec((1, T * H, dv), lambda b, p, pt, tl: (b, 0, 0)),
               pl.BlockSpec((R, T * H), lambda b, p, pt, tl: (0, 0)),
               pl.BlockSpec(lam4.shape, lambda b, p, pt, tl: (0, 0)),
               pl.BlockSpec((1, dv), lambda b, p, pt, tl: (0, 0))]),
        out_specs=pl.BlockSpec((1, T, H * dv), lambda b, p, pt, tl: (b, 0, 0)),
        scratch_shapes=[pltpu.VMEM((R, 1), F32), pltpu.VMEM((R, 1), F32), pltpu.VMEM((R, dv), F32)],
    )
    return pl.pallas_call(
        functools.partial(_decode_kernel, n_scalar=2, n_chunks=pp, diff=True, lam_init=lam_init,
                          n_heads=H, n_new=T),
        grid_spec=grid_spec,
        out_shape=jax.ShapeDtypeStruct((B, T, H * dv), BF16),
        compiler_params=_cparams(("arbitrary", "arbitrary")),
        name="diff_paged",
    )(page_table.reshape(-1), jnp.asarray(tile_of_page), qm, *([cache_k] * pp), *([cache_v] * pp),
      *([bm] * pp), kn2, vn2, bmn, lam4, g_out.reshape(1, dv))


def _rglru_kernel(y_ref, x_ref, cs_ref, h0_ref, cw_ref, cb_ref, wa_ref, ba_ref, wx_ref, bx_ref, lam_ref,
                  out_ref, conv_ref, hl_ref, xpad, hcar, *, l_valid):
    i = pl.program_id(1)
    T, D = x_ref.shape[1], x_ref.shape[2]

    @pl.when(i == 0)
    def _():
        xpad[0:8, :] = cs_ref[0]
        hcar[...] = h0_ref[0]

    xpad[8:8 + T, :] = x_ref[0]
    conv = cb_ref[...] + cw_ref[CONV_W - 1:CONV_W, :] * xpad[8:8 + T, :]
    for j in range(CONV_W - 1):
        s = CONV_W - 1 - j
        conv = conv + cw_ref[j:j + 1, :] * xpad[8 - s:8 - s + T, :]
    nb = wa_ref.shape[0]
    bs = D // nb
    r_parts, i_parts = [], []
    for n in range(nb):
        xb = conv[:, n * bs:(n + 1) * bs]
        r_parts.append(_bdot(xb, wa_ref[n]))
        i_parts.append(_bdot(xb, wx_ref[n]))
    r = jax.nn.sigmoid(jnp.concatenate(r_parts, axis=-1) + ba_ref[...])
    ig = jax.nn.sigmoid(jnp.concatenate(i_parts, axis=-1) + bx_ref[...])
    log_a = -LRU_C * r * _softplus(-lam_ref[...])
    a = jnp.exp(log_a)
    bb = jnp.sqrt(-jnp.tanh(log_a) * (1.0 + a * a)) * ig * conv
    rowid = lax.broadcasted_iota(jnp.int32, (T, D), 0)
    s = 1
    while s < T:
        keep = rowid >= s
        a_sh = pltpu.roll(a, s, 0)
        b_sh = pltpu.roll(bb, s, 0)
        bb = jnp.where(keep, a * b_sh + bb, bb)
        a = jnp.where(keep, a * a_sh, a)
        s *= 2
    hs = a * hcar[...] + bb
    out_ref[0] = (hs * _gelu(y_ref[0])).astype(out_ref.dtype)
    hcar[...] = hs[T - 1:T, :]
    tail = xpad[T:T + 8, :]
    xpad[0:8, :] = tail

    @pl.when(i == pl.num_programs(1) - 1)
    def _():
        hl_ref[0] = hs[l_valid - 1:l_valid, :]
        conv_ref[0] = xpad[8 + l_valid - (CONV_W - 1):8 + l_valid, :]


def rglru_core(z, conv_state, h0, conv_w, conv_b, w_a, b_a, w_x, b_x, lam, l_valid_last, tblk):
    B, L, D2 = z.shape
    D = D2 // 2
    cs8 = jnp.concatenate([jnp.zeros((B, 8 - (CONV_W - 1), D), F32), conv_state.astype(F32)], axis=1)
    vec = lambda a: a.reshape(1, D)
    out, conv_new, h_last = pl.pallas_call(
        functools.partial(_rglru_kernel, l_valid=l_valid_last),
        grid=(B, L // tblk),
        in_specs=[
            pl.BlockSpec((1, tblk, D), lambda b, i: (b, i, 0)),
            pl.BlockSpec((1, tblk, D), lambda b, i: (b, i, 1)),
            pl.BlockSpec((1, 8, D), lambda b, i: (b, 0, 0)),
            pl.BlockSpec((1, 1, D), lambda b, i: (b, 0, 0)),
            pl.BlockSpec((CONV_W, D), lambda b, i: (0, 0)),
            pl.BlockSpec((1, D), lambda b, i: (0, 0)),
            pl.BlockSpec(w_a.shape, lambda b, i: (0, 0, 0)),
            pl.BlockSpec((1, D), lambda b, i: (0, 0)),
            pl.BlockSpec(w_x.shape, lambda b, i: (0, 0, 0)),
            pl.BlockSpec((1, D), lambda b, i: (0, 0)),
            pl.BlockSpec((1, D), lambda b, i: (0, 0)),
        ],
        out_specs=[
            pl.BlockSpec((1, tblk, D), lambda b, i: (b, i, 0)),
            pl.BlockSpec((1, CONV_W - 1, D), lambda b, i: (b, 0, 0)),
            pl.BlockSpec((1, 1, D), lambda b, i: (b, 0, 0)),
        ],
        out_shape=[
            jax.ShapeDtypeStruct((B, L, D), BF16),
            jax.ShapeDtypeStruct((B, CONV_W - 1, D), F32),
            jax.ShapeDtypeStruct((B, 1, D), F32),
        ],
        scratch_shapes=[pltpu.VMEM((tblk + 8, D), F32), pltpu.VMEM((1, D), F32)],
        compiler_params=_cparams(("arbitrary", "arbitrary")),
        name="rglru",
    )(z, z, cs8, h0.astype(F32).reshape(B, 1, D), conv_w, vec(conv_b), w_a.astype(BF16), vec(b_a),
      w_x.astype(BF16), vec(b_x), vec(lam))
    return out, conv_new, h_last.reshape(B, D)


def _argmax_rows(s):
    n, tb = s.shape
    row8 = lax.broadcasted_iota(jnp.int32, (8, tb), 0).astype(F32)
    vals = [s[i:i + 8] for i in range(0, n, 8)]
    rows = [row8 + float(i) for i in range(0, n, 8)]
    while len(vals) > 1:
        nv, nr = [], []
        for a in range(0, len(vals) - 1, 2):
            first = vals[a] >= vals[a + 1]
            nv.append(jnp.maximum(vals[a], vals[a + 1]))
            nr.append(jnp.where(first, rows[a], rows[a + 1]))
        if len(vals) % 2:
            nv.append(vals[-1])
            nr.append(rows[-1])
        vals, rows = nv, nr
    v, r = vals[0], rows[0]
    m = jnp.max(v, axis=0, keepdims=True)
    return m, jnp.min(jnp.where(v == m, r, float(n)), axis=0, keepdims=True)


def _topk_rows(s, k):
    n, tb = s.shape
    rowf = lax.broadcasted_iota(jnp.int32, (n, tb), 0).astype(F32)
    vals, idxs = [], []
    for _ in range(k):
        m, idx = _argmax_rows(s)
        vals.append(m)
        idxs.append(idx)
        s = jnp.where(rowf == idx, -jnp.inf, s)
    return jnp.concatenate(vals, axis=0), jnp.concatenate(idxs, axis=0)


def _pair_plan(k):
    full, small = [], []
    a = 0
    while a < k and k // (a + 1) > 1:
        nb, b0 = k // (a + 1), 0
        while nb - b0 >= 8:
            full.append((a, b0))
            b0 += 8
        if nb > b0:
            small.append((a, b0, nb - b0))
        a += 1
    a0 = a
    assert k % 8 == 0 and (k - a0) % 8 == 0 and (k & (k - 1)) == 0
    bins = []
    for a_, b0, n in sorted(small, key=lambda p: -p[2]):
        for bn in bins:
            used = sum(p[3] for p in bn)
            if used + n <= 8:
                bn.append((a_, b0, used, n))
                break
        else:
            bins.append([(a_, b0, 0, n)])
    code = []
    for a_, b0 in full:
        code += [a_ * k + b0 + r for r in range(8)]
    for bn in bins:
        rows = [k * k] * 8
        for a_, b0, off, n in bn:
            for r in range(n):
                rows[off + r] = a_ * k + b0 + r
        code += rows
    code += [a_ * k for a_ in range(a0, k)]
    return full, bins, a0, np.array(code, np.float32)


def _route_kernel(*refs, topk, cast_experts):
    if cast_experts:
        h_ref, wq_ref, k1_ref, k2_ref, code_ref, u_ref, v_ref, e1_ref, e2_ref, g_ref, ut_ref, vb_ref = refs
        ut_ref[...] = u_ref[0].T.astype(BF16)
        vb_ref[...] = v_ref[0].astype(BF16)
    else:
        h_ref, wq_ref, k1_ref, k2_ref, code_ref, e1_ref, e2_ref, g_ref = refs
    k = topk
    q = jnp.dot(h_ref[...], wq_ref[...], preferred_element_type=F32)
    dk2 = k1_ref.shape[2]
    tb = q.shape[0]
    nt = (((1,), (1,)), ((), ()))
    s1 = lax.dot_general(k1_ref[0], q[:, :dk2], nt, precision=HIGHEST, preferred_element_type=F32)
    s2 = lax.dot_general(k2_ref[0], q[:, dk2:], nt, precision=HIGHEST, preferred_element_type=F32)
    sv1, si1 = _topk_rows(s1, k)
    sv2, si2 = _topk_rows(s2, k)
    full, bins, a0, _ = _pair_plan(k)
    row8 = lax.broadcasted_iota(jnp.int32, (8, tb), 0)
    groups = [sv1[a:a + 1] + sv2[b0:b0 + 8] for a, b0 in full]
    for bn in bins:
        v = jnp.full((8, tb), -jnp.inf, F32)
        for a, b0, off, n in bn:
            piece = sv1[a:a + 1] + (pltpu.roll(sv2[b0:b0 + 8], off, 0) if off else sv2[b0:b0 + 8])
            v = jnp.where((row8 >= off) & (row8 < off + n), piece, v)
        groups.append(v)
    for c in range(a0, k, 8):
        groups.append(sv1[c:c + 8] + sv2[0:1])
    cand = jnp.concatenate(groups, axis=0)
    code = code_ref[...]
    vals, poss = [], []
    for _ in range(k):
        m = jnp.max(cand, axis=0, keepdims=True)
        pos = jnp.min(jnp.where(cand == m, code, float(k * k)), axis=0, keepdims=True)
        vals.append(m)
        poss.append(pos)
        cand = jnp.where(code == pos, -jnp.inf, cand)
    val = jnp.concatenate(vals, axis=0)
    pos = jnp.concatenate(poss, axis=0)
    ra = jnp.floor(pos * (1.0 / k))
    rb = pos - k * ra
    e1 = jnp.zeros((k, tb), F32)
    e2 = jnp.zeros((k, tb), F32)
    for r in range(k):
        e1 = jnp.where(ra == float(r), si1[r:r + 1], e1)
        e2 = jnp.where(rb == float(r), si2[r:r + 1], e2)
    e1_ref[0] = e1
    e2_ref[0] = e2
    ex = jnp.exp(val - val[0:1])
    g_ref[0] = ex / jnp.sum(ex, axis=0, keepdims=True)


def peer_route(h, wq, keys, experts=None):
    N, D = h.shape
    nh, _, nk, dk2 = keys.shape
    tb = _pick(N, (2048, 1024, 512, 256, 128))
    nblk = N // tb
    code = _pair_plan(TOPK_P)[3]
    code = jnp.asarray(np.broadcast_to(code[:, None], (code.shape[0], tb)))
    shp = jax.ShapeDtypeStruct((nh, TOPK_P, N), F32)
    o_spec = pl.BlockSpec((1, TOPK_P, tb), lambda hh, i: (hh, 0, i))
    in_specs = [pl.BlockSpec((tb, D), lambda hh, i: (i, 0)),
                pl.BlockSpec((D, 2 * dk2), lambda hh, i: (0, hh)),
                pl.BlockSpec((1, nk, dk2), lambda hh, i: (2 * hh, 0, 0)),
                pl.BlockSpec((1, nk, dk2), lambda hh, i: (2 * hh + 1, 0, 0)),
                pl.BlockSpec(code.shape, lambda hh, i: (0, 0))]
    args = [h, wq, keys.reshape(nh * 2, nk, dk2), keys.reshape(nh * 2, nk, dk2), code]
    out_specs, out_shape = [o_spec, o_spec, o_spec], [shp, shp, shp]
    if experts is not None:
        u_all, v_all, layer = experts
        E = u_all.shape[1]
        rb = E // (nh * nblk)
        assert rb * nh * nblk == E and rb % LANES == 0
        in_specs += [pl.BlockSpec((1, rb, D), lambda hh, i: (layer, hh * nblk + i, 0))] * 2
        args += [u_all, v_all]
        out_specs += [pl.BlockSpec((D, rb), lambda hh, i: (0, hh * nblk + i)),
                      pl.BlockSpec((rb, D), lambda hh, i: (hh * nblk + i, 0))]
        out_shape += [jax.ShapeDtypeStruct((D, E), BF16), jax.ShapeDtypeStruct((E, D), BF16)]
    return pl.pallas_call(
        functools.partial(_route_kernel, topk=TOPK_P, cast_experts=experts is not None),
        grid=(nh, nblk),
        in_specs=in_specs,
        out_specs=out_specs,
        out_shape=out_shape,
        compiler_params=_cparams(("arbitrary", "arbitrary")),
        name="peer_route",
    )(*args)


def _route_build_kernel(e1_ref, e2_ref, g_ref, sub_ref, o_ref, *, nkeys):
    sub = sub_ref[...][None]
    one = jnp.ones((), BF16)
    zero = jnp.zeros((), BF16)
    at = jnp.where(e1_ref[...].astype(BF16) == sub, one, zero)
    bt = jnp.where(e2_ref[...].astype(BF16) == sub, g_ref[...].astype(BF16), zero)
    g3 = jnp.einsum('tik,tjk->tij', at, bt, preferred_element_type=F32)
    gt = pltpu.einshape('tij->itj', g3)
    for i in range(nkeys):
        o_ref[:, i * nkeys:(i + 1) * nkeys] = gt[i].astype(o_ref.dtype)


def peer_build(e1, e2, g, nkeys):
    nh, k, N = e1.shape
    ks = nh * k
    slot = lambda a: jnp.transpose(a, (2, 0, 1)).reshape(N, 1, ks)
    tb = _pick(N, (256, 128, 64, 32))
    spec = pl.BlockSpec((tb, 1, ks), lambda i: (i, 0, 0))
    sub = jnp.asarray(np.broadcast_to(np.arange(nkeys, dtype=np.float32)[:, None], (nkeys, ks)), BF16)
    return pl.pallas_call(
        functools.partial(_route_build_kernel, nkeys=nkeys),
        grid=(N // tb,),
        in_specs=[spec, spec, spec, pl.BlockSpec((nkeys, ks), lambda i: (0, 0))],
        out_specs=pl.BlockSpec((tb, nkeys * nkeys), lambda i: (i, 0)),
        out_shape=jax.ShapeDtypeStruct((N, nkeys * nkeys), BF16),
        compiler_params=_cparams(("arbitrary",)),
        name="peer_build",
    )(slot(e1), slot(e2), slot(g), sub)


def _peer_kernel(*refs, with_next, with_rider):
    if with_next:
        x_ref, ut_ref, v_ref, g_ref, r_ref, gt_ref, gn_ref, sh_ref, sc_ref = refs[:9]
        refs = refs[9:]
    else:
        x_ref, ut_ref, v_ref, g_ref, r_ref, gt_ref = refs[:6]
        refs = refs[6:]
    if with_rider:
        n_in = 6 if with_next else 4
        rider_in, refs = refs[:n_in], refs[n_in:]
    n_out = 2 if with_next else 1
    outs, refs = refs[:n_out], refs[n_out:]
    if with_rider:
        rider_out, refs = refs[:n_out], refs[n_out:]
    acc_s = refs[0]
    i = pl.program_id(0)
    e = pl.program_id(1)
    last = pl.num_programs(1) - 1

    def block(x_ref, g_ref, r_ref, gt_ref, shsc, out_refs, acc):
        @pl.when(e == 0)
        def _():
            acc[...] = jnp.zeros(acc.shape, F32)

        s = jnp.dot(x_ref[...], ut_ref[0], preferred_element_type=F32)
        p = (_gelu(s) * g_ref[...].astype(F32)).astype(BF16)
        acc[...] += jnp.dot(p, v_ref[0], preferred_element_type=F32)

        @pl.when(e == last)
        def _():
            xn = r_ref[...] + gt_ref[0] * acc[...]
            out_refs[0][...] = xn
            if with_next:
                out_refs[1][...] = _modulated(xn, gn_ref, *shsc).astype(out_refs[1].dtype)

    block(x_ref, g_ref, r_ref, gt_ref, (sh_ref, sc_ref) if with_next else None, outs, acc_s)
    if with_rider:
        pl.when(i == 0)(lambda: block(rider_in[0], rider_in[1], rider_in[2], rider_in[3],
                                      rider_in[4:6] if with_next else None, rider_out, refs[1]))


def peer_dense(h, ut_all, v_all, layer, G, res, gate, next_mod=None, rider=None):
    N, D = h.shape
    E = ut_all.shape[2]
    tb = _pick(N, (512, 256, 128))
    eb = _pick(E, (1024, 512, 256, 128))

    def param_spec(p):
        if p.shape[1] == 1:
            rows = N // p.shape[0]
            assert rows % tb == 0
            return pl.BlockSpec((1, 1, D), lambda i, e: (i // (rows // tb), 0, 0))
        return pl.BlockSpec((1, tb, D), lambda i, e: (0, i, 0))

    row = pl.BlockSpec((tb, D), lambda i, e: (i, 0))
    in_specs = [row,
                pl.BlockSpec((1, D, eb), lambda i, e: (layer, 0, e)),
                pl.BlockSpec((1, eb, D), lambda i, e: (layer, e, 0)),
                pl.BlockSpec((tb, eb), lambda i, e: (i, e)),
                row, param_spec(gate)]
    args = [h, ut_all, v_all, G, res, gate]
    out_specs, out_shape = [row], [jax.ShapeDtypeStruct((N, D), F32)]
    if next_mod is not None:
        g_norm, shift, scale = next_mod
        in_specs += [pl.BlockSpec((1, D), lambda i, e: (0, 0)), param_spec(shift), param_spec(scale)]
        args += [g_norm.reshape(1, D), shift, scale]
        out_specs.append(row)
        out_shape.append(jax.ShapeDtypeStruct((N, D), BF16))
    scratch = [pltpu.VMEM((tb, D), F32)]
    n_out = len(out_specs)
    if rider is not None:
        N2 = rider[0].shape[0]
        whole = pl.BlockSpec((N2, D), lambda i, e: (0, 0))
        per_row = pl.BlockSpec((1, N2, D), lambda i, e: (0, 0, 0))
        in_specs += [whole, pl.BlockSpec((N2, eb), lambda i, e: (0, e)), whole] + [per_row] * (len(rider) - 3)
        args += list(rider)
        out_specs += [whole] * n_out
        out_shape += [jax.ShapeDtypeStruct((N2, D), F32), jax.ShapeDtypeStruct((N2, D), BF16)][:n_out]
        scratch.append(pltpu.VMEM((N2, D), F32))
    outs = pl.pallas_call(
        functools.partial(_peer_kernel, with_next=next_mod is not None, with_rider=rider is not None),
        grid=(N // tb, E // eb),
        in_specs=in_specs,
        out_specs=out_specs,
        out_shape=out_shape,
        scratch_shapes=scratch,
        compiler_params=_cparams(("arbitrary", "arbitrary")),
        name="peer_dense",
    )(*args)
    outs = list(outs) + [None] * (4 - len(outs))
    if next_mod is None:
        outs = [outs[0], None, outs[1] if rider is not None else None, None]
    return tuple(outs)


def _pad_rows(a, n, value=0.0, axis=1):
    pad = [(0, 0)] * a.ndim
    pad[axis] = (0, n - a.shape[axis])
    return jnp.pad(a, pad, constant_values=value)


def kernel(x_prompt, x_sample, state_mlstm_C, state_mlstm_n, state_mlstm_m, cache_swa_k, cache_swa_v,
           cache_diff_k, cache_diff_v, state_rglru_conv, state_rglru_h, page_table, c_prompt, c_sample,
           w_ada, b_ada, g_norm_mix, g_norm_ffn, rel_bias,
           mlstm_w_in, mlstm_b_gates, mlstm_g_out, mlstm_w_out,
           swa_w_in, swa_g_q, swa_g_k, swa_w_out,
           diff_w_in, diff_g_q, diff_g_k, diff_lam_q1, diff_lam_k1, diff_lam_q2, diff_lam_k2, diff_g_out, diff_w_out,
           rglru_w_in, rglru_conv_w, rglru_conv_b, rglru_w_a, rglru_b_a, rglru_w_x, rglru_b_x, rglru_lambda, rglru_w_out,
           peer_w_q, peer_keys, peer_u, peer_v):
    xp, xs = x_prompt, x_sample
    Bp, S, D = xp.shape
    Bs, T, _ = xs.shape
    depth = w_ada.shape[0]
    Ns = Bs * T

    c_all = jnp.concatenate([c_prompt, c_sample], axis=0)
    mod = ada_all(c_all, w_ada, b_ada)
    mod = mod.reshape(depth, Bp + Bs, 6, 1, D)

    def mods(i, which):
        m = mod[i, :, which]
        return m[:Bp], m[Bp:]

    def rows(p):
        return jnp.broadcast_to(p, (Bs, T, D)).reshape(1, Ns, D)

    nk = peer_keys.shape[3]
    Ns_pad = -(-Ns // LANES) * LANES

    sh_p, sh_s = mods(0, 0)
    sc_p, sc_s = mods(0, 1)
    hp = modulate(xp, g_norm_mix[0], sh_p, sc_p)
    hs = modulate(xs, g_norm_mix[0], sh_s, sc_s)
    for i in range(depth):
        kind = i % 4
        gt_p, gt_s = mods(i, 2)
        hs_flat = hs.reshape(1, Ns, D)
        if kind == 0:
            H = NH_A
            dk = state_mlstm_C.shape[2]
            dv = state_mlstm_C.shape[3]
            nmain = 2 * H * dk + 2 * H * dv
            wb = mlstm_w_in.astype(BF16)
            w_gate = mlstm_w_in[:, nmain:].astype(BF16)
            zp = matmul(hp, wb, cols=(0, nmain))
            gp = matmul(hp, w_gate)
            zs = matmul(hs_flat, wb, cols=(0, nmain)).reshape(Bs, T, nmain)
            gs = matmul(hs_flat, w_gate).reshape(Bs, T, 2 * H)
            chunk_p = _pick(S, (CHUNK_A, 128))
            op, mC_p, mn_p, mm_p = mlstm_core(
                zp, gp, mlstm_b_gates, jnp.zeros((Bp, H, dk, dv), F32), jnp.zeros((Bp, H, dk), F32),
                jnp.full((Bp, H), M_INIT, F32), mlstm_g_out, chunk_p)
            Tp = 128
            zs_pad = _pad_rows(zs, Tp)
            gs_pad = jnp.concatenate([
                _pad_rows(gs[..., :H], Tp, NEG), _pad_rows(gs[..., H:], Tp, -NEG)], axis=-1)
            os_, mC_s, mn_s, mm_s = mlstm_core(
                zs_pad, gs_pad, mlstm_b_gates, state_mlstm_C.astype(F32), state_mlstm_n.astype(F32),
                state_mlstm_m.astype(F32), mlstm_g_out, Tp)
            os_ = os_[:, :T]
            w_out = mlstm_w_out
        elif kind == 1:
            H = NH_B
            dh = D // H
            wb = swa_w_in.astype(BF16)
            gq = jnp.tile(swa_g_q, H)
            gk = jnp.tile(swa_g_k, H)
            qn_p = matmul_headnorm(hp, wb, (0, D), gq, dh, dh ** -0.5 * LOG2E, BF16)
            kn_p = matmul_headnorm(hp, wb, (D, D), gk, dh, 1.0, F32)
            vp = matmul(hp, wb, cols=(2 * D, D))
            qn_s = matmul_headnorm(hs_flat, wb, (0, D), gq, dh, dh ** -0.5, BF16).reshape(Bs, T, D)
            kn_s = matmul_headnorm(hs_flat, wb, (D, D), gk, dh, 1.0, F32).reshape(Bs, T, D)
            vs_new = matmul(hs_flat, wb, cols=(2 * D, D)).reshape(Bs, T, D)
            t = _pick(S, (512, 256, 128))
            assert (_np_multiplicity(_toeplitz_dist_np(S // t, t)) > 0).any(axis=2).all()
            bm_tab = _toeplitz_bias(rel_bias, S // t, t, _swa_bias, LOG2E)
            op = flash_attention(qn_p, kn_p, vp, bm_tab, n_units=H, dqk=dh, dv=dh,
                                 qcol=lambda a: a, kcol=lambda a: a, vcol=lambda a: a, bmap=lambda a: a,
                                 t=t, out_dtype=BF16)
            swa_k_p = kn_p.reshape(Bp, S, H, dh)
            swa_v_p = vp.reshape(Bp, S, H, dh)
            os_ = swa_decode(qn_s, kn_s, vs_new, cache_swa_k, cache_swa_v, rel_bias)
            def shifted(cache, new):
                Wb = cache.shape[1]
                buf = lax.pad(cache.astype(F32), jnp.zeros((), F32), ((0, 0, 0), (-T, T, 0), (0, 0, 0), (0, 0, 0)))
                return lax.dynamic_update_slice(buf, new.reshape(Bs, T, H, dh), (0, Wb - T, 0, 0))

            swa_k_s = shifted(cache_swa_k, kn_s)
            swa_v_s = shifted(cache_swa_v, vs_new)
            w_out = swa_w_out
        elif kind == 2:
            H = NH_C
            dh = D // (2 * H)
            dv = 2 * dh
            lam_init = 0.8 - 0.6 * math.exp(-0.3 * i)
            lam4 = jnp.stack([diff_lam_q1, diff_lam_k1, diff_lam_q2, diff_lam_k2]).astype(F32)
            wb = diff_w_in.astype(BF16)
            gq = jnp.tile(diff_g_q.reshape(-1), H)
            gk = jnp.tile(diff_g_k.reshape(-1), H)
            qn_p = matmul_headnorm(hp, wb, (0, D), gq, dh, dh ** -0.5 * LOG2E, BF16)
            kn_p = matmul_headnorm(hp, wb, (D, D), gk, dh, 1.0, F32)
            vp = matmul(hp, wb, cols=(2 * D, D))
            qn_s = matmul_headnorm(hs_flat, wb, (0, D), gq, dh, dh ** -0.5, BF16).reshape(Bs, T, D)
            kn_s = matmul_headnorm(hs_flat, wb, (D, D), gk, dh, 1.0, F32).reshape(Bs, T, D)
            vs_new = matmul(hs_flat, wb, cols=(2 * D, D)).reshape(Bs, T, D)
            t = _pick(S, (512, 256, 128))
            bm_tab = _toeplitz_bias(rel_bias, S // t, t, _causal_bias, LOG2E)
            o2 = flash_attention(qn_p, kn_p, vp, bm_tab, n_units=2 * H, dqk=dh, dv=dv,
                                 qcol=lambda a: a, kcol=lambda a: a, vcol=lambda a: a // 2,
                                 bmap=lambda a: (a % 2) * H + a // 2, t=t, out_dtype=F32)
            op = diff_combine(o2, lam4, diff_g_out, lam_init)
            diff_k_p = kn_p.reshape(Bp, S, H, 2 * dh)
            diff_v_p = vp.reshape(Bp, S, H, dv)
            diff_k_s = kn_s.reshape(Bs, T, H, 2 * dh)
            diff_v_s = vs_new.reshape(Bs, T, H, dv)
            os_ = diff_paged(qn_s.reshape(Bs, T, H, 2, dh), diff_k_s, diff_v_s, cache_diff_k, cache_diff_v,
                             page_table, rel_bias, lam4, diff_g_out, lam_init)
            w_out = diff_w_out
        else:
            wb = rglru_w_in.astype(BF16)
            zp = matmul(hp, wb)
            zs = matmul(hs_flat, wb).reshape(Bs, T, -1)
            tblk = _pick(S, (256, 128))
            op, conv_p, h_p = rglru_core(zp, jnp.zeros((Bp, CONV_W - 1, D), F32), jnp.zeros((Bp, D), F32),
                                         rglru_conv_w, rglru_conv_b, rglru_w_a, rglru_b_a, rglru_w_x, rglru_b_x,
                                         rglru_lambda, tblk, tblk)
            os_, conv_s, h_s = rglru_core(_pad_rows(zs, 8), state_rglru_conv, state_rglru_h,
                                          rglru_conv_w, rglru_conv_b, rglru_w_a, rglru_b_a, rglru_w_x, rglru_b_x,
                                          rglru_lambda, T, 8)
            os_ = os_[:, :T]
            w_out = rglru_w_out
        sh_p, sh_s = mods(i, 3)
        sc_p, sc_s = mods(i, 4)
        wb = w_out.astype(BF16)
        xp, hp = matmul_res_mod(op, wb, xp, gt_p, g_norm_ffn[i], sh_p, sc_p)
        xs, hs = matmul_res_mod(os_.reshape(1, Ns, -1), wb, xs.reshape(1, Ns, D), rows(gt_s), g_norm_ffn[i],
                                rows(sh_s), rows(sc_s))
        hp = hp.reshape(Bp * S, D)
        hs = hs.reshape(Ns, D)

        gt_p, gt_s = mods(i, 5)
        wq = peer_w_q[i].astype(BF16)
        e1, e2, gw, ut, vb = peer_route(hp, wq, peer_keys[i], experts=(peer_u, peer_v, i))
        Gp = peer_build(e1, e2, gw, nk)
        Gs = peer_build(*peer_route(_pad_rows(hs, Ns_pad, axis=0), wq, peer_keys[i]), nk)[:Ns]
        if i + 1 < depth:
            sh_p, sh_s = mods(i + 1, 0)
            sc_p, sc_s = mods(i + 1, 1)
            next_p = (g_norm_mix[i + 1], sh_p, sc_p)
            next_s = (rows(sh_s), rows(sc_s))
        else:
            next_p, next_s = None, ()
        xp, hp, xs, hs = peer_dense(hp, ut[None], vb[None], 0, Gp, xp.reshape(Bp * S, D), gt_p, next_p,
                                    rider=(hs, Gs, xs.reshape(Ns, D), rows(gt_s)) + next_s)
        xp = xp.reshape(Bp, S, D)
        xs = xs.reshape(Bs, T, D)
        if hp is not None:
            hp = hp.reshape(Bp, S, D)
            hs = hs.reshape(Bs, T, D)

    return (xp, xs, mC_p, mC_s, mn_p, mn_s, mm_p, mm_s, swa_k_p, swa_k_s, swa_v_p, swa_v_s,
            diff_k_p, diff_k_s, diff_v_p, diff_v_s, conv_p, conv_s, h_p, h_s)
```

```python
import functools
import math

import numpy as np
import jax
import jax.numpy as jnp
from jax import lax
from jax.experimental import pallas as pl
from jax.experimental.pallas import tpu as pltpu

F32 = jnp.float32
BF16 = jnp.bfloat16
HIGHEST = lax.Precision.HIGHEST

EPS = 1e-6
NEG = -1e30
VMEM_LIMIT = 56 * 1024 * 1024
LANES = 128
LOG2E = math.log2(math.e)

NH_A, CHUNK_A, M_INIT = 8, 256, -1e30
NH_B = 16
DIL_PAIRS = ((128, 1), (512, 4), (2048, 16))
NH_C = 8
CONV_W, LRU_C = 4, 8.0
TOPK_P = 16
REL_BUCKETS, REL_MAX_DIST = 32, 2048


def _cparams(sem):
    return pltpu.CompilerParams(dimension_semantics=sem, vmem_limit_bytes=VMEM_LIMIT)


def _bdot(a, b):
    return jnp.dot(a.astype(BF16), b.astype(BF16), preferred_element_type=F32)


def _bdot_nt(a, b):
    return lax.dot_general(a.astype(BF16), b.astype(BF16), (((1,), (1,)), ((), ())),
                           preferred_element_type=F32)


def _gelu(x):
    return 0.5 * x * (1.0 + lax.erf(x * (1.0 / math.sqrt(2.0))))


def _log_sigmoid(x):
    return jnp.minimum(x, 0.0) - jnp.log1p(jnp.exp(-jnp.abs(x)))


def _softplus(x):
    return jnp.maximum(x, 0.0) + jnp.log1p(jnp.exp(-jnp.abs(x)))


def _pick(n, pref):
    for t in pref:
        if n % t == 0:
            return t
    return n


def _ada_kernel(c_ref, w_ref, b_ref, o_ref):
    c = c_ref[...]
    a = c * jax.nn.sigmoid(c)
    o_ref[0] = _bdot(a, w_ref[0]) + b_ref[0]


def ada_all(c_all, w_ada, b_ada):
    R, D = c_all.shape
    nl, _, N = w_ada.shape
    tn = _pick(N, (2048, 1024, 512, 256, 128))
    return pl.pallas_call(
        _ada_kernel,
        grid=(nl, N // tn),
        in_specs=[pl.BlockSpec((R, D), lambda l, j: (0, 0)),
                  pl.BlockSpec((1, D, tn), lambda l, j: (l, 0, j)),
                  pl.BlockSpec((1, 1, tn), lambda l, j: (l, 0, j))],
        out_specs=pl.BlockSpec((1, R, tn), lambda l, j: (l, 0, j)),
        out_shape=jax.ShapeDtypeStruct((nl, R, N), F32),
        compiler_params=_cparams(("arbitrary", "arbitrary")),
        name="ada",
    )(c_all, w_ada, b_ada.reshape(nl, 1, N))


def _modulate_kernel(x_ref, g_ref, sh_ref, sc_ref, o_ref):
    x = x_ref[0]
    ms = jnp.mean(x * x, axis=-1, keepdims=True)
    y = x * lax.rsqrt(ms + EPS) * g_ref[...]
    o_ref[0] = (y * (1.0 + sc_ref[0]) + sh_ref[0]).astype(o_ref.dtype)


def modulate(x, g, shift, scale):
    B, L, D = x.shape
    tl = _pick(L, (512, 256, 128))
    return pl.pallas_call(
        _modulate_kernel,
        grid=(B, L // tl),
        in_specs=[pl.BlockSpec((1, tl, D), lambda b, i: (b, i, 0)),
                  pl.BlockSpec((1, D), lambda b, i: (0, 0)),
                  pl.BlockSpec((1, 1, D), lambda b, i: (b, 0, 0)),
                  pl.BlockSpec((1, 1, D), lambda b, i: (b, 0, 0))],
        out_specs=pl.BlockSpec((1, tl, D), lambda b, i: (b, i, 0)),
        out_shape=jax.ShapeDtypeStruct((B, L, D), BF16),
        compiler_params=_cparams(("arbitrary", "arbitrary")),
        name="modulate",
    )(x, g.reshape(1, D), shift, scale)


def _mm_kernel(x_ref, w_ref, o_ref):
    o_ref[0] = jnp.dot(x_ref[0], w_ref[...], preferred_element_type=F32).astype(o_ref.dtype)


def matmul(x, w, out_dtype=F32, cols=None):
    B, L, K = x.shape
    c0, N = cols if cols is not None else (0, w.shape[1])
    tm = _pick(L, (1024, 512, 256, 128))
    tn = next((c for c in (1024, 512, 256, 128) if N % c == 0 and c0 % c == 0), N)
    assert c0 % tn == 0 and (tn == w.shape[1] or tn % LANES == 0)
    j0 = c0 // tn
    grid = (N // tn, B, L // tm)
    x_spec = pl.BlockSpec((1, tm, K), lambda j, b, i: (b, i, 0))
    w_spec = pl.BlockSpec((K, tn), lambda j, b, i: (0, j0 + j))
    o_spec = pl.BlockSpec((1, tm, tn), lambda j, b, i: (b, i, j))
    return pl.pallas_call(
        _mm_kernel, grid=grid, in_specs=[x_spec, w_spec], out_specs=o_spec,
        out_shape=jax.ShapeDtypeStruct((B, L, N), out_dtype),
        compiler_params=_cparams(("arbitrary",) * 3), name="matmul",
    )(x, w)


def _modulated(xn, gn_ref, sh_ref, sc_ref):
    ms = jnp.mean(xn * xn, axis=-1, keepdims=True)
    return xn * lax.rsqrt(ms + EPS) * gn_ref[...] * (1.0 + sc_ref[0]) + sh_ref[0]


def _mm_res_mod_kernel(x_ref, w_ref, r_ref, gt_ref, gn_ref, sh_ref, sc_ref, o_ref, h_ref):
    acc = jnp.dot(x_ref[0], w_ref[...], preferred_element_type=F32)
    xn = r_ref[0] + gt_ref[0] * acc
    o_ref[0] = xn
    h_ref[0] = _modulated(xn, gn_ref, sh_ref, sc_ref).astype(h_ref.dtype)


def _row_param_spec(p, tm, n):
    if p.shape[1] == 1:
        return pl.BlockSpec((1, 1, n), lambda b, i: (b, 0, 0))
    return pl.BlockSpec((1, tm, n), lambda b, i: (b, i, 0))


def matmul_res_mod(x, w, res, gate, g_norm, shift, scale):
    B, L, K = x.shape
    N = w.shape[1]
    tm = _pick(L, (512, 256, 128))
    row = pl.BlockSpec((1, tm, N), lambda b, i: (b, i, 0))
    return pl.pallas_call(
        _mm_res_mod_kernel,
        grid=(B, L // tm),
        in_specs=[pl.BlockSpec((1, tm, K), lambda b, i: (b, i, 0)),
                  pl.BlockSpec((K, N), lambda b, i: (0, 0)),
                  row, _row_param_spec(gate, tm, N),
                  pl.BlockSpec((1, N), lambda b, i: (0, 0)),
                  _row_param_spec(shift, tm, N), _row_param_spec(scale, tm, N)],
        out_specs=[row, row],
        out_shape=[jax.ShapeDtypeStruct((B, L, N), F32), jax.ShapeDtypeStruct((B, L, N), BF16)],
        compiler_params=_cparams(("arbitrary", "arbitrary")),
        name="matmul_res_mod",
    )(x, w, res, gate, g_norm.reshape(1, N), shift, scale)


def _mm_headnorm_kernel(x_ref, w_ref, g_ref, o_ref, *, dh, scale):
    acc = jnp.dot(x_ref[0], w_ref[...], preferred_element_type=F32)
    for h in range(acc.shape[1] // dh):
        sl = slice(h * dh, (h + 1) * dh)
        a = acc[:, sl]
        ms = jnp.mean(a * a, axis=-1, keepdims=True)
        y = a * lax.rsqrt(ms + EPS) * g_ref[:, sl]
        o_ref[0, :, sl] = (y * scale if scale != 1.0 else y).astype(o_ref.dtype)


def matmul_headnorm(x, w, cols, gain_full, dh, scale, out_dtype):
    B, L, K = x.shape
    c0, N = cols
    tm = _pick(L, (1024, 512, 256, 128))
    tn = next(c for c in (1024, 512, 256, 128) if N % c == 0 and c0 % c == 0)
    j0 = c0 // tn
    return pl.pallas_call(
        functools.partial(_mm_headnorm_kernel, dh=dh, scale=scale),
        grid=(N // tn, B, L // tm),
        in_specs=[pl.BlockSpec((1, tm, K), lambda j, b, i: (b, i, 0)),
                  pl.BlockSpec((K, tn), lambda j, b, i: (0, j0 + j)),
                  pl.BlockSpec((1, tn), lambda j, b, i: (0, j))],
        out_specs=pl.BlockSpec((1, tm, tn), lambda j, b, i: (b, i, j)),
        out_shape=jax.ShapeDtypeStruct((B, L, N), out_dtype),
        compiler_params=_cparams(("arbitrary",) * 3),
        name="matmul_headnorm",
    )(x, w, gain_full.reshape(1, N))


def _mlstm_kernel(q_ref, k_ref, v_ref, o_ref, gc_ref, gr_ref, bc_ref, br_ref,
                  C0_ref, n0_ref, m0_ref, gout_ref,
                  hs_ref, C_ref, n_ref, m_ref, C_s, n_s, m_s, *, dk, dv):
    ci = pl.program_id(2)
    hp = C_s.shape[0]

    @pl.when(ci == 0)
    def _():
        C_s[...] = C0_ref[0]
        n_s[...] = n0_ref[0]
        m_s[...] = m0_ref[0]

    c = q_ref.shape[1]
    row = lax.broadcasted_iota(jnp.int32, (c, c), 0)
    col = lax.broadcasted_iota(jnp.int32, (c, c), 1)
    causal = col <= row
    for hh in range(hp):
        q = q_ref[0, :, hh * dk:(hh + 1) * dk]
        k = k_ref[0, :, hh * dk:(hh + 1) * dk] * (dk ** -0.5)
        v = v_ref[0, :, hh * dv:(hh + 1) * dv]
        gc = gc_ref[0, hh] + bc_ref[hh]
        gr = gr_ref[0, hh] + br_ref[hh]
        li_c, lf_c = gc[:, 0:1], _log_sigmoid(gc[:, 1:2])
        li_r, lf_r = gr[0:1, :], _log_sigmoid(gr[1:2, :])
        b_c = jnp.sum(jnp.where(causal, lf_r, 0.0), axis=1, keepdims=True)
        b_r = jnp.sum(jnp.where(row <= col, lf_c, 0.0), axis=0, keepdims=True)
        m_prev = m_s[hh]
        Dm = jnp.where(causal, b_c - b_r + li_r, NEG)
        m_t = jnp.maximum(b_c + m_prev, jnp.max(Dm, axis=1, keepdims=True))
        S = _bdot_nt(q, k) * jnp.exp(Dm - m_t)
        inter = jnp.exp(b_c + m_prev - m_t)
        C = C_s[hh]
        n = n_s[hh]
        num = _bdot(S, v) + inter * _bdot(q, C)
        den = jnp.sum(S, axis=1, keepdims=True) + inter * jnp.sum(q * n, axis=1, keepdims=True)
        h = num / jnp.maximum(jnp.abs(den), jnp.exp(-m_t))
        sl = slice(hh * dv, (hh + 1) * dv)
        hg = jax.nn.sigmoid(o_ref[0, :, sl]) * h
        ms = jnp.mean(hg * hg, axis=-1, keepdims=True)
        hs_ref[0, :, sl] = (hg * lax.rsqrt(ms + EPS) * gout_ref[:, sl]).astype(hs_ref.dtype)
        m_new = m_t[c - 1:c, :]
        b_last = b_c[c - 1:c, :]
        w_r = jnp.exp(b_last - b_r + li_r - m_new)
        w_c = jnp.exp(b_last - b_c + li_c - m_new)
        decay = jnp.exp(b_last + m_prev - m_new)
        C_s[hh] = decay * C + _bdot(k.T, w_c * v)
        n_s[hh] = decay * n + jnp.dot(w_r, k, preferred_element_type=F32, precision=HIGHEST)
        m_s[hh] = m_new

    @pl.when(ci == pl.num_programs(2) - 1)
    def _():
        C_ref[0] = C_s[...]
        n_ref[0] = n_s[...]
        m_ref[0] = m_s[...]


def mlstm_core(z, g, b_gates, C0, n0, m0, g_out, chunk):
    B, L, _ = z.shape
    H = NH_A
    dk = C0.shape[2]
    dv = C0.shape[3]
    nc = L // chunk
    g4 = g.reshape(B, L, 2, H)
    gcol = jnp.transpose(g4, (0, 3, 1, 2))
    grow = jnp.transpose(g4, (0, 3, 2, 1))
    bg = b_gates.reshape(2, H)
    bcol = jnp.transpose(bg, (1, 0)).reshape(H, 1, 2)
    brow = jnp.transpose(bg, (1, 0)).reshape(H, 2, 1)
    hp = 2 if H % 2 == 0 else 1
    wk, wv = hp * dk, hp * dv
    kq = (H * dk) // wk
    vo = (2 * H * dk) // wv
    oo = vo + H // hp
    outs = pl.pallas_call(
        functools.partial(_mlstm_kernel, dk=dk, dv=dv),
        grid=(B, H // hp, nc),
        in_specs=[
            pl.BlockSpec((1, chunk, wk), lambda b, h, c: (b, c, h)),
            pl.BlockSpec((1, chunk, wk), lambda b, h, c: (b, c, kq + h)),
            pl.BlockSpec((1, chunk, wv), lambda b, h, c: (b, c, vo + h)),
            pl.BlockSpec((1, chunk, wv), lambda b, h, c: (b, c, oo + h)),
            pl.BlockSpec((1, hp, chunk, 2), lambda b, h, c: (b, h, c, 0)),
            pl.BlockSpec((1, hp, 2, chunk), lambda b, h, c: (b, h, 0, c)),
            pl.BlockSpec((hp, 1, 2), lambda b, h, c: (h, 0, 0)),
            pl.BlockSpec((hp, 2, 1), lambda b, h, c: (h, 0, 0)),
            pl.BlockSpec((1, hp, dk, dv), lambda b, h, c: (b, h, 0, 0)),
            pl.BlockSpec((1, hp, 1, dk), lambda b, h, c: (b, h, 0, 0)),
            pl.BlockSpec((1, hp, 1, 1), lambda b, h, c: (b, h, 0, 0)),
            pl.BlockSpec((1, wv), lambda b, h, c: (0, h)),
        ],
        out_specs=[
            pl.BlockSpec((1, chunk, wv), lambda b, h, c: (b, c, h)),
            pl.BlockSpec((1, hp, dk, dv), lambda b, h, c: (b, h, 0, 0)),
            pl.BlockSpec((1, hp, 1, dk), lambda b, h, c: (b, h, 0, 0)),
            pl.BlockSpec((1, hp, 1, 1), lambda b, h, c: (b, h, 0, 0)),
        ],
        out_shape=[
            jax.ShapeDtypeStruct((B, L, H * dv), BF16),
            jax.ShapeDtypeStruct((B, H, dk, dv), F32),
            jax.ShapeDtypeStruct((B, H, 1, dk), F32),
            jax.ShapeDtypeStruct((B, H, 1, 1), F32),
        ],
        scratch_shapes=[pltpu.VMEM((hp, dk, dv), F32), pltpu.VMEM((hp, 1, dk), F32), pltpu.VMEM((hp, 1, 1), F32)],
        compiler_params=_cparams(("arbitrary",) * 3),
        name="mlstm",
    )(z, z, z, z, gcol, grow, bcol, brow, C0, n0.reshape(B, H, 1, dk), m0.reshape(B, H, 1, 1),
      g_out.reshape(1, H * dv))
    hs, C, n, m = outs
    return hs, C, n.reshape(B, H, dk), m.reshape(B, H)


def _rel_bucket(dist):
    exact = REL_BUCKETS // 2
    d_f = jnp.maximum(dist, 1).astype(F32)
    large = exact + (jnp.log(d_f / exact) / math.log(REL_MAX_DIST / exact) * (REL_BUCKETS - exact)).astype(jnp.int32)
    large = jnp.minimum(large, REL_BUCKETS - 1)
    return jnp.where(dist < exact, dist, large)


def _bias_of_dist(rel_bias, dist):
    bucket = _rel_bucket(jnp.maximum(dist, 0))
    out = jnp.zeros((rel_bias.shape[1],) + dist.shape, F32)
    for b in range(REL_BUCKETS):
        out = jnp.where(bucket[None] == b, rel_bias[b].reshape((-1,) + (1,) * dist.ndim), out)
    return out


def _np_multiplicity(dist):
    cnt = np.zeros(dist.shape, np.int64)
    for w, d in DIL_PAIRS:
        cnt = cnt + ((dist % d == 0) & (dist <= w) & (dist >= 0))
    return cnt


def _swa_bias(rel_bias, dist_np):
    cnt = _np_multiplicity(dist_np)
    logc = jnp.log(jnp.asarray(np.maximum(cnt, 1), F32))
    bias = _bias_of_dist(rel_bias, jnp.asarray(np.maximum(dist_np, 0), jnp.int32))
    return jnp.where(jnp.asarray(cnt > 0)[None], bias + logc[None], NEG), cnt > 0


def _causal_bias(rel_bias, dist_np):
    bias = _bias_of_dist(rel_bias, jnp.asarray(np.maximum(dist_np, 0), jnp.int32))
    return jnp.where(jnp.asarray(dist_np >= 0)[None], bias, NEG), dist_np >= 0


def _toeplitz_dist_np(n_off, t):
    o = np.arange(n_off)[:, None, None]
    i = np.arange(t)[None, :, None]
    j = np.arange(t)[None, None, :]
    return o * t + i - j


def _toeplitz_kernel(b_ref, o_ref):
    t = o_ref.shape[2]
    rows = jnp.broadcast_to(b_ref[0, 0], (t, 2 * t))
    skew = pltpu.roll(rows, 0, 1, stride=1, stride_axis=0)
    o_ref[0, 0] = skew[:, t:]


def _toeplitz_bias(rel_bias, n_off, t, bias_fn, mult):
    dist = np.arange(-(t - 1), n_off * t)
    f, _ = bias_fn(rel_bias, dist)
    f = f * mult
    maps = f.shape[0]
    want = np.arange(n_off)[:, None] * t + t - np.arange(2 * t)[None, :]
    idx = np.minimum(want + (t - 1), len(dist) - 1)
    b = f[:, idx].reshape(maps, n_off, 1, 2 * t)
    return pl.pallas_call(
        _toeplitz_kernel,
        grid=(maps, n_off),
        in_specs=[pl.BlockSpec((1, 1, 1, 2 * t), lambda m, o: (m, o, 0, 0))],
        out_specs=pl.BlockSpec((1, 1, t, t), lambda m, o: (m, o, 0, 0)),
        out_shape=jax.ShapeDtypeStruct((maps, n_off, t, t), F32),
        compiler_params=_cparams(("arbitrary", "arbitrary")),
        name="toeplitz_bias",
    )(b)


def _flash_kernel(qi_ref, ki_ref, q_ref, k_ref, v_ref, bm_ref, o_ref, m_s, l_s, acc_s):
    p = pl.program_id(1)
    qi = qi_ref[p]
    ki = ki_ref[p]
    nb, tq, dv = acc_s.shape
    tk = k_ref.shape[1]

    @pl.when(ki == 0)
    def _():
        m_s[...] = jnp.full(m_s.shape, NEG, F32)
        l_s[...] = jnp.zeros(l_s.shape, F32)
        acc_s[...] = jnp.zeros(acc_s.shape, F32)

    bm = bm_ref[0, 0]
    for b in range(nb):
        s = _bdot_nt(q_ref[b], k_ref[b]) + bm
        m_prev = m_s[b]
        m_next = jnp.maximum(m_prev, jnp.max(s, axis=1, keepdims=True))
        pr = jnp.exp2(s - jnp.tile(m_next, (1, tk // LANES)))
        alpha = jnp.exp2(m_prev - m_next)
        l_s[b] = alpha * l_s[b] + jnp.sum(pr, axis=1, keepdims=True)
        acc_s[b] = acc_s[b] * jnp.tile(alpha, (1, dv // LANES)) + _bdot(pr, v_ref[b])
        m_s[b] = m_next

    @pl.when(ki == qi)
    def _():
        for b in range(nb):
            o_ref[b] = (acc_s[b] / jnp.tile(l_s[b], (1, dv // LANES))).astype(o_ref.dtype)


def flash_attention(q, k, v, bm_tab, *, n_units, dqk, dv, qcol, kcol, vcol, bmap, t, out_dtype):
    B, L = q.shape[:2]
    nq = L // t
    pairs = [(qi, ki) for qi in range(nq) for ki in range(qi + 1)]
    qi_arr = jnp.asarray(np.array([p[0] for p in pairs], np.int32))
    ki_arr = jnp.asarray(np.array([p[1] for p in pairs], np.int32))
    grid_spec = pltpu.PrefetchScalarGridSpec(
        num_scalar_prefetch=2,
        grid=(n_units, len(pairs)),
        in_specs=[
            pl.BlockSpec((B, t, dqk), lambda a, p, qa, ka: (0, qa[p], qcol(a))),
            pl.BlockSpec((B, t, dqk), lambda a, p, qa, ka: (0, ka[p], kcol(a))),
            pl.BlockSpec((B, t, dv), lambda a, p, qa, ka: (0, ka[p], vcol(a))),
            pl.BlockSpec((1, 1, t, t), lambda a, p, qa, ka: (bmap(a), qa[p] - ka[p], 0, 0)),
        ],
        out_specs=pl.BlockSpec((B, t, dv), lambda a, p, qa, ka: (0, qa[p], a)),
        scratch_shapes=[pltpu.VMEM((B, t, LANES), F32), pltpu.VMEM((B, t, LANES), F32),
                        pltpu.VMEM((B, t, dv), F32)],
    )
    return pl.pallas_call(
        _flash_kernel,
        grid_spec=grid_spec,
        out_shape=jax.ShapeDtypeStruct((B, L, n_units * dv), out_dtype),
        compiler_params=_cparams(("arbitrary", "arbitrary")),
        name="flash_attention",
    )(qi_arr, ki_arr, q, k, v, bm_tab)


def _diff_lambda(lam_ref, lam_init):
    lq1, lk1, lq2, lk2 = lam_ref[0:1, :], lam_ref[1:2, :], lam_ref[2:3, :], lam_ref[3:4, :]
    return (jnp.exp(jnp.sum(lq1 * lk1, axis=-1, keepdims=True))
            - jnp.exp(jnp.sum(lq2 * lk2, axis=-1, keepdims=True)) + lam_init)


def _diff_combine_kernel(o_ref, lam_ref, g_ref, out_ref, *, dv, lam_init):
    lam = _diff_lambda(lam_ref, lam_init)
    nh = out_ref.shape[2] // dv
    for h in range(nh):
        o0 = o_ref[0, :, (2 * h) * dv:(2 * h + 1) * dv]
        o1 = o_ref[0, :, (2 * h + 1) * dv:(2 * h + 2) * dv]
        d = o0 - lam * o1
        ms = jnp.mean(d * d, axis=-1, keepdims=True)
        out_ref[0, :, h * dv:(h + 1) * dv] = (d * lax.rsqrt(ms + EPS) * g_ref[...] * (1.0 - lam_init)
                                              ).astype(out_ref.dtype)


def diff_combine(o, lam4, g_out, lam_init):
    B, L, W2 = o.shape
    dv = g_out.shape[0]
    W = W2 // 2
    tl = _pick(L, (256, 128))
    return pl.pallas_call(
        functools.partial(_diff_combine_kernel, dv=dv, lam_init=lam_init),
        grid=(B, L // tl),
        in_specs=[pl.BlockSpec((1, tl, W2), lambda b, i: (b, i, 0)),
                  pl.BlockSpec(lam4.shape, lambda b, i: (0, 0)),
                  pl.BlockSpec((1, dv), lambda b, i: (0, 0))],
        out_specs=pl.BlockSpec((1, tl, W), lambda b, i: (b, i, 0)),
        out_shape=jax.ShapeDtypeStruct((B, L, W), BF16),
        compiler_params=_cparams(("arbitrary", "arbitrary")),
        name="diff_combine",
    )(o, lam4, g_out.reshape(1, dv))


def _decode_kernel(*refs, n_scalar, n_chunks, diff, lam_init, n_heads, n_new):
    refs = refs[n_scalar:]
    q_ref = refs[0]
    k_refs = refs[1:1 + n_chunks]
    v_refs = refs[1 + n_chunks:1 + 2 * n_chunks]
    bm_refs = refs[1 + 2 * n_chunks:1 + 3 * n_chunks]
    rest = refs[1 + 3 * n_chunks:]
    if diff:
        kn_ref, vn_ref, bmn_ref, lam_ref, g_ref, out_ref, m_s, l_s, acc_s = rest
    else:
        kn_ref, vn_ref, bmn_ref, out_ref, m_s, l_s, acc_s = rest
    p = pl.program_id(1)

    @pl.when(p == 0)
    def _():
        m_s[...] = jnp.full(m_s.shape, NEG, F32)
        l_s[...] = jnp.zeros(l_s.shape, F32)
        acc_s[...] = jnp.zeros(acc_s.shape, F32)

    q = q_ref[0]

    def absorb(k2s, v2s, bms):
        ss = [_bdot_nt(q, k2) + bm for k2, bm in zip(k2s, bms)]
        m_old = m_s[...]
        m_new = m_old
        for s in ss:
            m_new = jnp.maximum(m_new, jnp.max(s, axis=-1, keepdims=True))
        alpha = jnp.exp(m_old - m_new)
        l_new = alpha * l_s[...]
        acc = alpha * acc_s[...]
        for s, v2 in zip(ss, v2s):
            pr = jnp.exp(s - m_new)
            l_new = l_new + jnp.sum(pr, axis=-1, keepdims=True)
            acc = acc + _bdot(pr, v2)
        l_s[...] = l_new
        acc_s[...] = acc
        m_s[...] = m_new

    def rows2d(ref):
        x = ref[0]
        return x.reshape(x.shape[0] * x.shape[1], x.shape[2])

    absorb([rows2d(r) for r in k_refs], [rows2d(r) for r in v_refs], [r[0] for r in bm_refs])

    @pl.when(p == pl.num_programs(1) - 1)
    def _():
        absorb([kn_ref[0]], [vn_ref[0]], [bmn_ref[...]])
        o = acc_s[...] / l_s[...]
        if diff:
            half = n_heads * n_new
            lam = _diff_lambda(lam_ref, lam_init)
            d = o[0:half] - lam * o[half:2 * half]
            ms = jnp.mean(d * d, axis=-1, keepdims=True)
            o = d * lax.rsqrt(ms + EPS) * g_ref[...] * (1.0 - lam_init)
        dv = o.shape[1]
        for h in range(n_heads):
            out_ref[0, :, h * dv:(h + 1) * dv] = o[h * n_new:(h + 1) * n_new].astype(out_ref.dtype)


def _head_expand(bias_rows, row_head, n_heads):
    ok = jnp.asarray(row_head[:, None] == np.arange(n_heads)[None, :])
    out = jnp.where(ok[:, None, :], bias_rows[..., None], NEG)
    return out.reshape(bias_rows.shape[:-1] + (bias_rows.shape[-1] * n_heads,))


def swa_decode(qn, k_new, v_new, cache_k, cache_v, rel_bias):
    B, Wb, H, dh = cache_k.shape
    T = qn.shape[1]
    R = H * T
    tk = _pick(Wb, (512, 256, 128))
    nblk = Wb // tk
    q2 = jnp.transpose(qn.reshape(B, T, H, dh), (0, 2, 1, 3)).reshape(B, R, dh)
    row_head = np.arange(R) // T
    dist_np = (Wb + np.arange(T))[:, None] - np.arange(Wb)[None, :]
    bias, ok = _swa_bias(rel_bias, dist_np)
    assert ok[:, :tk].any(axis=1).all()
    bm = _head_expand(bias.reshape(R, Wb), row_head, H)
    bm = jnp.transpose(bm.reshape(R, nblk, tk * H), (1, 0, 2))
    dn = np.arange(T)[:, None] - np.arange(T)[None, :]
    bias_n, _ = _swa_bias(rel_bias, dn)
    bmn = _head_expand(bias_n.reshape(R, T), row_head, H)
    kn2 = k_new.reshape(B, T * H, dh)
    vn2 = v_new.reshape(B, T * H, dh)
    return pl.pallas_call(
        functools.partial(_decode_kernel, n_scalar=0, n_chunks=1, diff=False, lam_init=0.0, n_heads=H, n_new=T),
        grid=(B, nblk),
        in_specs=[
            pl.BlockSpec((1, R, dh), lambda b, p: (b, 0, 0)),
            pl.BlockSpec((1, tk, H, dh), lambda b, p: (b, p, 0, 0)),
            pl.BlockSpec((1, tk, H, dh), lambda b, p: (b, p, 0, 0)),
            pl.BlockSpec((1, R, tk * H), lambda b, p: (p, 0, 0)),
            pl.BlockSpec((1, T * H, dh), lambda b, p: (b, 0, 0)),
            pl.BlockSpec((1, T * H, dh), lambda b, p: (b, 0, 0)),
            pl.BlockSpec((R, T * H), lambda b, p: (0, 0)),
        ],
        out_specs=pl.BlockSpec((1, T, H * dh), lambda b, p: (b, 0, 0)),
        out_shape=jax.ShapeDtypeStruct((B, T, H * dh), BF16),
        scratch_shapes=[pltpu.VMEM((R, 1), F32), pltpu.VMEM((R, 1), F32), pltpu.VMEM((R, dh), F32)],
        compiler_params=_cparams(("arbitrary", "arbitrary")),
        name="swa_decode",
    )(q2, cache_k, cache_v, bm, kn2, vn2, bmn)


def _np_rel_bucket(dist):
    exact = REL_BUCKETS // 2
    d_f = np.maximum(dist, 1).astype(np.float32)
    large = exact + (np.log(d_f / np.float32(exact)) / np.float32(math.log(REL_MAX_DIST / exact))
                     * np.float32(REL_BUCKETS - exact)).astype(np.int32)
    large = np.minimum(large, REL_BUCKETS - 1)
    return np.where(dist < exact, dist, large)


def diff_paged(qn, k_new, v_new, cache_k, cache_v, page_table, rel_bias, lam4, g_out, lam_init):
    B, T, H, _, dh = qn.shape
    dv = v_new.shape[-1]
    n_pages = page_table.shape[1]
    page = cache_k.shape[1]
    P = n_pages * page
    R = 2 * H * T
    pp = next(c for c in (8, 4, 2, 1) if n_pages % c == 0)
    qt = jnp.transpose(qn, (0, 3, 2, 1, 4))
    zeros = jnp.zeros_like(qt[:, 0])
    qm = jnp.concatenate([jnp.concatenate([qt[:, 0], zeros], axis=-1),
                          jnp.concatenate([zeros, qt[:, 1]], axis=-1)], axis=1)
    qm = qm.reshape(B, R, 2 * dh)
    pg = np.arange(n_pages)[:, None, None]
    tt = np.arange(T)[None, :, None]
    ii = np.arange(page)[None, None, :]
    dist_np = P + tt - pg * page - ii
    bucket_np = _np_rel_bucket(dist_np)
    far = np.all(bucket_np == bucket_np[0:1], axis=(1, 2))
    near_pages = [int(x) for x in np.nonzero(~far)[0]]
    tile_pages = [0] + near_pages
    tile_of_page = np.zeros((n_pages,), np.int32)
    for ti, pgi in enumerate(near_pages):
        tile_of_page[pgi] = ti + 1
    nt = len(tile_pages)
    bias_t = _bias_of_dist(rel_bias, jnp.asarray(dist_np[tile_pages], jnp.int32))
    bias_rows = jnp.transpose(bias_t.reshape(2, H, nt, T, page), (2, 0, 1, 3, 4)).reshape(nt, R, page)
    row_head = (np.arange(R) // T) % H
    bm = _head_expand(bias_rows, row_head, H)
    dn = np.arange(T)[:, None] - np.arange(T)[None, :]
    bias_n, _ = _causal_bias(rel_bias, dn)
    bmn = _head_expand(bias_n.reshape(R, T), row_head, H)
    kn2 = k_new.reshape(B, T * H, 2 * dh)
    vn2 = v_new.reshape(B, T * H, dv)

    def kv_spec(j, d):
        return pl.BlockSpec((1, page, H, d), lambda b, p, pt, tl: (pt[b * n_pages + p * pp + j], 0, 0, 0))

    def bm_spec(j):
        return pl.BlockSpec((1, R, page * H), lambda b, p, pt, tl: (tl[p * pp + j], 0, 0))

    grid_spec = pltpu.PrefetchScalarGridSpec(
        num_scalar_prefetch=2,
        grid=(B, n_pages // pp),
        in_specs=(
            [pl.BlockSpec((1, R, 2 * dh), lambda b, p, pt, tl: (b, 0, 0))]
            + [kv_spec(j, 2 * dh) for j in range(pp)]
            + [kv_spec(j, dv) for j in range(pp)]
            + [bm_spec(j) for j in range(pp)]
            + [pl.BlockSpec((1, T * H, 2 * dh), lambda b, p, pt, tl: (b, 0, 0)),
               pl.BlockSpec((1, T * H, dv), lambda b, p, pt, tl: (b, 0, 0)),
               pl.BlockSpec((R, T * H), lambda b, p, pt, tl: (0, 0)),
               pl.BlockSpec(lam4.shape, lambda b, p, pt, tl: (0, 0)),
               pl.BlockSpec((1, dv), lambda b, p, pt, tl: (0, 0))]),
        out_specs=pl.BlockSpec((1, T, H * dv), lambda b, p, pt, tl: (b, 0, 0)),
        scratch_shapes=[pltpu.VMEM((R, 1), F32), pltpu.VMEM((R, 1), F32), pltpu.VMEM((R, dv), F32)],
    )
    return pl.pallas_call(
        functools.partial(_decode_kernel, n_scalar=2, n_chunks=pp, diff=True, lam_init=lam_init,
                          n_heads=H, n_new=T),
        grid_spec=grid_spec,
        out_shape=jax.ShapeDtypeStruct((B, T, H * dv), BF16),
        compiler_params=_cparams(("arbitrary", "arbitrary")),
        name="diff_paged",
    )(page_table.reshape(-1), jnp.asarray(tile_of_page), qm, *([cache_k] * pp), *([cache_v] * pp),
      *([bm] * pp), kn2, vn2, bmn, lam4, g_out.reshape(1, dv))


def _rglru_kernel(y_ref, x_ref, cs_ref, h0_ref, cw_ref, cb_ref, wa_ref, ba_ref, wx_ref, bx_ref, lam_ref,
                  out_ref, conv_ref, hl_ref, xpad, hcar, *, l_valid):
    i = pl.program_id(1)
    T, D = x_ref.shape[1], x_ref.shape[2]

    @pl.when(i == 0)
    def _():
        xpad[0:8, :] = cs_ref[0]
        hcar[...] = h0_ref[0]

    xpad[8:8 + T, :] = x_ref[0]
    conv = cb_ref[...] + cw_ref[CONV_W - 1:CONV_W, :] * xpad[8:8 + T, :]
    for j in range(CONV_W - 1):
        s = CONV_W - 1 - j
        conv = conv + cw_ref[j:j + 1, :] * xpad[8 - s:8 - s + T, :]
    nb = wa_ref.shape[0]
    bs = D // nb
    r_parts, i_parts = [], []
    for n in range(nb):
        xb = conv[:, n * bs:(n + 1) * bs]
        r_parts.append(_bdot(xb, wa_ref[n]))
        i_parts.append(_bdot(xb, wx_ref[n]))
    r = jax.nn.sigmoid(jnp.concatenate(r_parts, axis=-1) + ba_ref[...])
    ig = jax.nn.sigmoid(jnp.concatenate(i_parts, axis=-1) + bx_ref[...])
    log_a = -LRU_C * r * _softplus(-lam_ref[...])
    a = jnp.exp(log_a)
    bb = jnp.sqrt(-jnp.tanh(log_a) * (1.0 + a * a)) * ig * conv
    rowid = lax.broadcasted_iota(jnp.int32, (T, D), 0)
    s = 1
    while s < T:
        keep = rowid >= s
        a_sh = pltpu.roll(a, s, 0)
        b_sh = pltpu.roll(bb, s, 0)
        bb = jnp.where(keep, a * b_sh + bb, bb)
        a = jnp.where(keep, a * a_sh, a)
        s *= 2
    hs = a * hcar[...] + bb
    out_ref[0] = (hs * _gelu(y_ref[0])).astype(out_ref.dtype)
    hcar[...] = hs[T - 1:T, :]
    tail = xpad[T:T + 8, :]
    xpad[0:8, :] = tail

    @pl.when(i == pl.num_programs(1) - 1)
    def _():
        hl_ref[0] = hs[l_valid - 1:l_valid, :]
        conv_ref[0] = xpad[8 + l_valid - (CONV_W - 1):8 + l_valid, :]


def rglru_core(z, conv_state, h0, conv_w, conv_b, w_a, b_a, w_x, b_x, lam, l_valid_last, tblk):
    B, L, D2 = z.shape
    D = D2 // 2
    cs8 = jnp.concatenate([jnp.zeros((B, 8 - (CONV_W - 1), D), F32), conv_state.astype(F32)], axis=1)
    vec = lambda a: a.reshape(1, D)
    out, conv_new, h_last = pl.pallas_call(
        functools.partial(_rglru_kernel, l_valid=l_valid_last),
        grid=(B, L // tblk),
        in_specs=[
            pl.BlockSpec((1, tblk, D), lambda b, i: (b, i, 0)),
            pl.BlockSpec((1, tblk, D), lambda b, i: (b, i, 1)),
            pl.BlockSpec((1, 8, D), lambda b, i: (b, 0, 0)),
            pl.BlockSpec((1, 1, D), lambda b, i: (b, 0, 0)),
            pl.BlockSpec((CONV_W, D), lambda b, i: (0, 0)),
            pl.BlockSpec((1, D), lambda b, i: (0, 0)),
            pl.BlockSpec(w_a.shape, lambda b, i: (0, 0, 0)),
            pl.BlockSpec((1, D), lambda b, i: (0, 0)),
            pl.BlockSpec(w_x.shape, lambda b, i: (0, 0, 0)),
            pl.BlockSpec((1, D), lambda b, i: (0, 0)),
            pl.BlockSpec((1, D), lambda b, i: (0, 0)),
        ],
        out_specs=[
            pl.BlockSpec((1, tblk, D), lambda b, i: (b, i, 0)),
            pl.BlockSpec((1, CONV_W - 1, D), lambda b, i: (b, 0, 0)),
            pl.BlockSpec((1, 1, D), lambda b, i: (b, 0, 0)),
        ],
        out_shape=[
            jax.ShapeDtypeStruct((B, L, D), BF16),
            jax.ShapeDtypeStruct((B, CONV_W - 1, D), F32),
            jax.ShapeDtypeStruct((B, 1, D), F32),
        ],
        scratch_shapes=[pltpu.VMEM((tblk + 8, D), F32), pltpu.VMEM((1, D), F32)],
        compiler_params=_cparams(("arbitrary", "arbitrary")),
        name="rglru",
    )(z, z, cs8, h0.astype(F32).reshape(B, 1, D), conv_w, vec(conv_b), w_a.astype(BF16), vec(b_a),
      w_x.astype(BF16), vec(b_x), vec(lam))
    return out, conv_new, h_last.reshape(B, D)


def _argmax_rows(s):
    n, tb = s.shape
    row8 = lax.broadcasted_iota(jnp.int32, (8, tb), 0).astype(F32)
    vals = [s[i:i + 8] for i in range(0, n, 8)]
    rows = [row8 + float(i) for i in range(0, n, 8)]
    while len(vals) > 1:
        nv, nr = [], []
        for a in range(0, len(vals) - 1, 2):
            first = vals[a] >= vals[a + 1]
            nv.append(jnp.maximum(vals[a], vals[a + 1]))
            nr.append(jnp.where(first, rows[a], rows[a + 1]))
        if len(vals) % 2:
            nv.append(vals[-1])
            nr.append(rows[-1])
        vals, rows = nv, nr
    v, r = vals[0], rows[0]
    m = jnp.max(v, axis=0, keepdims=True)
    return m, jnp.min(jnp.where(v == m, r, float(n)), axis=0, keepdims=True)


def _topk_rows(s, k):
    n, tb = s.shape
    rowf = lax.broadcasted_iota(jnp.int32, (n, tb), 0).astype(F32)
    vals, idxs = [], []
    for _ in range(k):
        m, idx = _argmax_rows(s)
        vals.append(m)
        idxs.append(idx)
        s = jnp.where(rowf == idx, -jnp.inf, s)
    return jnp.concatenate(vals, axis=0), jnp.concatenate(idxs, axis=0)


def _pair_plan(k):
    full, small = [], []
    a = 0
    while a < k and k // (a + 1) > 1:
        nb, b0 = k // (a + 1), 0
        while nb - b0 >= 8:
            full.append((a, b0))
            b0 += 8
        if nb > b0:
            small.append((a, b0, nb - b0))
        a += 1
    a0 = a
    assert k % 8 == 0 and (k - a0) % 8 == 0 and (k & (k - 1)) == 0
    bins = []
    for a_, b0, n in sorted(small, key=lambda p: -p[2]):
        for bn in bins:
            used = sum(p[3] for p in bn)
            if used + n <= 8:
                bn.append((a_, b0, used, n))
                break
        else:
            bins.append([(a_, b0, 0, n)])
    code = []
    for a_, b0 in full:
        code += [a_ * k + b0 + r for r in range(8)]
    for bn in bins:
        rows = [k * k] * 8
        for a_, b0, off, n in bn:
            for r in range(n):
                rows[off + r] = a_ * k + b0 + r
        code += rows
    code += [a_ * k for a_ in range(a0, k)]
    return full, bins, a0, np.array(code, np.float32)


def _route_kernel(*refs, topk, cast_experts):
    if cast_experts:
        h_ref, wq_ref, k1_ref, k2_ref, code_ref, u_ref, v_ref, e1_ref, e2_ref, g_ref, ut_ref, vb_ref = refs
        ut_ref[...] = u_ref[0].T.astype(BF16)
        vb_ref[...] = v_ref[0].astype(BF16)
    else:
        h_ref, wq_ref, k1_ref, k2_ref, code_ref, e1_ref, e2_ref, g_ref = refs
    k = topk
    q = jnp.dot(h_ref[...], wq_ref[...], preferred_element_type=F32)
    dk2 = k1_ref.shape[2]
    tb = q.shape[0]
    nt = (((1,), (1,)), ((), ()))
    s1 = lax.dot_general(k1_ref[0], q[:, :dk2], nt, precision=HIGHEST, preferred_element_type=F32)
    s2 = lax.dot_general(k2_ref[0], q[:, dk2:], nt, precision=HIGHEST, preferred_element_type=F32)
    sv1, si1 = _topk_rows(s1, k)
    sv2, si2 = _topk_rows(s2, k)
    full, bins, a0, _ = _pair_plan(k)
    row8 = lax.broadcasted_iota(jnp.int32, (8, tb), 0)
    groups = [sv1[a:a + 1] + sv2[b0:b0 + 8] for a, b0 in full]
    for bn in bins:
        v = jnp.full((8, tb), -jnp.inf, F32)
        for a, b0, off, n in bn:
            piece = sv1[a:a + 1] + (pltpu.roll(sv2[b0:b0 + 8], off, 0) if off else sv2[b0:b0 + 8])
            v = jnp.where((row8 >= off) & (row8 < off + n), piece, v)
        groups.append(v)
    for c in range(a0, k, 8):
        groups.append(sv1[c:c + 8] + sv2[0:1])
    cand = jnp.concatenate(groups, axis=0)
    code = code_ref[...]
    vals, poss = [], []
    for _ in range(k):
        m = jnp.max(cand, axis=0, keepdims=True)
        pos = jnp.min(jnp.where(cand == m, code, float(k * k)), axis=0, keepdims=True)
        vals.append(m)
        poss.append(pos)
        cand = jnp.where(code == pos, -jnp.inf, cand)
    val = jnp.concatenate(vals, axis=0)
    pos = jnp.concatenate(poss, axis=0)
    ra = jnp.floor(pos * (1.0 / k))
    rb = pos - k * ra
    e1 = jnp.zeros((k, tb), F32)
    e2 = jnp.zeros((k, tb), F32)
    for r in range(k):
        e1 = jnp.where(ra == float(r), si1[r:r + 1], e1)
        e2 = jnp.where(rb == float(r), si2[r:r + 1], e2)
    e1_ref[0] = e1
    e2_ref[0] = e2
    ex = jnp.exp(val - val[0:1])
    g_ref[0] = ex / jnp.sum(ex, axis=0, keepdims=True)


def peer_route(h, wq, keys, experts=None):
    N, D = h.shape
    nh, _, nk, dk2 = keys.shape
    tb = _pick(N, (2048, 1024, 512, 256, 128))
    nblk = N // tb
    code = _pair_plan(TOPK_P)[3]
    code = jnp.asarray(np.broadcast_to(code[:, None], (code.shape[0], tb)))
    shp = jax.ShapeDtypeStruct((nh, TOPK_P, N), F32)
    o_spec = pl.BlockSpec((1, TOPK_P, tb), lambda hh, i: (hh, 0, i))
    in_specs = [pl.BlockSpec((tb, D), lambda hh, i: (i, 0)),
                pl.BlockSpec((D, 2 * dk2), lambda hh, i: (0, hh)),
                pl.BlockSpec((1, nk, dk2), lambda hh, i: (2 * hh, 0, 0)),
                pl.BlockSpec((1, nk, dk2), lambda hh, i: (2 * hh + 1, 0, 0)),
                pl.BlockSpec(code.shape, lambda hh, i: (0, 0))]
    args = [h, wq, keys.reshape(nh * 2, nk, dk2), keys.reshape(nh * 2, nk, dk2), code]
    out_specs, out_shape = [o_spec, o_spec, o_spec], [shp, shp, shp]
    if experts is not None:
        u_all, v_all, layer = experts
        E = u_all.shape[1]
        rb = E // (nh * nblk)
        assert rb * nh * nblk == E and rb % LANES == 0
        in_specs += [pl.BlockSpec((1, rb, D), lambda hh, i: (layer, hh * nblk + i, 0))] * 2
        args += [u_all, v_all]
        out_specs += [pl.BlockSpec((D, rb), lambda hh, i: (0, hh * nblk + i)),
                      pl.BlockSpec((rb, D), lambda hh, i: (hh * nblk + i, 0))]
        out_shape += [jax.ShapeDtypeStruct((D, E), BF16), jax.ShapeDtypeStruct((E, D), BF16)]
    return pl.pallas_call(
        functools.partial(_route_kernel, topk=TOPK_P, cast_experts=experts is not None),
        grid=(nh, nblk),
        in_specs=in_specs,
        out_specs=out_specs,
        out_shape=out_shape,
        compiler_params=_cparams(("arbitrary", "arbitrary")),
        name="peer_route",
    )(*args)


def _route_build_kernel(e1_ref, e2_ref, g_ref, sub_ref, o_ref, *, nkeys):
    sub = sub_ref[...][None]
    one = jnp.ones((), BF16)
    zero = jnp.zeros((), BF16)
    at = jnp.where(e1_ref[...].astype(BF16) == sub, one, zero)
    bt = jnp.where(e2_ref[...].astype(BF16) == sub, g_ref[...].astype(BF16), zero)
    g3 = jnp.einsum('tik,tjk->tij', at, bt, preferred_element_type=F32)
    gt = pltpu.einshape('tij->itj', g3)
    for i in range(nkeys):
        o_ref[:, i * nkeys:(i + 1) * nkeys] = gt[i].astype(o_ref.dtype)


def peer_build(e1, e2, g, nkeys):
    nh, k, N = e1.shape
    ks = nh * k
    slot = lambda a: jnp.transpose(a, (2, 0, 1)).reshape(N, 1, ks)
    tb = _pick(N, (256, 128, 64, 32))
    spec = pl.BlockSpec((tb, 1, ks), lambda i: (i, 0, 0))
    sub = jnp.asarray(np.broadcast_to(np.arange(nkeys, dtype=np.float32)[:, None], (nkeys, ks)), BF16)
    return pl.pallas_call(
        functools.partial(_route_build_kernel, nkeys=nkeys),
        grid=(N // tb,),
        in_specs=[spec, spec, spec, pl.BlockSpec((nkeys, ks), lambda i: (0, 0))],
        out_specs=pl.BlockSpec((tb, nkeys * nkeys), lambda i: (i, 0)),
        out_shape=jax.ShapeDtypeStruct((N, nkeys * nkeys), BF16),
        compiler_params=_cparams(("arbitrary",)),
        name="peer_build",
    )(slot(e1), slot(e2), slot(g), sub)


def _peer_kernel(*refs, with_next, with_rider):
    if with_next:
        x_ref, ut_ref, v_ref, g_ref, r_ref, gt_ref, gn_ref, sh_ref, sc_ref = refs[:9]
        refs = refs[9:]
    else:
        x_ref, ut_ref, v_ref, g_ref, r_ref, gt_ref = refs[:6]
        refs = refs[6:]
    if with_rider:
        n_in = 6 if with_next else 4
        rider_in, refs = refs[:n_in], refs[n_in:]
    n_out = 2 if with_next else 1
    outs, refs = refs[:n_out], refs[n_out:]
    if with_rider:
        rider_out, refs = refs[:n_out], refs[n_out:]
    acc_s = refs[0]
    i = pl.program_id(0)
    e = pl.program_id(1)
    last = pl.num_programs(1) - 1

    def block(x_ref, g_ref, r_ref, gt_ref, shsc, out_refs, acc):
        @pl.when(e == 0)
        def _():
            acc[...] = jnp.zeros(acc.shape, F32)

        s = jnp.dot(x_ref[...], ut_ref[...], preferred_element_type=F32)
        p = (_gelu(s) * g_ref[...].astype(F32)).astype(BF16)
        acc[...] += jnp.dot(p, v_ref[...], preferred_element_type=F32)

        @pl.when(e == last)
        def _():
            xn = r_ref[...] + gt_ref[0] * acc[...]
            out_refs[0][...] = xn
            if with_next:
                out_refs[1][...] = _modulated(xn, gn_ref, *shsc).astype(out_refs[1].dtype)

    block(x_ref, g_ref, r_ref, gt_ref, (sh_ref, sc_ref) if with_next else None, outs, acc_s)
    if with_rider:
        pl.when(i == 0)(lambda: block(rider_in[0], rider_in[1], rider_in[2], rider_in[3],
                                      rider_in[4:6] if with_next else None, rider_out, refs[1]))


def peer_dense(h, ut, v, G, res, gate, next_mod=None, rider=None):
    N, D = h.shape
    E = ut.shape[1]
    tb = _pick(N, (512, 256, 128))
    eb = _pick(E, (1024, 512, 256, 128))

    def param_spec(p):
        if p.shape[1] == 1:
            rows = N // p.shape[0]
            assert rows % tb == 0
            return pl.BlockSpec((1, 1, D), lambda i, e: (i // (rows // tb), 0, 0))
        return pl.BlockSpec((1, tb, D), lambda i, e: (0, i, 0))

    row = pl.BlockSpec((tb, D), lambda i, e: (i, 0))
    in_specs = [row,
                pl.BlockSpec((D, eb), lambda i, e: (0, e)),
                pl.BlockSpec((eb, D), lambda i, e: (e, 0)),
                pl.BlockSpec((tb, eb), lambda i, e: (i, e)),
                row, param_spec(gate)]
    args = [h, ut, v, G, res, gate]
    out_specs, out_shape = [row], [jax.ShapeDtypeStruct((N, D), F32)]
    if next_mod is not None:
        g_norm, shift, scale = next_mod
        in_specs += [pl.BlockSpec((1, D), lambda i, e: (0, 0)), param_spec(shift), param_spec(scale)]
        args += [g_norm.reshape(1, D), shift, scale]
        out_specs.append(row)
        out_shape.append(jax.ShapeDtypeStruct((N, D), BF16))
    scratch = [pltpu.VMEM((tb, D), F32)]
    n_out = len(out_specs)
    if rider is not None:
        N2 = rider[0].shape[0]
        whole = pl.BlockSpec((N2, D), lambda i, e: (0, 0))
        per_row = pl.BlockSpec((1, N2, D), lambda i, e: (0, 0, 0))
        in_specs += [whole, pl.BlockSpec((N2, eb), lambda i, e: (0, e)), whole] + [per_row] * (len(rider) - 3)
        args += list(rider)
        out_specs += [whole] * n_out
        out_shape += [jax.ShapeDtypeStruct((N2, D), F32), jax.ShapeDtypeStruct((N2, D), BF16)][:n_out]
        scratch.append(pltpu.VMEM((N2, D), F32))
    outs = pl.pallas_call(
        functools.partial(_peer_kernel, with_next=next_mod is not None, with_rider=rider is not None),
        grid=(N // tb, E // eb),
        in_specs=in_specs,
        out_specs=out_specs,
        out_shape=out_shape,
        scratch_shapes=scratch,
        compiler_params=_cparams(("arbitrary", "arbitrary")),
        name="peer_dense",
    )(*args)
    outs = list(outs) + [None] * (4 - len(outs))
    if next_mod is None:
        outs = [outs[0], None, outs[1] if rider is not None else None, None]
    return tuple(outs)


def _pad_rows(a, n, value=0.0, axis=1):
    pad = [(0, 0)] * a.ndim
    pad[axis] = (0, n - a.shape[axis])
    return jnp.pad(a, pad, constant_values=value)


def kernel(x_prompt, x_sample, state_mlstm_C, state_mlstm_n, state_mlstm_m, cache_swa_k, cache_swa_v,
           cache_diff_k, cache_diff_v, state_rglru_conv, state_rglru_h, page_table, c_prompt, c_sample,
           w_ada, b_ada, g_norm_mix, g_norm_ffn, rel_bias,
           mlstm_w_in, mlstm_b_gates, mlstm_g_out, mlstm_w_out,
           swa_w_in, swa_g_q, swa_g_k, swa_w_out,
           diff_w_in, diff_g_q, diff_g_k, diff_lam_q1, diff_lam_k1, diff_lam_q2, diff_lam_k2, diff_g_out, diff_w_out,
           rglru_w_in, rglru_conv_w, rglru_conv_b, rglru_w_a, rglru_b_a, rglru_w_x, rglru_b_x, rglru_lambda, rglru_w_out,
           peer_w_q, peer_keys, peer_u, peer_v):
    xp, xs = x_prompt, x_sample
    Bp, S, D = xp.shape
    Bs, T, _ = xs.shape
    depth = w_ada.shape[0]
    Ns = Bs * T

    c_all = jnp.concatenate([c_prompt, c_sample], axis=0)
    mod = ada_all(c_all, w_ada, b_ada)
    mod = mod.reshape(depth, Bp + Bs, 6, 1, D)

    def mods(i, which):
        m = mod[i, :, which]
        return m[:Bp], m[Bp:]

    def rows(p):
        return jnp.broadcast_to(p, (Bs, T, D)).reshape(1, Ns, D)

    nk = peer_keys.shape[3]
    Ns_pad = -(-Ns // LANES) * LANES

    sh_p, sh_s = mods(0, 0)
    sc_p, sc_s = mods(0, 1)
    hp = modulate(xp, g_norm_mix[0], sh_p, sc_p)
    hs = modulate(xs, g_norm_mix[0], sh_s, sc_s)
    for i in range(depth):
        kind = i % 4
        gt_p, gt_s = mods(i, 2)
        hs_flat = hs.reshape(1, Ns, D)
        if kind == 0:
            H = NH_A
            dk = state_mlstm_C.shape[2]
            dv = state_mlstm_C.shape[3]
            nmain = 2 * H * dk + 2 * H * dv
            wb = mlstm_w_in.astype(BF16)
            w_gate = mlstm_w_in[:, nmain:].astype(BF16)
            zp = matmul(hp, wb, cols=(0, nmain))
            gp = matmul(hp, w_gate)
            zs = matmul(hs_flat, wb, cols=(0, nmain)).reshape(Bs, T, nmain)
            gs = matmul(hs_flat, w_gate).reshape(Bs, T, 2 * H)
            chunk_p = _pick(S, (CHUNK_A, 128))
            op, mC_p, mn_p, mm_p = mlstm_core(
                zp, gp, mlstm_b_gates, jnp.zeros((Bp, H, dk, dv), F32), jnp.zeros((Bp, H, dk), F32),
                jnp.full((Bp, H), M_INIT, F32), mlstm_g_out, chunk_p)
            Tp = 128
            zs_pad = _pad_rows(zs, Tp)
            gs_pad = jnp.concatenate([
                _pad_rows(gs[..., :H], Tp, NEG), _pad_rows(gs[..., H:], Tp, -NEG)], axis=-1)
            os_, mC_s, mn_s, mm_s = mlstm_core(
                zs_pad, gs_pad, mlstm_b_gates, state_mlstm_C.astype(F32), state_mlstm_n.astype(F32),
                state_mlstm_m.astype(F32), mlstm_g_out, Tp)
            os_ = os_[:, :T]
            w_out = mlstm_w_out
        elif kind == 1:
            H = NH_B
            dh = D // H
            wb = swa_w_in.astype(BF16)
            gq = jnp.tile(swa_g_q, H)
            gk = jnp.tile(swa_g_k, H)
            qn_p = matmul_headnorm(hp, wb, (0, D), gq, dh, dh ** -0.5 * LOG2E, BF16)
            kn_p = matmul_headnorm(hp, wb, (D, D), gk, dh, 1.0, F32)
            vp = matmul(hp, wb, cols=(2 * D, D))
            qn_s = matmul_headnorm(hs_flat, wb, (0, D), gq, dh, dh ** -0.5, BF16).reshape(Bs, T, D)
            kn_s = matmul_headnorm(hs_flat, wb, (D, D), gk, dh, 1.0, F32).reshape(Bs, T, D)
            vs_new = matmul(hs_flat, wb, cols=(2 * D, D)).reshape(Bs, T, D)
            t = _pick(S, (512, 256, 128))
            assert (_np_multiplicity(_toeplitz_dist_np(S // t, t)) > 0).any(axis=2).all()
            bm_tab = _toeplitz_bias(rel_bias, S // t, t, _swa_bias, LOG2E)
            op = flash_attention(qn_p, kn_p, vp, bm_tab, n_units=H, dqk=dh, dv=dh,
                                 qcol=lambda a: a, kcol=lambda a: a, vcol=lambda a: a, bmap=lambda a: a,
                                 t=t, out_dtype=BF16)
            swa_k_p = kn_p.reshape(Bp, S, H, dh)
            swa_v_p = vp.reshape(Bp, S, H, dh)
            os_ = swa_decode(qn_s, kn_s, vs_new, cache_swa_k, cache_swa_v, rel_bias)
            def shifted(cache, new):
                Wb = cache.shape[1]
                buf = lax.pad(cache.astype(F32), jnp.zeros((), F32), ((0, 0, 0), (-T, T, 0), (0, 0, 0), (0, 0, 0)))
                return lax.dynamic_update_slice(buf, new.reshape(Bs, T, H, dh), (0, Wb - T, 0, 0))

            swa_k_s = shifted(cache_swa_k, kn_s)
            swa_v_s = shifted(cache_swa_v, vs_new)
            w_out = swa_w_out
        elif kind == 2:
            H = NH_C
            dh = D // (2 * H)
            dv = 2 * dh
            lam_init = 0.8 - 0.6 * math.exp(-0.3 * i)
            lam4 = jnp.stack([diff_lam_q1, diff_lam_k1, diff_lam_q2, diff_lam_k2]).astype(F32)
            wb = diff_w_in.astype(BF16)
            gq = jnp.tile(diff_g_q.reshape(-1), H)
            gk = jnp.tile(diff_g_k.reshape(-1), H)
            qn_p = matmul_headnorm(hp, wb, (0, D), gq, dh, dh ** -0.5 * LOG2E, BF16)
            kn_p = matmul_headnorm(hp, wb, (D, D), gk, dh, 1.0, F32)
            vp = matmul(hp, wb, cols=(2 * D, D))
            qn_s = matmul_headnorm(hs_flat, wb, (0, D), gq, dh, dh ** -0.5, BF16).reshape(Bs, T, D)
            kn_s = matmul_headnorm(hs_flat, wb, (D, D), gk, dh, 1.0, F32).reshape(Bs, T, D)
            vs_new = matmul(hs_flat, wb, cols=(2 * D, D)).reshape(Bs, T, D)
            t = _pick(S, (512, 256, 128))
            bm_tab = _toeplitz_bias(rel_bias, S // t, t, _causal_bias, LOG2E)
            o2 = flash_attention(qn_p, kn_p, vp, bm_tab, n_units=2 * H, dqk=dh, dv=dv,
                                 qcol=lambda a: a, kcol=lambda a: a, vcol=lambda a: a // 2,
                                 bmap=lambda a: (a % 2) * H + a // 2, t=t, out_dtype=F32)
            op = diff_combine(o2, lam4, diff_g_out, lam_init)
            diff_k_p = kn_p.reshape(Bp, S, H, 2 * dh)
            diff_v_p = vp.reshape(Bp, S, H, dv)
            diff_k_s = kn_s.reshape(Bs, T, H, 2 * dh)
            diff_v_s = vs_new.reshape(Bs, T, H, dv)
            os_ = diff_paged(qn_s.reshape(Bs, T, H, 2, dh), diff_k_s, diff_v_s, cache_diff_k, cache_diff_v,
                             page_table, rel_bias, lam4, diff_g_out, lam_init)
            w_out = diff_w_out
        else:
            wb = rglru_w_in.astype(BF16)
            zp = matmul(hp, wb)
            zs = matmul(hs_flat, wb).reshape(Bs, T, -1)
            tblk = _pick(S, (256, 128))
            op, conv_p, h_p = rglru_core(zp, jnp.zeros((Bp, CONV_W - 1, D), F32), jnp.zeros((Bp, D), F32),
                                         rglru_conv_w, rglru_conv_b, rglru_w_a, rglru_b_a, rglru_w_x, rglru_b_x,
                                         rglru_lambda, tblk, tblk)
            os_, conv_s, h_s = rglru_core(_pad_rows(zs, 8), state_rglru_conv, state_rglru_h,
                                          rglru_conv_w, rglru_conv_b, rglru_w_a, rglru_b_a, rglru_w_x, rglru_b_x,
                                          rglru_lambda, T, 8)
            os_ = os_[:, :T]
            w_out = rglru_w_out
        sh_p, sh_s = mods(i, 3)
        sc_p, sc_s = mods(i, 4)
        wb = w_out.astype(BF16)
        xp, hp = matmul_res_mod(op, wb, xp, gt_p, g_norm_ffn[i], sh_p, sc_p)
        xs, hs = matmul_res_mod(os_.reshape(1, Ns, -1), wb, xs.reshape(1, Ns, D), rows(gt_s), g_norm_ffn[i],
                                rows(sh_s), rows(sc_s))
        hp = hp.reshape(Bp * S, D)
        hs = hs.reshape(Ns, D)

        gt_p, gt_s = mods(i, 5)
        wq = peer_w_q[i].astype(BF16)
        e1, e2, gw, ut, vb = peer_route(hp, wq, peer_keys[i], experts=(peer_u, peer_v, i))
        Gp = peer_build(e1, e2, gw, nk)
        Gs = peer_build(*peer_route(_pad_rows(hs, Ns_pad, axis=0), wq, peer_keys[i]), nk)[:Ns]
        if i + 1 < depth:
            sh_p, sh_s = mods(i + 1, 0)
            sc_p, sc_s = mods(i + 1, 1)
            next_p = (g_norm_mix[i + 1], sh_p, sc_p)
            next_s = (rows(sh_s), rows(sc_s))
        else:
            next_p, next_s = None, ()
        xp, hp, xs, hs = peer_dense(hp, ut, vb, Gp, xp.reshape(Bp * S, D), gt_p, next_p,
                                    rider=(hs, Gs, xs.reshape(Ns, D), rows(gt_s)) + next_s)
        xp = xp.reshape(Bp, S, D)
        xs = xs.reshape(Bs, T, D)
        if hp is not None:
            hp = hp.reshape(Bp, S, D)
            hs = hs.reshape(Bs, T, D)

    return (xp, xs, mC_p, mC_s, mn_p, mn_s, mm_p, mm_s, swa_k_p, swa_k_s, swa_v_p, swa_v_s,
            diff_k_p, diff_k_s, diff_v_p, diff_v_s, conv_p, conv_s, h_p, h_s)
```
